```python
import math
import jax
import jax.numpy as jnp
from jax import lax
import numpy as np


D_MODEL = 1024
BATCH = 2
SEQ = 8192
DEPTH = 2

GRID_W = 64
CTX_LEN = 256
HEAD_DIM = 64
MIX_W = D_MODEL
Q_BLOCK = 128
ROPE_THETA = 10000.0
DIFF_HEADS = MIX_W // 2 // (2 * HEAD_DIM)
DIFF_QK = DIFF_HEADS * 2 * HEAD_DIM
DIFF_V = DIFF_HEADS * 2 * HEAD_DIM
GQA_HEADS = MIX_W // 2 // HEAD_DIM
GQA_KV_HEADS = max(1, GQA_HEADS // 4)
GQA_GROUP = GQA_HEADS // GQA_KV_HEADS
GQA_Q = GQA_HEADS * HEAD_DIM
GQA_KV_W = GQA_KV_HEADS * HEAD_DIM
AB_Q = DIFF_QK + GQA_Q
AB_KV = DIFF_QK + DIFF_V + 2 * GQA_KV_W
AB_OUT = DIFF_V + GQA_Q
WINDOW = 128
WIN_HEADS = MIX_W // 2 // HEAD_DIM
WIN_KV_HEADS = max(1, WIN_HEADS // 4)
WIN_GROUP = WIN_HEADS // WIN_KV_HEADS
WIN_Q = WIN_HEADS * HEAD_DIM
WIN_KV_W = WIN_KV_HEADS * HEAD_DIM
NA_HEADS = MIX_W // 2 // HEAD_DIM
NA_W = NA_HEADS * HEAD_DIM
NA_ROWS_MAX = 8
NA_COLS = 16
CD_Q = WIN_Q + NA_W
CD_KV = 2 * WIN_KV_W + 2 * NA_W
CD_OUT = WIN_Q + NA_W
N_GROUPS = 4
EXPERTS_PER_GROUP = 8
N_EXPERTS = N_GROUPS * EXPERTS_PER_GROUP
TOP_K_IN_GROUP = 2
D_EXPERT = D_MODEL // 2
MOE_BLOCK = 128
N_EVEN = (DEPTH + 1) // 2
N_ODD = DEPTH // 2
ALPHA = (2.0 * DEPTH) ** 0.25
BETA = (8.0 * DEPTH) ** -0.25
LN_EPS = 1e-5
RMS_EPS = 1e-6
ATTN_SCALE = HEAD_DIM ** -0.5
NEG_INF = -1e30

kernel_name = 'hybrid_flow_backbone'


def layer_norm(x, g, b):
    xf = x.astype(jnp.float32)
    mu = jnp.mean(xf, -1, keepdims=True)
    var = jnp.mean(jnp.square(xf - mu), -1, keepdims=True)
    y = (xf - mu) * lax.rsqrt(var + LN_EPS) * g.astype(jnp.float32) + b.astype(jnp.float32)
    return y.astype(x.dtype)


def rms_norm(x, g):
    xf = x.astype(jnp.float32)
    y = xf * lax.rsqrt(jnp.mean(jnp.square(xf), -1, keepdims=True) + RMS_EPS) * g.astype(jnp.float32)
    return y.astype(x.dtype)


def modulate(x, shift, scale):
    return x * (1.0 + scale) + shift


def lambda_init(layer_idx):
    return 0.8 - 0.6 * math.exp(-0.3 * layer_idx)


def axial_rope_tables(n_tokens):
    t = jnp.arange(n_tokens, dtype=jnp.int32)
    row = (t // GRID_W).astype(jnp.float32)
    col = (t % GRID_W).astype(jnp.float32)
    n_freq = HEAD_DIM // 4
    inv = ROPE_THETA ** (-jnp.arange(n_freq, dtype=jnp.float32) / n_freq)
    ang = jnp.concatenate([row[:, None] * inv, col[:, None] * inv], -1)
    return jnp.cos(ang), jnp.sin(ang)


def apply_rope(x, cos, sin):
    shp = (1, cos.shape[0]) + (1,) * (x.ndim - 3) + (cos.shape[1],)
    cos = cos.reshape(shp)
    sin = sin.reshape(shp)
    xf = x.astype(jnp.float32)
    x1, x2 = jnp.split(xf, 2, -1)
    return jnp.concatenate([x1 * cos - x2 * sin, x2 * cos + x1 * sin], -1).astype(x.dtype)


def sweep_query_blocks(fn, q):
    b, s = q.shape[:2]
    nb = s // Q_BLOCK
    qb = jnp.moveaxis(q.reshape((b, nb, Q_BLOCK) + q.shape[2:]), 1, 0)
    out = lax.map(lambda a: fn(a[0], a[1]), (jnp.arange(nb, dtype=jnp.int32), qb))
    out = jnp.moveaxis(out, 0, 1)
    return out.reshape((b, s) + out.shape[3:])


def diff_attend(qb, k, v, lam):
    s = jnp.einsum('bqhcd,bkhcd->bhcqk', qb, k).astype(jnp.float32) * ATTN_SCALE
    p = jax.nn.softmax(s, -1)
    a = p[:, :, 0] - lam * p[:, :, 1]
    return jnp.einsum('bhqk,bkhe->bqhe', a.astype(v.dtype), v)


def gqa_attend(qb, k, v):
    s = jnp.einsum('bqhgd,bkhd->bhgqk', qb, k).astype(jnp.float32) * ATTN_SCALE
    p = jax.nn.softmax(s, -1)
    return jnp.einsum('bhgqk,bkhd->bqhgd', p.astype(v.dtype), v)


def sink_attend(qb, k, v, sink, valid=None):
    s = jnp.einsum('bqhgd,bkhd->bhgqk', qb, k).astype(jnp.float32) * ATTN_SCALE
    if valid is not None:
        s = jnp.where(valid, s, NEG_INF)
    sk = jnp.broadcast_to(sink.astype(jnp.float32)[None, :, :, None, None], s.shape[:-1] + (1,))
    p = jax.nn.softmax(jnp.concatenate([s, sk], -1), -1)[..., :-1]
    return jnp.einsum('bhgqk,bkhd->bqhgd', p.astype(v.dtype), v)


def window_attention(q, k, v, kc, vc, sink):
    s = q.shape[1]
    n_ctx = kc.shape[1]
    span = Q_BLOCK + 2 * WINDOW
    pad = ((0, 0), (WINDOW, WINDOW), (0, 0), (0, 0))
    kp = jnp.pad(k, pad)
    vp = jnp.pad(v, pad)
    ctx_valid = jnp.ones((Q_BLOCK, n_ctx), dtype=bool)

    def block(bi, qb):
        start = bi * Q_BLOCK
        kb = lax.dynamic_slice_in_dim(kp, start, span, axis=1)
        vb = lax.dynamic_slice_in_dim(vp, start, span, axis=1)
        qpos = start + jnp.arange(Q_BLOCK)
        kpos = start - WINDOW + jnp.arange(span)
        local = (jnp.abs(qpos[:, None] - kpos[None, :]) <= WINDOW) & (kpos >= 0)[None, :] & (kpos < s)[None, :]
        valid = jnp.concatenate([local, ctx_valid], 1)
        return sink_attend(qb, jnp.concatenate([kb, kc], 1), jnp.concatenate([vb, vc], 1), sink, valid)

    return sweep_query_blocks(block, q)


def neighbourhood_attention(q, k, v, kc, vc, rpb):
    b, s, h, d = q.shape
    rows = s // GRID_W
    kr = min(NA_ROWS_MAX, rows)
    n_nb = kr * NA_COLS
    qg = jnp.moveaxis(q.reshape(b, rows, GRID_W, h, d), 1, 0)
    kg = k.reshape(b, rows, GRID_W, h, d)
    vg = v.reshape(b, rows, GRID_W, h, d)
    col = jnp.arange(GRID_W)
    col_start = jnp.clip(col - NA_COLS // 2, 0, GRID_W - NA_COLS)
    cols = col_start[:, None] + jnp.arange(NA_COLS)
    col_off = cols - col[:, None] + (NA_COLS - 1)

    def row_block(a):
        r, qr = a
        r0 = jnp.clip(r - kr // 2, 0, rows - kr)
        slab_k = lax.dynamic_slice_in_dim(kg, r0, kr, axis=1)
        slab_v = lax.dynamic_slice_in_dim(vg, r0, kr, axis=1)
        k_nb = jnp.moveaxis(slab_k[:, :, cols], 1, 2).reshape(b, GRID_W, n_nb, h, d)
        v_nb = jnp.moveaxis(slab_v[:, :, cols], 1, 2).reshape(b, GRID_W, n_nb, h, d)
        row_off = r0 + jnp.arange(kr) - r + (NA_ROWS_MAX - 1)
        bias = rpb[:, row_off[None, :, None], col_off[:, None, :]].reshape(h, GRID_W, n_nb)
        s_nb = jnp.einsum('bqhd,bqkhd->bhqk', qr, k_nb).astype(jnp.float32) * ATTN_SCALE + bias.astype(jnp.float32)[None]
        s_cx = jnp.einsum('bqhd,bkhd->bhqk', qr, kc).astype(jnp.float32) * ATTN_SCALE
        p = jax.nn.softmax(jnp.concatenate([s_nb, s_cx], -1), -1).astype(v.dtype)
        return (jnp.einsum('bhqk,bqkhd->bqhd', p[..., :n_nb], v_nb)
                + jnp.einsum('bhqk,bkhd->bqhd', p[..., n_nb:], vc))

    out = lax.map(row_block, (jnp.arange(rows, dtype=jnp.int32), qg))
    return jnp.moveaxis(out, 0, 1).reshape(b, s, h, d)


def mixer_ab(a_lat, a_ctx, ctx_kv_only, w_in, w_out, lam_vecs, subln_g, qk_g, lam_init, cos, sin):
    b = a_lat.shape[0]

    def split_q(p):
        n = p.shape[1]
        q_d = p[..., :DIFF_QK].reshape(b, n, DIFF_HEADS, 2, HEAD_DIM)
        q_g = rms_norm(p[..., DIFF_QK:AB_Q].reshape(b, n, GQA_KV_HEADS, GQA_GROUP, HEAD_DIM), qk_g[0])
        return q_d, q_g

    def split_kv(p):
        n = p.shape[1]
        o1 = DIFF_QK
        o2 = o1 + DIFF_V
        o3 = o2 + GQA_KV_W
        k_d = p[..., :o1].reshape(b, n, DIFF_HEADS, 2, HEAD_DIM)
        v_d = p[..., o1:o2].reshape(b, n, DIFF_HEADS, 2 * HEAD_DIM)
        k_g = rms_norm(p[..., o2:o3].reshape(b, n, GQA_KV_HEADS, HEAD_DIM), qk_g[1])
        v_g = p[..., o3:].reshape(b, n, GQA_KV_HEADS, HEAD_DIM)
        return k_d, v_d, k_g, v_g

    lv = lam_vecs.astype(jnp.float32)
    lam = jnp.exp(jnp.sum(lv[0] * lv[1])) - jnp.exp(jnp.sum(lv[2] * lv[3])) + lam_init

    p_lat = a_lat @ w_in
    q_d, q_g = split_q(p_lat[..., :AB_Q])
    k_d, v_d, k_g, v_g = split_kv(p_lat[..., AB_Q:])
    q_d, k_d, q_g, k_g = (apply_rope(t, cos, sin) for t in (q_d, k_d, q_g, k_g))
    if ctx_kv_only:
        ck_d, cv_d, ck_g, cv_g = split_kv(a_ctx @ w_in[:, AB_Q:])
    else:
        p_ctx = a_ctx @ w_in
        cq_d, cq_g = split_q(p_ctx[..., :AB_Q])
        ck_d, cv_d, ck_g, cv_g = split_kv(p_ctx[..., AB_Q:])

    kd_all = jnp.concatenate([k_d, ck_d], 1)
    vd_all = jnp.concatenate([v_d, cv_d], 1)
    kg_all = jnp.concatenate([k_g, ck_g], 1)
    vg_all = jnp.concatenate([v_g, cv_g], 1)
    o_d = sweep_query_blocks(lambda _, qb: diff_attend(qb, kd_all, vd_all, lam), q_d)
    o_g = sweep_query_blocks(lambda _, qb: gqa_attend(qb, kg_all, vg_all), q_g)

    def merge(od, og):
        n = od.shape[1]
        od = rms_norm(od, subln_g) * (1.0 - lam_init)
        return jnp.concatenate([od.reshape(b, n, DIFF_V), og.reshape(b, n, GQA_Q)], -1) @ w_out

    out_lat = merge(o_d, o_g)
    if ctx_kv_only:
        return out_lat, None
    out_ctx = merge(diff_attend(cq_d, ck_d, cv_d, lam), gqa_attend(cq_g, ck_g, cv_g))
    return out_lat, out_ctx


def mixer_cd(a_lat, a_ctx, ctx_kv_only, w_in, w_out, sink, rpb, cos, sin):
    b = a_lat.shape[0]
    sink_hg = sink.reshape(WIN_KV_HEADS, WIN_GROUP)

    def split_q(p):
        n = p.shape[1]
        q_w = p[..., :WIN_Q].reshape(b, n, WIN_KV_HEADS, WIN_GROUP, HEAD_DIM)
        q_n = p[..., WIN_Q:CD_Q].reshape(b, n, NA_HEADS, HEAD_DIM)
        return q_w, q_n

    def split_kv(p):
        n = p.shape[1]
        o1 = WIN_KV_W
        o2 = o1 + WIN_KV_W
        o3 = o2 + NA_W
        k_w = p[..., :o1].reshape(b, n, WIN_KV_HEADS, HEAD_DIM)
        v_w = p[..., o1:o2].reshape(b, n, WIN_KV_HEADS, HEAD_DIM)
        k_n = p[..., o2:o3].reshape(b, n, NA_HEADS, HEAD_DIM)
        v_n = p[..., o3:].reshape(b, n, NA_HEADS, HEAD_DIM)
        return k_w, v_w, k_n, v_n

    p_lat = a_lat @ w_in
    q_w, q_n = split_q(p_lat[..., :CD_Q])
    k_w, v_w, k_n, v_n = split_kv(p_lat[..., CD_Q:])
    q_w = apply_rope(q_w, cos, sin)
    k_w = apply_rope(k_w, cos, sin)
    if ctx_kv_only:
        ck_w, cv_w, ck_n, cv_n = split_kv(a_ctx @ w_in[:, CD_Q:])
    else:
        p_ctx = a_ctx @ w_in
        cq_w, cq_n = split_q(p_ctx[..., :CD_Q])
        ck_w, cv_w, ck_n, cv_n = split_kv(p_ctx[..., CD_Q:])

    o_w = window_attention(q_w, k_w, v_w, ck_w, cv_w, sink_hg)
    o_n = neighbourhood_attention(q_n, k_n, v_n, ck_n, cv_n, rpb)

    def merge(ow, on):
        n = ow.shape[1]
        return jnp.concatenate([ow.reshape(b, n, WIN_Q), on.reshape(b, n, NA_W)], -1) @ w_out

    out_lat = merge(o_w, o_n)
    if ctx_kv_only:
        return out_lat, None
    n_ctx = a_ctx.shape[1]
    co_w = sink_attend(cq_w, ck_w, cv_w, sink_hg)
    co_n = gqa_attend(cq_n.reshape(b, n_ctx, NA_HEADS, 1, HEAD_DIM), ck_n, cv_n)
    return out_lat, merge(co_w, co_n)


def hier_moe(h, w_group, b_group, w_router, b_router, w1, w3, w2):
    t, d = h.shape
    g_logits = (h @ w_group).astype(jnp.float32) + b_group.astype(jnp.float32)
    _, g_idx = lax.top_k(g_logits, 1)
    g_w = jnp.take_along_axis(jax.nn.softmax(g_logits, -1), g_idx, -1)
    e_logits = ((h @ w_router).astype(jnp.float32) + b_router.astype(jnp.float32)).reshape(t, N_GROUPS, EXPERTS_PER_GROUP)
    e_in = jnp.take_along_axis(e_logits, g_idx[:, :, None], 1)[:, 0]
    top_v, top_i = lax.top_k(e_in, TOP_K_IN_GROUP)
    weights = g_w * jax.nn.softmax(top_v, -1)
    experts = g_idx * EXPERTS_PER_GROUP + top_i

    n_assign = t * TOP_K_IN_GROUP
    e_flat = experts.reshape(n_assign)
    tok = jnp.repeat(jnp.arange(t, dtype=jnp.int32), TOP_K_IN_GROUP)
    w_flat = weights.reshape(n_assign)
    order = jnp.argsort(e_flat)
    e_s = e_flat[order]
    tok_s = tok[order]
    w_s = w_flat[order]
    counts = jnp.bincount(e_flat, length=N_EXPERTS)
    starts = jnp.cumsum(counts) - counts
    padded = (counts + MOE_BLOCK - 1) // MOE_BLOCK * MOE_BLOCK
    p_ends = jnp.cumsum(padded)
    p_starts = p_ends - padded
    dest = p_starts[e_s] + (jnp.arange(n_assign, dtype=jnp.int32) - starts[e_s])
    n_blocks = (n_assign + MOE_BLOCK - 1) // MOE_BLOCK + N_EXPERTS
    buf = jnp.zeros((n_blocks * MOE_BLOCK, d), h.dtype).at[dest].set(h[tok_s])
    blk_e = jnp.minimum(jnp.searchsorted(p_ends, jnp.arange(n_blocks) * MOE_BLOCK, side='right'), N_EXPERTS - 1)

    def expert_block(a):
        xb, e = a
        return (jax.nn.silu(xb @ w1[e]) * (xb @ w3[e])) @ w2[e]

    yb = lax.map(expert_block, (buf.reshape(n_blocks, MOE_BLOCK, d), blk_e)).reshape(n_blocks * MOE_BLOCK, d)
    return jnp.zeros((t, d), h.dtype).at[tok_s].add(w_s[:, None].astype(h.dtype) * yb[dest])


def setup_inputs(seed: int = 0) -> dict:
    key = jax.random.key(seed)
    ks = jax.random.split(key, 32)
    D = D_MODEL

    def nrm(k, shape, s):
        return jax.random.normal(k, shape, jnp.float32) * s

    ab_kv_scale = jnp.concatenate([jnp.ones((DIFF_QK,)), jnp.full((DIFF_V,), BETA), jnp.ones((GQA_KV_W,)), jnp.full((GQA_KV_W,), BETA)]).astype(jnp.float32)
    cd_kv_scale = jnp.concatenate([jnp.ones((WIN_KV_W,)), jnp.full((WIN_KV_W,), BETA), jnp.ones((NA_W,)), jnp.full((NA_W,), BETA)]).astype(jnp.float32)
    return {
        'x': nrm(ks[0], (BATCH, SEQ, D), 1.0),
        'c': nrm(ks[1], (BATCH, D), 1.0),
        'ctx': nrm(ks[2], (BATCH, CTX_LEN, D), 1.0),
        'c_ctx': nrm(ks[3], (D,), 1.0),
        'mod_w': nrm(ks[4], (DEPTH, D, 6 * D), 0.5 * D ** -0.5),
        'mod_b': nrm(ks[5], (DEPTH, 6 * D), 0.02),
        'ln_g': 1.0 + nrm(ks[6], (DEPTH, 2, D), 0.02),
        'ln_b': nrm(ks[7], (DEPTH, 2, D), 0.02),
        'ab_w_in': jnp.concatenate([nrm(ks[8], (N_EVEN, D, AB_Q), D ** -0.5), nrm(ks[9], (N_EVEN, D, AB_KV), D ** -0.5) * ab_kv_scale], -1),
        'ab_w_out': nrm(ks[10], (N_EVEN, AB_OUT, D), AB_OUT ** -0.5 * BETA),
        'diff_lambda': nrm(ks[11], (N_EVEN, 4, HEAD_DIM), 0.1),
        'diff_subln_g': 1.0 + nrm(ks[12], (N_EVEN, 2 * HEAD_DIM), 0.02),
        'gqa_qk_g': 1.0 + nrm(ks[13], (N_EVEN, 2, HEAD_DIM), 0.02),
        'cd_w_in': jnp.concatenate([nrm(ks[14], (N_ODD, D, CD_Q), D ** -0.5), nrm(ks[15], (N_ODD, D, CD_KV), D ** -0.5) * cd_kv_scale], -1),
        'cd_w_out': nrm(ks[16], (N_ODD, CD_OUT, D), CD_OUT ** -0.5 * BETA),
        'win_sink': nrm(ks[17], (N_ODD, WIN_HEADS), 0.5),
        'na_rpb': nrm(ks[18], (N_ODD, NA_HEADS, 2 * NA_ROWS_MAX - 1, 2 * NA_COLS - 1), 0.1),
        'moe_w_group': nrm(ks[19], (DEPTH, D, N_GROUPS), D ** -0.5),
        'moe_b_group': nrm(ks[20], (DEPTH, N_GROUPS), 0.01),
        'moe_w_router': nrm(ks[21], (DEPTH, D, N_EXPERTS), D ** -0.5),
        'moe_b_router': nrm(ks[22], (DEPTH, N_EXPERTS), 0.01),
        'moe_w1': nrm(ks[23], (DEPTH, N_EXPERTS, D, D_EXPERT), D ** -0.5),
        'moe_w3': nrm(ks[24], (DEPTH, N_EXPERTS, D, D_EXPERT), D ** -0.5),
        'moe_w2': nrm(ks[25], (DEPTH, N_EXPERTS, D_EXPERT, D), D_EXPERT ** -0.5 * BETA),
    }


def reference(x, c, ctx, c_ctx, mod_w, mod_b, ln_g, ln_b, ab_w_in, ab_w_out, diff_lambda, diff_subln_g, gqa_qk_g, cd_w_in, cd_w_out, win_sink, na_rpb, moe_w_group, moe_b_group, moe_w_router, moe_b_router, moe_w1, moe_w3, moe_w2):
    b, s, d = x.shape
    n_ctx = ctx.shape[1]
    cos, sin = axial_rope_tables(s)
    silu_c = jax.nn.silu(c)
    silu_cc = jax.nn.silu(c_ctx)
    h_lat, h_ctx = x, ctx
    for i in range(DEPTH):
        last = i == DEPTH - 1
        j = i // 2
        mods = (silu_c @ mod_w[i] + mod_b[i]).reshape(b, 1, 6, d)
        n_cm = 2 if last else 6
        cmods = (silu_cc @ mod_w[i][:, :n_cm * d] + mod_b[i][:n_cm * d]).reshape(1, 1, n_cm, d)

        a_lat = modulate(h_lat, mods[:, :, 0], mods[:, :, 1])
        a_ctx = modulate(h_ctx, cmods[:, :, 0], cmods[:, :, 1])
        if i % 2 == 0:
            o_lat, o_ctx = mixer_ab(a_lat, a_ctx, last, ab_w_in[j], ab_w_out[j], diff_lambda[j], diff_subln_g[j], gqa_qk_g[j], lambda_init(i), cos, sin)
        else:
            o_lat, o_ctx = mixer_cd(a_lat, a_ctx, last, cd_w_in[j], cd_w_out[j], win_sink[j], na_rpb[j], cos, sin)
        h_lat = layer_norm(ALPHA * h_lat + mods[:, :, 2] * o_lat, ln_g[i, 0], ln_b[i, 0])

        def moe(tokens):
            return hier_moe(tokens, moe_w_group[i], moe_b_group[i], moe_w_router[i], moe_b_router[i], moe_w1[i], moe_w3[i], moe_w2[i])

        f_lat = modulate(h_lat, mods[:, :, 3], mods[:, :, 4]).reshape(b * s, d)
        if last:
            y_lat = moe(f_lat)
        else:
            h_ctx = layer_norm(ALPHA * h_ctx + cmods[:, :, 2] * o_ctx, ln_g[i, 0], ln_b[i, 0])
            f_ctx = modulate(h_ctx, cmods[:, :, 3], cmods[:, :, 4]).reshape(b * n_ctx, d)
            y = moe(jnp.concatenate([f_lat, f_ctx], 0))
            y_lat = y[:b * s]
            h_ctx = layer_norm(ALPHA * h_ctx + cmods[:, :, 5] * y[b * s:].reshape(b, n_ctx, d), ln_g[i, 1], ln_b[i, 1])
        h_lat = layer_norm(ALPHA * h_lat + mods[:, :, 5] * y_lat.reshape(b, s, d), ln_g[i, 1], ln_b[i, 1])
    return h_lat
```

```python
import functools

import numpy as np
import jax
import jax.numpy as jnp
from jax import lax
from jax.experimental import pallas as pl
from jax.experimental.pallas import tpu as pltpu

F32 = jnp.float32
BF16 = jnp.bfloat16
HIGHEST = lax.Precision.HIGHEST

GRID_W = 64
HEAD_DIM = 64
ROPE_THETA = 10000.0
WINDOW = 128
NA_ROWS_MAX = 8
NA_COLS = 16
N_GROUPS = 4
EXPERTS_PER_GROUP = 8
N_EXPERTS = N_GROUPS * EXPERTS_PER_GROUP
TOP_K = 2
MOE_BLOCK = 128
LN_EPS = 1e-5
RMS_EPS = 1e-6
ATTN_SCALE = HEAD_DIM ** -0.5
NEG_INF = -1e30

LANES = 128
PAIR = 2 * HEAD_DIM
assert PAIR == LANES

VMEM_LIMIT = 48 * 1024 * 1024

TM = 256
TQ_DIFF = 256
TQ_GQA = 128
TK = 512
TQ_WIN = 128
NA_QROWS = 4
NA_SLAB = 12


def _cparams(n_axes):
    return pltpu.CompilerParams(dimension_semantics=("arbitrary",) * n_axes, vmem_limit_bytes=VMEM_LIMIT)


def _lane_iota():
    return lax.broadcasted_iota(jnp.int32, (1, LANES), 1)


def _mods_kernel(c_ref, w_ref, b_ref, o_ref):
    c = c_ref[...]
    sc = c / (1.0 + jnp.exp(-c))
    o_ref[...] = jnp.dot(sc, w_ref[...], precision=HIGHEST, preferred_element_type=F32) + b_ref[...]


def _mods_call(cvec, mod_w, mod_b):
    depth, d, n6 = mod_w.shape
    tn = n6 // 4
    return pl.pallas_call(
        _mods_kernel,
        grid=(depth, n6 // tn),
        in_specs=[
            pl.BlockSpec((8, d), lambda l, j: (0, 0)),
            pl.BlockSpec((None, d, tn), lambda l, j: (l, 0, j)),
            pl.BlockSpec((None, 1, tn), lambda l, j: (l, 0, j)),
        ],
        out_specs=pl.BlockSpec((None, 8, tn), lambda l, j: (l, 0, j)),
        out_shape=jax.ShapeDtypeStruct((depth, 8, n6), F32),
        compiler_params=_cparams(2),
        name="mods",
    )(cvec, mod_w, mod_b.reshape(depth, 1, n6))


def _inproj_kernel(h_ref, sh_ref, sc_ref, w_ref, cos_ref, sin_ref, g_ref, o_ref, *, roles):
    a = (h_ref[...] * sc_ref[...] + sh_ref[...]).astype(BF16)
    lane = _lane_iota()
    rope_first = (lane & (HEAD_DIM - 1)) < (HEAD_DIM // 2)
    lo = lane < HEAD_DIM
    n_blocks = len(roles)
    for grp in range(n_blocks // 2):
        p = jnp.dot(a, w_ref[:, grp * 2 * LANES:(grp + 1) * 2 * LANES], preferred_element_type=F32)
        for half in range(2):
            cb = grp * 2 + half
            rope, norm, scale = roles[cb]
            blk = p[:, half * LANES:(half + 1) * LANES]
            if norm is not None:
                sq = blk * blk
                s_lo = jnp.sum(jnp.where(lo, sq, 0.0), axis=-1, keepdims=True)
                s_hi = jnp.sum(jnp.where(lo, 0.0, sq), axis=-1, keepdims=True)
                ms = jnp.where(lo, s_lo, s_hi) * (1.0 / HEAD_DIM)
                blk = blk * lax.rsqrt(ms + RMS_EPS) * g_ref[norm:norm + 1, :]
            if rope:
                partner = jnp.where(rope_first, pltpu.roll(blk, LANES - HEAD_DIM // 2, 1),
                                    pltpu.roll(blk, HEAD_DIM // 2, 1))
                blk = blk * cos_ref[...] + partner * sin_ref[...]
            if scale:
                blk = blk * ATTN_SCALE
            o_ref[:, cb * LANES:(cb + 1) * LANES] = blk.astype(BF16)


def _inproj_call(h, shift, scale1p, w16, cos128, sin128, g128, roles, s_len):
    b, n, d = h.shape
    ncol = w16.shape[1]
    n_lat = s_len // TM
    seg = lambda bb, j: (bb, jnp.where(j >= n_lat, 1, 0), 0, 0)
    return pl.pallas_call(
        functools.partial(_inproj_kernel, roles=roles),
        grid=(b, n // TM),
        in_specs=[
            pl.BlockSpec((None, TM, d), lambda bb, j: (bb, j, 0)),
            pl.BlockSpec((None, None, 1, d), seg),
            pl.BlockSpec((None, None, 1, d), seg),
            pl.BlockSpec((d, ncol), lambda bb, j: (0, 0)),
            pl.BlockSpec((TM, LANES), lambda bb, j: (j, 0)),
            pl.BlockSpec((TM, LANES), lambda bb, j: (j, 0)),
            pl.BlockSpec((2, LANES), lambda bb, j: (0, 0)),
        ],
        out_specs=pl.BlockSpec((None, TM, ncol), lambda bb, j: (bb, j, 0)),
        out_shape=jax.ShapeDtypeStruct((b, n, ncol), BF16),
        compiler_params=_cparams(2),
        name="inproj",
    )(h, shift, scale1p, w16, cos128, sin128, g128)


def _stack_q(q_ref, n_qblocks):
    lo = _lane_iota() < HEAD_DIM
    parts = []
    for j in range(n_qblocks):
        q = q_ref[:, j * LANES:(j + 1) * LANES]
        zero = jnp.zeros_like(q)
        parts.append(jnp.where(lo, q, zero))
        parts.append(jnp.where(lo, zero, q))
    return jnp.concatenate(parts, axis=0)


def _scores(qs, k):
    return lax.dot_general(qs, k, (((1,), (1,)), ((), ())), preferred_element_type=F32)


def _flash_init(m_ref, l_ref, acc_ref):
    m_ref[...] = jnp.full(m_ref.shape, NEG_INF, F32)
    l_ref[...] = jnp.zeros(l_ref.shape, F32)
    acc_ref[...] = jnp.zeros(acc_ref.shape, F32)


def _flash_step(qs, k, v, m_ref, l_ref, acc_ref):
    s = _scores(qs, k)
    m_prev = m_ref[...]
    m_new = jnp.maximum(m_prev, jnp.max(s, axis=-1, keepdims=True))
    alpha = jnp.exp(m_prev - m_new)
    p = jnp.exp(s - m_new)
    l_ref[...] = alpha * l_ref[...] + jnp.sum(p, axis=-1, keepdims=True)
    acc_ref[...] = alpha * acc_ref[...] + jnp.dot(p.astype(BF16), v, preferred_element_type=F32)
    m_ref[...] = m_new


def _flash_sweep(qs, k_ref, v_ref, m_ref, l_ref, acc_ref, *, is_lat, s_len, n_ctx):
    _flash_init(m_ref, l_ref, acc_ref)

    @pl.when(is_lat)
    def _():
        def body(c, carry):
            off = pl.multiple_of(c * TK, TK)
            _flash_step(qs, k_ref[pl.ds(off, TK), :], v_ref[pl.ds(off, TK), :], m_ref, l_ref, acc_ref)
            return carry
        lax.fori_loop(0, s_len // TK, body, 0)

    _flash_step(qs, k_ref[s_len:s_len + n_ctx, :], v_ref[s_len:s_len + n_ctx, :], m_ref, l_ref, acc_ref)


def _merge_pairs(o, tq, n_qblocks, o_ref):
    lo = _lane_iota() < HEAD_DIM
    for j in range(n_qblocks):
        o_lo = o[(2 * j) * tq:(2 * j + 1) * tq]
        o_hi = o[(2 * j + 1) * tq:(2 * j + 2) * tq]
        o_ref[:, j * LANES:(j + 1) * LANES] = jnp.where(lo, o_lo, o_hi).astype(o_ref.dtype)


def _diff_attn_kernel(q_ref, k_ref, v_ref, lam_ref, g_ref, o_ref, m_ref, l_ref, acc_ref, *, s_len, n_ctx, lam_init):
    tq = q_ref.shape[0]
    i = pl.program_id(2)
    qs = _stack_q(q_ref, 1)
    _flash_sweep(qs, k_ref, v_ref, m_ref, l_ref, acc_ref, is_lat=i < s_len // tq, s_len=s_len, n_ctx=n_ctx)
    lv = lam_ref[...]
    lam = (jnp.exp(jnp.sum(lv[0:1] * lv[1:2], axis=-1, keepdims=True))
           - jnp.exp(jnp.sum(lv[2:3] * lv[3:4], axis=-1, keepdims=True)) + lam_init)
    o = acc_ref[...] / l_ref[...]
    od = o[:tq] - lam * o[tq:]
    ms = jnp.mean(od * od, axis=-1, keepdims=True)
    od = od * lax.rsqrt(ms + RMS_EPS) * g_ref[...] * (1.0 - lam_init)
    o_ref[...] = od.astype(o_ref.dtype)


def _diff_attn_call(p, lam_vecs, subln_g, *, s_len, n_ctx, n_heads, q_blk0, k_blk0, v_blk0, lam_init):
    b, n, _ = p.shape
    tq = TQ_DIFF
    return pl.pallas_call(
        functools.partial(_diff_attn_kernel, s_len=s_len, n_ctx=n_ctx, lam_init=lam_init),
        grid=(b, n_heads, n // tq),
        in_specs=[
            pl.BlockSpec((None, tq, LANES), lambda bb, hh, i: (bb, i, q_blk0 + hh)),
            pl.BlockSpec((None, n, LANES), lambda bb, hh, i: (bb, 0, k_blk0 + hh)),
            pl.BlockSpec((None, n, LANES), lambda bb, hh, i: (bb, 0, v_blk0 + hh)),
            pl.BlockSpec((4, HEAD_DIM), lambda bb, hh, i: (0, 0)),
            pl.BlockSpec((1, LANES), lambda bb, hh, i: (0, 0)),
        ],
        out_specs=pl.BlockSpec((None, tq, LANES), lambda bb, hh, i: (bb, i, hh)),
        out_shape=jax.ShapeDtypeStruct((b, n, n_heads * LANES), BF16),
        scratch_shapes=[pltpu.VMEM((2 * tq, 1), F32), pltpu.VMEM((2 * tq, 1), F32), pltpu.VMEM((2 * tq, LANES), F32)],
        compiler_params=_cparams(3),
        name="diff_attn",
    )(p, p, p, lam_vecs, subln_g.reshape(1, LANES))


def _gqa_attn_kernel(q_ref, k_ref, v_ref, o_ref, m_ref, l_ref, acc_ref, *, s_len, n_ctx):
    tq = q_ref.shape[0]
    n_qblocks = q_ref.shape[1] // LANES
    i = pl.program_id(1)
    qs = _stack_q(q_ref, n_qblocks)
    _flash_sweep(qs, k_ref, v_ref, m_ref, l_ref, acc_ref, is_lat=i < s_len // tq, s_len=s_len, n_ctx=n_ctx)
    _merge_pairs(acc_ref[...] / l_ref[...], tq, n_qblocks, o_ref)


def _gqa_attn_call(p, *, s_len, n_ctx, q_width, q_col0, k_blk, v_blk):
    b, n, _ = p.shape
    tq = TQ_GQA
    m_rows = 2 * (q_width // LANES) * tq
    return pl.pallas_call(
        functools.partial(_gqa_attn_kernel, s_len=s_len, n_ctx=n_ctx),
        grid=(b, n // tq),
        in_specs=[
            pl.BlockSpec((None, tq, q_width), lambda bb, i: (bb, i, q_col0 // q_width)),
            pl.BlockSpec((None, n, LANES), lambda bb, i: (bb, 0, k_blk)),
            pl.BlockSpec((None, n, LANES), lambda bb, i: (bb, 0, v_blk)),
        ],
        out_specs=pl.BlockSpec((None, tq, q_width), lambda bb, i: (bb, i, 0)),
        out_shape=jax.ShapeDtypeStruct((b, n, q_width), BF16),
        scratch_shapes=[pltpu.VMEM((m_rows, 1), F32), pltpu.VMEM((m_rows, 1), F32), pltpu.VMEM((m_rows, LANES), F32)],
        compiler_params=_cparams(2),
        name="gqa_attn",
    )(p, p, p)


def _win_attn_kernel(q_ref, k_ref, v_ref, sink_ref, o_ref, *, s_len, n_ctx):
    tq = q_ref.shape[0]
    n_qblocks = q_ref.shape[1] // LANES
    span = tq + 2 * WINDOW
    i = pl.program_id(1)
    ws = pl.multiple_of(jnp.clip(i * tq - WINDOW, 0, s_len - span), LANES)
    qs = _stack_q(q_ref, n_qblocks)
    m_rows = qs.shape[0]
    s_loc = _scores(qs, k_ref[pl.ds(ws, span), :])
    qpos = i * tq + (lax.broadcasted_iota(jnp.int32, (m_rows, span), 0) & (tq - 1))
    kpos = ws + lax.broadcasted_iota(jnp.int32, (m_rows, span), 1)
    s_loc = jnp.where(jnp.abs(qpos - kpos) <= WINDOW, s_loc, NEG_INF)
    s_ctx = _scores(qs, k_ref[s_len:s_len + n_ctx, :])
    sink = sink_ref[...]
    m = jnp.maximum(jnp.maximum(jnp.max(s_loc, axis=-1, keepdims=True), jnp.max(s_ctx, axis=-1, keepdims=True)), sink)
    p_loc = jnp.exp(s_loc - m)
    p_ctx = jnp.exp(s_ctx - m)
    denom = jnp.sum(p_loc, axis=-1, keepdims=True) + jnp.sum(p_ctx, axis=-1, keepdims=True) + jnp.exp(sink - m)
    acc = (jnp.dot(p_loc.astype(BF16), v_ref[pl.ds(ws, span), :], preferred_element_type=F32)
           + jnp.dot(p_ctx.astype(BF16), v_ref[s_len:s_len + n_ctx, :], preferred_element_type=F32))
    _merge_pairs(acc / denom, tq, n_qblocks, o_ref)


def _win_attn_call(p, sink_col, *, s_len, n_ctx, q_width, k_blk, v_blk):
    b, n, _ = p.shape
    tq = TQ_WIN
    m_rows = sink_col.shape[0]
    return pl.pallas_call(
        functools.partial(_win_attn_kernel, s_len=s_len, n_ctx=n_ctx),
        grid=(b, s_len // tq),
        in_specs=[
            pl.BlockSpec((None, tq, q_width), lambda bb, i: (bb, i, 0)),
            pl.BlockSpec((None, n, LANES), lambda bb, i: (bb, 0, k_blk)),
            pl.BlockSpec((None, n, LANES), lambda bb, i: (bb, 0, v_blk)),
            pl.BlockSpec((m_rows, 1), lambda bb, i: (0, 0)),
        ],
        out_specs=pl.BlockSpec((None, tq, q_width), lambda bb, i: (bb, i, 0)),
        out_shape=jax.ShapeDtypeStruct((b, s_len, q_width), BF16),
        compiler_params=_cparams(2),
        name="win_attn",
    )(p, p, p, sink_col)


def _na_attn_kernel(q_ref, k_ref, v_ref, bias_ref, o_ref, *, s_len, n_ctx):
    tq = q_ref.shape[0]
    slab = NA_SLAB * GRID_W
    rows = s_len // GRID_W
    mi = pl.program_id(2)
    ss = pl.multiple_of(jnp.clip(NA_QROWS * mi - NA_QROWS, 0, rows - NA_SLAB) * GRID_W, GRID_W)
    qs = _stack_q(q_ref, 1)
    s_nb = _scores(qs, k_ref[pl.ds(ss, slab), :]) + bias_ref[...]
    s_cx = _scores(qs, k_ref[s_len:s_len + n_ctx, :])
    m = jnp.maximum(jnp.max(s_nb, axis=-1, keepdims=True), jnp.max(s_cx, axis=-1, keepdims=True))
    p_nb = jnp.exp(s_nb - m)
    p_cx = jnp.exp(s_cx - m)
    denom = jnp.sum(p_nb, axis=-1, keepdims=True) + jnp.sum(p_cx, axis=-1, keepdims=True)
    acc = (jnp.dot(p_nb.astype(BF16), v_ref[pl.ds(ss, slab), :], preferred_element_type=F32)
           + jnp.dot(p_cx.astype(BF16), v_ref[s_len:s_len + n_ctx, :], preferred_element_type=F32))
    _merge_pairs(acc / denom, tq, 1, o_ref)


def _na_attn_call(p, bias_tab, *, s_len, n_ctx, n_pairs, q_blk0, k_blk0, v_blk0):
    b, n, _ = p.shape
    tq = NA_QROWS * GRID_W
    n_steps = s_len // tq
    tbl = lambda mi: jnp.where(mi == 0, 0, jnp.where(mi == n_steps - 1, 2, 1))
    return pl.pallas_call(
        functools.partial(_na_attn_kernel, s_len=s_len, n_ctx=n_ctx),
        grid=(b, n_pairs, n_steps),
        in_specs=[
            pl.BlockSpec((None, tq, LANES), lambda bb, j, mi: (bb, mi, q_blk0 + j)),
            pl.BlockSpec((None, n, LANES), lambda bb, j, mi: (bb, 0, k_blk0 + j)),
            pl.BlockSpec((None, n, LANES), lambda bb, j, mi: (bb, 0, v_blk0 + j)),
            pl.BlockSpec((None, None, 2 * tq, NA_SLAB * GRID_W), lambda bb, j, mi: (tbl(mi), j, 0, 0)),
        ],
        out_specs=pl.BlockSpec((None, tq, LANES), lambda bb, j, mi: (bb, mi, j)),
        out_shape=jax.ShapeDtypeStruct((b, s_len, n_pairs * LANES), BF16),
        compiler_params=_cparams(3),
        name="na_attn",
    )(p, p, p, bias_tab)


def _na_bias_tables(rpb, rows):
    n_heads = rpb.shape[0]
    qr = np.arange(NA_QROWS)[:, None, None, None]
    qc = np.arange(GRID_W)[None, :, None, None]
    kr = np.arange(NA_SLAB)[None, None, :, None]
    kc = np.arange(GRID_W)[None, None, None, :]
    full = (NA_QROWS, GRID_W, NA_SLAB, GRID_W)
    flat = (NA_QROWS * GRID_W, NA_SLAB * GRID_W)
    tabs = []
    for q0, s0 in ((0, 0), (NA_QROWS, 0), (rows - NA_QROWS, rows - NA_SLAB)):
        q_row = q0 + qr
        k_row = s0 + kr
        r0 = np.clip(q_row - NA_ROWS_MAX // 2, 0, rows - NA_ROWS_MAX)
        c0 = np.clip(qc - NA_COLS // 2, 0, GRID_W - NA_COLS)
        valid = (k_row >= r0) & (k_row < r0 + NA_ROWS_MAX) & (kc >= c0) & (kc < c0 + NA_COLS)
        valid = np.broadcast_to(valid, full).reshape(flat)
        r_off = np.broadcast_to(np.clip(k_row - q_row + NA_ROWS_MAX - 1, 0, 2 * NA_ROWS_MAX - 2), full).reshape(flat)
        c_off = np.broadcast_to(np.clip(kc - qc + NA_COLS - 1, 0, 2 * NA_COLS - 2), full).reshape(flat)
        tabs.append(jnp.where(valid[None], rpb[:, r_off, c_off].astype(F32), NEG_INF))
    tab = jnp.stack(tabs)
    return tab.reshape(3, n_heads // 2, 2 * flat[0], flat[1])


def _layer_norm(y, g, b):
    mu = jnp.mean(y, axis=-1, keepdims=True)
    yc = y - mu
    var = jnp.mean(yc * yc, axis=-1, keepdims=True)
    return yc * lax.rsqrt(var + LN_EPS) * g + b


def _route(logits):
    lane = _lane_iota().astype(F32)
    big = float(LANES)
    is_g = lane < N_GROUPS
    gl = jnp.where(is_g, logits, NEG_INF)
    g_max = jnp.max(gl, axis=-1, keepdims=True)
    g_idx = jnp.min(jnp.where(gl == g_max, lane, big), axis=-1, keepdims=True)
    g_w = 1.0 / jnp.sum(jnp.where(is_g, jnp.exp(gl - g_max), 0.0), axis=-1, keepdims=True)
    base = N_GROUPS + EXPERTS_PER_GROUP * g_idx
    el = jnp.where((lane >= base) & (lane < base + EXPERTS_PER_GROUP), logits, NEG_INF)
    v1 = jnp.max(el, axis=-1, keepdims=True)
    i1 = jnp.min(jnp.where(el == v1, lane, big), axis=-1, keepdims=True)
    el2 = jnp.where(lane == i1, NEG_INF, el)
    v2 = jnp.max(el2, axis=-1, keepdims=True)
    i2 = jnp.min(jnp.where(el2 == v2, lane, big), axis=-1, keepdims=True)
    t = jnp.exp(v2 - v1)
    w1 = g_w / (1.0 + t)
    w2 = g_w * t / (1.0 + t)
    return jnp.where(lane == 0, i1 - N_GROUPS,
                     jnp.where(lane == 1, i2 - N_GROUPS, jnp.where(lane == 2, w1, jnp.where(lane == 3, w2, 0.0))))


def _outproj_kernel(oa_ref, ob_ref, w_ref, h_ref, gate_ref, lng_ref, lnb_ref, sh_ref, sc_ref, wr_ref, br_ref,
                    h1_ref, f_ref, route_ref, *, alpha):
    ka = oa_ref.shape[1]
    o = (jnp.dot(oa_ref[...], w_ref[:ka, :], preferred_element_type=F32)
         + jnp.dot(ob_ref[...], w_ref[ka:, :], preferred_element_type=F32))
    h1 = _layer_norm(alpha * h_ref[...] + gate_ref[...] * o, lng_ref[...], lnb_ref[...])
    h1_ref[...] = h1
    f = h1 * sc_ref[...] + sh_ref[...]
    f_ref[...] = f
    logits = jnp.dot(f, wr_ref[...], precision=HIGHEST, preferred_element_type=F32) + br_ref[...]
    route_ref[...] = _route(logits)


def _outproj_call(oa, ob, w16, h, gate, ln_g, ln_b, shift, scale1p, w_route, b_route, *, n_rows, s_len, alpha):
    b, _, d = h.shape
    ka, kb = oa.shape[2], ob.shape[2]
    n_lat = s_len // TM
    seg = lambda bb, j: (bb, jnp.where(j >= n_lat, 1, 0), 0, 0)
    row = lambda bb, j: (bb, j, 0)
    const = lambda bb, j: (0, 0)
    return pl.pallas_call(
        functools.partial(_outproj_kernel, alpha=alpha),
        grid=(b, n_rows // TM),
        in_specs=[
            pl.BlockSpec((None, TM, ka), row),
            pl.BlockSpec((None, TM, kb), row),
            pl.BlockSpec((ka + kb, d), const),
            pl.BlockSpec((None, TM, d), row),
            pl.BlockSpec((None, None, 1, d), seg),
            pl.BlockSpec((1, d), const),
            pl.BlockSpec((1, d), const),
            pl.BlockSpec((None, None, 1, d), seg),
            pl.BlockSpec((None, None, 1, d), seg),
            pl.BlockSpec((d, LANES), const),
            pl.BlockSpec((1, LANES), const),
        ],
        out_specs=[
            pl.BlockSpec((None, TM, d), row),
            pl.BlockSpec((None, TM, d), row),
            pl.BlockSpec((None, TM, LANES), row),
        ],
        out_shape=[
            jax.ShapeDtypeStruct((b, n_rows, d), F32),
            jax.ShapeDtypeStruct((b, n_rows, d), F32),
            jax.ShapeDtypeStruct((b, n_rows, LANES), F32),
        ],
        compiler_params=_cparams(2),
        name="outproj",
    )(oa, ob, w16, h, gate, ln_g, ln_b, shift, scale1p, w_route, b_route)


def _row_copy(src, src_row, dst, dst_row, sem):
    return pltpu.make_async_copy(src.at[pl.ds(src_row, 1), :], dst.at[pl.ds(dst_row, 1), :], sem)


def _moe_kernel(blk_e_ref, nvalid_ref, tok_ref, dst_ref, f_hbm, w1_ref, w3_ref, w2_ref, y_hbm,
                xbuf, ybuf, gsem, ssem):
    del blk_e_ref
    i = pl.program_id(0)
    nv = nvalid_ref[i]

    @pl.when(i == 0)
    def _():
        xbuf[...] = jnp.zeros(xbuf.shape, xbuf.dtype)

    @pl.when(nv > 0)
    def _():
        def gather_start(r, carry):
            _row_copy(f_hbm, tok_ref[0, r], xbuf, r, gsem).start()
            return carry
        lax.fori_loop(0, nv, gather_start, 0)

        def gather_wait(r, carry):
            _row_copy(f_hbm, tok_ref[0, r], xbuf, r, gsem).wait()
            return carry
        lax.fori_loop(0, nv, gather_wait, 0)

        x = xbuf[...].astype(BF16)
        h1 = jnp.dot(x, w1_ref[...].astype(BF16), preferred_element_type=F32)
        h3 = jnp.dot(x, w3_ref[...].astype(BF16), preferred_element_type=F32)
        act = (h1 / (1.0 + jnp.exp(-h1))) * h3
        ybuf[...] = jnp.dot(act.astype(BF16), w2_ref[...].astype(BF16), preferred_element_type=F32)

        def scatter_start(r, carry):
            _row_copy(ybuf, r, y_hbm, dst_ref[0, r], ssem).start()
            return carry
        lax.fori_loop(0, nv, scatter_start, 0)

        def scatter_wait(r, carry):
            _row_copy(ybuf, r, y_hbm, dst_ref[0, r], ssem).wait()
            return carry
        lax.fori_loop(0, nv, scatter_wait, 0)


def _moe_call(f_rows, blk_e, nvalid, slot_tok, slot_dst, w1, w3, w2):
    t, d = f_rows.shape
    n_blocks = blk_e.shape[0]
    de = w1.shape[2]
    idx_spec = pl.BlockSpec((None, 1, MOE_BLOCK), lambda i, be, nv: (i, 0, 0), memory_space=pltpu.SMEM)
    grid_spec = pltpu.PrefetchScalarGridSpec(
        num_scalar_prefetch=2,
        grid=(n_blocks,),
        in_specs=[
            idx_spec,
            idx_spec,
            pl.BlockSpec(memory_space=pl.ANY),
            pl.BlockSpec((None, d, de), lambda i, be, nv: (be[i], 0, 0)),
            pl.BlockSpec((None, d, de), lambda i, be, nv: (be[i], 0, 0)),
            pl.BlockSpec((None, de, d), lambda i, be, nv: (be[i], 0, 0)),
        ],
        out_specs=pl.BlockSpec(memory_space=pl.ANY),
        scratch_shapes=[
            pltpu.VMEM((MOE_BLOCK, d), F32),
            pltpu.VMEM((MOE_BLOCK, d), F32),
            pltpu.SemaphoreType.DMA(()),
            pltpu.SemaphoreType.DMA(()),
        ],
    )
    return pl.pallas_call(
        _moe_kernel,
        grid_spec=grid_spec,
        out_shape=jax.ShapeDtypeStruct((TOP_K * t, d), F32),
        compiler_params=_cparams(1),
        name="moe_experts",
    )(blk_e, nvalid, slot_tok.reshape(n_blocks, 1, MOE_BLOCK), slot_dst.reshape(n_blocks, 1, MOE_BLOCK),
      f_rows, w1, w3, w2)


def _moe_plan(route, t):
    n_assign = TOP_K * t
    e_flat = route[:, :TOP_K].astype(jnp.int32).reshape(n_assign)
    order = jnp.argsort(e_flat, stable=True).astype(jnp.int32)
    counts = jnp.sum((e_flat[:, None] == jnp.arange(N_EXPERTS, dtype=jnp.int32)[None, :]).astype(jnp.int32), axis=0)
    starts = jnp.cumsum(counts) - counts
    padded = (counts + MOE_BLOCK - 1) // MOE_BLOCK * MOE_BLOCK
    p_ends = jnp.cumsum(padded)
    p_starts = p_ends - padded
    n_blocks = (n_assign + MOE_BLOCK - 1) // MOE_BLOCK + N_EXPERTS
    blk_e = jnp.minimum(jnp.searchsorted(p_ends, jnp.arange(n_blocks, dtype=jnp.int32) * MOE_BLOCK, side='right'),
                        N_EXPERTS - 1).astype(jnp.int32)
    slot = jnp.arange(n_blocks * MOE_BLOCK, dtype=jnp.int32)
    e_s = jnp.repeat(blk_e, MOE_BLOCK)
    j = slot - p_starts[e_s]
    valid = j < counts[e_s]
    a_idx = order[jnp.clip(starts[e_s] + j, 0, n_assign - 1)]
    slot_tok = jnp.where(valid, a_idx // TOP_K, 0).astype(jnp.int32)
    slot_dst = jnp.where(valid, (a_idx % TOP_K) * t + a_idx // TOP_K, 0).astype(jnp.int32)
    nvalid = jnp.sum(valid.reshape(n_blocks, MOE_BLOCK).astype(jnp.int32), axis=1)
    return blk_e, nvalid, slot_tok, slot_dst


def _combine_kernel(h_ref, y0_ref, y1_ref, route_ref, gate_ref, lng_ref, lnb_ref, o_ref, *, alpha):
    r = route_ref[...]
    y = r[:, 2:3] * y0_ref[...] + r[:, 3:4] * y1_ref[...]
    o_ref[...] = _layer_norm(alpha * h_ref[...] + gate_ref[...] * y, lng_ref[...], lnb_ref[...])


def _combine_call(h1, y, route, gate, ln_g, ln_b, *, s_len, alpha):
    b, n_rows, d = h1.shape
    n_lat = s_len // TM
    seg = lambda bb, j: (bb, jnp.where(j >= n_lat, 1, 0), 0, 0)
    row = lambda bb, j: (bb, j, 0)
    const = lambda bb, j: (0, 0)
    y4 = y.reshape(TOP_K, b, n_rows, d)
    return pl.pallas_call(
        functools.partial(_combine_kernel, alpha=alpha),
        grid=(b, n_rows // TM),
        in_specs=[
            pl.BlockSpec((None, TM, d), row),
            pl.BlockSpec((None, None, TM, d), lambda bb, j: (0, bb, j, 0)),
            pl.BlockSpec((None, None, TM, d), lambda bb, j: (1, bb, j, 0)),
            pl.BlockSpec((None, TM, LANES), row),
            pl.BlockSpec((None, None, 1, d), seg),
            pl.BlockSpec((1, d), const),
            pl.BlockSpec((1, d), const),
        ],
        out_specs=pl.BlockSpec((None, TM, d), row),
        out_shape=jax.ShapeDtypeStruct((b, n_rows, d), F32),
        compiler_params=_cparams(2),
        name="combine",
    )(h1, y4, y4, route, gate, ln_g, ln_b)


def _pair_heads(w, axis):
    w = jnp.moveaxis(w, axis, -1)
    lead = w.shape[:-1]
    n_heads = w.shape[-1] // HEAD_DIM
    w = w.reshape(lead + (2, n_heads // 2, HEAD_DIM)).swapaxes(-3, -2).reshape(lead + (n_heads * HEAD_DIM,))
    return jnp.moveaxis(w, -1, axis)


def _rope_tables(s_len, n_ctx):
    t = jnp.arange(s_len, dtype=jnp.int32)
    row = (t // GRID_W).astype(F32)
    col = (t % GRID_W).astype(F32)
    n_freq = HEAD_DIM // 4
    inv = ROPE_THETA ** (-jnp.arange(n_freq, dtype=F32) / n_freq)
    ang = jnp.concatenate([row[:, None] * inv, col[:, None] * inv], -1)
    cos = jnp.concatenate([jnp.cos(ang), jnp.ones((n_ctx, HEAD_DIM // 2), F32)], 0)
    sin = jnp.concatenate([jnp.sin(ang), jnp.zeros((n_ctx, HEAD_DIM // 2), F32)], 0)
    return jnp.tile(cos, (1, 4)), jnp.concatenate([-sin, sin, -sin, sin], -1)


def _lambda_init(layer_idx):
    return 0.8 - 0.6 * float(np.exp(-0.3 * layer_idx))


def _router_params(w_group, b_group, w_router, b_router):
    d = w_group.shape[0]
    pad = LANES - N_GROUPS - N_EXPERTS
    w = jnp.concatenate([w_group, w_router, jnp.zeros((d, pad), F32)], axis=1)
    bb = jnp.concatenate([b_group, b_router, jnp.zeros((pad,), F32)])
    return w, bb.reshape(1, LANES)


def _moe_layer(f, route, w1, w3, w2):
    b, n_rows, d = f.shape
    t = b * n_rows
    blk_e, nvalid, slot_tok, slot_dst = _moe_plan(route.reshape(t, LANES), t)
    return _moe_call(f.reshape(t, d), blk_e, nvalid, slot_tok, slot_dst, w1, w3, w2)


def kernel(x, c, ctx, c_ctx, mod_w, mod_b, ln_g, ln_b, ab_w_in, ab_w_out, diff_lambda, diff_subln_g, gqa_qk_g,
           cd_w_in, cd_w_out, win_sink, na_rpb, moe_w_group, moe_b_group, moe_w_router, moe_b_router,
           moe_w1, moe_w3, moe_w2):
    b, s, d = x.shape
    n_ctx = ctx.shape[1]
    n = s + n_ctx
    depth = mod_w.shape[0]
    rows = s // GRID_W
    assert depth == 2, "layer pattern implemented: one differential/GQA layer, then one window/neighbourhood layer"
    assert s % TM == 0 and n_ctx % TM == 0 and s % TK == 0 and b + 1 <= 8
    assert n_ctx % TQ_DIFF == 0 and n_ctx % TQ_GQA == 0
    assert rows >= NA_SLAB and rows % NA_QROWS == 0 and s % GRID_W == 0
    alpha = (2.0 * depth) ** 0.25
    qw = d // 2
    kvw = qw // 4
    assert qw == 4 * LANES and kvw == LANES

    cos128, sin128 = _rope_tables(s, n_ctx)
    cvec = jnp.concatenate([c, c_ctx[None, :], jnp.zeros((8 - b - 1, d), F32)], axis=0)
    mods_all = _mods_call(cvec, mod_w, mod_b)

    def seg_mods(i):
        m = mods_all[i]
        lat = m[:b].reshape(b, 6, d)
        cm = jnp.broadcast_to(m[b].reshape(1, 6, d), (b, 6, d))
        ms = jnp.stack([lat, cm], axis=1)
        pick = lambda k, one: (ms[:, :, k] + one)[:, :, None, :]
        return pick(0, 0.0), pick(1, 1.0), pick(2, 0.0), pick(3, 0.0), pick(4, 1.0), pick(5, 0.0)

    h = jnp.concatenate([x, ctx], axis=1)

    shift1, scale1p, gate1, shift2, scale2p, gate2 = seg_mods(0)
    w_in = ab_w_in[0]
    w_in = jnp.concatenate([w_in[:, :qw], _pair_heads(w_in[:, qw:2 * qw], 1), w_in[:, 2 * qw:]], axis=1).astype(BF16)
    roles0 = ((True, None, True),) * 4 + ((True, 0, True),) * 4 + ((True, None, False),) * 4 \
        + ((False, None, False),) * 4 + ((True, 1, False),) + ((False, None, False),)
    g128 = jnp.tile(gqa_qk_g[0], (1, 2))
    p0 = _inproj_call(h, shift1, scale1p, w_in, cos128, sin128, g128, roles0, s)
    o_d = _diff_attn_call(p0, diff_lambda[0], diff_subln_g[0], s_len=s, n_ctx=n_ctx, n_heads=4,
                          q_blk0=0, k_blk0=8, v_blk0=12, lam_init=_lambda_init(0))
    o_g = _gqa_attn_call(p0, s_len=s, n_ctx=n_ctx, q_width=qw, q_col0=qw, k_blk=16, v_blk=17)
    w_out = jnp.concatenate([ab_w_out[0][:qw], _pair_heads(ab_w_out[0][qw:], 0)], axis=0).astype(BF16)
    w_route, b_route = _router_params(moe_w_group[0], moe_b_group[0], moe_w_router[0], moe_b_router[0])
    h1, f, route = _outproj_call(o_d, o_g, w_out, h, gate1, ln_g[0, 0:1], ln_b[0, 0:1], shift2, scale2p,
                                 w_route, b_route, n_rows=n, s_len=s, alpha=alpha)
    y = _moe_layer(f, route, moe_w1[0], moe_w3[0], moe_w2[0])
    h = _combine_call(h1, y, route, gate2, ln_g[0, 1:2], ln_b[0, 1:2], s_len=s, alpha=alpha)

    shift1, scale1p, gate1, shift2, scale2p, gate2 = seg_mods(1)
    w_in = cd_w_in[0]
    w_in = jnp.concatenate([_pair_heads(w_in[:, :qw], 1), w_in[:, qw:]], axis=1).astype(BF16)
    roles1 = ((True, None, True),) * 4 + ((False, None, True),) * 4 + ((True, None, False),) \
        + ((False, None, False),) * 9
    p1 = _inproj_call(h, shift1, scale1p, w_in, cos128, sin128, jnp.ones((2, LANES), F32), roles1, s)
    sink_col = jnp.repeat(win_sink[0].reshape(2, 4).T.reshape(8), TQ_WIN).reshape(8 * TQ_WIN, 1)
    o_w = _win_attn_call(p1, sink_col, s_len=s, n_ctx=n_ctx, q_width=qw, k_blk=8, v_blk=9)
    o_n = _na_attn_call(p1, _na_bias_tables(na_rpb[0], rows), s_len=s, n_ctx=n_ctx, n_pairs=4,
                        q_blk0=4, k_blk0=10, v_blk0=14)
    w_out = jnp.concatenate([_pair_heads(cd_w_out[0][:qw], 0), cd_w_out[0][qw:]], axis=0).astype(BF16)
    w_route, b_route = _router_params(moe_w_group[1], moe_b_group[1], moe_w_router[1], moe_b_router[1])
    h1, f, route = _outproj_call(o_w, o_n, w_out, h, gate1, ln_g[1, 0:1], ln_b[1, 0:1], shift2, scale2p,
                                 w_route, b_route, n_rows=s, s_len=s, alpha=alpha)
    y = _moe_layer(f, route, moe_w1[1], moe_w3[1], moe_w2[1])
    return _combine_call(h1, y, route, gate2, ln_g[1, 1:2], ln_b[1, 1:2], s_len=s, alpha=alpha)
```

```python
import functools

import numpy as np
import jax
import jax.numpy as jnp
from jax import lax
from jax.experimental import pallas as pl
from jax.experimental.pallas import tpu as pltpu

F32 = jnp.float32
BF16 = jnp.bfloat16
HIGHEST = lax.Precision.HIGHEST

GRID_W = 64
HEAD_DIM = 64
ROPE_THETA = 10000.0
WINDOW = 128
NA_ROWS_MAX = 8
NA_COLS = 16
N_GROUPS = 4
EXPERTS_PER_GROUP = 8
N_EXPERTS = N_GROUPS * EXPERTS_PER_GROUP
TOP_K = 2
MOE_BLOCK = 128
LN_EPS = 1e-5
RMS_EPS = 1e-6
ATTN_SCALE = HEAD_DIM ** -0.5
NEG_INF = -1e30

LANES = 128
PAIR = 2 * HEAD_DIM
assert PAIR == LANES

VMEM_LIMIT = 48 * 1024 * 1024

TM = 256
TQ_DIFF = 256
TQ_GQA = 128
TK = 512
TQ_WIN = 128
NA_QROWS = 4
NA_SLAB = 12


def _cparams(n_axes):
    return pltpu.CompilerParams(dimension_semantics=("arbitrary",) * n_axes, vmem_limit_bytes=VMEM_LIMIT)


def _lane_iota():
    return lax.broadcasted_iota(jnp.int32, (1, LANES), 1)


def _mods_kernel(c_ref, w_ref, b_ref, o_ref):
    c = c_ref[...]
    sc = c / (1.0 + jnp.exp(-c))
    o_ref[...] = jnp.dot(sc, w_ref[...], precision=HIGHEST, preferred_element_type=F32) + b_ref[...]


def _mods_call(cvec, mod_w, mod_b):
    depth, d, n6 = mod_w.shape
    tn = n6 // 4
    return pl.pallas_call(
        _mods_kernel,
        grid=(depth, n6 // tn),
        in_specs=[
            pl.BlockSpec((8, d), lambda l, j: (0, 0)),
            pl.BlockSpec((None, d, tn), lambda l, j: (l, 0, j)),
            pl.BlockSpec((None, 1, tn), lambda l, j: (l, 0, j)),
        ],
        out_specs=pl.BlockSpec((None, 8, tn), lambda l, j: (l, 0, j)),
        out_shape=jax.ShapeDtypeStruct((depth, 8, n6), F32),
        compiler_params=_cparams(2),
        name="mods",
    )(cvec, mod_w, mod_b.reshape(depth, 1, n6))


def _inproj_kernel(h_ref, sh_ref, sc_ref, w_ref, wvt_ref, cos_ref, sin_ref, g_ref, o_ref, vt_ref, *, roles):
    a = (h_ref[...] * sc_ref[...] + sh_ref[...]).astype(BF16)
    vt = lax.dot_general(wvt_ref[...], a, (((1,), (1,)), ((), ())), preferred_element_type=F32)
    for j in range(vt_ref.shape[0]):
        vt_ref[j] = vt[j * LANES:(j + 1) * LANES, :].astype(BF16)
    lane = _lane_iota()
    rope_first = (lane & (HEAD_DIM - 1)) < (HEAD_DIM // 2)
    lo = lane < HEAD_DIM
    n_blocks = len(roles)
    for c0 in range(0, n_blocks, 2):
        width = min(2, n_blocks - c0)
        p = jnp.dot(a, w_ref[:, c0 * LANES:(c0 + width) * LANES], preferred_element_type=F32)
        for half in range(width):
            cb = c0 + half
            rope, norm, scale = roles[cb]
            blk = p[:, half * LANES:(half + 1) * LANES]
            if norm is not None:
                sq = blk * blk
                s_lo = jnp.sum(jnp.where(lo, sq, 0.0), axis=-1, keepdims=True)
                s_hi = jnp.sum(jnp.where(lo, 0.0, sq), axis=-1, keepdims=True)
                ms = jnp.where(lo, s_lo, s_hi) * (1.0 / HEAD_DIM)
                blk = blk * lax.rsqrt(ms + RMS_EPS) * g_ref[norm:norm + 1, :]
            if rope:
                partner = jnp.where(rope_first, pltpu.roll(blk, LANES - HEAD_DIM // 2, 1),
                                    pltpu.roll(blk, HEAD_DIM // 2, 1))
                blk = blk * cos_ref[...] + partner * sin_ref[...]
            if scale:
                blk = blk * ATTN_SCALE
            o_ref[:, cb * LANES:(cb + 1) * LANES] = blk.astype(BF16)


def _inproj_call(h, shift, scale1p, w16, wvt16, cos128, sin128, g128, roles, s_len):
    b, n, d = h.shape
    ncol = w16.shape[1]
    nvb = wvt16.shape[0] // LANES
    n_lat = s_len // TM
    seg = lambda bb, j: (bb, jnp.where(j >= n_lat, 1, 0), 0, 0)
    return pl.pallas_call(
        functools.partial(_inproj_kernel, roles=roles),
        grid=(b, n // TM),
        in_specs=[
            pl.BlockSpec((None, TM, d), lambda bb, j: (bb, j, 0)),
            pl.BlockSpec((None, None, 1, d), seg),
            pl.BlockSpec((None, None, 1, d), seg),
            pl.BlockSpec((d, ncol), lambda bb, j: (0, 0)),
            pl.BlockSpec((nvb * LANES, d), lambda bb, j: (0, 0)),
            pl.BlockSpec((TM, LANES), lambda bb, j: (j, 0)),
            pl.BlockSpec((TM, LANES), lambda bb, j: (j, 0)),
            pl.BlockSpec((2, LANES), lambda bb, j: (0, 0)),
        ],
        out_specs=[
            pl.BlockSpec((None, TM, ncol), lambda bb, j: (bb, j, 0)),
            pl.BlockSpec((None, nvb, LANES, TM), lambda bb, j: (bb, 0, 0, j)),
        ],
        out_shape=[
            jax.ShapeDtypeStruct((b, n, ncol), BF16),
            jax.ShapeDtypeStruct((b, nvb, LANES, n), BF16),
        ],
        compiler_params=_cparams(2),
        name="inproj",
    )(h, shift, scale1p, w16, wvt16, cos128, sin128, g128)


def _stack_q(q_ref, n_qblocks):
    lo = _lane_iota() < HEAD_DIM
    parts = []
    for j in range(n_qblocks):
        q = q_ref[:, j * LANES:(j + 1) * LANES]
        zero = jnp.zeros_like(q)
        parts.append(jnp.where(lo, q, zero))
        parts.append(jnp.where(lo, zero, q))
    return jnp.concatenate(parts, axis=0)


def _scores_t(k, qs):
    return lax.dot_general(k, qs, (((1,), (1,)), ((), ())), preferred_element_type=F32)


def _flash_init(m_ref, l_ref, acc_ref):
    m_ref[...] = jnp.full(m_ref.shape, NEG_INF, F32)
    l_ref[...] = jnp.zeros(l_ref.shape, F32)
    acc_ref[...] = jnp.zeros(acc_ref.shape, F32)


def _flash_step(qs, k, vt, m_ref, l_ref, acc_ref):
    st = _scores_t(k, qs)
    m_prev = m_ref[...]
    m_new = jnp.maximum(m_prev, jnp.max(st, axis=0, keepdims=True))
    alpha = jnp.exp(m_prev - m_new)
    pt = jnp.exp(st - m_new)
    l_ref[...] = alpha * l_ref[...] + jnp.sum(pt, axis=0, keepdims=True)
    acc_ref[...] = alpha * acc_ref[...] + jnp.dot(vt, pt.astype(BF16), preferred_element_type=F32)
    m_ref[...] = m_new


def _flash_sweep(qs, k_ref, vt_ref, m_ref, l_ref, acc_ref, *, is_lat, s_len, n_ctx):
    _flash_init(m_ref, l_ref, acc_ref)

    @pl.when(is_lat)
    def _():
        def body(c, carry):
            off = pl.multiple_of(c * TK, TK)
            _flash_step(qs, k_ref[pl.ds(off, TK), :], vt_ref[:, pl.ds(off, TK)], m_ref, l_ref, acc_ref)
            return carry
        lax.fori_loop(0, s_len // TK, body, 0)

    _flash_step(qs, k_ref[s_len:s_len + n_ctx, :], vt_ref[:, s_len:s_len + n_ctx], m_ref, l_ref, acc_ref)


def _merge_pairs(ot, tq, n_qblocks, o_ref):
    for j in range(n_qblocks):
        o_lo = ot[:HEAD_DIM, (2 * j) * tq:(2 * j + 1) * tq]
        o_hi = ot[HEAD_DIM:, (2 * j + 1) * tq:(2 * j + 2) * tq]
        blk = jnp.concatenate([o_lo, o_hi], axis=0)
        o_ref[:, j * LANES:(j + 1) * LANES] = blk.T.astype(o_ref.dtype)


def _diff_attn_kernel(q_ref, k_ref, vt_ref, lam_ref, g_ref, o_ref, m_ref, l_ref, acc_ref, *, s_len, n_ctx, lam_init):
    tq = q_ref.shape[0]
    i = pl.program_id(2)
    qs = _stack_q(q_ref, 1)
    _flash_sweep(qs, k_ref, vt_ref, m_ref, l_ref, acc_ref, is_lat=i < s_len // tq, s_len=s_len, n_ctx=n_ctx)
    lv = lam_ref[...]
    lam = (jnp.exp(jnp.sum(lv[0:1] * lv[1:2], axis=-1, keepdims=True))
           - jnp.exp(jnp.sum(lv[2:3] * lv[3:4], axis=-1, keepdims=True)) + lam_init)
    ot = acc_ref[...] / l_ref[...]
    od = ot[:, :tq] - lam * ot[:, tq:]
    ms = jnp.mean(od * od, axis=0, keepdims=True)
    od = od * lax.rsqrt(ms + RMS_EPS) * g_ref[...] * (1.0 - lam_init)
    o_ref[...] = od.T.astype(o_ref.dtype)


def _diff_attn_call(p, vt, lam_vecs, subln_g, *, s_len, n_ctx, n_heads, q_blk0, k_blk0, v_blk0, lam_init):
    b, n, _ = p.shape
    tq = TQ_DIFF
    return pl.pallas_call(
        functools.partial(_diff_attn_kernel, s_len=s_len, n_ctx=n_ctx, lam_init=lam_init),
        grid=(b, n_heads, n // tq),
        in_specs=[
            pl.BlockSpec((None, tq, LANES), lambda bb, hh, i: (bb, i, q_blk0 + hh)),
            pl.BlockSpec((None, n, LANES), lambda bb, hh, i: (bb, 0, k_blk0 + hh)),
            pl.BlockSpec((None, None, LANES, n), lambda bb, hh, i: (bb, v_blk0 + hh, 0, 0)),
            pl.BlockSpec((4, HEAD_DIM), lambda bb, hh, i: (0, 0)),
            pl.BlockSpec((LANES, 1), lambda bb, hh, i: (0, 0)),
        ],
        out_specs=pl.BlockSpec((None, tq, LANES), lambda bb, hh, i: (bb, i, hh)),
        out_shape=jax.ShapeDtypeStruct((b, n, n_heads * LANES), BF16),
        scratch_shapes=[pltpu.VMEM((1, 2 * tq), F32), pltpu.VMEM((1, 2 * tq), F32), pltpu.VMEM((LANES, 2 * tq), F32)],
        compiler_params=_cparams(3),
        name="diff_attn",
    )(p, p, vt, lam_vecs, subln_g.reshape(LANES, 1))


def _gqa_attn_kernel(q_ref, k_ref, vt_ref, o_ref, m_ref, l_ref, acc_ref, *, s_len, n_ctx):
    tq = q_ref.shape[0]
    n_qblocks = q_ref.shape[1] // LANES
    i = pl.program_id(1)
    qs = _stack_q(q_ref, n_qblocks)
    _flash_sweep(qs, k_ref, vt_ref, m_ref, l_ref, acc_ref, is_lat=i < s_len // tq, s_len=s_len, n_ctx=n_ctx)
    _merge_pairs(acc_ref[...] / l_ref[...], tq, n_qblocks, o_ref)


def _gqa_attn_call(p, vt, *, s_len, n_ctx, q_width, q_col0, k_blk, v_blk):
    b, n, _ = p.shape
    tq = TQ_GQA
    m_rows = 2 * (q_width // LANES) * tq
    return pl.pallas_call(
        functools.partial(_gqa_attn_kernel, s_len=s_len, n_ctx=n_ctx),
        grid=(b, n // tq),
        in_specs=[
            pl.BlockSpec((None, tq, q_width), lambda bb, i: (bb, i, q_col0 // q_width)),
            pl.BlockSpec((None, n, LANES), lambda bb, i: (bb, 0, k_blk)),
            pl.BlockSpec((None, None, LANES, n), lambda bb, i: (bb, v_blk, 0, 0)),
        ],
        out_specs=pl.BlockSpec((None, tq, q_width), lambda bb, i: (bb, i, 0)),
        out_shape=jax.ShapeDtypeStruct((b, n, q_width), BF16),
        scratch_shapes=[pltpu.VMEM((1, m_rows), F32), pltpu.VMEM((1, m_rows), F32), pltpu.VMEM((LANES, m_rows), F32)],
        compiler_params=_cparams(2),
        name="gqa_attn",
    )(p, p, vt)


def _win_attn_kernel(q_ref, k_ref, vt_ref, sink_ref, o_ref, *, s_len, n_ctx):
    tq = q_ref.shape[0]
    n_qblocks = q_ref.shape[1] // LANES
    span = tq + 2 * WINDOW
    i = pl.program_id(1)
    ws = pl.multiple_of(jnp.clip(i * tq - WINDOW, 0, s_len - span), LANES)
    qs = _stack_q(q_ref, n_qblocks)
    m_rows = qs.shape[0]
    s_loc = _scores_t(k_ref[pl.ds(ws, span), :], qs)
    kpos = ws + lax.broadcasted_iota(jnp.int32, (span, m_rows), 0)
    qpos = i * tq + (lax.broadcasted_iota(jnp.int32, (span, m_rows), 1) & (tq - 1))
    s_loc = jnp.where(jnp.abs(qpos - kpos) <= WINDOW, s_loc, NEG_INF)
    s_ctx = _scores_t(k_ref[s_len:s_len + n_ctx, :], qs)
    sink = sink_ref[...]
    m = jnp.maximum(jnp.maximum(jnp.max(s_loc, axis=0, keepdims=True), jnp.max(s_ctx, axis=0, keepdims=True)), sink)
    p_loc = jnp.exp(s_loc - m)
    p_ctx = jnp.exp(s_ctx - m)
    denom = jnp.sum(p_loc, axis=0, keepdims=True) + jnp.sum(p_ctx, axis=0, keepdims=True) + jnp.exp(sink - m)
    acc = (jnp.dot(vt_ref[:, pl.ds(ws, span)], p_loc.astype(BF16), preferred_element_type=F32)
           + jnp.dot(vt_ref[:, s_len:s_len + n_ctx], p_ctx.astype(BF16), preferred_element_type=F32))
    _merge_pairs(acc / denom, tq, n_qblocks, o_ref)


def _win_attn_call(p, vt, sink_row, *, s_len, n_ctx, q_width, k_blk, v_blk):
    b, n, _ = p.shape
    tq = TQ_WIN
    m_rows = sink_row.shape[1]
    return pl.pallas_call(
        functools.partial(_win_attn_kernel, s_len=s_len, n_ctx=n_ctx),
        grid=(b, s_len // tq),
        in_specs=[
            pl.BlockSpec((None, tq, q_width), lambda bb, i: (bb, i, 0)),
            pl.BlockSpec((None, n, LANES), lambda bb, i: (bb, 0, k_blk)),
            pl.BlockSpec((None, None, LANES, n), lambda bb, i: (bb, v_blk, 0, 0)),
            pl.BlockSpec((1, m_rows), lambda bb, i: (0, 0)),
        ],
        out_specs=pl.BlockSpec((None, tq, q_width), lambda bb, i: (bb, i, 0)),
        out_shape=jax.ShapeDtypeStruct((b, s_len, q_width), BF16),
        compiler_params=_cparams(2),
        name="win_attn",
    )(p, p, vt, sink_row)


def _na_attn_kernel(q_ref, k_ref, vt_ref, bias_ref, o_ref, *, s_len, n_ctx):
    tq = q_ref.shape[0]
    slab = NA_SLAB * GRID_W
    rows = s_len // GRID_W
    mi = pl.program_id(2)
    ss = pl.multiple_of(jnp.clip(mi - 1, 0, (rows - NA_SLAB) // NA_QROWS) * (NA_QROWS * GRID_W), NA_QROWS * GRID_W)
    qs = _stack_q(q_ref, 1)
    s_nb = _scores_t(k_ref[pl.ds(ss, slab), :], qs) + bias_ref[...]
    s_cx = _scores_t(k_ref[s_len:s_len + n_ctx, :], qs)
    m = jnp.maximum(jnp.max(s_nb, axis=0, keepdims=True), jnp.max(s_cx, axis=0, keepdims=True))
    p_nb = jnp.exp(s_nb - m)
    p_cx = jnp.exp(s_cx - m)
    denom = jnp.sum(p_nb, axis=0, keepdims=True) + jnp.sum(p_cx, axis=0, keepdims=True)
    acc = (jnp.dot(vt_ref[:, pl.ds(ss, slab)], p_nb.astype(BF16), preferred_element_type=F32)
           + jnp.dot(vt_ref[:, s_len:s_len + n_ctx], p_cx.astype(BF16), preferred_element_type=F32))
    _merge_pairs(acc / denom, tq, 1, o_ref)


def _na_attn_call(p, vt, bias_tab, *, s_len, n_ctx, n_pairs, q_blk0, k_blk0, v_blk0):
    b, n, _ = p.shape
    tq = NA_QROWS * GRID_W
    n_steps = s_len // tq
    tbl = lambda mi: jnp.where(mi == 0, 0, jnp.where(mi == n_steps - 1, 2, 1))
    return pl.pallas_call(
        functools.partial(_na_attn_kernel, s_len=s_len, n_ctx=n_ctx),
        grid=(b, n_pairs, n_steps),
        in_specs=[
            pl.BlockSpec((None, tq, LANES), lambda bb, j, mi: (bb, mi, q_blk0 + j)),
            pl.BlockSpec((None, n, LANES), lambda bb, j, mi: (bb, 0, k_blk0 + j)),
            pl.BlockSpec((None, None, LANES, n), lambda bb, j, mi: (bb, v_blk0 + j, 0, 0)),
            pl.BlockSpec((None, None, NA_SLAB * GRID_W, 2 * tq), lambda bb, j, mi: (tbl(mi), j, 0, 0)),
        ],
        out_specs=pl.BlockSpec((None, tq, LANES), lambda bb, j, mi: (bb, mi, j)),
        out_shape=jax.ShapeDtypeStruct((b, s_len, n_pairs * LANES), BF16),
        compiler_params=_cparams(3),
        name="na_attn",
    )(p, p, vt, bias_tab)


def _na_bias_tables(rpb, rows):
    n_heads = rpb.shape[0]
    rpb = rpb.astype(F32)
    pad = GRID_W - NA_COLS
    rpb_p = jnp.pad(rpb, ((0, 0), (0, 0), (pad, pad)))
    col_tab = jnp.stack([rpb_p[:, :, pad + NA_COLS - 1 - qc: pad + NA_COLS - 1 - qc + GRID_W] for qc in range(GRID_W)],
                        axis=2)
    qr = np.arange(NA_QROWS)[:, None, None, None]
    qc = np.arange(GRID_W)[None, :, None, None]
    kr = np.arange(NA_SLAB)[None, None, :, None]
    kc = np.arange(GRID_W)[None, None, None, :]
    full = (NA_QROWS, GRID_W, NA_SLAB, GRID_W)
    flat = (NA_QROWS * GRID_W, NA_SLAB * GRID_W)
    tabs = []
    for q0, s0 in ((0, 0), (NA_QROWS, 0), (rows - NA_QROWS, rows - NA_SLAB)):
        q_row = q0 + qr
        k_row = s0 + kr
        r0 = np.clip(q_row - NA_ROWS_MAX // 2, 0, rows - NA_ROWS_MAX)
        c0 = np.clip(qc - NA_COLS // 2, 0, GRID_W - NA_COLS)
        valid = (k_row >= r0) & (k_row < r0 + NA_ROWS_MAX) & (kc >= c0) & (kc < c0 + NA_COLS)
        valid = np.broadcast_to(valid, full).reshape(flat)
        r_off = np.clip(k_row - q_row + NA_ROWS_MAX - 1, 0, 2 * NA_ROWS_MAX - 2)[:, 0, :, 0]
        bias = jnp.stack([jnp.stack([col_tab[:, int(r_off[a, c])] for c in range(NA_SLAB)], axis=2)
                          for a in range(NA_QROWS)], axis=1)
        tabs.append(jnp.where(valid[None], bias.reshape((n_heads,) + flat), NEG_INF))
    tab = jnp.stack(tabs)
    return jnp.swapaxes(tab.reshape(3, n_heads // 2, 2 * flat[0], flat[1]), -1, -2)


def _layer_norm(y, g, b):
    mu = jnp.mean(y, axis=-1, keepdims=True)
    yc = y - mu
    var = jnp.mean(yc * yc, axis=-1, keepdims=True)
    return yc * lax.rsqrt(var + LN_EPS) * g + b


def _route(logits):
    lane = _lane_iota().astype(F32)
    big = float(LANES)
    is_g = lane < N_GROUPS
    gl = jnp.where(is_g, logits, NEG_INF)
    g_max = jnp.max(gl, axis=-1, keepdims=True)
    g_idx = jnp.min(jnp.where(gl == g_max, lane, big), axis=-1, keepdims=True)
    g_w = 1.0 / jnp.sum(jnp.where(is_g, jnp.exp(gl - g_max), 0.0), axis=-1, keepdims=True)
    base = N_GROUPS + EXPERTS_PER_GROUP * g_idx
    el = jnp.where((lane >= base) & (lane < base + EXPERTS_PER_GROUP), logits, NEG_INF)
    v1 = jnp.max(el, axis=-1, keepdims=True)
    i1 = jnp.min(jnp.where(el == v1, lane, big), axis=-1, keepdims=True)
    el2 = jnp.where(lane == i1, NEG_INF, el)
    v2 = jnp.max(el2, axis=-1, keepdims=True)
    i2 = jnp.min(jnp.where(el2 == v2, lane, big), axis=-1, keepdims=True)
    t = jnp.exp(v2 - v1)
    w1 = g_w / (1.0 + t)
    w2 = g_w * t / (1.0 + t)
    return jnp.where(lane == 0, i1 - N_GROUPS,
                     jnp.where(lane == 1, i2 - N_GROUPS, jnp.where(lane == 2, w1, jnp.where(lane == 3, w2, 0.0))))


def _outproj_kernel(oa_ref, ob_ref, w_ref, h_ref, gate_ref, lng_ref, lnb_ref, sh_ref, sc_ref, wr_ref, br_ref,
                    h1_ref, f_ref, route_ref, *, alpha):
    ka = oa_ref.shape[1]
    o = (jnp.dot(oa_ref[...], w_ref[:ka, :], preferred_element_type=F32)
         + jnp.dot(ob_ref[...], w_ref[ka:, :], preferred_element_type=F32))
    h1 = _layer_norm(alpha * h_ref[...] + gate_ref[...] * o, lng_ref[...], lnb_ref[...])
    h1_ref[...] = h1
    f = h1 * sc_ref[...] + sh_ref[...]
    f_ref[...] = f
    logits = jnp.dot(f, wr_ref[...], precision=HIGHEST, preferred_element_type=F32) + br_ref[...]
    route_ref[...] = _route(logits)


def _outproj_call(oa, ob, w16, h, gate, ln_g, ln_b, shift, scale1p, w_route, b_route, *, n_rows, s_len, alpha):
    b, _, d = h.shape
    ka, kb = oa.shape[2], ob.shape[2]
    n_lat = s_len // TM
    seg = lambda bb, j: (bb, jnp.where(j >= n_lat, 1, 0), 0, 0)
    row = lambda bb, j: (bb, j, 0)
    const = lambda bb, j: (0, 0)
    return pl.pallas_call(
        functools.partial(_outproj_kernel, alpha=alpha),
        grid=(b, n_rows // TM),
        in_specs=[
            pl.BlockSpec((None, TM, ka), row),
            pl.BlockSpec((None, TM, kb), row),
            pl.BlockSpec((ka + kb, d), const),
            pl.BlockSpec((None, TM, d), row),
            pl.BlockSpec((None, None, 1, d), seg),
            pl.BlockSpec((1, d), const),
            pl.BlockSpec((1, d), const),
            pl.BlockSpec((None, None, 1, d), seg),
            pl.BlockSpec((None, None, 1, d), seg),
            pl.BlockSpec((d, LANES), const),
            pl.BlockSpec((1, LANES), const),
        ],
        out_specs=[
            pl.BlockSpec((None, TM, d), row),
            pl.BlockSpec((None, TM, d), row),
            pl.BlockSpec((None, TM, LANES), row),
        ],
        out_shape=[
            jax.ShapeDtypeStruct((b, n_rows, d), F32),
            jax.ShapeDtypeStruct((b, n_rows, d), F32),
            jax.ShapeDtypeStruct((b, n_rows, LANES), F32),
        ],
        compiler_params=_cparams(2),
        name="outproj",
    )(oa, ob, w16, h, gate, ln_g, ln_b, shift, scale1p, w_route, b_route)


def _row_copy(src, src_row, dst, dst_row, sem):
    return pltpu.make_async_copy(src.at[pl.ds(src_row, 1), :], dst.at[pl.ds(dst_row, 1), :], sem)


def _moe_kernel(blk_e_ref, nvalid_ref, tok_ref, dst_ref, f_hbm, w1_ref, w3_ref, w2_ref, y_hbm,
                xbuf, ybuf, gsem, ssem):
    del blk_e_ref
    i = pl.program_id(0)
    nv = nvalid_ref[i]

    @pl.when(i == 0)
    def _():
        xbuf[...] = jnp.zeros(xbuf.shape, xbuf.dtype)

    @pl.when(nv > 0)
    def _():
        def gather_start(r, carry):
            _row_copy(f_hbm, tok_ref[0, r], xbuf, r, gsem).start()
            return carry
        lax.fori_loop(0, nv, gather_start, 0)

        def gather_wait(r, carry):
            _row_copy(f_hbm, tok_ref[0, r], xbuf, r, gsem).wait()
            return carry
        lax.fori_loop(0, nv, gather_wait, 0)

        x = xbuf[...].astype(BF16)
        h1 = jnp.dot(x, w1_ref[...].astype(BF16), preferred_element_type=F32)
        h3 = jnp.dot(x, w3_ref[...].astype(BF16), preferred_element_type=F32)
        act = (h1 / (1.0 + jnp.exp(-h1))) * h3
        ybuf[...] = jnp.dot(act.astype(BF16), w2_ref[...].astype(BF16), preferred_element_type=F32)

        def scatter_start(r, carry):
            _row_copy(ybuf, r, y_hbm, dst_ref[0, r], ssem).start()
            return carry
        lax.fori_loop(0, nv, scatter_start, 0)

        def scatter_wait(r, carry):
            _row_copy(ybuf, r, y_hbm, dst_ref[0, r], ssem).wait()
            return carry
        lax.fori_loop(0, nv, scatter_wait, 0)


def _moe_call(f_rows, blk_e, nvalid, slot_tok, slot_dst, w1, w3, w2, layer):
    t, d = f_rows.shape
    n_blocks = blk_e.shape[0]
    de = w1.shape[3]
    idx_spec = pl.BlockSpec((None, 1, MOE_BLOCK), lambda i, be, nv: (i, 0, 0), memory_space=pltpu.SMEM)
    grid_spec = pltpu.PrefetchScalarGridSpec(
        num_scalar_prefetch=2,
        grid=(n_blocks,),
        in_specs=[
            idx_spec,
            idx_spec,
            pl.BlockSpec(memory_space=pl.ANY),
            pl.BlockSpec((None, None, d, de), lambda i, be, nv: (layer, be[i], 0, 0)),
            pl.BlockSpec((None, None, d, de), lambda i, be, nv: (layer, be[i], 0, 0)),
            pl.BlockSpec((None, None, de, d), lambda i, be, nv: (layer, be[i], 0, 0)),
        ],
        out_specs=pl.BlockSpec(memory_space=pl.ANY),
        scratch_shapes=[
            pltpu.VMEM((MOE_BLOCK, d), F32),
            pltpu.VMEM((MOE_BLOCK, d), F32),
            pltpu.SemaphoreType.DMA(()),
            pltpu.SemaphoreType.DMA(()),
        ],
    )
    return pl.pallas_call(
        _moe_kernel,
        grid_spec=grid_spec,
        out_shape=jax.ShapeDtypeStruct((TOP_K * t, d), F32),
        compiler_params=_cparams(1),
        name="moe_experts",
    )(blk_e, nvalid, slot_tok.reshape(n_blocks, 1, MOE_BLOCK), slot_dst.reshape(n_blocks, 1, MOE_BLOCK),
      f_rows, w1, w3, w2)


def _moe_plan(route, t):
    n_assign = TOP_K * t
    e_flat = route[:, :TOP_K].astype(jnp.int32).reshape(n_assign)
    order = jnp.argsort(e_flat, stable=True).astype(jnp.int32)
    counts = jnp.sum((e_flat[:, None] == jnp.arange(N_EXPERTS, dtype=jnp.int32)[None, :]).astype(jnp.int32), axis=0)
    starts = jnp.cumsum(counts) - counts
    padded = (counts + MOE_BLOCK - 1) // MOE_BLOCK * MOE_BLOCK
    p_ends = jnp.cumsum(padded)
    p_starts = p_ends - padded
    n_blocks = (n_assign + MOE_BLOCK - 1) // MOE_BLOCK + N_EXPERTS
    blk_start = jnp.arange(n_blocks, dtype=jnp.int32) * MOE_BLOCK
    blk_e = jnp.minimum(jnp.sum((p_ends[None, :] <= blk_start[:, None]).astype(jnp.int32), axis=1), N_EXPERTS - 1)
    slot = jnp.arange(n_blocks * MOE_BLOCK, dtype=jnp.int32)
    e_s = jnp.repeat(blk_e, MOE_BLOCK)
    j = slot - p_starts[e_s]
    valid = j < counts[e_s]
    a_idx = order[jnp.clip(starts[e_s] + j, 0, n_assign - 1)]
    slot_tok = jnp.where(valid, a_idx // TOP_K, 0).astype(jnp.int32)
    slot_dst = jnp.where(valid, (a_idx % TOP_K) * t + a_idx // TOP_K, 0).astype(jnp.int32)
    nvalid = jnp.sum(valid.reshape(n_blocks, MOE_BLOCK).astype(jnp.int32), axis=1)
    return blk_e, nvalid, slot_tok, slot_dst


def _combine_kernel(h_ref, y0_ref, y1_ref, route_ref, gate_ref, lng_ref, lnb_ref, o_ref, *, alpha):
    r = route_ref[...]
    y = r[:, 2:3] * y0_ref[...] + r[:, 3:4] * y1_ref[...]
    o_ref[...] = _layer_norm(alpha * h_ref[...] + gate_ref[...] * y, lng_ref[...], lnb_ref[...])


def _combine_call(h1, y, route, gate, ln_g, ln_b, *, s_len, alpha):
    b, n_rows, d = h1.shape
    n_lat = s_len // TM
    seg = lambda bb, j: (bb, jnp.where(j >= n_lat, 1, 0), 0, 0)
    row = lambda bb, j: (bb, j, 0)
    const = lambda bb, j: (0, 0)
    y4 = y.reshape(TOP_K, b, n_rows, d)
    return pl.pallas_call(
        functools.partial(_combine_kernel, alpha=alpha),
        grid=(b, n_rows // TM),
        in_specs=[
            pl.BlockSpec((None, TM, d), row),
            pl.BlockSpec((None, None, TM, d), lambda bb, j: (0, bb, j, 0)),
            pl.BlockSpec((None, None, TM, d), lambda bb, j: (1, bb, j, 0)),
            pl.BlockSpec((None, TM, LANES), row),
            pl.BlockSpec((None, None, 1, d), seg),
            pl.BlockSpec((1, d), const),
            pl.BlockSpec((1, d), const),
        ],
        out_specs=pl.BlockSpec((None, TM, d), row),
        out_shape=jax.ShapeDtypeStruct((b, n_rows, d), F32),
        compiler_params=_cparams(2),
        name="combine",
    )(h1, y4, y4, route, gate, ln_g, ln_b)


def _pair_heads(w, axis):
    w = jnp.moveaxis(w, axis, -1)
    lead = w.shape[:-1]
    n_heads = w.shape[-1] // HEAD_DIM
    w = w.reshape(lead + (2, n_heads // 2, HEAD_DIM)).swapaxes(-3, -2).reshape(lead + (n_heads * HEAD_DIM,))
    return jnp.moveaxis(w, -1, axis)


def _rope_tables(s_len, n_ctx):
    t = jnp.arange(s_len, dtype=jnp.int32)
    row = (t // GRID_W).astype(F32)
    col = (t % GRID_W).astype(F32)
    n_freq = HEAD_DIM // 4
    inv = ROPE_THETA ** (-jnp.arange(n_freq, dtype=F32) / n_freq)
    ang = jnp.concatenate([row[:, None] * inv, col[:, None] * inv], -1)
    cos = jnp.concatenate([jnp.cos(ang), jnp.ones((n_ctx, HEAD_DIM // 2), F32)], 0)
    sin = jnp.concatenate([jnp.sin(ang), jnp.zeros((n_ctx, HEAD_DIM // 2), F32)], 0)
    return jnp.tile(cos, (1, 4)), jnp.concatenate([-sin, sin, -sin, sin], -1)


def _lambda_init(layer_idx):
    return 0.8 - 0.6 * float(np.exp(-0.3 * layer_idx))


def _router_params(w_group, b_group, w_router, b_router):
    d = w_group.shape[0]
    pad = LANES - N_GROUPS - N_EXPERTS
    w = jnp.concatenate([w_group, w_router, jnp.zeros((d, pad), F32)], axis=1)
    bb = jnp.concatenate([b_group, b_router, jnp.zeros((pad,), F32)])
    return w, bb.reshape(1, LANES)


def _moe_layer(f, route, w1, w3, w2, layer):
    b, n_rows, d = f.shape
    t = b * n_rows
    blk_e, nvalid, slot_tok, slot_dst = _moe_plan(route.reshape(t, LANES), t)
    return _moe_call(f.reshape(t, d), blk_e, nvalid, slot_tok, slot_dst, w1, w3, w2, layer)


def kernel(x, c, ctx, c_ctx, mod_w, mod_b, ln_g, ln_b, ab_w_in, ab_w_out, diff_lambda, diff_subln_g, gqa_qk_g,
           cd_w_in, cd_w_out, win_sink, na_rpb, moe_w_group, moe_b_group, moe_w_router, moe_b_router,
           moe_w1, moe_w3, moe_w2):
    b, s, d = x.shape
    n_ctx = ctx.shape[1]
    n = s + n_ctx
    depth = mod_w.shape[0]
    rows = s // GRID_W
    assert depth == 2, "layer pattern implemented: one differential/GQA layer, then one window/neighbourhood layer"
    assert s % TM == 0 and n_ctx % TM == 0 and s % TK == 0 and b + 1 <= 8
    assert n_ctx % TQ_DIFF == 0 and n_ctx % TQ_GQA == 0
    assert rows >= NA_SLAB and rows % NA_QROWS == 0 and s % GRID_W == 0
    alpha = (2.0 * depth) ** 0.25
    qw = d // 2
    kvw = qw // 4
    assert qw == 4 * LANES and kvw == LANES

    cos128, sin128 = _rope_tables(s, n_ctx)
    cvec = jnp.concatenate([c, c_ctx[None, :], jnp.zeros((8 - b - 1, d), F32)], axis=0)
    mods_all = _mods_call(cvec, mod_w, mod_b)

    def seg_mods(i):
        m = mods_all[i]
        lat = m[:b].reshape(b, 6, d)
        cm = jnp.broadcast_to(m[b].reshape(1, 6, d), (b, 6, d))
        ms = jnp.stack([lat, cm], axis=1)
        pick = lambda k, one: (ms[:, :, k] + one)[:, :, None, :]
        return pick(0, 0.0), pick(1, 1.0), pick(2, 0.0), pick(3, 0.0), pick(4, 1.0), pick(5, 0.0)

    h = jnp.concatenate([x, ctx], axis=1)

    shift1, scale1p, gate1, shift2, scale2p, gate2 = seg_mods(0)
    w_in = ab_w_in[0]
    q_d, q_g, k_d, v_d = (w_in[:, k * qw:(k + 1) * qw] for k in range(4))
    k_g, v_g = w_in[:, 4 * qw:4 * qw + kvw], w_in[:, 4 * qw + kvw:]
    w_qk = jnp.concatenate([q_d, _pair_heads(q_g, 1), k_d, k_g], axis=1).astype(BF16)
    roles0 = ((True, None, True),) * 4 + ((True, 0, True),) * 4 + ((True, None, False),) * 4 + ((True, 1, False),)
    w_vt = jnp.concatenate([v_d, v_g], axis=1).T.astype(BF16)
    g128 = jnp.tile(gqa_qk_g[0], (1, 2))
    p0, vt0 = _inproj_call(h, shift1, scale1p, w_qk, w_vt, cos128, sin128, g128, roles0, s)
    o_d = _diff_attn_call(p0, vt0, diff_lambda[0], diff_subln_g[0], s_len=s, n_ctx=n_ctx, n_heads=4,
                          q_blk0=0, k_blk0=8, v_blk0=0, lam_init=_lambda_init(0))
    o_g = _gqa_attn_call(p0, vt0, s_len=s, n_ctx=n_ctx, q_width=qw, q_col0=qw, k_blk=12, v_blk=4)
    w_out = jnp.concatenate([ab_w_out[0][:qw], _pair_heads(ab_w_out[0][qw:], 0)], axis=0).astype(BF16)
    w_route, b_route = _router_params(moe_w_group[0], moe_b_group[0], moe_w_router[0], moe_b_router[0])
    h1, f, route = _outproj_call(o_d, o_g, w_out, h, gate1, ln_g[0, 0:1], ln_b[0, 0:1], shift2, scale2p,
                                 w_route, b_route, n_rows=n, s_len=s, alpha=alpha)
    y = _moe_layer(f, route, moe_w1, moe_w3, moe_w2, 0)
    h = _combine_call(h1, y, route, gate2, ln_g[0, 1:2], ln_b[0, 1:2], s_len=s, alpha=alpha)

    shift1, scale1p, gate1, shift2, scale2p, gate2 = seg_mods(1)
    w_in = cd_w_in[0]
    q_w, q_n = w_in[:, :qw], w_in[:, qw:2 * qw]
    k_w, v_w = w_in[:, 2 * qw:2 * qw + kvw], w_in[:, 2 * qw + kvw:2 * qw + 2 * kvw]
    k_n, v_n = w_in[:, 2 * qw + 2 * kvw:3 * qw + 2 * kvw], w_in[:, 3 * qw + 2 * kvw:]
    w_qk = jnp.concatenate([_pair_heads(q_w, 1), q_n, k_w, k_n], axis=1).astype(BF16)
    roles1 = ((True, None, True),) * 4 + ((False, None, True),) * 4 + ((True, None, False),) \
        + ((False, None, False),) * 4
    w_vt = jnp.concatenate([v_w, v_n], axis=1).T.astype(BF16)
    p1, vt1 = _inproj_call(h, shift1, scale1p, w_qk, w_vt, cos128, sin128, jnp.ones((2, LANES), F32), roles1, s)
    sink_row = jnp.repeat(win_sink[0].reshape(2, 4).T.reshape(8), TQ_WIN).reshape(1, 8 * TQ_WIN)
    o_w = _win_attn_call(p1, vt1, sink_row, s_len=s, n_ctx=n_ctx, q_width=qw, k_blk=8, v_blk=0)
    o_n = _na_attn_call(p1, vt1, _na_bias_tables(na_rpb[0], rows), s_len=s, n_ctx=n_ctx, n_pairs=4,
                        q_blk0=4, k_blk0=9, v_blk0=1)
    w_out = jnp.concatenate([_pair_heads(cd_w_out[0][:qw], 0), cd_w_out[0][qw:]], axis=0).astype(BF16)
    w_route, b_route = _router_params(moe_w_group[1], moe_b_group[1], moe_w_router[1], moe_b_router[1])
    h1, f, route = _outproj_call(o_w, o_n, w_out, h, gate1, ln_g[1, 0:1], ln_b[1, 0:1], shift2, scale2p,
                                 w_route, b_route, n_rows=s, s_len=s, alpha=alpha)
    y = _moe_layer(f, route, moe_w1, moe_w3, moe_w2, 1)
    return _combine_call(h1, y, route, gate2, ln_g[1, 1:2], ln_b[1, 1:2], s_len=s, alpha=alpha)
```

```python
import functools

import numpy as np
import jax
import jax.numpy as jnp
from jax import lax
from jax.experimental import pallas as pl
from jax.experimental.pallas import tpu as pltpu

F32 = jnp.float32
BF16 = jnp.bfloat16
HIGHEST = lax.Precision.HIGHEST

GRID_W = 64
HEAD_DIM = 64
ROPE_THETA = 10000.0
WINDOW = 128
NA_ROWS_MAX = 8
NA_COLS = 16
N_GROUPS = 4
EXPERTS_PER_GROUP = 8
N_EXPERTS = N_GROUPS * EXPERTS_PER_GROUP
TOP_K = 2
MOE_BLOCK = 128
LN_EPS = 1e-5
RMS_EPS = 1e-6
ATTN_SCALE = HEAD_DIM ** -0.5
NEG_INF = -1e30
LOG2E = 1.4426950408889634
Q_SCALE = ATTN_SCALE * LOG2E

LANES = 128
PAIR = 2 * HEAD_DIM
assert PAIR == LANES

VMEM_LIMIT = 48 * 1024 * 1024

TM = 256
TQ_DIFF = 256
TQ_GQA = 128
TK = 1024
Q_STRIP = 256
TQ_WIN = 128
NA_QROWS = 4
NA_SLAB = 12


def _cparams(n_axes):
    return pltpu.CompilerParams(dimension_semantics=("arbitrary",) * n_axes, vmem_limit_bytes=VMEM_LIMIT)


def _lane_iota():
    return lax.broadcasted_iota(jnp.int32, (1, LANES), 1)


def _mods_kernel(c_ref, w_ref, b_ref, o_ref):
    c = c_ref[...]
    sc = c / (1.0 + jnp.exp(-c))
    o_ref[...] = jnp.dot(sc, w_ref[...], precision=HIGHEST, preferred_element_type=F32) + b_ref[...]


def _mods_call(cvec, mod_w, mod_b):
    depth, d, n6 = mod_w.shape
    tn = n6 // 4
    return pl.pallas_call(
        _mods_kernel,
        grid=(depth, n6 // tn),
        in_specs=[
            pl.BlockSpec((8, d), lambda l, j: (0, 0)),
            pl.BlockSpec((None, d, tn), lambda l, j: (l, 0, j)),
            pl.BlockSpec((None, 1, tn), lambda l, j: (l, 0, j)),
        ],
        out_specs=pl.BlockSpec((None, 8, tn), lambda l, j: (l, 0, j)),
        out_shape=jax.ShapeDtypeStruct((depth, 8, n6), F32),
        compiler_params=_cparams(2),
        name="mods",
    )(cvec, mod_w, mod_b.reshape(depth, 1, n6))


def _inproj_kernel(h_ref, sh_ref, sc_ref, w_ref, wvt_ref, cos_ref, sin_ref, g_ref, o_ref, vt_ref, *, roles):
    a = (h_ref[...] * sc_ref[...] + sh_ref[...]).astype(BF16)
    vt = lax.dot_general(wvt_ref[...], a, (((1,), (1,)), ((), ())), preferred_element_type=F32)
    for j in range(vt_ref.shape[0]):
        vt_ref[j] = vt[j * LANES:(j + 1) * LANES, :].astype(BF16)
    lane = _lane_iota()
    rope_first = (lane & (HEAD_DIM - 1)) < (HEAD_DIM // 2)
    lo = lane < HEAD_DIM
    n_blocks = len(roles)
    for c0 in range(0, n_blocks, 2):
        width = min(2, n_blocks - c0)
        p = jnp.dot(a, w_ref[:, c0 * LANES:(c0 + width) * LANES], preferred_element_type=F32)
        for half in range(width):
            cb = c0 + half
            rope, norm, scale = roles[cb]
            blk = p[:, half * LANES:(half + 1) * LANES]
            if norm is not None:
                sq = blk * blk
                s_lo = jnp.sum(jnp.where(lo, sq, 0.0), axis=-1, keepdims=True)
                s_hi = jnp.sum(jnp.where(lo, 0.0, sq), axis=-1, keepdims=True)
                ms = jnp.where(lo, s_lo, s_hi) * (1.0 / HEAD_DIM)
                blk = blk * lax.rsqrt(ms + RMS_EPS) * g_ref[norm:norm + 1, :]
            if rope:
                partner = jnp.where(rope_first, pltpu.roll(blk, LANES - HEAD_DIM // 2, 1),
                                    pltpu.roll(blk, HEAD_DIM // 2, 1))
                blk = blk * cos_ref[...] + partner * sin_ref[...]
            if scale:
                blk = blk * Q_SCALE
            o_ref[:, cb * LANES:(cb + 1) * LANES] = blk.astype(BF16)


def _inproj_call(h, shift, scale1p, w16, wvt16, cos128, sin128, g128, roles, s_len):
    b, n, d = h.shape
    ncol = w16.shape[1]
    nvb = wvt16.shape[0] // LANES
    n_lat = s_len // TM
    seg = lambda bb, j: (bb, jnp.where(j >= n_lat, 1, 0), 0, 0)
    return pl.pallas_call(
        functools.partial(_inproj_kernel, roles=roles),
        grid=(b, n // TM),
        in_specs=[
            pl.BlockSpec((None, TM, d), lambda bb, j: (bb, j, 0)),
            pl.BlockSpec((None, None, 1, d), seg),
            pl.BlockSpec((None, None, 1, d), seg),
            pl.BlockSpec((d, ncol), lambda bb, j: (0, 0)),
            pl.BlockSpec((nvb * LANES, d), lambda bb, j: (0, 0)),
            pl.BlockSpec((TM, LANES), lambda bb, j: (j, 0)),
            pl.BlockSpec((TM, LANES), lambda bb, j: (j, 0)),
            pl.BlockSpec((2, LANES), lambda bb, j: (0, 0)),
        ],
        out_specs=[
            pl.BlockSpec((None, TM, ncol), lambda bb, j: (bb, j, 0)),
            pl.BlockSpec((None, nvb, LANES, TM), lambda bb, j: (bb, 0, 0, j)),
        ],
        out_shape=[
            jax.ShapeDtypeStruct((b, n, ncol), BF16),
            jax.ShapeDtypeStruct((b, nvb, LANES, n), BF16),
        ],
        compiler_params=_cparams(2),
        name="inproj",
    )(h, shift, scale1p, w16, wvt16, cos128, sin128, g128)


def _stack_q(q_ref, n_qblocks):
    lo = _lane_iota() < HEAD_DIM
    parts = []
    for j in range(n_qblocks):
        q = q_ref[:, j * LANES:(j + 1) * LANES]
        zero = jnp.zeros_like(q)
        parts.append(jnp.where(lo, q, zero))
        parts.append(jnp.where(lo, zero, q))
    return jnp.concatenate(parts, axis=0)


def _scores_t(k, qs):
    return lax.dot_general(k, qs, (((1,), (1,)), ((), ())), preferred_element_type=F32)


def _flash_init(m_ref, l_ref, acc_ref):
    m_ref[...] = jnp.full(m_ref.shape, NEG_INF, F32)
    l_ref[...] = jnp.zeros(l_ref.shape, F32)
    acc_ref[...] = jnp.zeros(acc_ref.shape, F32)


def _flash_update(st, vt, m_ref, l_ref, acc_ref):
    m_prev = m_ref[...]
    m_new = jnp.maximum(m_prev, jnp.max(st, axis=0, keepdims=True))
    alpha = jnp.exp2(m_prev - m_new)
    pt = jnp.exp2(st - m_new)
    l_ref[...] = alpha * l_ref[...] + jnp.sum(pt, axis=0, keepdims=True)
    acc_ref[...] = alpha * acc_ref[...] + jnp.dot(vt, pt.astype(BF16), preferred_element_type=F32)
    m_ref[...] = m_new


def _flash_sweep(qs, k_ref, vt_ref, m_ref, l_ref, acc_ref, s_ref, *, is_lat, s_len, n_ctx):
    _flash_init(m_ref, l_ref, acc_ref)
    n_chunks = s_len // TK
    assert n_chunks % 2 == 0 and n_chunks >= 2

    def k_chunk(c):
        if isinstance(c, int):
            return k_ref[c * TK:(c + 1) * TK, :]
        return k_ref[pl.ds(pl.multiple_of(c * TK, TK), TK), :]

    def vt_chunk(c):
        if isinstance(c, int):
            return vt_ref[:, c * TK:(c + 1) * TK]
        return vt_ref[:, pl.ds(pl.multiple_of(c * TK, TK), TK)]

    k_ctx = k_ref[s_len:s_len + n_ctx, :]
    vt_ctx = vt_ref[:, s_len:s_len + n_ctx]

    @pl.when(is_lat)
    def _():
        s_ref[...] = _scores_t(k_chunk(0), qs)

        def body(j, carry):
            c = 2 * j
            s_odd = _scores_t(k_chunk(c + 1), qs)
            _flash_update(s_ref[...], vt_chunk(c), m_ref, l_ref, acc_ref)
            s_ref[...] = _scores_t(k_chunk(c + 2), qs)
            _flash_update(s_odd, vt_chunk(c + 1), m_ref, l_ref, acc_ref)
            return carry
        lax.fori_loop(0, n_chunks // 2 - 1, body, 0)

        c = n_chunks - 2
        s_odd = _scores_t(k_chunk(c + 1), qs)
        _flash_update(s_ref[...], vt_chunk(c), m_ref, l_ref, acc_ref)
        s_last = _scores_t(k_ctx, qs)
        _flash_update(s_odd, vt_chunk(c + 1), m_ref, l_ref, acc_ref)
        _flash_update(s_last, vt_ctx, m_ref, l_ref, acc_ref)

    @pl.when(jnp.logical_not(is_lat))
    def _():
        _flash_update(_scores_t(k_ctx, qs), vt_ctx, m_ref, l_ref, acc_ref)


def _merge_pairs(ot, tq, n_qblocks, o_ref):
    for j in range(n_qblocks):
        o_lo = ot[:HEAD_DIM, (2 * j) * tq:(2 * j + 1) * tq]
        o_hi = ot[HEAD_DIM:, (2 * j + 1) * tq:(2 * j + 2) * tq]
        blk = jnp.concatenate([o_lo, o_hi], axis=0)
        o_ref[:, j * LANES:(j + 1) * LANES] = blk.T.astype(o_ref.dtype)


def _diff_attn_kernel(q_ref, k_ref, vt_ref, lam_ref, g_ref, o_ref, m_ref, l_ref, acc_ref, s_ref, *,
                      s_len, n_ctx, lam_init):
    tq = q_ref.shape[0]
    i = pl.program_id(2)
    qs = _stack_q(q_ref, 1)
    _flash_sweep(qs, k_ref, vt_ref, m_ref, l_ref, acc_ref, s_ref, is_lat=i < s_len // tq, s_len=s_len, n_ctx=n_ctx)
    lv = lam_ref[...]
    lam = (jnp.exp(jnp.sum(lv[0:1] * lv[1:2], axis=-1, keepdims=True))
           - jnp.exp(jnp.sum(lv[2:3] * lv[3:4], axis=-1, keepdims=True)) + lam_init)
    ot = acc_ref[...] / l_ref[...]
    od = ot[:, :tq] - lam * ot[:, tq:]
    ms = jnp.mean(od * od, axis=0, keepdims=True)
    od = od * lax.rsqrt(ms + RMS_EPS) * g_ref[...] * (1.0 - lam_init)
    o_ref[...] = od.T.astype(o_ref.dtype)


def _diff_attn_call(p, vt, lam_vecs, subln_g, *, s_len, n_ctx, n_heads, q_blk0, k_blk0, v_blk0, lam_init):
    b, n, _ = p.shape
    tq = TQ_DIFF
    return pl.pallas_call(
        functools.partial(_diff_attn_kernel, s_len=s_len, n_ctx=n_ctx, lam_init=lam_init),
        grid=(b, n_heads, n // tq),
        in_specs=[
            pl.BlockSpec((None, tq, LANES), lambda bb, hh, i: (bb, i, q_blk0 + hh)),
            pl.BlockSpec((None, n, LANES), lambda bb, hh, i: (bb, 0, k_blk0 + hh)),
            pl.BlockSpec((None, None, LANES, n), lambda bb, hh, i: (bb, v_blk0 + hh, 0, 0)),
            pl.BlockSpec((4, HEAD_DIM), lambda bb, hh, i: (0, 0)),
            pl.BlockSpec((LANES, 1), lambda bb, hh, i: (0, 0)),
        ],
        out_specs=pl.BlockSpec((None, tq, LANES), lambda bb, hh, i: (bb, i, hh)),
        out_shape=jax.ShapeDtypeStruct((b, n, n_heads * LANES), BF16),
        scratch_shapes=[pltpu.VMEM((1, 2 * tq), F32), pltpu.VMEM((1, 2 * tq), F32), pltpu.VMEM((LANES, 2 * tq), F32),
                        pltpu.VMEM((TK, 2 * tq), F32)],
        compiler_params=_cparams(3),
        name="diff_attn",
    )(p, p, vt, lam_vecs, subln_g.reshape(LANES, 1))


def _gqa_attn_kernel(q_ref, k_ref, vt_ref, o_ref, m_ref, l_ref, acc_ref, s_ref, *, s_len, n_ctx):
    tq = q_ref.shape[0]
    n_qblocks = q_ref.shape[1] // LANES
    i = pl.program_id(1)
    qs = _stack_q(q_ref, n_qblocks)
    _flash_sweep(qs, k_ref, vt_ref, m_ref, l_ref, acc_ref, s_ref, is_lat=i < s_len // tq, s_len=s_len, n_ctx=n_ctx)
    _merge_pairs(acc_ref[...] / l_ref[...], tq, n_qblocks, o_ref)


def _gqa_attn_call(p, vt, *, s_len, n_ctx, q_width, q_col0, k_blk, v_blk):
    b, n, _ = p.shape
    tq = TQ_GQA
    m_rows = 2 * (q_width // LANES) * tq
    return pl.pallas_call(
        functools.partial(_gqa_attn_kernel, s_len=s_len, n_ctx=n_ctx),
        grid=(b, n // tq),
        in_specs=[
            pl.BlockSpec((None, tq, q_width), lambda bb, i: (bb, i, q_col0 // q_width)),
            pl.BlockSpec((None, n, LANES), lambda bb, i: (bb, 0, k_blk)),
            pl.BlockSpec((None, None, LANES, n), lambda bb, i: (bb, v_blk, 0, 0)),
        ],
        out_specs=pl.BlockSpec((None, tq, q_width), lambda bb, i: (bb, i, 0)),
        out_shape=jax.ShapeDtypeStruct((b, n, q_width), BF16),
        scratch_shapes=[pltpu.VMEM((1, m_rows), F32), pltpu.VMEM((1, m_rows), F32), pltpu.VMEM((LANES, m_rows), F32),
                        pltpu.VMEM((TK, m_rows), F32)],
        compiler_params=_cparams(2),
        name="gqa_attn",
    )(p, p, vt)


def _win_attn_kernel(q_ref, k_ref, vt_ref, sink_ref, o_ref, *, s_len, n_ctx):
    tq = q_ref.shape[0]
    n_qblocks = q_ref.shape[1] // LANES
    span = tq + 2 * WINDOW
    i = pl.program_id(1)
    ws = pl.multiple_of(jnp.clip(i * tq - WINDOW, 0, s_len - span), LANES)
    qs = _stack_q(q_ref, n_qblocks)
    m_rows = qs.shape[0]
    s_loc = _scores_t(k_ref[pl.ds(ws, span), :], qs)
    kpos = ws + lax.broadcasted_iota(jnp.int32, (span, m_rows), 0)
    qpos = i * tq + (lax.broadcasted_iota(jnp.int32, (span, m_rows), 1) & (tq - 1))
    s_loc = jnp.where(jnp.abs(qpos - kpos) <= WINDOW, s_loc, NEG_INF)
    s_ctx = _scores_t(k_ref[s_len:s_len + n_ctx, :], qs)
    sink = sink_ref[...] * LOG2E
    m = jnp.maximum(jnp.maximum(jnp.max(s_loc, axis=0, keepdims=True), jnp.max(s_ctx, axis=0, keepdims=True)), sink)
    p_loc = jnp.exp2(s_loc - m)
    p_ctx = jnp.exp2(s_ctx - m)
    denom = jnp.sum(p_loc, axis=0, keepdims=True) + jnp.sum(p_ctx, axis=0, keepdims=True) + jnp.exp2(sink - m)
    acc = (jnp.dot(vt_ref[:, pl.ds(ws, span)], p_loc.astype(BF16), preferred_element_type=F32)
           + jnp.dot(vt_ref[:, s_len:s_len + n_ctx], p_ctx.astype(BF16), preferred_element_type=F32))
    _merge_pairs(acc / denom, tq, n_qblocks, o_ref)


def _win_attn_call(p, vt, sink_row, *, s_len, n_ctx, q_width, k_blk, v_blk):
    b, n, _ = p.shape
    tq = TQ_WIN
    m_rows = sink_row.shape[1]
    return pl.pallas_call(
        functools.partial(_win_attn_kernel, s_len=s_len, n_ctx=n_ctx),
        grid=(b, s_len // tq),
        in_specs=[
            pl.BlockSpec((None, tq, q_width), lambda bb, i: (bb, i, 0)),
            pl.BlockSpec((None, n, LANES), lambda bb, i: (bb, 0, k_blk)),
            pl.BlockSpec((None, None, LANES, n), lambda bb, i: (bb, v_blk, 0, 0)),
            pl.BlockSpec((1, m_rows), lambda bb, i: (0, 0)),
        ],
        out_specs=pl.BlockSpec((None, tq, q_width), lambda bb, i: (bb, i, 0)),
        out_shape=jax.ShapeDtypeStruct((b, s_len, q_width), BF16),
        compiler_params=_cparams(2),
        name="win_attn",
    )(p, p, vt, sink_row)


def _na_attn_kernel(q_ref, k_ref, vt_ref, bias_ref, o_ref, *, s_len, n_ctx):
    tq = q_ref.shape[0]
    slab = NA_SLAB * GRID_W
    rows = s_len // GRID_W
    mi = pl.program_id(2)
    ss = pl.multiple_of(jnp.clip(mi - 1, 0, (rows - NA_SLAB) // NA_QROWS) * (NA_QROWS * GRID_W), NA_QROWS * GRID_W)
    qs = _stack_q(q_ref, 1)
    s_nb = _scores_t(k_ref[pl.ds(ss, slab), :], qs) + bias_ref[...]
    s_cx = _scores_t(k_ref[s_len:s_len + n_ctx, :], qs)
    m = jnp.maximum(jnp.max(s_nb, axis=0, keepdims=True), jnp.max(s_cx, axis=0, keepdims=True))
    p_nb = jnp.exp2(s_nb - m)
    p_cx = jnp.exp2(s_cx - m)
    denom = jnp.sum(p_nb, axis=0, keepdims=True) + jnp.sum(p_cx, axis=0, keepdims=True)
    acc = (jnp.dot(vt_ref[:, pl.ds(ss, slab)], p_nb.astype(BF16), preferred_element_type=F32)
           + jnp.dot(vt_ref[:, s_len:s_len + n_ctx], p_cx.astype(BF16), preferred_element_type=F32))
    _merge_pairs(acc / denom, tq, 1, o_ref)


def _na_attn_call(p, vt, bias_tab, *, s_len, n_ctx, n_pairs, q_blk0, k_blk0, v_blk0):
    b, n, _ = p.shape
    tq = NA_QROWS * GRID_W
    n_steps = s_len // tq
    tbl = lambda mi: jnp.where(mi == 0, 0, jnp.where(mi == n_steps - 1, 2, 1))
    return pl.pallas_call(
        functools.partial(_na_attn_kernel, s_len=s_len, n_ctx=n_ctx),
        grid=(b, n_pairs, n_steps),
        in_specs=[
            pl.BlockSpec((None, tq, LANES), lambda bb, j, mi: (bb, mi, q_blk0 + j)),
            pl.BlockSpec((None, n, LANES), lambda bb, j, mi: (bb, 0, k_blk0 + j)),
            pl.BlockSpec((None, None, LANES, n), lambda bb, j, mi: (bb, v_blk0 + j, 0, 0)),
            pl.BlockSpec((None, None, NA_SLAB * GRID_W, 2 * tq), lambda bb, j, mi: (tbl(mi), j, 0, 0)),
        ],
        out_specs=pl.BlockSpec((None, tq, LANES), lambda bb, j, mi: (bb, mi, j)),
        out_shape=jax.ShapeDtypeStruct((b, s_len, n_pairs * LANES), BF16),
        compiler_params=_cparams(3),
        name="na_attn",
    )(p, p, vt, bias_tab)


def _na_bias_tables(rpb, rows):
    n_heads = rpb.shape[0]
    rpb = rpb.astype(F32)
    pad = GRID_W - NA_COLS
    rpb_p = jnp.pad(rpb, ((0, 0), (0, 0), (pad, pad)))
    col_tab = jnp.stack([rpb_p[:, :, pad + NA_COLS - 1 - qc: pad + NA_COLS - 1 - qc + GRID_W] for qc in range(GRID_W)],
                        axis=2)
    qr = np.arange(NA_QROWS)[:, None, None, None]
    qc = np.arange(GRID_W)[None, :, None, None]
    kr = np.arange(NA_SLAB)[None, None, :, None]
    kc = np.arange(GRID_W)[None, None, None, :]
    full = (NA_QROWS, GRID_W, NA_SLAB, GRID_W)
    flat = (NA_QROWS * GRID_W, NA_SLAB * GRID_W)
    tabs = []
    for q0, s0 in ((0, 0), (NA_QROWS, 0), (rows - NA_QROWS, rows - NA_SLAB)):
        q_row = q0 + qr
        k_row = s0 + kr
        r0 = np.clip(q_row - NA_ROWS_MAX // 2, 0, rows - NA_ROWS_MAX)
        c0 = np.clip(qc - NA_COLS // 2, 0, GRID_W - NA_COLS)
        valid = (k_row >= r0) & (k_row < r0 + NA_ROWS_MAX) & (kc >= c0) & (kc < c0 + NA_COLS)
        valid = np.broadcast_to(valid, full).reshape(flat)
        r_off = np.clip(k_row - q_row + NA_ROWS_MAX - 1, 0, 2 * NA_ROWS_MAX - 2)[:, 0, :, 0]
        bias = jnp.stack([jnp.stack([col_tab[:, int(r_off[a, c])] for c in range(NA_SLAB)], axis=2)
                          for a in range(NA_QROWS)], axis=1)
        tabs.append(jnp.where(valid[None], bias.reshape((n_heads,) + flat) * LOG2E, NEG_INF))
    tab = jnp.stack(tabs)
    return jnp.swapaxes(tab.reshape(3, n_heads // 2, 2 * flat[0], flat[1]), -1, -2)


def _layer_norm(y, g, b):
    mu = jnp.mean(y, axis=-1, keepdims=True)
    yc = y - mu
    var = jnp.mean(yc * yc, axis=-1, keepdims=True)
    return yc * lax.rsqrt(var + LN_EPS) * g + b


def _route(logits):
    lane = _lane_iota().astype(F32)
    big = float(LANES)
    is_g = lane < N_GROUPS
    gl = jnp.where(is_g, logits, NEG_INF)
    g_max = jnp.max(gl, axis=-1, keepdims=True)
    g_idx = jnp.min(jnp.where(gl == g_max, lane, big), axis=-1, keepdims=True)
    g_w = 1.0 / jnp.sum(jnp.where(is_g, jnp.exp(gl - g_max), 0.0), axis=-1, keepdims=True)
    base = N_GROUPS + EXPERTS_PER_GROUP * g_idx
    el = jnp.where((lane >= base) & (lane < base + EXPERTS_PER_GROUP), logits, NEG_INF)
    v1 = jnp.max(el, axis=-1, keepdims=True)
    i1 = jnp.min(jnp.where(el == v1, lane, big), axis=-1, keepdims=True)
    el2 = jnp.where(lane == i1, NEG_INF, el)
    v2 = jnp.max(el2, axis=-1, keepdims=True)
    i2 = jnp.min(jnp.where(el2 == v2, lane, big), axis=-1, keepdims=True)
    t = jnp.exp(v2 - v1)
    w1 = g_w / (1.0 + t)
    w2 = g_w * t / (1.0 + t)
    return jnp.where(lane == 0, i1 - N_GROUPS,
                     jnp.where(lane == 1, i2 - N_GROUPS, jnp.where(lane == 2, w1, jnp.where(lane == 3, w2, 0.0))))


def _outproj_kernel(oa_ref, ob_ref, w_ref, h_ref, gate_ref, lng_ref, lnb_ref, sh_ref, sc_ref, wr_ref, br_ref,
                    h1_ref, f_ref, route_ref, *, alpha):
    ka = oa_ref.shape[1]
    o = (jnp.dot(oa_ref[...], w_ref[:ka, :], preferred_element_type=F32)
         + jnp.dot(ob_ref[...], w_ref[ka:, :], preferred_element_type=F32))
    h1 = _layer_norm(alpha * h_ref[...] + gate_ref[...] * o, lng_ref[...], lnb_ref[...])
    h1_ref[...] = h1
    f = h1 * sc_ref[...] + sh_ref[...]
    f_ref[...] = f
    logits = jnp.dot(f, wr_ref[...], precision=HIGHEST, preferred_element_type=F32) + br_ref[...]
    route_ref[...] = _route(logits)


def _outproj_call(oa, ob, w16, h, gate, ln_g, ln_b, shift, scale1p, w_route, b_route, *, n_rows, s_len, alpha):
    b, _, d = h.shape
    ka, kb = oa.shape[2], ob.shape[2]
    n_lat = s_len // TM
    seg = lambda bb, j: (bb, jnp.where(j >= n_lat, 1, 0), 0, 0)
    row = lambda bb, j: (bb, j, 0)
    const = lambda bb, j: (0, 0)
    return pl.pallas_call(
        functools.partial(_outproj_kernel, alpha=alpha),
        grid=(b, n_rows // TM),
        in_specs=[
            pl.BlockSpec((None, TM, ka), row),
            pl.BlockSpec((None, TM, kb), row),
            pl.BlockSpec((ka + kb, d), const),
            pl.BlockSpec((None, TM, d), row),
            pl.BlockSpec((None, None, 1, d), seg),
            pl.BlockSpec((1, d), const),
            pl.BlockSpec((1, d), const),
            pl.BlockSpec((None, None, 1, d), seg),
            pl.BlockSpec((None, None, 1, d), seg),
            pl.BlockSpec((d, LANES), const),
            pl.BlockSpec((1, LANES), const),
        ],
        out_specs=[
            pl.BlockSpec((None, TM, d), row),
            pl.BlockSpec((None, TM, d), row),
            pl.BlockSpec((None, TM, LANES), row),
        ],
        out_shape=[
            jax.ShapeDtypeStruct((b, n_rows, d), F32),
            jax.ShapeDtypeStruct((b, n_rows, d), F32),
            jax.ShapeDtypeStruct((b, n_rows, LANES), F32),
        ],
        compiler_params=_cparams(2),
        name="outproj",
    )(oa, ob, w16, h, gate, ln_g, ln_b, shift, scale1p, w_route, b_route)


def _row_copy(src, src_row, dst, dst_row, sem):
    return pltpu.make_async_copy(src.at[pl.ds(src_row, 1), :], dst.at[pl.ds(dst_row, 1), :], sem)


def _block_copy_for_wait(src, dst, sem):
    return pltpu.make_async_copy(src, dst, sem)


def _moe_kernel(blk_e_ref, tok0_ref, tok_next_ref, dst_ref, f_hbm, w1_ref, w3_ref, w2_ref, y_hbm,
                xbuf, ybuf, w1b, w3b, w2b, gsem, ssem):
    i = pl.program_id(0)
    last = pl.num_programs(0) - 1
    slot = lax.rem(i, 2)
    other = 1 - slot

    def start_gather(tok_ref, s):
        def body(r, carry):
            _row_copy(f_hbm, tok_ref[0, r], xbuf.at[s], r, gsem.at[s]).start()
            return carry
        lax.fori_loop(0, MOE_BLOCK, body, 0, unroll=8)

    @pl.when(i == 0)
    def _():
        start_gather(tok0_ref, 0)

    _block_copy_for_wait(f_hbm.at[pl.ds(0, MOE_BLOCK), :], xbuf.at[slot], gsem.at[slot]).wait()

    @pl.when(i < last)
    def _():
        start_gather(tok_next_ref, other)

    @pl.when(jnp.logical_or(i == 0, blk_e_ref[i] != blk_e_ref[jnp.maximum(i - 1, 0)]))
    def _():
        w1b[...] = w1_ref[...].astype(BF16)
        w3b[...] = w3_ref[...].astype(BF16)
        w2b[...] = w2_ref[...].astype(BF16)

    x = xbuf[slot].astype(BF16)
    h1 = jnp.dot(x, w1b[...], preferred_element_type=F32)
    h3 = jnp.dot(x, w3b[...], preferred_element_type=F32)
    act = (h1 / (1.0 + jnp.exp(-h1))) * h3
    ybuf[slot] = jnp.dot(act.astype(BF16), w2b[...], preferred_element_type=F32)

    @pl.when(i > 0)
    def _():
        _block_copy_for_wait(ybuf.at[other], y_hbm.at[pl.ds(0, MOE_BLOCK), :], ssem.at[other]).wait()

    def scatter_body(r, carry):
        _row_copy(ybuf.at[slot], r, y_hbm, dst_ref[0, r], ssem.at[slot]).start()
        return carry
    lax.fori_loop(0, MOE_BLOCK, scatter_body, 0, unroll=8)

    @pl.when(i == last)
    def _():
        _block_copy_for_wait(ybuf.at[slot], y_hbm.at[pl.ds(0, MOE_BLOCK), :], ssem.at[slot]).wait()


def _moe_call(f_rows, blk_e, slot_tok, slot_dst, w1, w3, w2, layer):
    t, d = f_rows.shape
    n_blocks = blk_e.shape[0]
    de = w1.shape[3]
    smem_block = lambda index_map: pl.BlockSpec((None, 1, MOE_BLOCK), index_map, memory_space=pltpu.SMEM)
    grid_spec = pltpu.PrefetchScalarGridSpec(
        num_scalar_prefetch=1,
        grid=(n_blocks,),
        in_specs=[
            smem_block(lambda i, be: (0, 0, 0)),
            smem_block(lambda i, be: (jnp.minimum(i + 1, n_blocks - 1), 0, 0)),
            smem_block(lambda i, be: (i, 0, 0)),
            pl.BlockSpec(memory_space=pl.ANY),
            pl.BlockSpec((None, None, d, de), lambda i, be: (layer, be[i], 0, 0)),
            pl.BlockSpec((None, None, d, de), lambda i, be: (layer, be[i], 0, 0)),
            pl.BlockSpec((None, None, de, d), lambda i, be: (layer, be[i], 0, 0)),
        ],
        out_specs=pl.BlockSpec(memory_space=pl.ANY),
        scratch_shapes=[
            pltpu.VMEM((2, MOE_BLOCK, d), F32),
            pltpu.VMEM((2, MOE_BLOCK, d), F32),
            pltpu.VMEM((d, de), BF16),
            pltpu.VMEM((d, de), BF16),
            pltpu.VMEM((de, d), BF16),
            pltpu.SemaphoreType.DMA((2,)),
            pltpu.SemaphoreType.DMA((2,)),
        ],
    )
    tok3 = slot_tok.reshape(n_blocks, 1, MOE_BLOCK)
    return pl.pallas_call(
        _moe_kernel,
        grid_spec=grid_spec,
        out_shape=jax.ShapeDtypeStruct((n_blocks * MOE_BLOCK, d), F32),
        compiler_params=_cparams(1),
        name="moe_experts",
    )(blk_e, tok3, tok3, slot_dst.reshape(n_blocks, 1, MOE_BLOCK), f_rows, w1, w3, w2)


def _moe_plan(route, t):
    n_assign = TOP_K * t
    e_flat = route[:, :TOP_K].astype(jnp.int32).reshape(n_assign)
    order = jnp.argsort(e_flat, stable=True).astype(jnp.int32)
    counts = jnp.sum((e_flat[:, None] == jnp.arange(N_EXPERTS, dtype=jnp.int32)[None, :]).astype(jnp.int32), axis=0)
    starts = jnp.cumsum(counts) - counts
    padded = (counts + MOE_BLOCK - 1) // MOE_BLOCK * MOE_BLOCK
    p_ends = jnp.cumsum(padded)
    p_starts = p_ends - padded
    n_blocks = (n_assign + MOE_BLOCK - 1) // MOE_BLOCK + N_EXPERTS
    blk_start = jnp.arange(n_blocks, dtype=jnp.int32) * MOE_BLOCK
    blk_e = jnp.minimum(jnp.sum((p_ends[None, :] <= blk_start[:, None]).astype(jnp.int32), axis=1), N_EXPERTS - 1)
    slot = jnp.arange(n_blocks * MOE_BLOCK, dtype=jnp.int32)
    e_s = jnp.repeat(blk_e, MOE_BLOCK)
    j = slot - p_starts[e_s]
    valid = j < counts[e_s]
    a_idx = order[jnp.clip(starts[e_s] + j, 0, n_assign - 1)]
    slot_tok = jnp.where(valid, a_idx // TOP_K, 0).astype(jnp.int32)
    spare = n_assign + jnp.cumsum(jnp.logical_not(valid).astype(jnp.int32)) - 1
    slot_dst = jnp.where(valid, (a_idx % TOP_K) * t + a_idx // TOP_K, spare).astype(jnp.int32)
    return blk_e, slot_tok, slot_dst


def _combine_kernel(h_ref, y0_ref, y1_ref, route_ref, gate_ref, lng_ref, lnb_ref, o_ref, *, alpha):
    r = route_ref[...]
    y = r[:, 2:3] * y0_ref[...] + r[:, 3:4] * y1_ref[...]
    o_ref[...] = _layer_norm(alpha * h_ref[...] + gate_ref[...] * y, lng_ref[...], lnb_ref[...])


def _combine_call(h1, y, route, gate, ln_g, ln_b, *, s_len, alpha):
    b, n_rows, d = h1.shape
    n_lat = s_len // TM
    seg = lambda bb, j: (bb, jnp.where(j >= n_lat, 1, 0), 0, 0)
    row = lambda bb, j: (bb, j, 0)
    const = lambda bb, j: (0, 0)
    blocks_per_batch = n_rows // TM
    blocks_per_k = b * blocks_per_batch
    return pl.pallas_call(
        functools.partial(_combine_kernel, alpha=alpha),
        grid=(b, n_rows // TM),
        in_specs=[
            pl.BlockSpec((None, TM, d), row),
            pl.BlockSpec((TM, d), lambda bb, j: (bb * blocks_per_batch + j, 0)),
            pl.BlockSpec((TM, d), lambda bb, j: (blocks_per_k + bb * blocks_per_batch + j, 0)),
            pl.BlockSpec((None, TM, LANES), row),
            pl.BlockSpec((None, None, 1, d), seg),
            pl.BlockSpec((1, d), const),
            pl.BlockSpec((1, d), const),
        ],
        out_specs=pl.BlockSpec((None, TM, d), row),
        out_shape=jax.ShapeDtypeStruct((b, n_rows, d), F32),
        compiler_params=_cparams(2),
        name="combine",
    )(h1, y, y, route, gate, ln_g, ln_b)


def _pair_heads(w, axis):
    w = jnp.moveaxis(w, axis, -1)
    lead = w.shape[:-1]
    n_heads = w.shape[-1] // HEAD_DIM
    w = w.reshape(lead + (2, n_heads // 2, HEAD_DIM)).swapaxes(-3, -2).reshape(lead + (n_heads * HEAD_DIM,))
    return jnp.moveaxis(w, -1, axis)


def _rope_tables(s_len, n_ctx):
    t = jnp.arange(s_len, dtype=jnp.int32)
    row = (t // GRID_W).astype(F32)
    col = (t % GRID_W).astype(F32)
    n_freq = HEAD_DIM // 4
    inv = ROPE_THETA ** (-jnp.arange(n_freq, dtype=F32) / n_freq)
    ang = jnp.concatenate([row[:, None] * inv, col[:, None] * inv], -1)
    cos = jnp.concatenate([jnp.cos(ang), jnp.ones((n_ctx, HEAD_DIM // 2), F32)], 0)
    sin = jnp.concatenate([jnp.sin(ang), jnp.zeros((n_ctx, HEAD_DIM // 2), F32)], 0)
    return jnp.tile(cos, (1, 4)), jnp.concatenate([-sin, sin, -sin, sin], -1)


def _lambda_init(layer_idx):
    return 0.8 - 0.6 * float(np.exp(-0.3 * layer_idx))


def _router_params(w_group, b_group, w_router, b_router):
    d = w_group.shape[0]
    pad = LANES - N_GROUPS - N_EXPERTS
    w = jnp.concatenate([w_group, w_router, jnp.zeros((d, pad), F32)], axis=1)
    bb = jnp.concatenate([b_group, b_router, jnp.zeros((pad,), F32)])
    return w, bb.reshape(1, LANES)


def _moe_layer(f, route, w1, w3, w2, layer):
    b, n_rows, d = f.shape
    t = b * n_rows
    blk_e, slot_tok, slot_dst = _moe_plan(route.reshape(t, LANES), t)
    return _moe_call(f.reshape(t, d), blk_e, slot_tok, slot_dst, w1, w3, w2, layer)


def kernel(x, c, ctx, c_ctx, mod_w, mod_b, ln_g, ln_b, ab_w_in, ab_w_out, diff_lambda, diff_subln_g, gqa_qk_g,
           cd_w_in, cd_w_out, win_sink, na_rpb, moe_w_group, moe_b_group, moe_w_router, moe_b_router,
           moe_w1, moe_w3, moe_w2):
    b, s, d = x.shape
    n_ctx = ctx.shape[1]
    n = s + n_ctx
    depth = mod_w.shape[0]
    rows = s // GRID_W
    assert depth == 2, "layer pattern implemented: one differential/GQA layer, then one window/neighbourhood layer"
    assert s % TM == 0 and n_ctx % TM == 0 and s % TK == 0 and b + 1 <= 8
    assert n_ctx % TQ_DIFF == 0 and n_ctx % TQ_GQA == 0
    assert rows >= NA_SLAB and rows % NA_QROWS == 0 and s % GRID_W == 0
    alpha = (2.0 * depth) ** 0.25
    qw = d // 2
    kvw = qw // 4
    assert qw == 4 * LANES and kvw == LANES

    cos128, sin128 = _rope_tables(s, n_ctx)
    cvec = jnp.concatenate([c, c_ctx[None, :], jnp.zeros((8 - b - 1, d), F32)], axis=0)
    mods_all = _mods_call(cvec, mod_w, mod_b)

    def seg_mods(i):
        m = mods_all[i]
        lat = m[:b].reshape(b, 6, d)
        cm = jnp.broadcast_to(m[b].reshape(1, 6, d), (b, 6, d))
        ms = jnp.stack([lat, cm], axis=1)
        pick = lambda k, one: (ms[:, :, k] + one)[:, :, None, :]
        return pick(0, 0.0), pick(1, 1.0), pick(2, 0.0), pick(3, 0.0), pick(4, 1.0), pick(5, 0.0)

    h = jnp.concatenate([x, ctx], axis=1)

    shift1, scale1p, gate1, shift2, scale2p, gate2 = seg_mods(0)
    w_in = ab_w_in[0]
    q_d, q_g, k_d, v_d = (w_in[:, k * qw:(k + 1) * qw] for k in range(4))
    k_g, v_g = w_in[:, 4 * qw:4 * qw + kvw], w_in[:, 4 * qw + kvw:]
    w_qk = jnp.concatenate([q_d, _pair_heads(q_g, 1), k_d, k_g], axis=1).astype(BF16)
    roles0 = ((True, None, True),) * 4 + ((True, 0, True),) * 4 + ((True, None, False),) * 4 + ((True, 1, False),)
    w_vt = jnp.concatenate([v_d, v_g], axis=1).T.astype(BF16)
    g128 = jnp.tile(gqa_qk_g[0], (1, 2))
    p0, vt0 = _inproj_call(h, shift1, scale1p, w_qk, w_vt, cos128, sin128, g128, roles0, s)
    o_d = _diff_attn_call(p0, vt0, diff_lambda[0], diff_subln_g[0], s_len=s, n_ctx=n_ctx, n_heads=4,
                          q_blk0=0, k_blk0=8, v_blk0=0, lam_init=_lambda_init(0))
    o_g = _gqa_attn_call(p0, vt0, s_len=s, n_ctx=n_ctx, q_width=qw, q_col0=qw, k_blk=12, v_blk=4)
    w_out = jnp.concatenate([ab_w_out[0][:qw], _pair_heads(ab_w_out[0][qw:], 0)], axis=0).astype(BF16)
    w_route, b_route = _router_params(moe_w_group[0], moe_b_group[0], moe_w_router[0], moe_b_router[0])
    h1, f, route = _outproj_call(o_d, o_g, w_out, h, gate1, ln_g[0, 0:1], ln_b[0, 0:1], shift2, scale2p,
                                 w_route, b_route, n_rows=n, s_len=s, alpha=alpha)
    y = _moe_layer(f, route, moe_w1, moe_w3, moe_w2, 0)
    h = _combine_call(h1, y, route, gate2, ln_g[0, 1:2], ln_b[0, 1:2], s_len=s, alpha=alpha)

    shift1, scale1p, gate1, shift2, scale2p, gate2 = seg_mods(1)
    w_in = cd_w_in[0]
    q_w, q_n = w_in[:, :qw], w_in[:, qw:2 * qw]
    k_w, v_w = w_in[:, 2 * qw:2 * qw + kvw], w_in[:, 2 * qw + kvw:2 * qw + 2 * kvw]
    k_n, v_n = w_in[:, 2 * qw + 2 * kvw:3 * qw + 2 * kvw], w_in[:, 3 * qw + 2 * kvw:]
    w_qk = jnp.concatenate([_pair_heads(q_w, 1), q_n, k_w, k_n], axis=1).astype(BF16)
    roles1 = ((True, None, True),) * 4 + ((False, None, True),) * 4 + ((True, None, False),) \
        + ((False, None, False),) * 4
    w_vt = jnp.concatenate([v_w, v_n], axis=1).T.astype(BF16)
    p1, vt1 = _inproj_call(h, shift1, scale1p, w_qk, w_vt, cos128, sin128, jnp.ones((2, LANES), F32), roles1, s)
    sink_row = jnp.repeat(win_sink[0].reshape(2, 4).T.reshape(8), TQ_WIN).reshape(1, 8 * TQ_WIN)
    o_w = _win_attn_call(p1, vt1, sink_row, s_len=s, n_ctx=n_ctx, q_width=qw, k_blk=8, v_blk=0)
    o_n = _na_attn_call(p1, vt1, _na_bias_tables(na_rpb[0], rows), s_len=s, n_ctx=n_ctx, n_pairs=4,
                        q_blk0=4, k_blk0=9, v_blk0=1)
    w_out = jnp.concatenate([_pair_heads(cd_w_out[0][:qw], 0), cd_w_out[0][qw:]], axis=0).astype(BF16)
    w_route, b_route = _router_params(moe_w_group[1], moe_b_group[1], moe_w_router[1], moe_b_router[1])
    h1, f, route = _outproj_call(o_w, o_n, w_out, h, gate1, ln_g[1, 0:1], ln_b[1, 0:1], shift2, scale2p,
                                 w_route, b_route, n_rows=s, s_len=s, alpha=alpha)
    y = _moe_layer(f, route, moe_w1, moe_w3, moe_w2, 1)
    return _combine_call(h1, y, route, gate2, ln_g[1, 1:2], ln_b[1, 1:2], s_len=s, alpha=alpha)
```

```python
import functools

import numpy as np
import jax
import jax.numpy as jnp
from jax import lax
from jax.experimental import pallas as pl
from jax.experimental.pallas import tpu as pltpu

F32 = jnp.float32
BF16 = jnp.bfloat16
HIGHEST = lax.Precision.HIGHEST

GRID_W = 64
HEAD_DIM = 64
ROPE_THETA = 10000.0
WINDOW = 128
NA_ROWS_MAX = 8
NA_COLS = 16
N_GROUPS = 4
EXPERTS_PER_GROUP = 8
N_EXPERTS = N_GROUPS * EXPERTS_PER_GROUP
TOP_K = 2
MOE_BLOCK = 128
LN_EPS = 1e-5
RMS_EPS = 1e-6
ATTN_SCALE = HEAD_DIM ** -0.5
NEG_INF = -1e30
LOG2E = 1.4426950408889634
Q_SCALE = ATTN_SCALE * LOG2E

LANES = 128
PAIR = 2 * HEAD_DIM
assert PAIR == LANES

VMEM_LIMIT = 48 * 1024 * 1024

TM = 256
TQ_DIFF = 256
TQ_GQA = 256
TK = 1024
Q_STRIP = 256
TQ_WIN = 128
NA_QROWS = 4
NA_SLAB = 12


def _cparams(n_axes):
    return pltpu.CompilerParams(dimension_semantics=("arbitrary",) * n_axes, vmem_limit_bytes=VMEM_LIMIT)


def _lane_iota():
    return lax.broadcasted_iota(jnp.int32, (1, LANES), 1)


def _mods_kernel(c_ref, w_ref, b_ref, o_ref):
    c = c_ref[...]
    sc = c / (1.0 + jnp.exp(-c))
    o_ref[...] = jnp.dot(sc, w_ref[...], precision=HIGHEST, preferred_element_type=F32) + b_ref[...]


def _mods_call(cvec, mod_w, mod_b):
    depth, d, n6 = mod_w.shape
    tn = n6 // 4
    return pl.pallas_call(
        _mods_kernel,
        grid=(depth, n6 // tn),
        in_specs=[
            pl.BlockSpec((8, d), lambda l, j: (0, 0)),
            pl.BlockSpec((None, d, tn), lambda l, j: (l, 0, j)),
            pl.BlockSpec((None, 1, tn), lambda l, j: (l, 0, j)),
        ],
        out_specs=pl.BlockSpec((None, 8, tn), lambda l, j: (l, 0, j)),
        out_shape=jax.ShapeDtypeStruct((depth, 8, n6), F32),
        compiler_params=_cparams(2),
        name="mods",
    )(cvec, mod_w, mod_b.reshape(depth, 1, n6))


def _inproj_kernel(h_ref, sh_ref, sc_ref, w_ref, wvt_ref, cos_ref, sin_ref, g_ref, o_ref, vt_ref, *, roles):
    a = (h_ref[...] * sc_ref[...] + sh_ref[...]).astype(BF16)
    vt = lax.dot_general(wvt_ref[...], a, (((1,), (1,)), ((), ())), preferred_element_type=F32)
    for j in range(vt_ref.shape[0]):
        vt_ref[j] = vt[j * LANES:(j + 1) * LANES, :].astype(BF16)
    lane = _lane_iota()
    rope_first = (lane & (HEAD_DIM - 1)) < (HEAD_DIM // 2)
    lo = lane < HEAD_DIM
    n_blocks = len(roles)
    for c0 in range(0, n_blocks, 2):
        width = min(2, n_blocks - c0)
        p = jnp.dot(a, w_ref[:, c0 * LANES:(c0 + width) * LANES], preferred_element_type=F32)
        for half in range(width):
            cb = c0 + half
            rope, norm, scale = roles[cb]
            blk = p[:, half * LANES:(half + 1) * LANES]
            if norm is not None:
                sq = blk * blk
                s_lo = jnp.sum(jnp.where(lo, sq, 0.0), axis=-1, keepdims=True)
                s_hi = jnp.sum(jnp.where(lo, 0.0, sq), axis=-1, keepdims=True)
                ms = jnp.where(lo, s_lo, s_hi) * (1.0 / HEAD_DIM)
                blk = blk * lax.rsqrt(ms + RMS_EPS) * g_ref[norm:norm + 1, :]
            if rope:
                partner = jnp.where(rope_first, pltpu.roll(blk, LANES - HEAD_DIM // 2, 1),
                                    pltpu.roll(blk, HEAD_DIM // 2, 1))
                blk = blk * cos_ref[...] + partner * sin_ref[...]
            if scale:
                blk = blk * Q_SCALE
            o_ref[:, cb * LANES:(cb + 1) * LANES] = blk.astype(BF16)


def _inproj_call(h, shift, scale1p, w16, wvt16, cos128, sin128, g128, roles, s_len):
    b, n, d = h.shape
    ncol = w16.shape[1]
    nvb = wvt16.shape[0] // LANES
    n_lat = s_len // TM
    seg = lambda bb, j: (bb, jnp.where(j >= n_lat, 1, 0), 0, 0)
    return pl.pallas_call(
        functools.partial(_inproj_kernel, roles=roles),
        grid=(b, n // TM),
        in_specs=[
            pl.BlockSpec((None, TM, d), lambda bb, j: (bb, j, 0)),
            pl.BlockSpec((None, None, 1, d), seg),
            pl.BlockSpec((None, None, 1, d), seg),
            pl.BlockSpec((d, ncol), lambda bb, j: (0, 0)),
            pl.BlockSpec((nvb * LANES, d), lambda bb, j: (0, 0)),
            pl.BlockSpec((TM, LANES), lambda bb, j: (j, 0)),
            pl.BlockSpec((TM, LANES), lambda bb, j: (j, 0)),
            pl.BlockSpec((2, LANES), lambda bb, j: (0, 0)),
        ],
        out_specs=[
            pl.BlockSpec((None, TM, ncol), lambda bb, j: (bb, j, 0)),
            pl.BlockSpec((None, nvb, LANES, TM), lambda bb, j: (bb, 0, 0, j)),
        ],
        out_shape=[
            jax.ShapeDtypeStruct((b, n, ncol), BF16),
            jax.ShapeDtypeStruct((b, nvb, LANES, n), BF16),
        ],
        compiler_params=_cparams(2),
        name="inproj",
    )(h, shift, scale1p, w16, wvt16, cos128, sin128, g128)


def _stack_q(q_ref, n_qblocks):
    lo = _lane_iota() < HEAD_DIM
    parts = []
    for j in range(n_qblocks):
        q = q_ref[:, j * LANES:(j + 1) * LANES]
        zero = jnp.zeros_like(q)
        parts.append(jnp.where(lo, q, zero))
        parts.append(jnp.where(lo, zero, q))
    return jnp.concatenate(parts, axis=0)


def _scores_t(k, qs):
    return lax.dot_general(k, qs, (((1,), (1,)), ((), ())), preferred_element_type=F32)


def _flash_init(m_ref, l_ref, acc_ref):
    m_ref[...] = jnp.full(m_ref.shape, NEG_INF, F32)
    l_ref[...] = jnp.zeros(l_ref.shape, F32)
    acc_ref[...] = jnp.zeros(acc_ref.shape, F32)


def _flash_update(st, vt, m_ref, l_ref, acc_ref):
    m_prev = m_ref[...]
    m_new = jnp.maximum(m_prev, jnp.max(st, axis=0, keepdims=True))
    alpha = jnp.exp2(m_prev - m_new)
    pt = jnp.exp2(st - m_new)
    l_ref[...] = alpha * l_ref[...] + jnp.sum(pt, axis=0, keepdims=True)
    acc_ref[...] = alpha * acc_ref[...] + jnp.dot(vt, pt.astype(BF16), preferred_element_type=F32)
    m_ref[...] = m_new


def _flash_sweep(qs, k_ref, vt_ref, m_ref, l_ref, acc_ref, s_ref, *, is_lat, s_len, n_ctx):
    _flash_init(m_ref, l_ref, acc_ref)
    n_chunks = s_len // TK
    assert n_chunks % 2 == 0 and n_chunks >= 2

    def k_chunk(c):
        if isinstance(c, int):
            return k_ref[c * TK:(c + 1) * TK, :]
        return k_ref[pl.ds(pl.multiple_of(c * TK, TK), TK), :]

    def vt_chunk(c):
        if isinstance(c, int):
            return vt_ref[:, c * TK:(c + 1) * TK]
        return vt_ref[:, pl.ds(pl.multiple_of(c * TK, TK), TK)]

    k_ctx = k_ref[s_len:s_len + n_ctx, :]
    vt_ctx = vt_ref[:, s_len:s_len + n_ctx]

    @pl.when(is_lat)
    def _():
        s_ref[...] = _scores_t(k_chunk(0), qs)

        def body(j, carry):
            c = 2 * j
            s_odd = _scores_t(k_chunk(c + 1), qs)
            _flash_update(s_ref[...], vt_chunk(c), m_ref, l_ref, acc_ref)
            s_ref[...] = _scores_t(k_chunk(c + 2), qs)
            _flash_update(s_odd, vt_chunk(c + 1), m_ref, l_ref, acc_ref)
            return carry
        lax.fori_loop(0, n_chunks // 2 - 1, body, 0)

        c = n_chunks - 2
        s_odd = _scores_t(k_chunk(c + 1), qs)
        _flash_update(s_ref[...], vt_chunk(c), m_ref, l_ref, acc_ref)
        s_last = _scores_t(k_ctx, qs)
        _flash_update(s_odd, vt_chunk(c + 1), m_ref, l_ref, acc_ref)
        _flash_update(s_last, vt_ctx, m_ref, l_ref, acc_ref)

    @pl.when(jnp.logical_not(is_lat))
    def _():
        _flash_update(_scores_t(k_ctx, qs), vt_ctx, m_ref, l_ref, acc_ref)


def _merge_pairs(ot, tq, n_qblocks, o_ref):
    for j in range(n_qblocks):
        o_lo = ot[:HEAD_DIM, (2 * j) * tq:(2 * j + 1) * tq]
        o_hi = ot[HEAD_DIM:, (2 * j + 1) * tq:(2 * j + 2) * tq]
        blk = jnp.concatenate([o_lo, o_hi], axis=0)
        o_ref[:, j * LANES:(j + 1) * LANES] = blk.T.astype(o_ref.dtype)


def _diff_attn_kernel(q_ref, k_ref, vt_ref, lam_ref, g_ref, o_ref, m_ref, l_ref, acc_ref, s_ref, *,
                      s_len, n_ctx, lam_init):
    tq = q_ref.shape[0]
    i = pl.program_id(2)
    qs = _stack_q(q_ref, 1)
    _flash_sweep(qs, k_ref, vt_ref, m_ref, l_ref, acc_ref, s_ref, is_lat=i < s_len // tq, s_len=s_len, n_ctx=n_ctx)
    lv = lam_ref[...]
    lam = (jnp.exp(jnp.sum(lv[0:1] * lv[1:2], axis=-1, keepdims=True))
           - jnp.exp(jnp.sum(lv[2:3] * lv[3:4], axis=-1, keepdims=True)) + lam_init)
    ot = acc_ref[...] / l_ref[...]
    od = ot[:, :tq] - lam * ot[:, tq:]
    ms = jnp.mean(od * od, axis=0, keepdims=True)
    od = od * lax.rsqrt(ms + RMS_EPS) * g_ref[...] * (1.0 - lam_init)
    o_ref[...] = od.T.astype(o_ref.dtype)


def _diff_attn_call(p, vt, lam_vecs, subln_g, *, s_len, n_ctx, n_heads, q_blk0, k_blk0, v_blk0, lam_init):
    b, n, _ = p.shape
    tq = TQ_DIFF
    return pl.pallas_call(
        functools.partial(_diff_attn_kernel, s_len=s_len, n_ctx=n_ctx, lam_init=lam_init),
        grid=(b, n_heads, n // tq),
        in_specs=[
            pl.BlockSpec((None, tq, LANES), lambda bb, hh, i: (bb, i, q_blk0 + hh)),
            pl.BlockSpec((None, n, LANES), lambda bb, hh, i: (bb, 0, k_blk0 + hh)),
            pl.BlockSpec((None, None, LANES, n), lambda bb, hh, i: (bb, v_blk0 + hh, 0, 0)),
            pl.BlockSpec((4, HEAD_DIM), lambda bb, hh, i: (0, 0)),
            pl.BlockSpec((LANES, 1), lambda bb, hh, i: (0, 0)),
        ],
        out_specs=pl.BlockSpec((None, tq, LANES), lambda bb, hh, i: (bb, i, hh)),
        out_shape=jax.ShapeDtypeStruct((b, n, n_heads * LANES), BF16),
        scratch_shapes=[pltpu.VMEM((1, 2 * tq), F32), pltpu.VMEM((1, 2 * tq), F32), pltpu.VMEM((LANES, 2 * tq), F32),
                        pltpu.VMEM((TK, 2 * tq), F32)],
        compiler_params=_cparams(3),
        name="diff_attn",
    )(p, p, vt, lam_vecs, subln_g.reshape(LANES, 1))


def _gqa_attn_kernel(q_ref, k_ref, vt_ref, o_ref, m_ref, l_ref, acc_ref, s_ref, *, s_len, n_ctx):
    tq = q_ref.shape[0]
    n_qblocks = q_ref.shape[1] // LANES
    i = pl.program_id(1)
    qs = _stack_q(q_ref, n_qblocks)
    _flash_sweep(qs, k_ref, vt_ref, m_ref, l_ref, acc_ref, s_ref, is_lat=i < s_len // tq, s_len=s_len, n_ctx=n_ctx)
    _merge_pairs(acc_ref[...] / l_ref[...], tq, n_qblocks, o_ref)


def _gqa_attn_call(p, vt, *, s_len, n_ctx, q_width, q_col0, k_blk, v_blk):
    b, n, _ = p.shape
    tq = TQ_GQA
    m_rows = 2 * (q_width // LANES) * tq
    return pl.pallas_call(
        functools.partial(_gqa_attn_kernel, s_len=s_len, n_ctx=n_ctx),
        grid=(b, n // tq),
        in_specs=[
            pl.BlockSpec((None, tq, q_width), lambda bb, i: (bb, i, q_col0 // q_width)),
            pl.BlockSpec((None, n, LANES), lambda bb, i: (bb, 0, k_blk)),
            pl.BlockSpec((None, None, LANES, n), lambda bb, i: (bb, v_blk, 0, 0)),
        ],
        out_specs=pl.BlockSpec((None, tq, q_width), lambda bb, i: (bb, i, 0)),
        out_shape=jax.ShapeDtypeStruct((b, n, q_width), BF16),
        scratch_shapes=[pltpu.VMEM((1, m_rows), F32), pltpu.VMEM((1, m_rows), F32), pltpu.VMEM((LANES, m_rows), F32),
                        pltpu.VMEM((TK, m_rows), F32)],
        compiler_params=_cparams(2),
        name="gqa_attn",
    )(p, p, vt)


def _win_attn_kernel(q_ref, k_ref, vt_ref, sink_ref, o_ref, *, s_len, n_ctx):
    tq = q_ref.shape[0]
    n_qblocks = q_ref.shape[1] // LANES
    span = tq + 2 * WINDOW
    i = pl.program_id(1)
    ws = pl.multiple_of(jnp.clip(i * tq - WINDOW, 0, s_len - span), LANES)
    qs = _stack_q(q_ref, n_qblocks)
    m_rows = qs.shape[0]
    s_loc = _scores_t(k_ref[pl.ds(ws, span), :], qs)
    kpos = ws + lax.broadcasted_iota(jnp.int32, (span, m_rows), 0)
    qpos = i * tq + (lax.broadcasted_iota(jnp.int32, (span, m_rows), 1) & (tq - 1))
    s_loc = jnp.where(jnp.abs(qpos - kpos) <= WINDOW, s_loc, NEG_INF)
    s_ctx = _scores_t(k_ref[s_len:s_len + n_ctx, :], qs)
    sink = sink_ref[...] * LOG2E
    m = jnp.maximum(jnp.maximum(jnp.max(s_loc, axis=0, keepdims=True), jnp.max(s_ctx, axis=0, keepdims=True)), sink)
    p_loc = jnp.exp2(s_loc - m)
    p_ctx = jnp.exp2(s_ctx - m)
    denom = jnp.sum(p_loc, axis=0, keepdims=True) + jnp.sum(p_ctx, axis=0, keepdims=True) + jnp.exp2(sink - m)
    acc = (jnp.dot(vt_ref[:, pl.ds(ws, span)], p_loc.astype(BF16), preferred_element_type=F32)
           + jnp.dot(vt_ref[:, s_len:s_len + n_ctx], p_ctx.astype(BF16), preferred_element_type=F32))
    _merge_pairs(acc / denom, tq, n_qblocks, o_ref)


def _win_attn_call(p, vt, sink_row, *, s_len, n_ctx, q_width, k_blk, v_blk):
    b, n, _ = p.shape
    tq = TQ_WIN
    m_rows = sink_row.shape[1]
    return pl.pallas_call(
        functools.partial(_win_attn_kernel, s_len=s_len, n_ctx=n_ctx),
        grid=(b, s_len // tq),
        in_specs=[
            pl.BlockSpec((None, tq, q_width), lambda bb, i: (bb, i, 0)),
            pl.BlockSpec((None, n, LANES), lambda bb, i: (bb, 0, k_blk)),
            pl.BlockSpec((None, None, LANES, n), lambda bb, i: (bb, v_blk, 0, 0)),
            pl.BlockSpec((1, m_rows), lambda bb, i: (0, 0)),
        ],
        out_specs=pl.BlockSpec((None, tq, q_width), lambda bb, i: (bb, i, 0)),
        out_shape=jax.ShapeDtypeStruct((b, s_len, q_width), BF16),
        compiler_params=_cparams(2),
        name="win_attn",
    )(p, p, vt, sink_row)


def _na_attn_kernel(q_ref, k_ref, vt_ref, bias_ref, o_ref, *, s_len, n_ctx):
    tq = q_ref.shape[0]
    slab = NA_SLAB * GRID_W
    rows = s_len // GRID_W
    mi = pl.program_id(2)
    ss = pl.multiple_of(jnp.clip(mi - 1, 0, (rows - NA_SLAB) // NA_QROWS) * (NA_QROWS * GRID_W), NA_QROWS * GRID_W)
    qs = _stack_q(q_ref, 1)
    s_nb = _scores_t(k_ref[pl.ds(ss, slab), :], qs) + bias_ref[...]
    s_cx = _scores_t(k_ref[s_len:s_len + n_ctx, :], qs)
    m = jnp.maximum(jnp.max(s_nb, axis=0, keepdims=True), jnp.max(s_cx, axis=0, keepdims=True))
    p_nb = jnp.exp2(s_nb - m)
    p_cx = jnp.exp2(s_cx - m)
    denom = jnp.sum(p_nb, axis=0, keepdims=True) + jnp.sum(p_cx, axis=0, keepdims=True)
    acc = (jnp.dot(vt_ref[:, pl.ds(ss, slab)], p_nb.astype(BF16), preferred_element_type=F32)
           + jnp.dot(vt_ref[:, s_len:s_len + n_ctx], p_cx.astype(BF16), preferred_element_type=F32))
    _merge_pairs(acc / denom, tq, 1, o_ref)


def _na_attn_call(p, vt, bias_tab, *, s_len, n_ctx, n_pairs, q_blk0, k_blk0, v_blk0):
    b, n, _ = p.shape
    tq = NA_QROWS * GRID_W
    n_steps = s_len // tq
    tbl = lambda mi: jnp.where(mi == 0, 0, jnp.where(mi == n_steps - 1, 2, 1))
    return pl.pallas_call(
        functools.partial(_na_attn_kernel, s_len=s_len, n_ctx=n_ctx),
        grid=(b, n_pairs, n_steps),
        in_specs=[
            pl.BlockSpec((None, tq, LANES), lambda bb, j, mi: (bb, mi, q_blk0 + j)),
            pl.BlockSpec((None, n, LANES), lambda bb, j, mi: (bb, 0, k_blk0 + j)),
            pl.BlockSpec((None, None, LANES, n), lambda bb, j, mi: (bb, v_blk0 + j, 0, 0)),
            pl.BlockSpec((None, None, NA_SLAB * GRID_W, 2 * tq), lambda bb, j, mi: (tbl(mi), j, 0, 0)),
        ],
        out_specs=pl.BlockSpec((None, tq, LANES), lambda bb, j, mi: (bb, mi, j)),
        out_shape=jax.ShapeDtypeStruct((b, s_len, n_pairs * LANES), BF16),
        compiler_params=_cparams(3),
        name="na_attn",
    )(p, p, vt, bias_tab)


def _na_bias_tables(rpb, rows):
    n_heads = rpb.shape[0]
    rpb = rpb.astype(F32)
    pad = GRID_W - NA_COLS
    rpb_p = jnp.pad(rpb, ((0, 0), (0, 0), (pad, pad)))
    col_tab = jnp.stack([rpb_p[:, :, pad + NA_COLS - 1 - qc: pad + NA_COLS - 1 - qc + GRID_W] for qc in range(GRID_W)],
                        axis=2)
    qr = np.arange(NA_QROWS)[:, None, None, None]
    qc = np.arange(GRID_W)[None, :, None, None]
    kr = np.arange(NA_SLAB)[None, None, :, None]
    kc = np.arange(GRID_W)[None, None, None, :]
    full = (NA_QROWS, GRID_W, NA_SLAB, GRID_W)
    flat = (NA_QROWS * GRID_W, NA_SLAB * GRID_W)
    tabs = []
    for q0, s0 in ((0, 0), (NA_QROWS, 0), (rows - NA_QROWS, rows - NA_SLAB)):
        q_row = q0 + qr
        k_row = s0 + kr
        r0 = np.clip(q_row - NA_ROWS_MAX // 2, 0, rows - NA_ROWS_MAX)
        c0 = np.clip(qc - NA_COLS // 2, 0, GRID_W - NA_COLS)
        valid = (k_row >= r0) & (k_row < r0 + NA_ROWS_MAX) & (kc >= c0) & (kc < c0 + NA_COLS)
        valid = np.broadcast_to(valid, full).reshape(flat)
        r_off = np.clip(k_row - q_row + NA_ROWS_MAX - 1, 0, 2 * NA_ROWS_MAX - 2)[:, 0, :, 0]
        bias = jnp.stack([jnp.stack([col_tab[:, int(r_off[a, c])] for c in range(NA_SLAB)], axis=2)
                          for a in range(NA_QROWS)], axis=1)
        tabs.append(jnp.where(valid[None], bias.reshape((n_heads,) + flat) * LOG2E, NEG_INF))
    tab = jnp.stack(tabs)
    return jnp.swapaxes(tab.reshape(3, n_heads // 2, 2 * flat[0], flat[1]), -1, -2)


def _layer_norm(y, g, b):
    mu = jnp.mean(y, axis=-1, keepdims=True)
    yc = y - mu
    var = jnp.mean(yc * yc, axis=-1, keepdims=True)
    return yc * lax.rsqrt(var + LN_EPS) * g + b


def _route(logits):
    lane = _lane_iota().astype(F32)
    big = float(LANES)
    is_g = lane < N_GROUPS
    gl = jnp.where(is_g, logits, NEG_INF)
    g_max = jnp.max(gl, axis=-1, keepdims=True)
    g_idx = jnp.min(jnp.where(gl == g_max, lane, big), axis=-1, keepdims=True)
    g_w = 1.0 / jnp.sum(jnp.where(is_g, jnp.exp(gl - g_max), 0.0), axis=-1, keepdims=True)
    base = N_GROUPS + EXPERTS_PER_GROUP * g_idx
    el = jnp.where((lane >= base) & (lane < base + EXPERTS_PER_GROUP), logits, NEG_INF)
    v1 = jnp.max(el, axis=-1, keepdims=True)
    i1 = jnp.min(jnp.where(el == v1, lane, big), axis=-1, keepdims=True)
    el2 = jnp.where(lane == i1, NEG_INF, el)
    v2 = jnp.max(el2, axis=-1, keepdims=True)
    i2 = jnp.min(jnp.where(el2 == v2, lane, big), axis=-1, keepdims=True)
    t = jnp.exp(v2 - v1)
    w1 = g_w / (1.0 + t)
    w2 = g_w * t / (1.0 + t)
    return jnp.where(lane == 0, i1 - N_GROUPS,
                     jnp.where(lane == 1, i2 - N_GROUPS, jnp.where(lane == 2, w1, jnp.where(lane == 3, w2, 0.0))))


def _outproj_kernel(oa_ref, ob_ref, w_ref, h_ref, gate_ref, lng_ref, lnb_ref, sh_ref, sc_ref, wr_ref, br_ref,
                    h1_ref, f_ref, route_ref, *, alpha):
    ka = oa_ref.shape[1]
    o = (jnp.dot(oa_ref[...], w_ref[:ka, :], preferred_element_type=F32)
         + jnp.dot(ob_ref[...], w_ref[ka:, :], preferred_element_type=F32))
    h1 = _layer_norm(alpha * h_ref[...] + gate_ref[...] * o, lng_ref[...], lnb_ref[...])
    h1_ref[...] = h1
    f = h1 * sc_ref[...] + sh_ref[...]
    f_ref[...] = f
    logits = jnp.dot(f, wr_ref[...], precision=HIGHEST, preferred_element_type=F32) + br_ref[...]
    route_ref[...] = _route(logits)


def _outproj_call(oa, ob, w16, h, gate, ln_g, ln_b, shift, scale1p, w_route, b_route, *, n_rows, s_len, alpha):
    b, _, d = h.shape
    ka, kb = oa.shape[2], ob.shape[2]
    n_lat = s_len // TM
    seg = lambda bb, j: (bb, jnp.where(j >= n_lat, 1, 0), 0, 0)
    row = lambda bb, j: (bb, j, 0)
    const = lambda bb, j: (0, 0)
    return pl.pallas_call(
        functools.partial(_outproj_kernel, alpha=alpha),
        grid=(b, n_rows // TM),
        in_specs=[
            pl.BlockSpec((None, TM, ka), row),
            pl.BlockSpec((None, TM, kb), row),
            pl.BlockSpec((ka + kb, d), const),
            pl.BlockSpec((None, TM, d), row),
            pl.BlockSpec((None, None, 1, d), seg),
            pl.BlockSpec((1, d), const),
            pl.BlockSpec((1, d), const),
            pl.BlockSpec((None, None, 1, d), seg),
            pl.BlockSpec((None, None, 1, d), seg),
            pl.BlockSpec((d, LANES), const),
            pl.BlockSpec((1, LANES), const),
        ],
        out_specs=[
            pl.BlockSpec((None, TM, d), row),
            pl.BlockSpec((None, TM, d), row),
            pl.BlockSpec((None, TM, LANES), row),
        ],
        out_shape=[
            jax.ShapeDtypeStruct((b, n_rows, d), F32),
            jax.ShapeDtypeStruct((b, n_rows, d), F32),
            jax.ShapeDtypeStruct((b, n_rows, LANES), F32),
        ],
        compiler_params=_cparams(2),
        name="outproj",
    )(oa, ob, w16, h, gate, ln_g, ln_b, shift, scale1p, w_route, b_route)


def _row_copy(src, src_row, dst, dst_row, sem):
    return pltpu.make_async_copy(src.at[pl.ds(src_row, 1), :], dst.at[pl.ds(dst_row, 1), :], sem)


def _block_copy_for_wait(src, dst, sem):
    return pltpu.make_async_copy(src, dst, sem)


MOE_BUFS = 3


def _moe_kernel(blk_e_ref, tok0_ref, tok1_ref, tok_ahead_ref, dst_prev_ref, dst_last_ref, f_hbm,
                w1_ref, w3_ref, w2_ref, y_hbm, xbuf, ybuf, w1b, w3b, w2b, gsem, ssem):
    i = pl.program_id(0)
    last = pl.num_programs(0) - 1
    cur = lax.rem(i, MOE_BUFS)
    ahead = lax.rem(i + 2, MOE_BUFS)

    def wait_gather(s):
        _block_copy_for_wait(f_hbm.at[pl.ds(0, MOE_BLOCK), :], xbuf.at[s], gsem.at[s]).wait()

    def wait_scatter(s):
        _block_copy_for_wait(ybuf.at[s], y_hbm.at[pl.ds(0, MOE_BLOCK), :], ssem.at[s]).wait()

    def start_gather(tok_ref, s):
        for r in range(MOE_BLOCK):
            _row_copy(f_hbm, tok_ref[0, r], xbuf.at[s], r, gsem.at[s]).start()

    def start_scatter(dst_ref, s):
        for r in range(MOE_BLOCK):
            _row_copy(ybuf.at[s], r, y_hbm, dst_ref[0, r], ssem.at[s]).start()

    @pl.when(i == 0)
    def _():
        ybuf[...] = jnp.zeros(ybuf.shape, ybuf.dtype)
        start_gather(tok0_ref, 0)
        start_gather(tok1_ref, 1)

    wait_gather(cur)

    @pl.when(i >= 2)
    def _():
        wait_scatter(cur)

    @pl.when(jnp.logical_or(i == 0, blk_e_ref[i] != blk_e_ref[jnp.maximum(i - 1, 0)]))
    def _():
        w1b[...] = w1_ref[...].astype(BF16)
        w3b[...] = w3_ref[...].astype(BF16)
        w2b[...] = w2_ref[...].astype(BF16)

    x = xbuf[cur].astype(BF16)
    start_gather(tok_ahead_ref, ahead)
    start_scatter(dst_prev_ref, ahead)
    h1 = jnp.dot(x, w1b[...], preferred_element_type=F32)
    h3 = jnp.dot(x, w3b[...], preferred_element_type=F32)
    act = (h1 / (1.0 + jnp.exp(-h1))) * h3
    ybuf[cur] = jnp.dot(act.astype(BF16), w2b[...], preferred_element_type=F32)

    @pl.when(i == last)
    def _():
        start_scatter(dst_last_ref, cur)
        for s in range(MOE_BUFS):
            wait_scatter(s)
        wait_gather(lax.rem(i + 1, MOE_BUFS))
        wait_gather(ahead)


def _moe_call(f_rows, blk_e, slot_tok, slot_dst, w1, w3, w2, layer):
    t, d = f_rows.shape
    n_blocks = blk_e.shape[0]
    n_slots = n_blocks * MOE_BLOCK
    de = w1.shape[3]
    assert n_blocks >= MOE_BUFS
    smem_block = lambda index_map: pl.BlockSpec((None, 1, MOE_BLOCK), index_map, memory_space=pltpu.SMEM)
    grid_spec = pltpu.PrefetchScalarGridSpec(
        num_scalar_prefetch=1,
        grid=(n_blocks,),
        in_specs=[
            smem_block(lambda i, be: (0, 0, 0)),
            smem_block(lambda i, be: (1, 0, 0)),
            smem_block(lambda i, be: (jnp.minimum(i + 2, n_blocks - 1), 0, 0)),
            smem_block(lambda i, be: (i, 0, 0)),
            smem_block(lambda i, be: (i + 1, 0, 0)),
            pl.BlockSpec(memory_space=pl.ANY),
            pl.BlockSpec((None, None, d, de), lambda i, be: (layer, be[i], 0, 0)),
            pl.BlockSpec((None, None, d, de), lambda i, be: (layer, be[i], 0, 0)),
            pl.BlockSpec((None, None, de, d), lambda i, be: (layer, be[i], 0, 0)),
        ],
        out_specs=pl.BlockSpec(memory_space=pl.ANY),
        scratch_shapes=[
            pltpu.VMEM((MOE_BUFS, MOE_BLOCK, d), F32),
            pltpu.VMEM((MOE_BUFS, MOE_BLOCK, d), F32),
            pltpu.VMEM((d, de), BF16),
            pltpu.VMEM((d, de), BF16),
            pltpu.VMEM((de, d), BF16),
            pltpu.SemaphoreType.DMA((MOE_BUFS,)),
            pltpu.SemaphoreType.DMA((MOE_BUFS,)),
        ],
    )
    tok3 = slot_tok.reshape(n_blocks, 1, MOE_BLOCK)
    dst_ext = jnp.concatenate([n_slots + jnp.arange(MOE_BLOCK, dtype=jnp.int32), slot_dst])
    dst3 = dst_ext.reshape(n_blocks + 1, 1, MOE_BLOCK)
    return pl.pallas_call(
        _moe_kernel,
        grid_spec=grid_spec,
        out_shape=jax.ShapeDtypeStruct((n_slots + MOE_BLOCK, d), F32),
        compiler_params=_cparams(1),
        name="moe_experts",
    )(blk_e, tok3, tok3, tok3, dst3, dst3, f_rows, w1, w3, w2)


def _moe_plan(route, t):
    n_assign = TOP_K * t
    e_flat = route[:, :TOP_K].astype(jnp.int32).reshape(n_assign)
    order = jnp.argsort(e_flat, stable=True).astype(jnp.int32)
    counts = jnp.sum((e_flat[:, None] == jnp.arange(N_EXPERTS, dtype=jnp.int32)[None, :]).astype(jnp.int32), axis=0)
    starts = jnp.cumsum(counts) - counts
    padded = (counts + MOE_BLOCK - 1) // MOE_BLOCK * MOE_BLOCK
    p_ends = jnp.cumsum(padded)
    p_starts = p_ends - padded
    n_blocks = (n_assign + MOE_BLOCK - 1) // MOE_BLOCK + N_EXPERTS
    blk_start = jnp.arange(n_blocks, dtype=jnp.int32) * MOE_BLOCK
    blk_e = jnp.minimum(jnp.sum((p_ends[None, :] <= blk_start[:, None]).astype(jnp.int32), axis=1), N_EXPERTS - 1)
    slot = jnp.arange(n_blocks * MOE_BLOCK, dtype=jnp.int32)
    e_s = jnp.repeat(blk_e, MOE_BLOCK)
    j = slot - p_starts[e_s]
    valid = j < counts[e_s]
    a_idx = order[jnp.clip(starts[e_s] + j, 0, n_assign - 1)]
    slot_tok = jnp.where(valid, a_idx // TOP_K, 0).astype(jnp.int32)
    spare = n_assign + jnp.cumsum(jnp.logical_not(valid).astype(jnp.int32)) - 1
    slot_dst = jnp.where(valid, (a_idx % TOP_K) * t + a_idx // TOP_K, spare).astype(jnp.int32)
    return blk_e, slot_tok, slot_dst


def _combine_kernel(h_ref, y0_ref, y1_ref, route_ref, gate_ref, lng_ref, lnb_ref, o_ref, *, alpha):
    r = route_ref[...]
    y = r[:, 2:3] * y0_ref[...] + r[:, 3:4] * y1_ref[...]
    o_ref[...] = _layer_norm(alpha * h_ref[...] + gate_ref[...] * y, lng_ref[...], lnb_ref[...])


def _combine_call(h1, y, route, gate, ln_g, ln_b, *, s_len, alpha):
    b, n_rows, d = h1.shape
    n_lat = s_len // TM
    seg = lambda bb, j: (bb, jnp.where(j >= n_lat, 1, 0), 0, 0)
    row = lambda bb, j: (bb, j, 0)
    const = lambda bb, j: (0, 0)
    blocks_per_batch = n_rows // TM
    blocks_per_k = b * blocks_per_batch
    return pl.pallas_call(
        functools.partial(_combine_kernel, alpha=alpha),
        grid=(b, n_rows // TM),
        in_specs=[
            pl.BlockSpec((None, TM, d), row),
            pl.BlockSpec((TM, d), lambda bb, j: (bb * blocks_per_batch + j, 0)),
            pl.BlockSpec((TM, d), lambda bb, j: (blocks_per_k + bb * blocks_per_batch + j, 0)),
            pl.BlockSpec((None, TM, LANES), row),
            pl.BlockSpec((None, None, 1, d), seg),
            pl.BlockSpec((1, d), const),
            pl.BlockSpec((1, d), const),
        ],
        out_specs=pl.BlockSpec((None, TM, d), row),
        out_shape=jax.ShapeDtypeStruct((b, n_rows, d), F32),
        compiler_params=_cparams(2),
        name="combine",
    )(h1, y, y, route, gate, ln_g, ln_b)


def _pair_heads(w, axis):
    w = jnp.moveaxis(w, axis, -1)
    lead = w.shape[:-1]
    n_heads = w.shape[-1] // HEAD_DIM
    w = w.reshape(lead + (2, n_heads // 2, HEAD_DIM)).swapaxes(-3, -2).reshape(lead + (n_heads * HEAD_DIM,))
    return jnp.moveaxis(w, -1, axis)


def _rope_tables(s_len, n_ctx):
    t = jnp.arange(s_len, dtype=jnp.int32)
    row = (t // GRID_W).astype(F32)
    col = (t % GRID_W).astype(F32)
    n_freq = HEAD_DIM // 4
    inv = ROPE_THETA ** (-jnp.arange(n_freq, dtype=F32) / n_freq)
    ang = jnp.concatenate([row[:, None] * inv, col[:, None] * inv], -1)
    cos = jnp.concatenate([jnp.cos(ang), jnp.ones((n_ctx, HEAD_DIM // 2), F32)], 0)
    sin = jnp.concatenate([jnp.sin(ang), jnp.zeros((n_ctx, HEAD_DIM // 2), F32)], 0)
    return jnp.tile(cos, (1, 4)), jnp.concatenate([-sin, sin, -sin, sin], -1)


def _lambda_init(layer_idx):
    return 0.8 - 0.6 * float(np.exp(-0.3 * layer_idx))


def _router_params(w_group, b_group, w_router, b_router):
    d = w_group.shape[0]
    pad = LANES - N_GROUPS - N_EXPERTS
    w = jnp.concatenate([w_group, w_router, jnp.zeros((d, pad), F32)], axis=1)
    bb = jnp.concatenate([b_group, b_router, jnp.zeros((pad,), F32)])
    return w, bb.reshape(1, LANES)


def _moe_layer(f, route, w1, w3, w2, layer):
    b, n_rows, d = f.shape
    t = b * n_rows
    blk_e, slot_tok, slot_dst = _moe_plan(route.reshape(t, LANES), t)
    return _moe_call(f.reshape(t, d), blk_e, slot_tok, slot_dst, w1, w3, w2, layer)


def kernel(x, c, ctx, c_ctx, mod_w, mod_b, ln_g, ln_b, ab_w_in, ab_w_out, diff_lambda, diff_subln_g, gqa_qk_g,
           cd_w_in, cd_w_out, win_sink, na_rpb, moe_w_group, moe_b_group, moe_w_router, moe_b_router,
           moe_w1, moe_w3, moe_w2):
    b, s, d = x.shape
    n_ctx = ctx.shape[1]
    n = s + n_ctx
    depth = mod_w.shape[0]
    rows = s // GRID_W
    assert depth == 2, "layer pattern implemented: one differential/GQA layer, then one window/neighbourhood layer"
    assert s % TM == 0 and n_ctx % TM == 0 and s % TK == 0 and b + 1 <= 8
    assert n_ctx % TQ_DIFF == 0 and n_ctx % TQ_GQA == 0
    assert rows >= NA_SLAB and rows % NA_QROWS == 0 and s % GRID_W == 0
    alpha = (2.0 * depth) ** 0.25
    qw = d // 2
    kvw = qw // 4
    assert qw == 4 * LANES and kvw == LANES

    cos128, sin128 = _rope_tables(s, n_ctx)
    cvec = jnp.concatenate([c, c_ctx[None, :], jnp.zeros((8 - b - 1, d), F32)], axis=0)
    mods_all = _mods_call(cvec, mod_w, mod_b)

    def seg_mods(i):
        m = mods_all[i]
        lat = m[:b].reshape(b, 6, d)
        cm = jnp.broadcast_to(m[b].reshape(1, 6, d), (b, 6, d))
        ms = jnp.stack([lat, cm], axis=1)
        pick = lambda k, one: (ms[:, :, k] + one)[:, :, None, :]
        return pick(0, 0.0), pick(1, 1.0), pick(2, 0.0), pick(3, 0.0), pick(4, 1.0), pick(5, 0.0)

    h = jnp.concatenate([x, ctx], axis=1)

    shift1, scale1p, gate1, shift2, scale2p, gate2 = seg_mods(0)
    w_in = ab_w_in[0]
    q_d, q_g, k_d, v_d = (w_in[:, k * qw:(k + 1) * qw] for k in range(4))
    k_g, v_g = w_in[:, 4 * qw:4 * qw + kvw], w_in[:, 4 * qw + kvw:]
    w_qk = jnp.concatenate([q_d, _pair_heads(q_g, 1), k_d, k_g], axis=1).astype(BF16)
    roles0 = ((True, None, True),) * 4 + ((True, 0, True),) * 4 + ((True, None, False),) * 4 + ((True, 1, False),)
    w_vt = jnp.concatenate([v_d, v_g], axis=1).T.astype(BF16)
    g128 = jnp.tile(gqa_qk_g[0], (1, 2))
    p0, vt0 = _inproj_call(h, shift1, scale1p, w_qk, w_vt, cos128, sin128, g128, roles0, s)
    o_d = _diff_attn_call(p0, vt0, diff_lambda[0], diff_subln_g[0], s_len=s, n_ctx=n_ctx, n_heads=4,
                          q_blk0=0, k_blk0=8, v_blk0=0, lam_init=_lambda_init(0))
    o_g = _gqa_attn_call(p0, vt0, s_len=s, n_ctx=n_ctx, q_width=qw, q_col0=qw, k_blk=12, v_blk=4)
    w_out = jnp.concatenate([ab_w_out[0][:qw], _pair_heads(ab_w_out[0][qw:], 0)], axis=0).astype(BF16)
    w_route, b_route = _router_params(moe_w_group[0], moe_b_group[0], moe_w_router[0], moe_b_router[0])
    h1, f, route = _outproj_call(o_d, o_g, w_out, h, gate1, ln_g[0, 0:1], ln_b[0, 0:1], shift2, scale2p,
                                 w_route, b_route, n_rows=n, s_len=s, alpha=alpha)
    y = _moe_layer(f, route, moe_w1, moe_w3, moe_w2, 0)
    h = _combine_call(h1, y, route, gate2, ln_g[0, 1:2], ln_b[0, 1:2], s_len=s, alpha=alpha)

    shift1, scale1p, gate1, shift2, scale2p, gate2 = seg_mods(1)
    w_in = cd_w_in[0]
    q_w, q_n = w_in[:, :qw], w_in[:, qw:2 * qw]
    k_w, v_w = w_in[:, 2 * qw:2 * qw + kvw], w_in[:, 2 * qw + kvw:2 * qw + 2 * kvw]
    k_n, v_n = w_in[:, 2 * qw + 2 * kvw:3 * qw + 2 * kvw], w_in[:, 3 * qw + 2 * kvw:]
    w_qk = jnp.concatenate([_pair_heads(q_w, 1), q_n, k_w, k_n], axis=1).astype(BF16)
    roles1 = ((True, None, True),) * 4 + ((False, None, True),) * 4 + ((True, None, False),) \
        + ((False, None, False),) * 4
    w_vt = jnp.concatenate([v_w, v_n], axis=1).T.astype(BF16)
    p1, vt1 = _inproj_call(h, shift1, scale1p, w_qk, w_vt, cos128, sin128, jnp.ones((2, LANES), F32), roles1, s)
    sink_row = jnp.repeat(win_sink[0].reshape(2, 4).T.reshape(8), TQ_WIN).reshape(1, 8 * TQ_WIN)
    o_w = _win_attn_call(p1, vt1, sink_row, s_len=s, n_ctx=n_ctx, q_width=qw, k_blk=8, v_blk=0)
    o_n = _na_attn_call(p1, vt1, _na_bias_tables(na_rpb[0], rows), s_len=s, n_ctx=n_ctx, n_pairs=4,
                        q_blk0=4, k_blk0=9, v_blk0=1)
    w_out = jnp.concatenate([_pair_heads(cd_w_out[0][:qw], 0), cd_w_out[0][qw:]], axis=0).astype(BF16)
    w_route, b_route = _router_params(moe_w_group[1], moe_b_group[1], moe_w_router[1], moe_b_router[1])
    h1, f, route = _outproj_call(o_w, o_n, w_out, h, gate1, ln_g[1, 0:1], ln_b[1, 0:1], shift2, scale2p,
                                 w_route, b_route, n_rows=s, s_len=s, alpha=alpha)
    y = _moe_layer(f, route, moe_w1, moe_w3, moe_w2, 1)
    return _combine_call(h1, y, route, gate2, ln_g[1, 1:2], ln_b[1, 1:2], s_len=s, alpha=alpha)
```

```python
import functools

import numpy as np
import jax
import jax.numpy as jnp
from jax import lax
from jax.experimental import pallas as pl
from jax.experimental.pallas import tpu as pltpu

F32 = jnp.float32
BF16 = jnp.bfloat16
HIGHEST = lax.Precision.HIGHEST

GRID_W = 64
HEAD_DIM = 64
ROPE_THETA = 10000.0
WINDOW = 128
NA_ROWS_MAX = 8
NA_COLS = 16
N_GROUPS = 4
EXPERTS_PER_GROUP = 8
N_EXPERTS = N_GROUPS * EXPERTS_PER_GROUP
TOP_K = 2
MOE_BLOCK = 128
LN_EPS = 1e-5
RMS_EPS = 1e-6
ATTN_SCALE = HEAD_DIM ** -0.5
NEG_INF = -1e30
LOG2E = 1.4426950408889634
Q_SCALE = ATTN_SCALE * LOG2E

LANES = 128
PAIR = 2 * HEAD_DIM
assert PAIR == LANES

VMEM_LIMIT = 48 * 1024 * 1024

TM = 256
TQ_DIFF = 256
TQ_GQA = 256
TK = 1024
Q_STRIP = 256
TQ_WIN = 256
NA_QROWS = 8
NA_SLAB = NA_QROWS + NA_ROWS_MAX


def _cparams(n_axes):
    return pltpu.CompilerParams(dimension_semantics=("arbitrary",) * n_axes, vmem_limit_bytes=VMEM_LIMIT)


def _lane_iota():
    return lax.broadcasted_iota(jnp.int32, (1, LANES), 1)


def _mods_kernel(c_ref, w_ref, b_ref, o_ref):
    c = c_ref[...]
    sc = c / (1.0 + jnp.exp(-c))
    o_ref[...] = jnp.dot(sc, w_ref[...], precision=HIGHEST, preferred_element_type=F32) + b_ref[...]


def _mods_call(cvec, mod_w, mod_b):
    depth, d, n6 = mod_w.shape
    tn = n6 // 4
    return pl.pallas_call(
        _mods_kernel,
        grid=(depth, n6 // tn),
        in_specs=[
            pl.BlockSpec((8, d), lambda l, j: (0, 0)),
            pl.BlockSpec((None, d, tn), lambda l, j: (l, 0, j)),
            pl.BlockSpec((None, 1, tn), lambda l, j: (l, 0, j)),
        ],
        out_specs=pl.BlockSpec((None, 8, tn), lambda l, j: (l, 0, j)),
        out_shape=jax.ShapeDtypeStruct((depth, 8, n6), F32),
        compiler_params=_cparams(2),
        name="mods",
    )(cvec, mod_w, mod_b.reshape(depth, 1, n6))


def _inproj_kernel(h_ref, sh_ref, sc_ref, w_ref, wvt_ref, cos_ref, sin_ref, g_ref, o_ref, vt_ref, *, roles):
    a = (h_ref[...] * sc_ref[...] + sh_ref[...]).astype(BF16)
    vt = lax.dot_general(wvt_ref[...], a, (((1,), (1,)), ((), ())), preferred_element_type=F32)
    for j in range(vt_ref.shape[0]):
        vt_ref[j] = vt[j * LANES:(j + 1) * LANES, :].astype(BF16)
    lane = _lane_iota()
    rope_first = (lane & (HEAD_DIM - 1)) < (HEAD_DIM // 2)
    lo = lane < HEAD_DIM
    n_blocks = len(roles)
    for c0 in range(0, n_blocks, 2):
        width = min(2, n_blocks - c0)
        p = jnp.dot(a, w_ref[:, c0 * LANES:(c0 + width) * LANES], preferred_element_type=F32)
        for half in range(width):
            cb = c0 + half
            rope, norm, scale = roles[cb]
            blk = p[:, half * LANES:(half + 1) * LANES]
            if norm is not None:
                sq = blk * blk
                s_lo = jnp.sum(jnp.where(lo, sq, 0.0), axis=-1, keepdims=True)
                s_hi = jnp.sum(jnp.where(lo, 0.0, sq), axis=-1, keepdims=True)
                ms = jnp.where(lo, s_lo, s_hi) * (1.0 / HEAD_DIM)
                blk = blk * lax.rsqrt(ms + RMS_EPS) * g_ref[norm:norm + 1, :]
            if rope:
                partner = jnp.where(rope_first, pltpu.roll(blk, LANES - HEAD_DIM // 2, 1),
                                    pltpu.roll(blk, HEAD_DIM // 2, 1))
                blk = blk * cos_ref[...] + partner * sin_ref[...]
            if scale:
                blk = blk * Q_SCALE
            o_ref[:, cb * LANES:(cb + 1) * LANES] = blk.astype(BF16)


def _inproj_call(h, shift, scale1p, w16, wvt16, cos128, sin128, g128, roles, s_len):
    b, n, d = h.shape
    ncol = w16.shape[1]
    nvb = wvt16.shape[0] // LANES
    n_lat = s_len // TM
    seg = lambda bb, j: (bb, jnp.where(j >= n_lat, 1, 0), 0, 0)
    return pl.pallas_call(
        functools.partial(_inproj_kernel, roles=roles),
        grid=(b, n // TM),
        in_specs=[
            pl.BlockSpec((None, TM, d), lambda bb, j: (bb, j, 0)),
            pl.BlockSpec((None, None, 1, d), seg),
            pl.BlockSpec((None, None, 1, d), seg),
            pl.BlockSpec((d, ncol), lambda bb, j: (0, 0)),
            pl.BlockSpec((nvb * LANES, d), lambda bb, j: (0, 0)),
            pl.BlockSpec((TM, LANES), lambda bb, j: (j, 0)),
            pl.BlockSpec((TM, LANES), lambda bb, j: (j, 0)),
            pl.BlockSpec((2, LANES), lambda bb, j: (0, 0)),
        ],
        out_specs=[
            pl.BlockSpec((None, TM, ncol), lambda bb, j: (bb, j, 0)),
            pl.BlockSpec((None, nvb, LANES, TM), lambda bb, j: (bb, 0, 0, j)),
        ],
        out_shape=[
            jax.ShapeDtypeStruct((b, n, ncol), BF16),
            jax.ShapeDtypeStruct((b, nvb, LANES, n), BF16),
        ],
        compiler_params=_cparams(2),
        name="inproj",
    )(h, shift, scale1p, w16, wvt16, cos128, sin128, g128)


def _stack_q(q_ref, n_qblocks):
    lo = _lane_iota() < HEAD_DIM
    parts = []
    for j in range(n_qblocks):
        q = q_ref[:, j * LANES:(j + 1) * LANES]
        zero = jnp.zeros_like(q)
        parts.append(jnp.where(lo, q, zero))
        parts.append(jnp.where(lo, zero, q))
    return jnp.concatenate(parts, axis=0)


def _scores_t(k, qs):
    return lax.dot_general(k, qs, (((1,), (1,)), ((), ())), preferred_element_type=F32)


def _flash_init(m_ref, l_ref, acc_ref):
    m_ref[...] = jnp.full(m_ref.shape, NEG_INF, F32)
    l_ref[...] = jnp.zeros(l_ref.shape, F32)
    acc_ref[...] = jnp.zeros(acc_ref.shape, F32)


def _flash_update(st, vt, m_ref, l_ref, acc_ref):
    m_prev = m_ref[...]
    m_new = jnp.maximum(m_prev, jnp.max(st, axis=0, keepdims=True))
    alpha = jnp.exp2(m_prev - m_new)
    pt = jnp.exp2(st - m_new)
    l_ref[...] = alpha * l_ref[...] + jnp.sum(pt, axis=0, keepdims=True)
    acc_ref[...] = alpha * acc_ref[...] + jnp.dot(vt, pt.astype(BF16), preferred_element_type=F32)
    m_ref[...] = m_new


def _flash_sweep(qs, k_ref, vt_ref, m_ref, l_ref, acc_ref, s_ref, *, is_lat, s_len, n_ctx):
    _flash_init(m_ref, l_ref, acc_ref)
    n_chunks = s_len // TK
    assert n_chunks % 2 == 0 and n_chunks >= 2

    def k_chunk(c):
        if isinstance(c, int):
            return k_ref[c * TK:(c + 1) * TK, :]
        return k_ref[pl.ds(pl.multiple_of(c * TK, TK), TK), :]

    def vt_chunk(c):
        if isinstance(c, int):
            return vt_ref[:, c * TK:(c + 1) * TK]
        return vt_ref[:, pl.ds(pl.multiple_of(c * TK, TK), TK)]

    k_ctx = k_ref[s_len:s_len + n_ctx, :]
    vt_ctx = vt_ref[:, s_len:s_len + n_ctx]

    @pl.when(is_lat)
    def _():
        s_ref[...] = _scores_t(k_chunk(0), qs)

        def body(j, carry):
            c = 2 * j
            s_odd = _scores_t(k_chunk(c + 1), qs)
            _flash_update(s_ref[...], vt_chunk(c), m_ref, l_ref, acc_ref)
            s_ref[...] = _scores_t(k_chunk(c + 2), qs)
            _flash_update(s_odd, vt_chunk(c + 1), m_ref, l_ref, acc_ref)
            return carry
        lax.fori_loop(0, n_chunks // 2 - 1, body, 0)

        c = n_chunks - 2
        s_odd = _scores_t(k_chunk(c + 1), qs)
        _flash_update(s_ref[...], vt_chunk(c), m_ref, l_ref, acc_ref)
        s_last = _scores_t(k_ctx, qs)
        _flash_update(s_odd, vt_chunk(c + 1), m_ref, l_ref, acc_ref)
        _flash_update(s_last, vt_ctx, m_ref, l_ref, acc_ref)

    @pl.when(jnp.logical_not(is_lat))
    def _():
        _flash_update(_scores_t(k_ctx, qs), vt_ctx, m_ref, l_ref, acc_ref)


def _merge_pairs(ot, tq, n_qblocks, o_ref):
    for j in range(n_qblocks):
        o_lo = ot[:HEAD_DIM, (2 * j) * tq:(2 * j + 1) * tq]
        o_hi = ot[HEAD_DIM:, (2 * j + 1) * tq:(2 * j + 2) * tq]
        blk = jnp.concatenate([o_lo, o_hi], axis=0)
        o_ref[:, j * LANES:(j + 1) * LANES] = blk.T.astype(o_ref.dtype)


def _diff_attn_kernel(q_ref, k_ref, vt_ref, lam_ref, g_ref, o_ref, m_ref, l_ref, acc_ref, s_ref, *,
                      s_len, n_ctx, lam_init):
    tq = q_ref.shape[0]
    i = pl.program_id(2)
    qs = _stack_q(q_ref, 1)
    _flash_sweep(qs, k_ref, vt_ref, m_ref, l_ref, acc_ref, s_ref, is_lat=i < s_len // tq, s_len=s_len, n_ctx=n_ctx)
    lv = lam_ref[...]
    lam = (jnp.exp(jnp.sum(lv[0:1] * lv[1:2], axis=-1, keepdims=True))
           - jnp.exp(jnp.sum(lv[2:3] * lv[3:4], axis=-1, keepdims=True)) + lam_init)
    ot = acc_ref[...] / l_ref[...]
    od = ot[:, :tq] - lam * ot[:, tq:]
    ms = jnp.mean(od * od, axis=0, keepdims=True)
    od = od * lax.rsqrt(ms + RMS_EPS) * g_ref[...] * (1.0 - lam_init)
    o_ref[...] = od.T.astype(o_ref.dtype)


def _diff_attn_call(p, vt, lam_vecs, subln_g, *, s_len, n_ctx, n_heads, q_blk0, k_blk0, v_blk0, lam_init):
    b, n, _ = p.shape
    tq = TQ_DIFF
    return pl.pallas_call(
        functools.partial(_diff_attn_kernel, s_len=s_len, n_ctx=n_ctx, lam_init=lam_init),
        grid=(b, n_heads, n // tq),
        in_specs=[
            pl.BlockSpec((None, tq, LANES), lambda bb, hh, i: (bb, i, q_blk0 + hh)),
            pl.BlockSpec((None, n, LANES), lambda bb, hh, i: (bb, 0, k_blk0 + hh)),
            pl.BlockSpec((None, None, LANES, n), lambda bb, hh, i: (bb, v_blk0 + hh, 0, 0)),
            pl.BlockSpec((4, HEAD_DIM), lambda bb, hh, i: (0, 0)),
            pl.BlockSpec((LANES, 1), lambda bb, hh, i: (0, 0)),
        ],
        out_specs=pl.BlockSpec((None, tq, LANES), lambda bb, hh, i: (bb, i, hh)),
        out_shape=jax.ShapeDtypeStruct((b, n, n_heads * LANES), BF16),
        scratch_shapes=[pltpu.VMEM((1, 2 * tq), F32), pltpu.VMEM((1, 2 * tq), F32), pltpu.VMEM((LANES, 2 * tq), F32),
                        pltpu.VMEM((TK, 2 * tq), F32)],
        compiler_params=_cparams(3),
        name="diff_attn",
    )(p, p, vt, lam_vecs, subln_g.reshape(LANES, 1))


def _gqa_attn_kernel(q_ref, k_ref, vt_ref, o_ref, m_ref, l_ref, acc_ref, s_ref, *, s_len, n_ctx):
    tq = q_ref.shape[0]
    n_qblocks = q_ref.shape[1] // LANES
    i = pl.program_id(1)
    qs = _stack_q(q_ref, n_qblocks)
    _flash_sweep(qs, k_ref, vt_ref, m_ref, l_ref, acc_ref, s_ref, is_lat=i < s_len // tq, s_len=s_len, n_ctx=n_ctx)
    _merge_pairs(acc_ref[...] / l_ref[...], tq, n_qblocks, o_ref)


def _gqa_attn_call(p, vt, *, s_len, n_ctx, q_width, q_col0, k_blk, v_blk):
    b, n, _ = p.shape
    tq = TQ_GQA
    m_rows = 2 * (q_width // LANES) * tq
    return pl.pallas_call(
        functools.partial(_gqa_attn_kernel, s_len=s_len, n_ctx=n_ctx),
        grid=(b, n // tq),
        in_specs=[
            pl.BlockSpec((None, tq, q_width), lambda bb, i: (bb, i, q_col0 // q_width)),
            pl.BlockSpec((None, n, LANES), lambda bb, i: (bb, 0, k_blk)),
            pl.BlockSpec((None, None, LANES, n), lambda bb, i: (bb, v_blk, 0, 0)),
        ],
        out_specs=pl.BlockSpec((None, tq, q_width), lambda bb, i: (bb, i, 0)),
        out_shape=jax.ShapeDtypeStruct((b, n, q_width), BF16),
        scratch_shapes=[pltpu.VMEM((1, m_rows), F32), pltpu.VMEM((1, m_rows), F32), pltpu.VMEM((LANES, m_rows), F32),
                        pltpu.VMEM((TK, m_rows), F32)],
        compiler_params=_cparams(2),
        name="gqa_attn",
    )(p, p, vt)


def _win_attn_kernel(q_ref, k_ref, vt_ref, sink_ref, o_ref, *, s_len, n_ctx):
    tq = q_ref.shape[0]
    n_qblocks = q_ref.shape[1] // LANES
    span = tq + 2 * WINDOW
    i = pl.program_id(1)
    ws = pl.multiple_of(jnp.clip(i * tq - WINDOW, 0, s_len - span), LANES)
    qs = _stack_q(q_ref, n_qblocks)
    m_rows = qs.shape[0]
    s_loc = _scores_t(k_ref[pl.ds(ws, span), :], qs)
    kpos = ws + lax.broadcasted_iota(jnp.int32, (span, m_rows), 0)
    qpos = i * tq + (lax.broadcasted_iota(jnp.int32, (span, m_rows), 1) & (tq - 1))
    s_loc = jnp.where(jnp.abs(qpos - kpos) <= WINDOW, s_loc, NEG_INF)
    s_ctx = _scores_t(k_ref[s_len:s_len + n_ctx, :], qs)
    sink = sink_ref[...] * LOG2E
    m = jnp.maximum(jnp.maximum(jnp.max(s_loc, axis=0, keepdims=True), jnp.max(s_ctx, axis=0, keepdims=True)), sink)
    p_loc = jnp.exp2(s_loc - m)
    p_ctx = jnp.exp2(s_ctx - m)
    denom = jnp.sum(p_loc, axis=0, keepdims=True) + jnp.sum(p_ctx, axis=0, keepdims=True) + jnp.exp2(sink - m)
    acc = (jnp.dot(vt_ref[:, pl.ds(ws, span)], p_loc.astype(BF16), preferred_element_type=F32)
           + jnp.dot(vt_ref[:, s_len:s_len + n_ctx], p_ctx.astype(BF16), preferred_element_type=F32))
    _merge_pairs(acc / denom, tq, n_qblocks, o_ref)


def _win_attn_call(p, vt, sink_row, *, s_len, n_ctx, q_width, k_blk, v_blk):
    b, n, _ = p.shape
    tq = TQ_WIN
    m_rows = sink_row.shape[1]
    return pl.pallas_call(
        functools.partial(_win_attn_kernel, s_len=s_len, n_ctx=n_ctx),
        grid=(b, s_len // tq),
        in_specs=[
            pl.BlockSpec((None, tq, q_width), lambda bb, i: (bb, i, 0)),
            pl.BlockSpec((None, n, LANES), lambda bb, i: (bb, 0, k_blk)),
            pl.BlockSpec((None, None, LANES, n), lambda bb, i: (bb, v_blk, 0, 0)),
            pl.BlockSpec((1, m_rows), lambda bb, i: (0, 0)),
        ],
        out_specs=pl.BlockSpec((None, tq, q_width), lambda bb, i: (bb, i, 0)),
        out_shape=jax.ShapeDtypeStruct((b, s_len, q_width), BF16),
        compiler_params=_cparams(2),
        name="win_attn",
    )(p, p, vt, sink_row)


def _na_attn_kernel(q_ref, k_ref, vt_ref, bias_ref, o_ref, *, s_len, n_ctx):
    tq = q_ref.shape[0]
    slab = NA_SLAB * GRID_W
    rows = s_len // GRID_W
    mi = pl.program_id(2)
    ss = pl.multiple_of(jnp.clip(NA_QROWS * mi - NA_ROWS_MAX // 2, 0, rows - NA_SLAB) * GRID_W, LANES)
    qs = _stack_q(q_ref, 1)
    s_nb = _scores_t(k_ref[pl.ds(ss, slab), :], qs) + bias_ref[...]
    s_cx = _scores_t(k_ref[s_len:s_len + n_ctx, :], qs)
    m = jnp.maximum(jnp.max(s_nb, axis=0, keepdims=True), jnp.max(s_cx, axis=0, keepdims=True))
    p_nb = jnp.exp2(s_nb - m)
    p_cx = jnp.exp2(s_cx - m)
    denom = jnp.sum(p_nb, axis=0, keepdims=True) + jnp.sum(p_cx, axis=0, keepdims=True)
    acc = (jnp.dot(vt_ref[:, pl.ds(ss, slab)], p_nb.astype(BF16), preferred_element_type=F32)
           + jnp.dot(vt_ref[:, s_len:s_len + n_ctx], p_cx.astype(BF16), preferred_element_type=F32))
    _merge_pairs(acc / denom, tq, 1, o_ref)


def _na_attn_call(p, vt, bias_tab, *, s_len, n_ctx, n_pairs, q_blk0, k_blk0, v_blk0):
    b, n, _ = p.shape
    tq = NA_QROWS * GRID_W
    n_steps = s_len // tq
    tbl = lambda mi: jnp.where(mi == 0, 0, jnp.where(mi == n_steps - 1, 2, 1))
    return pl.pallas_call(
        functools.partial(_na_attn_kernel, s_len=s_len, n_ctx=n_ctx),
        grid=(b, n_pairs, n_steps),
        in_specs=[
            pl.BlockSpec((None, tq, LANES), lambda bb, j, mi: (bb, mi, q_blk0 + j)),
            pl.BlockSpec((None, n, LANES), lambda bb, j, mi: (bb, 0, k_blk0 + j)),
            pl.BlockSpec((None, None, LANES, n), lambda bb, j, mi: (bb, v_blk0 + j, 0, 0)),
            pl.BlockSpec((None, None, NA_SLAB * GRID_W, 2 * tq), lambda bb, j, mi: (tbl(mi), j, 0, 0)),
        ],
        out_specs=pl.BlockSpec((None, tq, LANES), lambda bb, j, mi: (bb, mi, j)),
        out_shape=jax.ShapeDtypeStruct((b, s_len, n_pairs * LANES), BF16),
        compiler_params=_cparams(3),
        name="na_attn",
    )(p, p, vt, bias_tab)


def _na_bias_tables(rpb, rows):
    n_heads = rpb.shape[0]
    rpb = rpb.astype(F32)
    pad = GRID_W - NA_COLS
    rpb_p = jnp.pad(rpb, ((0, 0), (0, 0), (pad, pad)))
    col_tab = jnp.stack([rpb_p[:, :, pad + NA_COLS - 1 - qc: pad + NA_COLS - 1 - qc + GRID_W] for qc in range(GRID_W)],
                        axis=2)
    qr = np.arange(NA_QROWS)[:, None, None, None]
    qc = np.arange(GRID_W)[None, :, None, None]
    kr = np.arange(NA_SLAB)[None, None, :, None]
    kc = np.arange(GRID_W)[None, None, None, :]
    full = (NA_QROWS, GRID_W, NA_SLAB, GRID_W)
    flat = (NA_QROWS * GRID_W, NA_SLAB * GRID_W)
    tabs = []
    for q0, s0 in ((0, 0), (NA_QROWS, NA_QROWS - NA_ROWS_MAX // 2), (rows - NA_QROWS, rows - NA_SLAB)):
        q_row = q0 + qr
        k_row = s0 + kr
        r0 = np.clip(q_row - NA_ROWS_MAX // 2, 0, rows - NA_ROWS_MAX)
        c0 = np.clip(qc - NA_COLS // 2, 0, GRID_W - NA_COLS)
        valid = (k_row >= r0) & (k_row < r0 + NA_ROWS_MAX) & (kc >= c0) & (kc < c0 + NA_COLS)
        valid = np.broadcast_to(valid, full).reshape(flat)
        r_off = np.clip(k_row - q_row + NA_ROWS_MAX - 1, 0, 2 * NA_ROWS_MAX - 2)[:, 0, :, 0]
        bias = jnp.stack([jnp.stack([col_tab[:, int(r_off[a, c])] for c in range(NA_SLAB)], axis=2)
                          for a in range(NA_QROWS)], axis=1)
        tabs.append(jnp.where(valid[None], bias.reshape((n_heads,) + flat) * LOG2E, NEG_INF))
    tab = jnp.stack(tabs)
    return jnp.swapaxes(tab.reshape(3, n_heads // 2, 2 * flat[0], flat[1]), -1, -2)


def _split_hi_lo(x):
    c = x * 65537.0
    hi = c - (c - x)
    return hi.astype(BF16), (x - hi).astype(BF16)


def _layer_norm(y, g, b):
    mu = jnp.mean(y, axis=-1, keepdims=True)
    yc = y - mu
    var = jnp.mean(yc * yc, axis=-1, keepdims=True)
    return yc * lax.rsqrt(var + LN_EPS) * g + b


def _route(logits):
    lane = _lane_iota().astype(F32)
    big = float(LANES)
    is_g = lane < N_GROUPS
    gl = jnp.where(is_g, logits, NEG_INF)
    g_max = jnp.max(gl, axis=-1, keepdims=True)
    g_idx = jnp.min(jnp.where(gl == g_max, lane, big), axis=-1, keepdims=True)
    g_w = 1.0 / jnp.sum(jnp.where(is_g, jnp.exp(gl - g_max), 0.0), axis=-1, keepdims=True)
    base = N_GROUPS + EXPERTS_PER_GROUP * g_idx
    el = jnp.where((lane >= base) & (lane < base + EXPERTS_PER_GROUP), logits, NEG_INF)
    v1 = jnp.max(el, axis=-1, keepdims=True)
    i1 = jnp.min(jnp.where(el == v1, lane, big), axis=-1, keepdims=True)
    el2 = jnp.where(lane == i1, NEG_INF, el)
    v2 = jnp.max(el2, axis=-1, keepdims=True)
    i2 = jnp.min(jnp.where(el2 == v2, lane, big), axis=-1, keepdims=True)
    t = jnp.exp(v2 - v1)
    w1 = g_w / (1.0 + t)
    w2 = g_w * t / (1.0 + t)
    return jnp.where(lane == 0, i1 - N_GROUPS,
                     jnp.where(lane == 1, i2 - N_GROUPS, jnp.where(lane == 2, w1, jnp.where(lane == 3, w2, 0.0))))


def _outproj_kernel(oa_ref, ob_ref, w_ref, h_ref, gate_ref, lng_ref, lnb_ref, sh_ref, sc_ref, wr_ref, br_ref,
                    h1_ref, f_ref, route_ref, *, alpha):
    ka = oa_ref.shape[1]
    o = (jnp.dot(oa_ref[...], w_ref[:ka, :], preferred_element_type=F32)
         + jnp.dot(ob_ref[...], w_ref[ka:, :], preferred_element_type=F32))
    h1 = _layer_norm(alpha * h_ref[...] + gate_ref[...] * o, lng_ref[...], lnb_ref[...])
    h1_ref[...] = h1
    f = h1 * sc_ref[...] + sh_ref[...]
    f_ref[...] = f
    f_hi, f_lo = _split_hi_lo(f)
    hi_prod = jnp.dot(f_hi, wr_ref[...], preferred_element_type=F32)
    lo_prod = jnp.dot(f_lo, wr_ref[...], preferred_element_type=F32)
    logits = (hi_prod[:, :LANES] + hi_prod[:, LANES:]) + (lo_prod[:, :LANES] + lo_prod[:, LANES:]) + br_ref[...]
    route_ref[...] = _route(logits)


def _outproj_call(oa, ob, w16, h, gate, ln_g, ln_b, shift, scale1p, w_route, b_route, *, n_rows, s_len, alpha):
    b, _, d = h.shape
    ka, kb = oa.shape[2], ob.shape[2]
    n_lat = s_len // TM
    seg = lambda bb, j: (bb, jnp.where(j >= n_lat, 1, 0), 0, 0)
    row = lambda bb, j: (bb, j, 0)
    const = lambda bb, j: (0, 0)
    return pl.pallas_call(
        functools.partial(_outproj_kernel, alpha=alpha),
        grid=(b, n_rows // TM),
        in_specs=[
            pl.BlockSpec((None, TM, ka), row),
            pl.BlockSpec((None, TM, kb), row),
            pl.BlockSpec((ka + kb, d), const),
            pl.BlockSpec((None, TM, d), row),
            pl.BlockSpec((None, None, 1, d), seg),
            pl.BlockSpec((1, d), const),
            pl.BlockSpec((1, d), const),
            pl.BlockSpec((None, None, 1, d), seg),
            pl.BlockSpec((None, None, 1, d), seg),
            pl.BlockSpec((d, 2 * LANES), const),
            pl.BlockSpec((1, LANES), const),
        ],
        out_specs=[
            pl.BlockSpec((None, TM, d), row),
            pl.BlockSpec((None, TM, d), row),
            pl.BlockSpec((None, TM, LANES), row),
        ],
        out_shape=[
            jax.ShapeDtypeStruct((b, n_rows, d), F32),
            jax.ShapeDtypeStruct((b, n_rows, d), F32),
            jax.ShapeDtypeStruct((b, n_rows, LANES), F32),
        ],
        compiler_params=_cparams(2),
        name="outproj",
    )(oa, ob, w16, h, gate, ln_g, ln_b, shift, scale1p, w_route, b_route)


def _row_copy(src, src_row, dst, dst_row, sem):
    return pltpu.make_async_copy(src.at[pl.ds(src_row, 1), :], dst.at[pl.ds(dst_row, 1), :], sem)


def _block_copy_for_wait(src, dst, sem):
    return pltpu.make_async_copy(src, dst, sem)


MOE_BUFS = 3
DMA_PRIORITIES = 2


def _moe_kernel(blk_e_ref, tok0_ref, tok1_ref, tok_ahead_ref, dst_prev_ref, dst_last_ref, f_hbm,
                w1_ref, w3_ref, w2_ref, y_hbm, xbuf, ybuf, w1b, w3b, w2b, gsem, ssem):
    i = pl.program_id(0)
    last = pl.num_programs(0) - 1
    cur = lax.rem(i, MOE_BUFS)
    ahead = lax.rem(i + 2, MOE_BUFS)

    def wait_gather(s):
        _block_copy_for_wait(f_hbm.at[pl.ds(0, MOE_BLOCK), :], xbuf.at[s], gsem.at[s]).wait()

    def wait_scatter(s):
        _block_copy_for_wait(ybuf.at[s], y_hbm.at[pl.ds(0, MOE_BLOCK), :], ssem.at[s]).wait()

    def start_gather(tok_ref, s):
        for r in range(MOE_BLOCK):
            _row_copy(f_hbm, tok_ref[0, r], xbuf.at[s], r, gsem.at[s]).start(priority=r % DMA_PRIORITIES)

    def start_scatter(dst_ref, s):
        for r in range(MOE_BLOCK):
            _row_copy(ybuf.at[s], r, y_hbm, dst_ref[0, r], ssem.at[s]).start(priority=r % DMA_PRIORITIES)

    @pl.when(i == 0)
    def _():
        ybuf[...] = jnp.zeros(ybuf.shape, ybuf.dtype)
        start_gather(tok0_ref, 0)
        start_gather(tok1_ref, 1)

    wait_gather(cur)

    @pl.when(i >= 2)
    def _():
        wait_scatter(cur)

    @pl.when(jnp.logical_or(i == 0, blk_e_ref[i] != blk_e_ref[jnp.maximum(i - 1, 0)]))
    def _():
        w1b[...] = w1_ref[...].astype(BF16)
        w3b[...] = w3_ref[...].astype(BF16)
        w2b[...] = w2_ref[...].astype(BF16)

    x = xbuf[cur].astype(BF16)
    start_gather(tok_ahead_ref, ahead)
    start_scatter(dst_prev_ref, ahead)
    h1 = jnp.dot(x, w1b[...], preferred_element_type=F32)
    h3 = jnp.dot(x, w3b[...], preferred_element_type=F32)
    act = (h1 / (1.0 + jnp.exp(-h1))) * h3
    ybuf[cur] = jnp.dot(act.astype(BF16), w2b[...], preferred_element_type=F32)

    @pl.when(i == last)
    def _():
        start_scatter(dst_last_ref, cur)
        for s in range(MOE_BUFS):
            wait_scatter(s)
        wait_gather(lax.rem(i + 1, MOE_BUFS))
        wait_gather(ahead)


def _moe_call(f_rows, blk_e, slot_tok, slot_dst, w1, w3, w2, layer):
    t, d = f_rows.shape
    n_blocks = blk_e.shape[0]
    n_slots = n_blocks * MOE_BLOCK
    de = w1.shape[3]
    assert n_blocks >= MOE_BUFS
    smem_block = lambda index_map: pl.BlockSpec((None, 1, MOE_BLOCK), index_map, memory_space=pltpu.SMEM)
    grid_spec = pltpu.PrefetchScalarGridSpec(
        num_scalar_prefetch=1,
        grid=(n_blocks,),
        in_specs=[
            smem_block(lambda i, be: (0, 0, 0)),
            smem_block(lambda i, be: (1, 0, 0)),
            smem_block(lambda i, be: (jnp.minimum(i + 2, n_blocks - 1), 0, 0)),
            smem_block(lambda i, be: (i, 0, 0)),
            smem_block(lambda i, be: (i + 1, 0, 0)),
            pl.BlockSpec(memory_space=pl.ANY),
            pl.BlockSpec((None, None, d, de), lambda i, be: (layer, be[i], 0, 0)),
            pl.BlockSpec((None, None, d, de), lambda i, be: (layer, be[i], 0, 0)),
            pl.BlockSpec((None, None, de, d), lambda i, be: (layer, be[i], 0, 0)),
        ],
        out_specs=pl.BlockSpec(memory_space=pl.ANY),
        scratch_shapes=[
            pltpu.VMEM((MOE_BUFS, MOE_BLOCK, d), F32),
            pltpu.VMEM((MOE_BUFS, MOE_BLOCK, d), F32),
            pltpu.VMEM((d, de), BF16),
            pltpu.VMEM((d, de), BF16),
            pltpu.VMEM((de, d), BF16),
            pltpu.SemaphoreType.DMA((MOE_BUFS,)),
            pltpu.SemaphoreType.DMA((MOE_BUFS,)),
        ],
    )
    tok3 = slot_tok.reshape(n_blocks, 1, MOE_BLOCK)
    dst_ext = jnp.concatenate([n_slots + jnp.arange(MOE_BLOCK, dtype=jnp.int32), slot_dst])
    dst3 = dst_ext.reshape(n_blocks + 1, 1, MOE_BLOCK)
    return pl.pallas_call(
        _moe_kernel,
        grid_spec=grid_spec,
        out_shape=jax.ShapeDtypeStruct((n_slots + MOE_BLOCK, d), F32),
        compiler_params=_cparams(1),
        name="moe_experts",
    )(blk_e, tok3, tok3, tok3, dst3, dst3, f_rows, w1, w3, w2)


def _moe_plan(route, t):
    n_assign = TOP_K * t
    e_flat = route[:, :TOP_K].astype(jnp.int32).reshape(n_assign)
    order = jnp.argsort(e_flat, stable=True).astype(jnp.int32)
    counts = jnp.sum((e_flat[:, None] == jnp.arange(N_EXPERTS, dtype=jnp.int32)[None, :]).astype(jnp.int32), axis=0)
    starts = jnp.cumsum(counts) - counts
    padded = (counts + MOE_BLOCK - 1) // MOE_BLOCK * MOE_BLOCK
    p_ends = jnp.cumsum(padded)
    p_starts = p_ends - padded
    n_blocks = (n_assign + MOE_BLOCK - 1) // MOE_BLOCK + N_EXPERTS
    blk_start = jnp.arange(n_blocks, dtype=jnp.int32) * MOE_BLOCK
    blk_e = jnp.minimum(jnp.sum((p_ends[None, :] <= blk_start[:, None]).astype(jnp.int32), axis=1), N_EXPERTS - 1)
    slot = jnp.arange(n_blocks * MOE_BLOCK, dtype=jnp.int32)
    e_s = jnp.repeat(blk_e, MOE_BLOCK)
    j = slot - p_starts[e_s]
    valid = j < counts[e_s]
    a_idx = order[jnp.clip(starts[e_s] + j, 0, n_assign - 1)]
    slot_tok = jnp.where(valid, a_idx // TOP_K, 0).astype(jnp.int32)
    spare = n_assign + jnp.cumsum(jnp.logical_not(valid).astype(jnp.int32)) - 1
    slot_dst = jnp.where(valid, (a_idx % TOP_K) * t + a_idx // TOP_K, spare).astype(jnp.int32)
    return blk_e, slot_tok, slot_dst


def _combine_kernel(h_ref, y0_ref, y1_ref, route_ref, gate_ref, lng_ref, lnb_ref, o_ref, *, alpha):
    r = route_ref[...]
    y = r[:, 2:3] * y0_ref[...] + r[:, 3:4] * y1_ref[...]
    o_ref[...] = _layer_norm(alpha * h_ref[...] + gate_ref[...] * y, lng_ref[...], lnb_ref[...])


def _combine_call(h1, y, route, gate, ln_g, ln_b, *, s_len, alpha):
    b, n_rows, d = h1.shape
    n_lat = s_len // TM
    seg = lambda bb, j: (bb, jnp.where(j >= n_lat, 1, 0), 0, 0)
    row = lambda bb, j: (bb, j, 0)
    const = lambda bb, j: (0, 0)
    blocks_per_batch = n_rows // TM
    blocks_per_k = b * blocks_per_batch
    return pl.pallas_call(
        functools.partial(_combine_kernel, alpha=alpha),
        grid=(b, n_rows // TM),
        in_specs=[
            pl.BlockSpec((None, TM, d), row),
            pl.BlockSpec((TM, d), lambda bb, j: (bb * blocks_per_batch + j, 0)),
            pl.BlockSpec((TM, d), lambda bb, j: (blocks_per_k + bb * blocks_per_batch + j, 0)),
            pl.BlockSpec((None, TM, LANES), row),
            pl.BlockSpec((None, None, 1, d), seg),
            pl.BlockSpec((1, d), const),
            pl.BlockSpec((1, d), const),
        ],
        out_specs=pl.BlockSpec((None, TM, d), row),
        out_shape=jax.ShapeDtypeStruct((b, n_rows, d), F32),
        compiler_params=_cparams(2),
        name="combine",
    )(h1, y, y, route, gate, ln_g, ln_b)


def _pair_heads(w, axis):
    w = jnp.moveaxis(w, axis, -1)
    lead = w.shape[:-1]
    n_heads = w.shape[-1] // HEAD_DIM
    w = w.reshape(lead + (2, n_heads // 2, HEAD_DIM)).swapaxes(-3, -2).reshape(lead + (n_heads * HEAD_DIM,))
    return jnp.moveaxis(w, -1, axis)


def _rope_tables(s_len, n_ctx):
    t = jnp.arange(s_len, dtype=jnp.int32)
    row = (t // GRID_W).astype(F32)
    col = (t % GRID_W).astype(F32)
    n_freq = HEAD_DIM // 4
    inv = ROPE_THETA ** (-jnp.arange(n_freq, dtype=F32) / n_freq)
    ang = jnp.concatenate([row[:, None] * inv, col[:, None] * inv], -1)
    cos = jnp.concatenate([jnp.cos(ang), jnp.ones((n_ctx, HEAD_DIM // 2), F32)], 0)
    sin = jnp.concatenate([jnp.sin(ang), jnp.zeros((n_ctx, HEAD_DIM // 2), F32)], 0)
    return jnp.tile(cos, (1, 4)), jnp.concatenate([-sin, sin, -sin, sin], -1)


def _lambda_init(layer_idx):
    return 0.8 - 0.6 * float(np.exp(-0.3 * layer_idx))


def _router_params(w_group, b_group, w_router, b_router):
    d = w_group.shape[0]
    pad = LANES - N_GROUPS - N_EXPERTS
    w = jnp.concatenate([w_group, w_router, jnp.zeros((d, pad), F32)], axis=1)
    bb = jnp.concatenate([b_group, b_router, jnp.zeros((pad,), F32)])
    w_hi, w_lo = _split_hi_lo(w)
    return jnp.concatenate([w_hi, w_lo], axis=1), bb.reshape(1, LANES)


def _moe_layer(f, route, w1, w3, w2, layer):
    b, n_rows, d = f.shape
    t = b * n_rows
    blk_e, slot_tok, slot_dst = _moe_plan(route.reshape(t, LANES), t)
    return _moe_call(f.reshape(t, d), blk_e, slot_tok, slot_dst, w1, w3, w2, layer)


def kernel(x, c, ctx, c_ctx, mod_w, mod_b, ln_g, ln_b, ab_w_in, ab_w_out, diff_lambda, diff_subln_g, gqa_qk_g,
           cd_w_in, cd_w_out, win_sink, na_rpb, moe_w_group, moe_b_group, moe_w_router, moe_b_router,
           moe_w1, moe_w3, moe_w2):
    b, s, d = x.shape
    n_ctx = ctx.shape[1]
    n = s + n_ctx
    depth = mod_w.shape[0]
    rows = s // GRID_W
    assert depth == 2, "layer pattern implemented: one differential/GQA layer, then one window/neighbourhood layer"
    assert s % TM == 0 and n_ctx % TM == 0 and s % TK == 0 and b + 1 <= 8
    assert n_ctx % TQ_DIFF == 0 and n_ctx % TQ_GQA == 0
    assert rows >= NA_SLAB and rows % NA_QROWS == 0 and s % GRID_W == 0
    alpha = (2.0 * depth) ** 0.25
    qw = d // 2
    kvw = qw // 4
    assert qw == 4 * LANES and kvw == LANES

    cos128, sin128 = _rope_tables(s, n_ctx)
    cvec = jnp.concatenate([c, c_ctx[None, :], jnp.zeros((8 - b - 1, d), F32)], axis=0)
    mods_all = _mods_call(cvec, mod_w, mod_b)

    def seg_mods(i):
        m = mods_all[i]
        lat = m[:b].reshape(b, 6, d)
        cm = jnp.broadcast_to(m[b].reshape(1, 6, d), (b, 6, d))
        ms = jnp.stack([lat, cm], axis=1)
        pick = lambda k, one: (ms[:, :, k] + one)[:, :, None, :]
        return pick(0, 0.0), pick(1, 1.0), pick(2, 0.0), pick(3, 0.0), pick(4, 1.0), pick(5, 0.0)

    h = jnp.concatenate([x, ctx], axis=1)

    shift1, scale1p, gate1, shift2, scale2p, gate2 = seg_mods(0)
    w_in = ab_w_in[0]
    q_d, q_g, k_d, v_d = (w_in[:, k * qw:(k + 1) * qw] for k in range(4))
    k_g, v_g = w_in[:, 4 * qw:4 * qw + kvw], w_in[:, 4 * qw + kvw:]
    w_qk = jnp.concatenate([q_d, _pair_heads(q_g, 1), k_d, k_g], axis=1).astype(BF16)
    roles0 = ((True, None, True),) * 4 + ((True, 0, True),) * 4 + ((True, None, False),) * 4 + ((True, 1, False),)
    w_vt = jnp.concatenate([v_d, v_g], axis=1).T.astype(BF16)
    g128 = jnp.tile(gqa_qk_g[0], (1, 2))
    p0, vt0 = _inproj_call(h, shift1, scale1p, w_qk, w_vt, cos128, sin128, g128, roles0, s)
    o_d = _diff_attn_call(p0, vt0, diff_lambda[0], diff_subln_g[0], s_len=s, n_ctx=n_ctx, n_heads=4,
                          q_blk0=0, k_blk0=8, v_blk0=0, lam_init=_lambda_init(0))
    o_g = _gqa_attn_call(p0, vt0, s_len=s, n_ctx=n_ctx, q_width=qw, q_col0=qw, k_blk=12, v_blk=4)
    w_out = jnp.concatenate([ab_w_out[0][:qw], _pair_heads(ab_w_out[0][qw:], 0)], axis=0).astype(BF16)
    w_route, b_route = _router_params(moe_w_group[0], moe_b_group[0], moe_w_router[0], moe_b_router[0])
    h1, f, route = _outproj_call(o_d, o_g, w_out, h, gate1, ln_g[0, 0:1], ln_b[0, 0:1], shift2, scale2p,
                                 w_route, b_route, n_rows=n, s_len=s, alpha=alpha)
    y = _moe_layer(f, route, moe_w1, moe_w3, moe_w2, 0)
    h = _combine_call(h1, y, route, gate2, ln_g[0, 1:2], ln_b[0, 1:2], s_len=s, alpha=alpha)

    shift1, scale1p, gate1, shift2, scale2p, gate2 = seg_mods(1)
    w_in = cd_w_in[0]
    q_w, q_n = w_in[:, :qw], w_in[:, qw:2 * qw]
    k_w, v_w = w_in[:, 2 * qw:2 * qw + kvw], w_in[:, 2 * qw + kvw:2 * qw + 2 * kvw]
    k_n, v_n = w_in[:, 2 * qw + 2 * kvw:3 * qw + 2 * kvw], w_in[:, 3 * qw + 2 * kvw:]
    w_qk = jnp.concatenate([_pair_heads(q_w, 1), q_n, k_w, k_n], axis=1).astype(BF16)
    roles1 = ((True, None, True),) * 4 + ((False, None, True),) * 4 + ((True, None, False),) \
        + ((False, None, False),) * 4
    w_vt = jnp.concatenate([v_w, v_n], axis=1).T.astype(BF16)
    p1, vt1 = _inproj_call(h, shift1, scale1p, w_qk, w_vt, cos128, sin128, jnp.ones((2, LANES), F32), roles1, s)
    sink_row = jnp.repeat(win_sink[0].reshape(2, 4).T.reshape(8), TQ_WIN).reshape(1, 8 * TQ_WIN)
    o_w = _win_attn_call(p1, vt1, sink_row, s_len=s, n_ctx=n_ctx, q_width=qw, k_blk=8, v_blk=0)
    o_n = _na_attn_call(p1, vt1, _na_bias_tables(na_rpb[0], rows), s_len=s, n_ctx=n_ctx, n_pairs=4,
                        q_blk0=4, k_blk0=9, v_blk0=1)
    w_out = jnp.concatenate([_pair_heads(cd_w_out[0][:qw], 0), cd_w_out[0][qw:]], axis=0).astype(BF16)
    w_route, b_route = _router_params(moe_w_group[1], moe_b_group[1], moe_w_router[1], moe_b_router[1])
    h1, f, route = _outproj_call(o_w, o_n, w_out, h, gate1, ln_g[1, 0:1], ln_b[1, 0:1], shift2, scale2p,
                                 w_route, b_route, n_rows=s, s_len=s, alpha=alpha)
    y = _moe_layer(f, route, moe_w1, moe_w3, moe_w2, 1)
    return _combine_call(h1, y, route, gate2, ln_g[1, 1:2], ln_b[1, 1:2], s_len=s, alpha=alpha)
```

```python
import functools

import numpy as np
import jax
import jax.numpy as jnp
from jax import lax
from jax.experimental import pallas as pl
from jax.experimental.pallas import tpu as pltpu

F32 = jnp.float32
BF16 = jnp.bfloat16
HIGHEST = lax.Precision.HIGHEST

GRID_W = 64
HEAD_DIM = 64
ROPE_THETA = 10000.0
WINDOW = 128
NA_ROWS_MAX = 8
NA_COLS = 16
N_GROUPS = 4
EXPERTS_PER_GROUP = 8
N_EXPERTS = N_GROUPS * EXPERTS_PER_GROUP
TOP_K = 2
MOE_BLOCK = 128
LN_EPS = 1e-5
RMS_EPS = 1e-6
ATTN_SCALE = HEAD_DIM ** -0.5
NEG_INF = -1e30
LOG2E = 1.4426950408889634
Q_SCALE = ATTN_SCALE * LOG2E

LANES = 128
PAIR = 2 * HEAD_DIM
assert PAIR == LANES

VMEM_LIMIT = 48 * 1024 * 1024

TM = 256
TQ_DIFF = 256
TQ_GQA = 128
TK = 1024
Q_STRIP = 256
TQ_WIN = 256
NA_QROWS = 4
NA_SLAB = NA_QROWS + NA_ROWS_MAX


def _cparams(n_axes):
    return pltpu.CompilerParams(dimension_semantics=("arbitrary",) * n_axes, vmem_limit_bytes=VMEM_LIMIT)


def _lane_iota():
    return lax.broadcasted_iota(jnp.int32, (1, LANES), 1)


def _mods_kernel(c_ref, w_ref, b_ref, o_ref):
    c = c_ref[...]
    sc = c / (1.0 + jnp.exp(-c))
    o_ref[...] = jnp.dot(sc, w_ref[...], precision=HIGHEST, preferred_element_type=F32) + b_ref[...]


def _mods_call(cvec, mod_w, mod_b):
    depth, d, n6 = mod_w.shape
    tn = n6 // 4
    return pl.pallas_call(
        _mods_kernel,
        grid=(depth, n6 // tn),
        in_specs=[
            pl.BlockSpec((8, d), lambda l, j: (0, 0)),
            pl.BlockSpec((None, d, tn), lambda l, j: (l, 0, j)),
            pl.BlockSpec((None, 1, tn), lambda l, j: (l, 0, j)),
        ],
        out_specs=pl.BlockSpec((None, 8, tn), lambda l, j: (l, 0, j)),
        out_shape=jax.ShapeDtypeStruct((depth, 8, n6), F32),
        compiler_params=_cparams(2),
        name="mods",
    )(cvec, mod_w, mod_b.reshape(depth, 1, n6))


def _inproj_kernel(h_ref, sh_ref, sc_ref, w_ref, wvt_ref, cos_ref, sin_ref, g_ref, o_ref, vt_ref, *, roles):
    a = (h_ref[...] * sc_ref[...] + sh_ref[...]).astype(BF16)
    vt = lax.dot_general(wvt_ref[...], a, (((1,), (1,)), ((), ())), preferred_element_type=F32)
    for j in range(vt_ref.shape[0]):
        vt_ref[j] = vt[j * LANES:(j + 1) * LANES, :].astype(BF16)
    lane = _lane_iota()
    rope_first = (lane & (HEAD_DIM - 1)) < (HEAD_DIM // 2)
    lo = lane < HEAD_DIM
    n_blocks = len(roles)
    for c0 in range(0, n_blocks, 2):
        width = min(2, n_blocks - c0)
        p = jnp.dot(a, w_ref[:, c0 * LANES:(c0 + width) * LANES], preferred_element_type=F32)
        for half in range(width):
            cb = c0 + half
            rope, norm, scale = roles[cb]
            blk = p[:, half * LANES:(half + 1) * LANES]
            if norm is not None:
                sq = blk * blk
                s_lo = jnp.sum(jnp.where(lo, sq, 0.0), axis=-1, keepdims=True)
                s_hi = jnp.sum(jnp.where(lo, 0.0, sq), axis=-1, keepdims=True)
                ms = jnp.where(lo, s_lo, s_hi) * (1.0 / HEAD_DIM)
                blk = blk * lax.rsqrt(ms + RMS_EPS) * g_ref[norm:norm + 1, :]
            if rope:
                partner = jnp.where(rope_first, pltpu.roll(blk, LANES - HEAD_DIM // 2, 1),
                                    pltpu.roll(blk, HEAD_DIM // 2, 1))
                blk = blk * cos_ref[...] + partner * sin_ref[...]
            if scale:
                blk = blk * Q_SCALE
            o_ref[:, cb * LANES:(cb + 1) * LANES] = blk.astype(BF16)


def _inproj_call(h, shift, scale1p, w16, wvt16, cos128, sin128, g128, roles, s_len):
    b, n, d = h.shape
    ncol = w16.shape[1]
    nvb = wvt16.shape[0] // LANES
    n_lat = s_len // TM
    seg = lambda bb, j: (bb, jnp.where(j >= n_lat, 1, 0), 0, 0)
    return pl.pallas_call(
        functools.partial(_inproj_kernel, roles=roles),
        grid=(b, n // TM),
        in_specs=[
            pl.BlockSpec((None, TM, d), lambda bb, j: (bb, j, 0)),
            pl.BlockSpec((None, None, 1, d), seg),
            pl.BlockSpec((None, None, 1, d), seg),
            pl.BlockSpec((d, ncol), lambda bb, j: (0, 0)),
            pl.BlockSpec((nvb * LANES, d), lambda bb, j: (0, 0)),
            pl.BlockSpec((TM, LANES), lambda bb, j: (j, 0)),
            pl.BlockSpec((TM, LANES), lambda bb, j: (j, 0)),
            pl.BlockSpec((2, LANES), lambda bb, j: (0, 0)),
        ],
        out_specs=[
            pl.BlockSpec((None, TM, ncol), lambda bb, j: (bb, j, 0)),
            pl.BlockSpec((None, nvb, LANES, TM), lambda bb, j: (bb, 0, 0, j)),
        ],
        out_shape=[
            jax.ShapeDtypeStruct((b, n, ncol), BF16),
            jax.ShapeDtypeStruct((b, nvb, LANES, n), BF16),
        ],
        compiler_params=_cparams(2),
        name="inproj",
    )(h, shift, scale1p, w16, wvt16, cos128, sin128, g128)


def _stack_q(q_ref, n_qblocks):
    lo = _lane_iota() < HEAD_DIM
    parts = []
    for j in range(n_qblocks):
        q = q_ref[:, j * LANES:(j + 1) * LANES]
        zero = jnp.zeros_like(q)
        parts.append(jnp.where(lo, q, zero))
        parts.append(jnp.where(lo, zero, q))
    return jnp.concatenate(parts, axis=0)


def _scores_t(k, qs):
    return lax.dot_general(k, qs, (((1,), (1,)), ((), ())), preferred_element_type=F32)


def _flash_init(m_ref, l_ref, acc_ref):
    m_ref[...] = jnp.full(m_ref.shape, NEG_INF, F32)
    l_ref[...] = jnp.zeros(l_ref.shape, F32)
    acc_ref[...] = jnp.zeros(acc_ref.shape, F32)


def _flash_update(st, vt, m_ref, l_ref, acc_ref):
    m_prev = m_ref[...]
    m_new = jnp.maximum(m_prev, jnp.max(st, axis=0, keepdims=True))
    alpha = jnp.exp2(m_prev - m_new)
    pt = jnp.exp2(st - m_new)
    l_ref[...] = alpha * l_ref[...] + jnp.sum(pt, axis=0, keepdims=True)
    acc_ref[...] = alpha * acc_ref[...] + jnp.dot(vt, pt.astype(BF16), preferred_element_type=F32)
    m_ref[...] = m_new


def _flash_sweep(qs, k_ref, vt_ref, m_ref, l_ref, acc_ref, *, is_lat, s_len, n_ctx):
    _flash_init(m_ref, l_ref, acc_ref)
    n_chunks = s_len // TK
    k_ctx = k_ref[s_len:s_len + n_ctx, :]
    vt_ctx = vt_ref[:, s_len:s_len + n_ctx]

    @pl.when(is_lat)
    def _():
        chunks = [(k_ref[c * TK:(c + 1) * TK, :], vt_ref[:, c * TK:(c + 1) * TK]) for c in range(n_chunks)]
        chunks.append((k_ctx, vt_ctx))
        s_next = _scores_t(chunks[0][0], qs)
        for idx, (_, vt_c) in enumerate(chunks):
            s_cur = s_next
            if idx + 1 < len(chunks):
                s_next = _scores_t(chunks[idx + 1][0], qs)
            _flash_update(s_cur, vt_c, m_ref, l_ref, acc_ref)

    @pl.when(jnp.logical_not(is_lat))
    def _():
        _flash_update(_scores_t(k_ctx, qs), vt_ctx, m_ref, l_ref, acc_ref)


def _merge_pairs(ot, tq, n_qblocks, o_ref):
    for j in range(n_qblocks):
        o_lo = ot[:HEAD_DIM, (2 * j) * tq:(2 * j + 1) * tq]
        o_hi = ot[HEAD_DIM:, (2 * j + 1) * tq:(2 * j + 2) * tq]
        blk = jnp.concatenate([o_lo, o_hi], axis=0)
        o_ref[:, j * LANES:(j + 1) * LANES] = blk.T.astype(o_ref.dtype)


def _diff_attn_kernel(q_ref, k_ref, vt_ref, lam_ref, g_ref, o_ref, m_ref, l_ref, acc_ref, *, s_len, n_ctx, lam_init):
    tq = q_ref.shape[0]
    i = pl.program_id(2)
    qs = _stack_q(q_ref, 1)
    _flash_sweep(qs, k_ref, vt_ref, m_ref, l_ref, acc_ref, is_lat=i < s_len // tq, s_len=s_len, n_ctx=n_ctx)
    lv = lam_ref[...]
    lam = (jnp.exp(jnp.sum(lv[0:1] * lv[1:2], axis=-1, keepdims=True))
           - jnp.exp(jnp.sum(lv[2:3] * lv[3:4], axis=-1, keepdims=True)) + lam_init)
    ot = acc_ref[...] / l_ref[...]
    od = ot[:, :tq] - lam * ot[:, tq:]
    ms = jnp.mean(od * od, axis=0, keepdims=True)
    od = od * lax.rsqrt(ms + RMS_EPS) * g_ref[...] * (1.0 - lam_init)
    o_ref[...] = od.T.astype(o_ref.dtype)


def _diff_attn_call(p, vt, lam_vecs, subln_g, *, s_len, n_ctx, n_heads, q_blk0, k_blk0, v_blk0, lam_init):
    b, n, _ = p.shape
    tq = TQ_DIFF
    return pl.pallas_call(
        functools.partial(_diff_attn_kernel, s_len=s_len, n_ctx=n_ctx, lam_init=lam_init),
        grid=(b, n_heads, n // tq),
        in_specs=[
            pl.BlockSpec((None, tq, LANES), lambda bb, hh, i: (bb, i, q_blk0 + hh)),
            pl.BlockSpec((None, n, LANES), lambda bb, hh, i: (bb, 0, k_blk0 + hh)),
            pl.BlockSpec((None, None, LANES, n), lambda bb, hh, i: (bb, v_blk0 + hh, 0, 0)),
            pl.BlockSpec((4, HEAD_DIM), lambda bb, hh, i: (0, 0)),
            pl.BlockSpec((LANES, 1), lambda bb, hh, i: (0, 0)),
        ],
        out_specs=pl.BlockSpec((None, tq, LANES), lambda bb, hh, i: (bb, i, hh)),
        out_shape=jax.ShapeDtypeStruct((b, n, n_heads * LANES), BF16),
        scratch_shapes=[pltpu.VMEM((1, 2 * tq), F32), pltpu.VMEM((1, 2 * tq), F32), pltpu.VMEM((LANES, 2 * tq), F32)],
        compiler_params=_cparams(3),
        name="diff_attn",
    )(p, p, vt, lam_vecs, subln_g.reshape(LANES, 1))


def _gqa_attn_kernel(q_ref, k_ref, vt_ref, o_ref, m_ref, l_ref, acc_ref, *, s_len, n_ctx):
    tq = q_ref.shape[0]
    n_qblocks = q_ref.shape[1] // LANES
    i = pl.program_id(1)
    qs = _stack_q(q_ref, n_qblocks)
    _flash_sweep(qs, k_ref, vt_ref, m_ref, l_ref, acc_ref, is_lat=i < s_len // tq, s_len=s_len, n_ctx=n_ctx)
    _merge_pairs(acc_ref[...] / l_ref[...], tq, n_qblocks, o_ref)


def _gqa_attn_call(p, vt, *, s_len, n_ctx, q_width, q_col0, k_blk, v_blk):
    b, n, _ = p.shape
    tq = TQ_GQA
    m_rows = 2 * (q_width // LANES) * tq
    return pl.pallas_call(
        functools.partial(_gqa_attn_kernel, s_len=s_len, n_ctx=n_ctx),
        grid=(b, n // tq),
        in_specs=[
            pl.BlockSpec((None, tq, q_width), lambda bb, i: (bb, i, q_col0 // q_width)),
            pl.BlockSpec((None, n, LANES), lambda bb, i: (bb, 0, k_blk)),
            pl.BlockSpec((None, None, LANES, n), lambda bb, i: (bb, v_blk, 0, 0)),
        ],
        out_specs=pl.BlockSpec((None, tq, q_width), lambda bb, i: (bb, i, 0)),
        out_shape=jax.ShapeDtypeStruct((b, n, q_width), BF16),
        scratch_shapes=[pltpu.VMEM((1, m_rows), F32), pltpu.VMEM((1, m_rows), F32), pltpu.VMEM((LANES, m_rows), F32)],
        compiler_params=_cparams(2),
        name="gqa_attn",
    )(p, p, vt)


def _win_attn_kernel(q_ref, k_ref, vt_ref, sink_ref, o_ref, *, s_len, n_ctx):
    tq = q_ref.shape[0]
    n_qblocks = q_ref.shape[1] // LANES
    span = tq + 2 * WINDOW
    i = pl.program_id(1)
    ws = pl.multiple_of(jnp.clip(i * tq - WINDOW, 0, s_len - span), LANES)
    qs = _stack_q(q_ref, n_qblocks)
    m_rows = qs.shape[0]
    s_loc = _scores_t(k_ref[pl.ds(ws, span), :], qs)
    kpos = ws + lax.broadcasted_iota(jnp.int32, (span, m_rows), 0)
    qpos = i * tq + (lax.broadcasted_iota(jnp.int32, (span, m_rows), 1) & (tq - 1))
    s_loc = jnp.where(jnp.abs(qpos - kpos) <= WINDOW, s_loc, NEG_INF)
    s_ctx = _scores_t(k_ref[s_len:s_len + n_ctx, :], qs)
    sink = sink_ref[...] * LOG2E
    m = jnp.maximum(jnp.maximum(jnp.max(s_loc, axis=0, keepdims=True), jnp.max(s_ctx, axis=0, keepdims=True)), sink)
    p_loc = jnp.exp2(s_loc - m)
    p_ctx = jnp.exp2(s_ctx - m)
    denom = jnp.sum(p_loc, axis=0, keepdims=True) + jnp.sum(p_ctx, axis=0, keepdims=True) + jnp.exp2(sink - m)
    acc = (jnp.dot(vt_ref[:, pl.ds(ws, span)], p_loc.astype(BF16), preferred_element_type=F32)
           + jnp.dot(vt_ref[:, s_len:s_len + n_ctx], p_ctx.astype(BF16), preferred_element_type=F32))
    _merge_pairs(acc / denom, tq, n_qblocks, o_ref)


def _win_attn_call(p, vt, sink_row, *, s_len, n_ctx, q_width, k_blk, v_blk):
    b, n, _ = p.shape
    tq = TQ_WIN
    m_rows = sink_row.shape[1]
    return pl.pallas_call(
        functools.partial(_win_attn_kernel, s_len=s_len, n_ctx=n_ctx),
        grid=(b, s_len // tq),
        in_specs=[
            pl.BlockSpec((None, tq, q_width), lambda bb, i: (bb, i, 0)),
            pl.BlockSpec((None, n, LANES), lambda bb, i: (bb, 0, k_blk)),
            pl.BlockSpec((None, None, LANES, n), lambda bb, i: (bb, v_blk, 0, 0)),
            pl.BlockSpec((1, m_rows), lambda bb, i: (0, 0)),
        ],
        out_specs=pl.BlockSpec((None, tq, q_width), lambda bb, i: (bb, i, 0)),
        out_shape=jax.ShapeDtypeStruct((b, s_len, q_width), BF16),
        compiler_params=_cparams(2),
        name="win_attn",
    )(p, p, vt, sink_row)


def _na_attn_kernel(q_ref, k_ref, vt_ref, bias_ref, o_ref, *, s_len, n_ctx):
    tq = q_ref.shape[0]
    slab = NA_SLAB * GRID_W
    rows = s_len // GRID_W
    mi = pl.program_id(2)
    ss = pl.multiple_of(jnp.clip(NA_QROWS * mi - NA_ROWS_MAX // 2, 0, rows - NA_SLAB) * GRID_W, LANES)
    qs = _stack_q(q_ref, 1)
    s_nb = _scores_t(k_ref[pl.ds(ss, slab), :], qs) + bias_ref[...]
    s_cx = _scores_t(k_ref[s_len:s_len + n_ctx, :], qs)
    m = jnp.maximum(jnp.max(s_nb, axis=0, keepdims=True), jnp.max(s_cx, axis=0, keepdims=True))
    p_nb = jnp.exp2(s_nb - m)
    p_cx = jnp.exp2(s_cx - m)
    denom = jnp.sum(p_nb, axis=0, keepdims=True) + jnp.sum(p_cx, axis=0, keepdims=True)
    acc = (jnp.dot(vt_ref[:, pl.ds(ss, slab)], p_nb.astype(BF16), preferred_element_type=F32)
           + jnp.dot(vt_ref[:, s_len:s_len + n_ctx], p_cx.astype(BF16), preferred_element_type=F32))
    _merge_pairs(acc / denom, tq, 1, o_ref)


def _na_attn_call(p, vt, bias_tab, *, s_len, n_ctx, n_pairs, q_blk0, k_blk0, v_blk0):
    b, n, _ = p.shape
    tq = NA_QROWS * GRID_W
    n_steps = s_len // tq
    tbl = lambda mi: jnp.where(mi == 0, 0, jnp.where(mi == n_steps - 1, 2, 1))
    return pl.pallas_call(
        functools.partial(_na_attn_kernel, s_len=s_len, n_ctx=n_ctx),
        grid=(b, n_pairs, n_steps),
        in_specs=[
            pl.BlockSpec((None, tq, LANES), lambda bb, j, mi: (bb, mi, q_blk0 + j)),
            pl.BlockSpec((None, n, LANES), lambda bb, j, mi: (bb, 0, k_blk0 + j)),
            pl.BlockSpec((None, None, LANES, n), lambda bb, j, mi: (bb, v_blk0 + j, 0, 0)),
            pl.BlockSpec((None, None, NA_SLAB * GRID_W, 2 * tq), lambda bb, j, mi: (tbl(mi), j, 0, 0)),
        ],
        out_specs=pl.BlockSpec((None, tq, LANES), lambda bb, j, mi: (bb, mi, j)),
        out_shape=jax.ShapeDtypeStruct((b, s_len, n_pairs * LANES), BF16),
        compiler_params=_cparams(3),
        name="na_attn",
    )(p, p, vt, bias_tab)


def _na_bias_tables(rpb, rows):
    n_heads = rpb.shape[0]
    rpb = rpb.astype(F32)
    pad = GRID_W - NA_COLS
    rpb_p = jnp.pad(rpb, ((0, 0), (0, 0), (pad, pad)))
    col_tab = jnp.stack([rpb_p[:, :, pad + NA_COLS - 1 - qc: pad + NA_COLS - 1 - qc + GRID_W] for qc in range(GRID_W)],
                        axis=2)
    qr = np.arange(NA_QROWS)[:, None, None, None]
    qc = np.arange(GRID_W)[None, :, None, None]
    kr = np.arange(NA_SLAB)[None, None, :, None]
    kc = np.arange(GRID_W)[None, None, None, :]
    full = (NA_QROWS, GRID_W, NA_SLAB, GRID_W)
    flat = (NA_QROWS * GRID_W, NA_SLAB * GRID_W)
    tabs = []
    for q0, s0 in ((0, 0), (NA_QROWS, NA_QROWS - NA_ROWS_MAX // 2), (rows - NA_QROWS, rows - NA_SLAB)):
        q_row = q0 + qr
        k_row = s0 + kr
        r0 = np.clip(q_row - NA_ROWS_MAX // 2, 0, rows - NA_ROWS_MAX)
        c0 = np.clip(qc - NA_COLS // 2, 0, GRID_W - NA_COLS)
        valid = (k_row >= r0) & (k_row < r0 + NA_ROWS_MAX) & (kc >= c0) & (kc < c0 + NA_COLS)
        valid = np.broadcast_to(valid, full).reshape(flat)
        r_off = np.clip(k_row - q_row + NA_ROWS_MAX - 1, 0, 2 * NA_ROWS_MAX - 2)[:, 0, :, 0]
        bias = jnp.stack([jnp.stack([col_tab[:, int(r_off[a, c])] for c in range(NA_SLAB)], axis=2)
                          for a in range(NA_QROWS)], axis=1)
        tabs.append(jnp.where(valid[None], bias.reshape((n_heads,) + flat) * LOG2E, NEG_INF))
    tab = jnp.stack(tabs)
    return jnp.swapaxes(tab.reshape(3, n_heads // 2, 2 * flat[0], flat[1]), -1, -2)


def _split_hi_lo(x):
    c = x * 65537.0
    hi = c - (c - x)
    return hi.astype(BF16), (x - hi).astype(BF16)


def _layer_norm(y, g, b):
    mu = jnp.mean(y, axis=-1, keepdims=True)
    yc = y - mu
    var = jnp.mean(yc * yc, axis=-1, keepdims=True)
    return yc * lax.rsqrt(var + LN_EPS) * g + b


def _route(logits):
    lane = _lane_iota().astype(F32)
    big = float(LANES)
    is_g = lane < N_GROUPS
    gl = jnp.where(is_g, logits, NEG_INF)
    g_max = jnp.max(gl, axis=-1, keepdims=True)
    g_idx = jnp.min(jnp.where(gl == g_max, lane, big), axis=-1, keepdims=True)
    g_w = 1.0 / jnp.sum(jnp.where(is_g, jnp.exp(gl - g_max), 0.0), axis=-1, keepdims=True)
    base = N_GROUPS + EXPERTS_PER_GROUP * g_idx
    el = jnp.where((lane >= base) & (lane < base + EXPERTS_PER_GROUP), logits, NEG_INF)
    v1 = jnp.max(el, axis=-1, keepdims=True)
    i1 = jnp.min(jnp.where(el == v1, lane, big), axis=-1, keepdims=True)
    el2 = jnp.where(lane == i1, NEG_INF, el)
    v2 = jnp.max(el2, axis=-1, keepdims=True)
    i2 = jnp.min(jnp.where(el2 == v2, lane, big), axis=-1, keepdims=True)
    t = jnp.exp(v2 - v1)
    w1 = g_w / (1.0 + t)
    w2 = g_w * t / (1.0 + t)
    return jnp.where(lane == 0, i1 - N_GROUPS,
                     jnp.where(lane == 1, i2 - N_GROUPS, jnp.where(lane == 2, w1, jnp.where(lane == 3, w2, 0.0))))


def _outproj_kernel(oa_ref, ob_ref, w_ref, h_ref, gate_ref, lng_ref, lnb_ref, sh_ref, sc_ref, wr_ref, br_ref,
                    h1_ref, f_ref, route_ref, *, alpha):
    ka = oa_ref.shape[1]
    o = (jnp.dot(oa_ref[...], w_ref[:ka, :], preferred_element_type=F32)
         + jnp.dot(ob_ref[...], w_ref[ka:, :], preferred_element_type=F32))
    h1 = _layer_norm(alpha * h_ref[...] + gate_ref[...] * o, lng_ref[...], lnb_ref[...])
    h1_ref[...] = h1
    f = h1 * sc_ref[...] + sh_ref[...]
    f_ref[...] = f
    f_hi, f_lo = _split_hi_lo(f)
    hi_prod = jnp.dot(f_hi, wr_ref[...], preferred_element_type=F32)
    lo_prod = jnp.dot(f_lo, wr_ref[...], preferred_element_type=F32)
    logits = (hi_prod[:, :LANES] + hi_prod[:, LANES:]) + (lo_prod[:, :LANES] + lo_prod[:, LANES:]) + br_ref[...]
    route_ref[...] = _route(logits)


def _outproj_call(oa, ob, w16, h, gate, ln_g, ln_b, shift, scale1p, w_route, b_route, *, n_rows, s_len, alpha):
    b, _, d = h.shape
    ka, kb = oa.shape[2], ob.shape[2]
    n_lat = s_len // TM
    seg = lambda bb, j: (bb, jnp.where(j >= n_lat, 1, 0), 0, 0)
    row = lambda bb, j: (bb, j, 0)
    const = lambda bb, j: (0, 0)
    return pl.pallas_call(
        functools.partial(_outproj_kernel, alpha=alpha),
        grid=(b, n_rows // TM),
        in_specs=[
            pl.BlockSpec((None, TM, ka), row),
            pl.BlockSpec((None, TM, kb), row),
            pl.BlockSpec((ka + kb, d), const),
            pl.BlockSpec((None, TM, d), row),
            pl.BlockSpec((None, None, 1, d), seg),
            pl.BlockSpec((1, d), const),
            pl.BlockSpec((1, d), const),
            pl.BlockSpec((None, None, 1, d), seg),
            pl.BlockSpec((None, None, 1, d), seg),
            pl.BlockSpec((d, 2 * LANES), const),
            pl.BlockSpec((1, LANES), const),
        ],
        out_specs=[
            pl.BlockSpec((None, TM, d), row),
            pl.BlockSpec((None, TM, d), row),
            pl.BlockSpec((None, TM, LANES), row),
        ],
        out_shape=[
            jax.ShapeDtypeStruct((b, n_rows, d), F32),
            jax.ShapeDtypeStruct((b, n_rows, d), F32),
            jax.ShapeDtypeStruct((b, n_rows, LANES), F32),
        ],
        compiler_params=_cparams(2),
        name="outproj",
    )(oa, ob, w16, h, gate, ln_g, ln_b, shift, scale1p, w_route, b_route)


def _row_copy(src, src_row, dst, dst_row, sem):
    return pltpu.make_async_copy(src.at[pl.ds(src_row, 1), :], dst.at[pl.ds(dst_row, 1), :], sem)


def _block_copy_for_wait(src, dst, sem):
    return pltpu.make_async_copy(src, dst, sem)


MOE_BUFS = 3
DMA_PRIORITIES = 2


def _moe_kernel(blk_e_ref, tok0_ref, tok1_ref, tok_ahead_ref, dst_prev_ref, dst_last_ref, f_hbm,
                w1_ref, w3_ref, w2_ref, y_hbm, xbuf, ybuf, w1b, w3b, w2b, gsem, ssem):
    i = pl.program_id(0)
    last = pl.num_programs(0) - 1
    cur = lax.rem(i, MOE_BUFS)
    ahead = lax.rem(i + 2, MOE_BUFS)

    def wait_gather(s):
        _block_copy_for_wait(f_hbm.at[pl.ds(0, MOE_BLOCK), :], xbuf.at[s], gsem.at[s]).wait()

    def wait_scatter(s):
        _block_copy_for_wait(ybuf.at[s], y_hbm.at[pl.ds(0, MOE_BLOCK), :], ssem.at[s]).wait()

    def start_gather(tok_ref, s):
        for r in range(MOE_BLOCK):
            _row_copy(f_hbm, tok_ref[0, r], xbuf.at[s], r, gsem.at[s]).start(priority=r % DMA_PRIORITIES)

    def start_scatter(dst_ref, s):
        for r in range(MOE_BLOCK):
            _row_copy(ybuf.at[s], r, y_hbm, dst_ref[0, r], ssem.at[s]).start(priority=r % DMA_PRIORITIES)

    @pl.when(i == 0)
    def _():
        ybuf[...] = jnp.zeros(ybuf.shape, ybuf.dtype)
        start_gather(tok0_ref, 0)
        start_gather(tok1_ref, 1)

    wait_gather(cur)

    @pl.when(i >= 2)
    def _():
        wait_scatter(cur)

    @pl.when(jnp.logical_or(i == 0, blk_e_ref[i] != blk_e_ref[jnp.maximum(i - 1, 0)]))
    def _():
        w1b[...] = w1_ref[...].astype(BF16)
        w3b[...] = w3_ref[...].astype(BF16)
        w2b[...] = w2_ref[...].astype(BF16)

    x = xbuf[cur].astype(BF16)
    start_gather(tok_ahead_ref, ahead)
    start_scatter(dst_prev_ref, ahead)
    h1 = jnp.dot(x, w1b[...], preferred_element_type=F32)
    h3 = jnp.dot(x, w3b[...], preferred_element_type=F32)
    act = (h1 / (1.0 + jnp.exp(-h1))) * h3
    ybuf[cur] = jnp.dot(act.astype(BF16), w2b[...], preferred_element_type=F32)

    @pl.when(i == last)
    def _():
        start_scatter(dst_last_ref, cur)
        for s in range(MOE_BUFS):
            wait_scatter(s)
        wait_gather(lax.rem(i + 1, MOE_BUFS))
        wait_gather(ahead)


def _moe_call(f_rows, blk_e, slot_tok, slot_dst, w1, w3, w2, layer):
    t, d = f_rows.shape
    n_blocks = blk_e.shape[0]
    n_slots = n_blocks * MOE_BLOCK
    de = w1.shape[3]
    assert n_blocks >= MOE_BUFS
    smem_block = lambda index_map: pl.BlockSpec((None, 1, MOE_BLOCK), index_map, memory_space=pltpu.SMEM)
    grid_spec = pltpu.PrefetchScalarGridSpec(
        num_scalar_prefetch=1,
        grid=(n_blocks,),
        in_specs=[
            smem_block(lambda i, be: (0, 0, 0)),
            smem_block(lambda i, be: (1, 0, 0)),
            smem_block(lambda i, be: (jnp.minimum(i + 2, n_blocks - 1), 0, 0)),
            smem_block(lambda i, be: (i, 0, 0)),
            smem_block(lambda i, be: (i + 1, 0, 0)),
            pl.BlockSpec(memory_space=pl.ANY),
            pl.BlockSpec((None, None, d, de), lambda i, be: (layer, be[i], 0, 0)),
            pl.BlockSpec((None, None, d, de), lambda i, be: (layer, be[i], 0, 0)),
            pl.BlockSpec((None, None, de, d), lambda i, be: (layer, be[i], 0, 0)),
        ],
        out_specs=pl.BlockSpec(memory_space=pl.ANY),
        scratch_shapes=[
            pltpu.VMEM((MOE_BUFS, MOE_BLOCK, d), F32),
            pltpu.VMEM((MOE_BUFS, MOE_BLOCK, d), F32),
            pltpu.VMEM((d, de), BF16),
            pltpu.VMEM((d, de), BF16),
            pltpu.VMEM((de, d), BF16),
            pltpu.SemaphoreType.DMA((MOE_BUFS,)),
            pltpu.SemaphoreType.DMA((MOE_BUFS,)),
        ],
    )
    tok3 = slot_tok.reshape(n_blocks, 1, MOE_BLOCK)
    n_rows = n_slots + MOE_BLOCK
    first = (n_slots + jnp.arange(MOE_BLOCK, dtype=jnp.int32))[None, :]
    dst3 = jnp.concatenate([first, slot_dst], axis=0).reshape(n_blocks + 1, 1, MOE_BLOCK)
    return pl.pallas_call(
        _moe_kernel,
        grid_spec=grid_spec,
        out_shape=jax.ShapeDtypeStruct((n_rows, d), F32),
        compiler_params=_cparams(1),
        name="moe_experts",
    )(blk_e, tok3, tok3, tok3, dst3, dst3, f_rows, w1, w3, w2)


def _moe_plan(route, t):
    n_assign = TOP_K * t
    experts = jnp.arange(N_EXPERTS, dtype=jnp.int32)
    e_flat = route[:, :TOP_K].astype(jnp.int32).reshape(n_assign)
    order = jnp.argsort(e_flat, stable=True).astype(jnp.int32)
    counts = jnp.sum((e_flat[:, None] == experts[None, :]).astype(jnp.int32), axis=0)
    starts = jnp.cumsum(counts) - counts
    padded = (counts + MOE_BLOCK - 1) // MOE_BLOCK * MOE_BLOCK
    p_ends = jnp.cumsum(padded)
    p_starts = p_ends - padded
    n_blocks = (n_assign + MOE_BLOCK - 1) // MOE_BLOCK + N_EXPERTS
    blk_start = jnp.arange(n_blocks, dtype=jnp.int32) * MOE_BLOCK
    blk_e = jnp.minimum(jnp.sum((p_ends[None, :] <= blk_start[:, None]).astype(jnp.int32), axis=1), N_EXPERTS - 1)
    of_block = lambda v: jnp.sum(jnp.where(blk_e[:, None] == experts[None, :], v[None, :], 0), axis=1)[:, None]
    slot = blk_start[:, None] + jnp.arange(MOE_BLOCK, dtype=jnp.int32)[None, :]
    j = slot - of_block(p_starts)
    valid = j < of_block(counts)
    a_idx = order[jnp.clip(of_block(starts) + j, 0, n_assign - 1)]
    slot_tok = jnp.where(valid, a_idx // TOP_K, 0).astype(jnp.int32)
    spare = n_assign + slot - of_block(starts + counts)
    slot_dst = jnp.where(valid, (a_idx % TOP_K) * t + a_idx // TOP_K, spare).astype(jnp.int32)
    return blk_e, slot_tok, slot_dst


def _combine_kernel(h_ref, y0_ref, y1_ref, route_ref, gate_ref, lng_ref, lnb_ref, o_ref, *, alpha):
    r = route_ref[...]
    y = r[:, 2:3] * y0_ref[...] + r[:, 3:4] * y1_ref[...]
    o_ref[...] = _layer_norm(alpha * h_ref[...] + gate_ref[...] * y, lng_ref[...], lnb_ref[...])


def _combine_call(h1, y, route, gate, ln_g, ln_b, *, s_len, alpha):
    b, n_rows, d = h1.shape
    n_lat = s_len // TM
    seg = lambda bb, j: (bb, jnp.where(j >= n_lat, 1, 0), 0, 0)
    row = lambda bb, j: (bb, j, 0)
    const = lambda bb, j: (0, 0)
    blocks_per_batch = n_rows // TM
    blocks_per_k = b * blocks_per_batch
    return pl.pallas_call(
        functools.partial(_combine_kernel, alpha=alpha),
        grid=(b, n_rows // TM),
        in_specs=[
            pl.BlockSpec((None, TM, d), row),
            pl.BlockSpec((TM, d), lambda bb, j: (bb * blocks_per_batch + j, 0)),
            pl.BlockSpec((TM, d), lambda bb, j: (blocks_per_k + bb * blocks_per_batch + j, 0)),
            pl.BlockSpec((None, TM, LANES), row),
            pl.BlockSpec((None, None, 1, d), seg),
            pl.BlockSpec((1, d), const),
            pl.BlockSpec((1, d), const),
        ],
        out_specs=pl.BlockSpec((None, TM, d), row),
        out_shape=jax.ShapeDtypeStruct((b, n_rows, d), F32),
        compiler_params=_cparams(2),
        name="combine",
    )(h1, y, y, route, gate, ln_g, ln_b)


def _pair_heads(w, axis):
    w = jnp.moveaxis(w, axis, -1)
    lead = w.shape[:-1]
    n_heads = w.shape[-1] // HEAD_DIM
    w = w.reshape(lead + (2, n_heads // 2, HEAD_DIM)).swapaxes(-3, -2).reshape(lead + (n_heads * HEAD_DIM,))
    return jnp.moveaxis(w, -1, axis)


def _rope_tables(s_len, n_ctx):
    t = jnp.arange(s_len, dtype=jnp.int32)
    row = (t // GRID_W).astype(F32)
    col = (t % GRID_W).astype(F32)
    n_freq = HEAD_DIM // 4
    inv = ROPE_THETA ** (-jnp.arange(n_freq, dtype=F32) / n_freq)
    ang = jnp.concatenate([row[:, None] * inv, col[:, None] * inv], -1)
    cos = jnp.concatenate([jnp.cos(ang), jnp.ones((n_ctx, HEAD_DIM // 2), F32)], 0)
    sin = jnp.concatenate([jnp.sin(ang), jnp.zeros((n_ctx, HEAD_DIM // 2), F32)], 0)
    return jnp.tile(cos, (1, 4)), jnp.concatenate([-sin, sin, -sin, sin], -1)


def _lambda_init(layer_idx):
    return 0.8 - 0.6 * float(np.exp(-0.3 * layer_idx))


def _router_params(w_group, b_group, w_router, b_router):
    d = w_group.shape[0]
    pad = LANES - N_GROUPS - N_EXPERTS
    w = jnp.concatenate([w_group, w_router, jnp.zeros((d, pad), F32)], axis=1)
    bb = jnp.concatenate([b_group, b_router, jnp.zeros((pad,), F32)])
    w_hi, w_lo = _split_hi_lo(w)
    return jnp.concatenate([w_hi, w_lo], axis=1), bb.reshape(1, LANES)


def _moe_layer(f, route, w1, w3, w2, layer):
    b, n_rows, d = f.shape
    t = b * n_rows
    blk_e, slot_tok, slot_dst = _moe_plan(route.reshape(t, LANES), t)
    return _moe_call(f.reshape(t, d), blk_e, slot_tok, slot_dst, w1, w3, w2, layer)


def kernel(x, c, ctx, c_ctx, mod_w, mod_b, ln_g, ln_b, ab_w_in, ab_w_out, diff_lambda, diff_subln_g, gqa_qk_g,
           cd_w_in, cd_w_out, win_sink, na_rpb, moe_w_group, moe_b_group, moe_w_router, moe_b_router,
           moe_w1, moe_w3, moe_w2):
    b, s, d = x.shape
    n_ctx = ctx.shape[1]
    n = s + n_ctx
    depth = mod_w.shape[0]
    rows = s // GRID_W
    assert depth == 2, "layer pattern implemented: one differential/GQA layer, then one window/neighbourhood layer"
    assert s % TM == 0 and n_ctx % TM == 0 and s % TK == 0 and b + 1 <= 8
    assert n_ctx % TQ_DIFF == 0 and n_ctx % TQ_GQA == 0
    assert rows >= NA_SLAB and rows % NA_QROWS == 0 and s % GRID_W == 0
    alpha = (2.0 * depth) ** 0.25
    qw = d // 2
    kvw = qw // 4
    assert qw == 4 * LANES and kvw == LANES

    cos128, sin128 = _rope_tables(s, n_ctx)
    cvec = jnp.concatenate([c, c_ctx[None, :], jnp.zeros((8 - b - 1, d), F32)], axis=0)
    mods_all = _mods_call(cvec, mod_w, mod_b)

    def seg_mods(i):
        m = mods_all[i]
        lat = m[:b].reshape(b, 6, d)
        cm = jnp.broadcast_to(m[b].reshape(1, 6, d), (b, 6, d))
        ms = jnp.stack([lat, cm], axis=1)
        pick = lambda k, one: (ms[:, :, k] + one)[:, :, None, :]
        return pick(0, 0.0), pick(1, 1.0), pick(2, 0.0), pick(3, 0.0), pick(4, 1.0), pick(5, 0.0)

    h = jnp.concatenate([x, ctx], axis=1)

    shift1, scale1p, gate1, shift2, scale2p, gate2 = seg_mods(0)
    w_in = ab_w_in[0]
    q_d, q_g, k_d, v_d = (w_in[:, k * qw:(k + 1) * qw] for k in range(4))
    k_g, v_g = w_in[:, 4 * qw:4 * qw + kvw], w_in[:, 4 * qw + kvw:]
    w_qk = jnp.concatenate([q_d, _pair_heads(q_g, 1), k_d, k_g], axis=1).astype(BF16)
    roles0 = ((True, None, True),) * 4 + ((True, 0, True),) * 4 + ((True, None, False),) * 4 + ((True, 1, False),)
    w_vt = jnp.concatenate([v_d, v_g], axis=1).T.astype(BF16)
    g128 = jnp.tile(gqa_qk_g[0], (1, 2))
    p0, vt0 = _inproj_call(h, shift1, scale1p, w_qk, w_vt, cos128, sin128, g128, roles0, s)
    o_d = _diff_attn_call(p0, vt0, diff_lambda[0], diff_subln_g[0], s_len=s, n_ctx=n_ctx, n_heads=4,
                          q_blk0=0, k_blk0=8, v_blk0=0, lam_init=_lambda_init(0))
    o_g = _gqa_attn_call(p0, vt0, s_len=s, n_ctx=n_ctx, q_width=qw, q_col0=qw, k_blk=12, v_blk=4)
    w_out = jnp.concatenate([ab_w_out[0][:qw], _pair_heads(ab_w_out[0][qw:], 0)], axis=0).astype(BF16)
    w_route, b_route = _router_params(moe_w_group[0], moe_b_group[0], moe_w_router[0], moe_b_router[0])
    h1, f, route = _outproj_call(o_d, o_g, w_out, h, gate1, ln_g[0, 0:1], ln_b[0, 0:1], shift2, scale2p,
                                 w_route, b_route, n_rows=n, s_len=s, alpha=alpha)
    y = _moe_layer(f, route, moe_w1, moe_w3, moe_w2, 0)
    h = _combine_call(h1, y, route, gate2, ln_g[0, 1:2], ln_b[0, 1:2], s_len=s, alpha=alpha)

    shift1, scale1p, gate1, shift2, scale2p, gate2 = seg_mods(1)
    w_in = cd_w_in[0]
    q_w, q_n = w_in[:, :qw], w_in[:, qw:2 * qw]
    k_w, v_w = w_in[:, 2 * qw:2 * qw + kvw], w_in[:, 2 * qw + kvw:2 * qw + 2 * kvw]
    k_n, v_n = w_in[:, 2 * qw + 2 * kvw:3 * qw + 2 * kvw], w_in[:, 3 * qw + 2 * kvw:]
    w_qk = jnp.concatenate([_pair_heads(q_w, 1), q_n, k_w, k_n], axis=1).astype(BF16)
    roles1 = ((True, None, True),) * 4 + ((False, None, True),) * 4 + ((True, None, False),) \
        + ((False, None, False),) * 4
    w_vt = jnp.concatenate([v_w, v_n], axis=1).T.astype(BF16)
    p1, vt1 = _inproj_call(h, shift1, scale1p, w_qk, w_vt, cos128, sin128, jnp.ones((2, LANES), F32), roles1, s)
    sink_row = jnp.repeat(win_sink[0].reshape(2, 4).T.reshape(8), TQ_WIN).reshape(1, 8 * TQ_WIN)
    o_w = _win_attn_call(p1, vt1, sink_row, s_len=s, n_ctx=n_ctx, q_width=qw, k_blk=8, v_blk=0)
    o_n = _na_attn_call(p1, vt1, _na_bias_tables(na_rpb[0], rows), s_len=s, n_ctx=n_ctx, n_pairs=4,
                        q_blk0=4, k_blk0=9, v_blk0=1)
    w_out = jnp.concatenate([_pair_heads(cd_w_out[0][:qw], 0), cd_w_out[0][qw:]], axis=0).astype(BF16)
    w_route, b_route = _router_params(moe_w_group[1], moe_b_group[1], moe_w_router[1], moe_b_router[1])
    h1, f, route = _outproj_call(o_w, o_n, w_out, h, gate1, ln_g[1, 0:1], ln_b[1, 0:1], shift2, scale2p,
                                 w_route, b_route, n_rows=s, s_len=s, alpha=alpha)
    y = _moe_layer(f, route, moe_w1, moe_w3, moe_w2, 1)
    return _combine_call(h1, y, route, gate2, ln_g[1, 1:2], ln_b[1, 1:2], s_len=s, alpha=alpha)
```

```python
import functools

import numpy as np
import jax
import jax.numpy as jnp
from jax import lax
from jax.experimental import pallas as pl
from jax.experimental.pallas import tpu as pltpu

F32 = jnp.float32
BF16 = jnp.bfloat16
HIGHEST = lax.Precision.HIGHEST

GRID_W = 64
HEAD_DIM = 64
ROPE_THETA = 10000.0
WINDOW = 128
NA_ROWS_MAX = 8
NA_COLS = 16
N_GROUPS = 4
EXPERTS_PER_GROUP = 8
N_EXPERTS = N_GROUPS * EXPERTS_PER_GROUP
TOP_K = 2
MOE_BLOCK = 128
LN_EPS = 1e-5
RMS_EPS = 1e-6
ATTN_SCALE = HEAD_DIM ** -0.5
NEG_INF = -1e30
LOG2E = 1.4426950408889634
Q_SCALE = ATTN_SCALE * LOG2E

LANES = 128
PAIR = 2 * HEAD_DIM
assert PAIR == LANES

VMEM_LIMIT = 48 * 1024 * 1024

TM = 256
TQ_DIFF = 256
TQ_GQA = 128
TK = 1024
Q_STRIP = 256
TQ_WIN = 256
NA_QROWS = 4
NA_SLAB = NA_QROWS + NA_ROWS_MAX


def _cparams(n_axes):
    return pltpu.CompilerParams(dimension_semantics=("arbitrary",) * n_axes, vmem_limit_bytes=VMEM_LIMIT)


def _lane_iota():
    return lax.broadcasted_iota(jnp.int32, (1, LANES), 1)


def _mods_kernel(c_ref, w_ref, b_ref, o_ref):
    c = c_ref[...]
    sc = c / (1.0 + jnp.exp(-c))
    o_ref[...] = jnp.dot(sc, w_ref[...], precision=HIGHEST, preferred_element_type=F32) + b_ref[...]


def _mods_call(cvec, mod_w, mod_b):
    depth, d, n6 = mod_w.shape
    tn = n6 // 4
    return pl.pallas_call(
        _mods_kernel,
        grid=(depth, n6 // tn),
        in_specs=[
            pl.BlockSpec((8, d), lambda l, j: (0, 0)),
            pl.BlockSpec((None, d, tn), lambda l, j: (l, 0, j)),
            pl.BlockSpec((None, 1, tn), lambda l, j: (l, 0, j)),
        ],
        out_specs=pl.BlockSpec((None, 8, tn), lambda l, j: (l, 0, j)),
        out_shape=jax.ShapeDtypeStruct((depth, 8, n6), F32),
        compiler_params=_cparams(2),
        name="mods",
    )(cvec, mod_w, mod_b.reshape(depth, 1, n6))


def _inproj_kernel(h_ref, sh_ref, sc_ref, w_ref, wvt_ref, cos_ref, sin_ref, g_ref, o_ref, vt_ref, *, roles):
    a = (h_ref[...] * sc_ref[...] + sh_ref[...]).astype(BF16)
    vt = lax.dot_general(wvt_ref[...], a, (((1,), (1,)), ((), ())), preferred_element_type=F32)
    for j in range(vt_ref.shape[0]):
        vt_ref[j] = vt[j * LANES:(j + 1) * LANES, :].astype(BF16)
    lane = _lane_iota()
    rope_first = (lane & (HEAD_DIM - 1)) < (HEAD_DIM // 2)
    lo = lane < HEAD_DIM
    n_blocks = len(roles)
    for c0 in range(0, n_blocks, 2):
        width = min(2, n_blocks - c0)
        p = jnp.dot(a, w_ref[:, c0 * LANES:(c0 + width) * LANES], preferred_element_type=F32)
        for half in range(width):
            cb = c0 + half
            rope, norm, scale = roles[cb]
            blk = p[:, half * LANES:(half + 1) * LANES]
            if norm is not None:
                sq = blk * blk
                s_lo = jnp.sum(jnp.where(lo, sq, 0.0), axis=-1, keepdims=True)
                s_hi = jnp.sum(jnp.where(lo, 0.0, sq), axis=-1, keepdims=True)
                ms = jnp.where(lo, s_lo, s_hi) * (1.0 / HEAD_DIM)
                blk = blk * lax.rsqrt(ms + RMS_EPS) * g_ref[norm:norm + 1, :]
            if rope:
                partner = jnp.where(rope_first, pltpu.roll(blk, LANES - HEAD_DIM // 2, 1),
                                    pltpu.roll(blk, HEAD_DIM // 2, 1))
                blk = blk * cos_ref[...] + partner * sin_ref[...]
            if scale:
                blk = blk * Q_SCALE
            o_ref[:, cb * LANES:(cb + 1) * LANES] = blk.astype(BF16)


def _inproj_call(h, shift, scale1p, w16, wvt16, cos128, sin128, g128, roles, s_len):
    b, n, d = h.shape
    ncol = w16.shape[1]
    nvb = wvt16.shape[0] // LANES
    n_lat = s_len // TM
    seg = lambda bb, j: (bb, jnp.where(j >= n_lat, 1, 0), 0, 0)
    return pl.pallas_call(
        functools.partial(_inproj_kernel, roles=roles),
        grid=(b, n // TM),
        in_specs=[
            pl.BlockSpec((None, TM, d), lambda bb, j: (bb, j, 0)),
            pl.BlockSpec((None, None, 1, d), seg),
            pl.BlockSpec((None, None, 1, d), seg),
            pl.BlockSpec((d, ncol), lambda bb, j: (0, 0)),
            pl.BlockSpec((nvb * LANES, d), lambda bb, j: (0, 0)),
            pl.BlockSpec((TM, LANES), lambda bb, j: (j, 0)),
            pl.BlockSpec((TM, LANES), lambda bb, j: (j, 0)),
            pl.BlockSpec((2, LANES), lambda bb, j: (0, 0)),
        ],
        out_specs=[
            pl.BlockSpec((None, TM, ncol), lambda bb, j: (bb, j, 0)),
            pl.BlockSpec((None, nvb, LANES, TM), lambda bb, j: (bb, 0, 0, j)),
        ],
        out_shape=[
            jax.ShapeDtypeStruct((b, n, ncol), BF16),
            jax.ShapeDtypeStruct((b, nvb, LANES, n), BF16),
        ],
        compiler_params=_cparams(2),
        name="inproj",
    )(h, shift, scale1p, w16, wvt16, cos128, sin128, g128)


def _stack_q(q_ref, n_qblocks):
    lo = _lane_iota() < HEAD_DIM
    parts = []
    for j in range(n_qblocks):
        q = q_ref[:, j * LANES:(j + 1) * LANES]
        zero = jnp.zeros_like(q)
        parts.append(jnp.where(lo, q, zero))
        parts.append(jnp.where(lo, zero, q))
    return jnp.concatenate(parts, axis=0)


def _scores_t(k, qs):
    return lax.dot_general(k, qs, (((1,), (1,)), ((), ())), preferred_element_type=F32)


def _flash_init(m_ref, l_ref, acc_ref):
    m_ref[...] = jnp.full(m_ref.shape, NEG_INF, F32)
    l_ref[...] = jnp.zeros(l_ref.shape, F32)
    acc_ref[...] = jnp.zeros(acc_ref.shape, F32)


def _flash_update(st, vt, m_ref, l_ref, acc_ref):
    m_prev = m_ref[...]
    m_new = jnp.maximum(m_prev, jnp.max(st, axis=0, keepdims=True))
    alpha = jnp.exp2(m_prev - m_new)
    pt = jnp.exp2(st - m_new)
    l_ref[...] = alpha * l_ref[...] + jnp.sum(pt, axis=0, keepdims=True)
    acc_ref[...] = alpha * acc_ref[...] + jnp.dot(vt, pt.astype(BF16), preferred_element_type=F32)
    m_ref[...] = m_new


def _flash_sweep(qs, k_ref, vt_ref, m_ref, l_ref, acc_ref, *, is_lat, s_len, n_ctx):
    _flash_init(m_ref, l_ref, acc_ref)
    n_chunks = s_len // TK
    k_ctx = k_ref[s_len:s_len + n_ctx, :]
    vt_ctx = vt_ref[:, s_len:s_len + n_ctx]

    @pl.when(is_lat)
    def _():
        chunks = [(k_ref[c * TK:(c + 1) * TK, :], vt_ref[:, c * TK:(c + 1) * TK]) for c in range(n_chunks)]
        chunks.append((k_ctx, vt_ctx))
        s_next = _scores_t(chunks[0][0], qs)
        for idx, (_, vt_c) in enumerate(chunks):
            s_cur = s_next
            if idx + 1 < len(chunks):
                s_next = _scores_t(chunks[idx + 1][0], qs)
            _flash_update(s_cur, vt_c, m_ref, l_ref, acc_ref)

    @pl.when(jnp.logical_not(is_lat))
    def _():
        _flash_update(_scores_t(k_ctx, qs), vt_ctx, m_ref, l_ref, acc_ref)


def _merge_pairs(ot, tq, n_qblocks, o_ref):
    for j in range(n_qblocks):
        o_lo = ot[:HEAD_DIM, (2 * j) * tq:(2 * j + 1) * tq]
        o_hi = ot[HEAD_DIM:, (2 * j + 1) * tq:(2 * j + 2) * tq]
        blk = jnp.concatenate([o_lo, o_hi], axis=0)
        o_ref[:, j * LANES:(j + 1) * LANES] = blk.T.astype(o_ref.dtype)


def _diff_attn_kernel(q_ref, k_ref, vt_ref, lam_ref, g_ref, o_ref, m_ref, l_ref, acc_ref, *, s_len, n_ctx, lam_init):
    tq = q_ref.shape[0]
    i = pl.program_id(2)
    qs = _stack_q(q_ref, 1)
    _flash_sweep(qs, k_ref, vt_ref, m_ref, l_ref, acc_ref, is_lat=i < s_len // tq, s_len=s_len, n_ctx=n_ctx)
    lv = lam_ref[...]
    lam = (jnp.exp(jnp.sum(lv[0:1] * lv[1:2], axis=-1, keepdims=True))
           - jnp.exp(jnp.sum(lv[2:3] * lv[3:4], axis=-1, keepdims=True)) + lam_init)
    ot = acc_ref[...] / l_ref[...]
    od = ot[:, :tq] - lam * ot[:, tq:]
    ms = jnp.mean(od * od, axis=0, keepdims=True)
    od = od * lax.rsqrt(ms + RMS_EPS) * g_ref[...] * (1.0 - lam_init)
    o_ref[...] = od.T.astype(o_ref.dtype)


def _diff_attn_call(p, vt, lam_vecs, subln_g, *, s_len, n_ctx, n_heads, q_blk0, k_blk0, v_blk0, lam_init):
    b, n, _ = p.shape
    tq = TQ_DIFF
    return pl.pallas_call(
        functools.partial(_diff_attn_kernel, s_len=s_len, n_ctx=n_ctx, lam_init=lam_init),
        grid=(b, n_heads, n // tq),
        in_specs=[
            pl.BlockSpec((None, tq, LANES), lambda bb, hh, i: (bb, i, q_blk0 + hh)),
            pl.BlockSpec((None, n, LANES), lambda bb, hh, i: (bb, 0, k_blk0 + hh)),
            pl.BlockSpec((None, None, LANES, n), lambda bb, hh, i: (bb, v_blk0 + hh, 0, 0)),
            pl.BlockSpec((4, HEAD_DIM), lambda bb, hh, i: (0, 0)),
            pl.BlockSpec((LANES, 1), lambda bb, hh, i: (0, 0)),
        ],
        out_specs=pl.BlockSpec((None, tq, LANES), lambda bb, hh, i: (bb, i, hh)),
        out_shape=jax.ShapeDtypeStruct((b, n, n_heads * LANES), BF16),
        scratch_shapes=[pltpu.VMEM((1, 2 * tq), F32), pltpu.VMEM((1, 2 * tq), F32), pltpu.VMEM((LANES, 2 * tq), F32)],
        compiler_params=_cparams(3),
        name="diff_attn",
    )(p, p, vt, lam_vecs, subln_g.reshape(LANES, 1))


def _gqa_attn_kernel(q_ref, k_ref, vt_ref, o_ref, m_ref, l_ref, acc_ref, *, s_len, n_ctx):
    tq = q_ref.shape[0]
    n_qblocks = q_ref.shape[1] // LANES
    i = pl.program_id(1)
    qs = _stack_q(q_ref, n_qblocks)
    _flash_sweep(qs, k_ref, vt_ref, m_ref, l_ref, acc_ref, is_lat=i < s_len // tq, s_len=s_len, n_ctx=n_ctx)
    _merge_pairs(acc_ref[...] / l_ref[...], tq, n_qblocks, o_ref)


def _gqa_attn_call(p, vt, *, s_len, n_ctx, q_width, q_col0, k_blk, v_blk):
    b, n, _ = p.shape
    tq = TQ_GQA
    m_rows = 2 * (q_width // LANES) * tq
    return pl.pallas_call(
        functools.partial(_gqa_attn_kernel, s_len=s_len, n_ctx=n_ctx),
        grid=(b, n // tq),
        in_specs=[
            pl.BlockSpec((None, tq, q_width), lambda bb, i: (bb, i, q_col0 // q_width)),
            pl.BlockSpec((None, n, LANES), lambda bb, i: (bb, 0, k_blk)),
            pl.BlockSpec((None, None, LANES, n), lambda bb, i: (bb, v_blk, 0, 0)),
        ],
        out_specs=pl.BlockSpec((None, tq, q_width), lambda bb, i: (bb, i, 0)),
        out_shape=jax.ShapeDtypeStruct((b, n, q_width), BF16),
        scratch_shapes=[pltpu.VMEM((1, m_rows), F32), pltpu.VMEM((1, m_rows), F32), pltpu.VMEM((LANES, m_rows), F32)],
        compiler_params=_cparams(2),
        name="gqa_attn",
    )(p, p, vt)


def _win_attn_kernel(q_ref, k_ref, vt_ref, sink_ref, o_ref, *, s_len, n_ctx):
    tq = q_ref.shape[0]
    n_qblocks = q_ref.shape[1] // LANES
    span = tq + 2 * WINDOW
    i = pl.program_id(1)
    ws = pl.multiple_of(jnp.clip(i * tq - WINDOW, 0, s_len - span), LANES)
    qs = _stack_q(q_ref, n_qblocks)
    m_rows = qs.shape[0]
    s_loc = _scores_t(k_ref[pl.ds(ws, span), :], qs)
    kpos = ws + lax.broadcasted_iota(jnp.int32, (span, m_rows), 0)
    qpos = i * tq + (lax.broadcasted_iota(jnp.int32, (span, m_rows), 1) & (tq - 1))
    s_loc = jnp.where(jnp.abs(qpos - kpos) <= WINDOW, s_loc, NEG_INF)
    s_ctx = _scores_t(k_ref[s_len:s_len + n_ctx, :], qs)
    sink = sink_ref[...] * LOG2E
    m = jnp.maximum(jnp.maximum(jnp.max(s_loc, axis=0, keepdims=True), jnp.max(s_ctx, axis=0, keepdims=True)), sink)
    p_loc = jnp.exp2(s_loc - m)
    p_ctx = jnp.exp2(s_ctx - m)
    denom = jnp.sum(p_loc, axis=0, keepdims=True) + jnp.sum(p_ctx, axis=0, keepdims=True) + jnp.exp2(sink - m)
    acc = (jnp.dot(vt_ref[:, pl.ds(ws, span)], p_loc.astype(BF16), preferred_element_type=F32)
           + jnp.dot(vt_ref[:, s_len:s_len + n_ctx], p_ctx.astype(BF16), preferred_element_type=F32))
    _merge_pairs(acc / denom, tq, n_qblocks, o_ref)


def _win_attn_call(p, vt, sink_row, *, s_len, n_ctx, q_width, k_blk, v_blk):
    b, n, _ = p.shape
    tq = TQ_WIN
    m_rows = sink_row.shape[1]
    return pl.pallas_call(
        functools.partial(_win_attn_kernel, s_len=s_len, n_ctx=n_ctx),
        grid=(b, s_len // tq),
        in_specs=[
            pl.BlockSpec((None, tq, q_width), lambda bb, i: (bb, i, 0)),
            pl.BlockSpec((None, n, LANES), lambda bb, i: (bb, 0, k_blk)),
            pl.BlockSpec((None, None, LANES, n), lambda bb, i: (bb, v_blk, 0, 0)),
            pl.BlockSpec((1, m_rows), lambda bb, i: (0, 0)),
        ],
        out_specs=pl.BlockSpec((None, tq, q_width), lambda bb, i: (bb, i, 0)),
        out_shape=jax.ShapeDtypeStruct((b, s_len, q_width), BF16),
        compiler_params=_cparams(2),
        name="win_attn",
    )(p, p, vt, sink_row)


def _na_attn_kernel(q_ref, k_ref, vt_ref, bias_ref, o_ref, *, s_len, n_ctx):
    tq = q_ref.shape[0]
    slab = NA_SLAB * GRID_W
    rows = s_len // GRID_W
    mi = pl.program_id(2)
    ss = pl.multiple_of(jnp.clip(NA_QROWS * mi - NA_ROWS_MAX // 2, 0, rows - NA_SLAB) * GRID_W, LANES)
    qs = _stack_q(q_ref, 1)
    s_nb = _scores_t(k_ref[pl.ds(ss, slab), :], qs) + bias_ref[...]
    s_cx = _scores_t(k_ref[s_len:s_len + n_ctx, :], qs)
    m = jnp.maximum(jnp.max(s_nb, axis=0, keepdims=True), jnp.max(s_cx, axis=0, keepdims=True))
    p_nb = jnp.exp2(s_nb - m)
    p_cx = jnp.exp2(s_cx - m)
    denom = jnp.sum(p_nb, axis=0, keepdims=True) + jnp.sum(p_cx, axis=0, keepdims=True)
    acc = (jnp.dot(vt_ref[:, pl.ds(ss, slab)], p_nb.astype(BF16), preferred_element_type=F32)
           + jnp.dot(vt_ref[:, s_len:s_len + n_ctx], p_cx.astype(BF16), preferred_element_type=F32))
    _merge_pairs(acc / denom, tq, 1, o_ref)


def _na_attn_call(p, vt, bias_tab, *, s_len, n_ctx, n_pairs, q_blk0, k_blk0, v_blk0):
    b, n, _ = p.shape
    tq = NA_QROWS * GRID_W
    n_steps = s_len // tq
    tbl = lambda mi: jnp.where(mi == 0, 0, jnp.where(mi == n_steps - 1, 2, 1))
    return pl.pallas_call(
        functools.partial(_na_attn_kernel, s_len=s_len, n_ctx=n_ctx),
        grid=(b, n_pairs, n_steps),
        in_specs=[
            pl.BlockSpec((None, tq, LANES), lambda bb, j, mi: (bb, mi, q_blk0 + j)),
            pl.BlockSpec((None, n, LANES), lambda bb, j, mi: (bb, 0, k_blk0 + j)),
            pl.BlockSpec((None, None, LANES, n), lambda bb, j, mi: (bb, v_blk0 + j, 0, 0)),
            pl.BlockSpec((None, None, NA_SLAB * GRID_W, 2 * tq), lambda bb, j, mi: (tbl(mi), j, 0, 0)),
        ],
        out_specs=pl.BlockSpec((None, tq, LANES), lambda bb, j, mi: (bb, mi, j)),
        out_shape=jax.ShapeDtypeStruct((b, s_len, n_pairs * LANES), BF16),
        compiler_params=_cparams(3),
        name="na_attn",
    )(p, p, vt, bias_tab)


def _na_bias_tables(rpb, rows):
    n_heads = rpb.shape[0]
    rpb = rpb.astype(F32)
    pad = GRID_W - NA_COLS
    rpb_p = jnp.pad(rpb, ((0, 0), (0, 0), (pad, pad)))
    col_tab = jnp.stack([rpb_p[:, :, pad + NA_COLS - 1 - qc: pad + NA_COLS - 1 - qc + GRID_W] for qc in range(GRID_W)],
                        axis=2)
    qr = np.arange(NA_QROWS)[:, None, None, None]
    qc = np.arange(GRID_W)[None, :, None, None]
    kr = np.arange(NA_SLAB)[None, None, :, None]
    kc = np.arange(GRID_W)[None, None, None, :]
    full = (NA_QROWS, GRID_W, NA_SLAB, GRID_W)
    flat = (NA_QROWS * GRID_W, NA_SLAB * GRID_W)
    tabs = []
    for q0, s0 in ((0, 0), (NA_QROWS, NA_QROWS - NA_ROWS_MAX // 2), (rows - NA_QROWS, rows - NA_SLAB)):
        q_row = q0 + qr
        k_row = s0 + kr
        r0 = np.clip(q_row - NA_ROWS_MAX // 2, 0, rows - NA_ROWS_MAX)
        c0 = np.clip(qc - NA_COLS // 2, 0, GRID_W - NA_COLS)
        valid = (k_row >= r0) & (k_row < r0 + NA_ROWS_MAX) & (kc >= c0) & (kc < c0 + NA_COLS)
        valid = np.broadcast_to(valid, full).reshape(flat)
        r_off = np.clip(k_row - q_row + NA_ROWS_MAX - 1, 0, 2 * NA_ROWS_MAX - 2)[:, 0, :, 0]
        bias = jnp.stack([jnp.stack([col_tab[:, int(r_off[a, c])] for c in range(NA_SLAB)], axis=2)
                          for a in range(NA_QROWS)], axis=1)
        tabs.append(jnp.where(valid[None], bias.reshape((n_heads,) + flat) * LOG2E, NEG_INF))
    tab = jnp.stack(tabs)
    return jnp.swapaxes(tab.reshape(3, n_heads // 2, 2 * flat[0], flat[1]), -1, -2)


def _store_row_tiles(ref, x):
    for s in range(ref.shape[1]):
        ref[:, s, :] = x[:, s * LANES:(s + 1) * LANES]


def _load_row_tiles(ref):
    return jnp.concatenate([ref[:, s, :] for s in range(ref.shape[1])], axis=1)


def _split_hi_lo(x):
    c = x * 65537.0
    hi = c - (c - x)
    return hi.astype(BF16), (x - hi).astype(BF16)


def _layer_norm(y, g, b):
    mu = jnp.mean(y, axis=-1, keepdims=True)
    yc = y - mu
    var = jnp.mean(yc * yc, axis=-1, keepdims=True)
    return yc * lax.rsqrt(var + LN_EPS) * g + b


def _route(logits):
    lane = _lane_iota().astype(F32)
    big = float(LANES)
    is_g = lane < N_GROUPS
    gl = jnp.where(is_g, logits, NEG_INF)
    g_max = jnp.max(gl, axis=-1, keepdims=True)
    g_idx = jnp.min(jnp.where(gl == g_max, lane, big), axis=-1, keepdims=True)
    g_w = 1.0 / jnp.sum(jnp.where(is_g, jnp.exp(gl - g_max), 0.0), axis=-1, keepdims=True)
    base = N_GROUPS + EXPERTS_PER_GROUP * g_idx
    el = jnp.where((lane >= base) & (lane < base + EXPERTS_PER_GROUP), logits, NEG_INF)
    v1 = jnp.max(el, axis=-1, keepdims=True)
    i1 = jnp.min(jnp.where(el == v1, lane, big), axis=-1, keepdims=True)
    el2 = jnp.where(lane == i1, NEG_INF, el)
    v2 = jnp.max(el2, axis=-1, keepdims=True)
    i2 = jnp.min(jnp.where(el2 == v2, lane, big), axis=-1, keepdims=True)
    t = jnp.exp(v2 - v1)
    w1 = g_w / (1.0 + t)
    w2 = g_w * t / (1.0 + t)
    return jnp.where(lane == 0, i1 - N_GROUPS,
                     jnp.where(lane == 1, i2 - N_GROUPS, jnp.where(lane == 2, w1, jnp.where(lane == 3, w2, 0.0))))


def _outproj_kernel(oa_ref, ob_ref, w_ref, h_ref, gate_ref, lng_ref, lnb_ref, sh_ref, sc_ref, wr_ref, br_ref,
                    h1_ref, f_ref, route_ref, *, alpha):
    ka = oa_ref.shape[1]
    o = (jnp.dot(oa_ref[...], w_ref[:ka, :], preferred_element_type=F32)
         + jnp.dot(ob_ref[...], w_ref[ka:, :], preferred_element_type=F32))
    h1 = _layer_norm(alpha * h_ref[...] + gate_ref[...] * o, lng_ref[...], lnb_ref[...])
    h1_ref[...] = h1
    f = h1 * sc_ref[...] + sh_ref[...]
    _store_row_tiles(f_ref, f)
    f_hi, f_lo = _split_hi_lo(f)
    hi_prod = jnp.dot(f_hi, wr_ref[...], preferred_element_type=F32)
    lo_prod = jnp.dot(f_lo, wr_ref[...], preferred_element_type=F32)
    logits = (hi_prod[:, :LANES] + hi_prod[:, LANES:]) + (lo_prod[:, :LANES] + lo_prod[:, LANES:]) + br_ref[...]
    route_ref[...] = _route(logits)


def _outproj_call(oa, ob, w16, h, gate, ln_g, ln_b, shift, scale1p, w_route, b_route, *, n_rows, s_len, alpha):
    b, _, d = h.shape
    ka, kb = oa.shape[2], ob.shape[2]
    n_lat = s_len // TM
    seg = lambda bb, j: (bb, jnp.where(j >= n_lat, 1, 0), 0, 0)
    row = lambda bb, j: (bb, j, 0)
    const = lambda bb, j: (0, 0)
    return pl.pallas_call(
        functools.partial(_outproj_kernel, alpha=alpha),
        grid=(b, n_rows // TM),
        in_specs=[
            pl.BlockSpec((None, TM, ka), row),
            pl.BlockSpec((None, TM, kb), row),
            pl.BlockSpec((ka + kb, d), const),
            pl.BlockSpec((None, TM, d), row),
            pl.BlockSpec((None, None, 1, d), seg),
            pl.BlockSpec((1, d), const),
            pl.BlockSpec((1, d), const),
            pl.BlockSpec((None, None, 1, d), seg),
            pl.BlockSpec((None, None, 1, d), seg),
            pl.BlockSpec((d, 2 * LANES), const),
            pl.BlockSpec((1, LANES), const),
        ],
        out_specs=[
            pl.BlockSpec((None, TM, d), row),
            pl.BlockSpec((None, TM, d // LANES, LANES), lambda bb, j: (bb, j, 0, 0)),
            pl.BlockSpec((None, TM, LANES), row),
        ],
        out_shape=[
            jax.ShapeDtypeStruct((b, n_rows, d), F32),
            jax.ShapeDtypeStruct((b, n_rows, d // LANES, LANES), F32),
            jax.ShapeDtypeStruct((b, n_rows, LANES), F32),
        ],
        compiler_params=_cparams(2),
        name="outproj",
    )(oa, ob, w16, h, gate, ln_g, ln_b, shift, scale1p, w_route, b_route)


def _row_copy(src, src_row, dst, dst_row, sem):
    return pltpu.make_async_copy(src.at[src_row], dst.at[dst_row], sem)


def _block_copy_for_wait(src, dst, sem):
    return pltpu.make_async_copy(src, dst, sem)


MOE_BUFS = 3
DMA_PRIORITIES = 2


def _moe_kernel(blk_e_ref, tok0_ref, tok1_ref, tok_ahead_ref, dst_prev_ref, dst_last_ref, f_hbm,
                w1_ref, w3_ref, w2_ref, y_hbm, xbuf, ybuf, w1b, w3b, w2b, gsem, ssem):
    i = pl.program_id(0)
    last = pl.num_programs(0) - 1
    cur = lax.rem(i, MOE_BUFS)
    ahead = lax.rem(i + 2, MOE_BUFS)

    def wait_gather(s):
        _block_copy_for_wait(f_hbm.at[pl.ds(0, MOE_BLOCK)], xbuf.at[s], gsem.at[s]).wait()

    def wait_scatter(s):
        _block_copy_for_wait(ybuf.at[s], y_hbm.at[pl.ds(0, MOE_BLOCK)], ssem.at[s]).wait()

    def start_gather(tok_ref, s):
        for r in range(MOE_BLOCK):
            _row_copy(f_hbm, tok_ref[0, r], xbuf.at[s], r, gsem.at[s]).start(priority=r % DMA_PRIORITIES)

    def start_scatter(dst_ref, s):
        for r in range(MOE_BLOCK):
            _row_copy(ybuf.at[s], r, y_hbm, dst_ref[0, r], ssem.at[s]).start(priority=r % DMA_PRIORITIES)

    @pl.when(i == 0)
    def _():
        ybuf[...] = jnp.zeros(ybuf.shape, ybuf.dtype)
        start_gather(tok0_ref, 0)
        start_gather(tok1_ref, 1)

    wait_gather(cur)

    @pl.when(i >= 2)
    def _():
        wait_scatter(cur)

    @pl.when(jnp.logical_or(i == 0, blk_e_ref[i] != blk_e_ref[jnp.maximum(i - 1, 0)]))
    def _():
        w1b[...] = w1_ref[...].astype(BF16)
        w3b[...] = w3_ref[...].astype(BF16)
        w2b[...] = w2_ref[...].astype(BF16)

    x = _load_row_tiles(xbuf.at[cur]).astype(BF16)
    start_gather(tok_ahead_ref, ahead)
    start_scatter(dst_prev_ref, ahead)
    h1 = jnp.dot(x, w1b[...], preferred_element_type=F32)
    h3 = jnp.dot(x, w3b[...], preferred_element_type=F32)
    act = (h1 / (1.0 + jnp.exp(-h1))) * h3
    _store_row_tiles(ybuf.at[cur], jnp.dot(act.astype(BF16), w2b[...], preferred_element_type=F32))

    @pl.when(i == last)
    def _():
        start_scatter(dst_last_ref, cur)
        for s in range(MOE_BUFS):
            wait_scatter(s)
        wait_gather(lax.rem(i + 1, MOE_BUFS))
        wait_gather(ahead)


def _moe_call(f_rows, blk_e, slot_tok, slot_dst, w1, w3, w2, layer):
    t, n_tiles, _ = f_rows.shape
    d = n_tiles * LANES
    n_blocks = blk_e.shape[0]
    n_slots = n_blocks * MOE_BLOCK
    de = w1.shape[3]
    assert n_blocks >= MOE_BUFS
    smem_block = lambda index_map: pl.BlockSpec((None, 1, MOE_BLOCK), index_map, memory_space=pltpu.SMEM)
    grid_spec = pltpu.PrefetchScalarGridSpec(
        num_scalar_prefetch=1,
        grid=(n_blocks,),
        in_specs=[
            smem_block(lambda i, be: (0, 0, 0)),
            smem_block(lambda i, be: (1, 0, 0)),
            smem_block(lambda i, be: (jnp.minimum(i + 2, n_blocks - 1), 0, 0)),
            smem_block(lambda i, be: (i, 0, 0)),
            smem_block(lambda i, be: (i + 1, 0, 0)),
            pl.BlockSpec(memory_space=pl.ANY),
            pl.BlockSpec((None, None, d, de), lambda i, be: (layer, be[i], 0, 0)),
            pl.BlockSpec((None, None, d, de), lambda i, be: (layer, be[i], 0, 0)),
            pl.BlockSpec((None, None, de, d), lambda i, be: (layer, be[i], 0, 0)),
        ],
        out_specs=pl.BlockSpec(memory_space=pl.ANY),
        scratch_shapes=[
            pltpu.VMEM((MOE_BUFS, MOE_BLOCK, n_tiles, LANES), F32),
            pltpu.VMEM((MOE_BUFS, MOE_BLOCK, n_tiles, LANES), F32),
            pltpu.VMEM((d, de), BF16),
            pltpu.VMEM((d, de), BF16),
            pltpu.VMEM((de, d), BF16),
            pltpu.SemaphoreType.DMA((MOE_BUFS,)),
            pltpu.SemaphoreType.DMA((MOE_BUFS,)),
        ],
    )
    tok3 = slot_tok.reshape(n_blocks, 1, MOE_BLOCK)
    n_rows = n_slots + MOE_BLOCK
    first = (n_slots + jnp.arange(MOE_BLOCK, dtype=jnp.int32))[None, :]
    dst3 = jnp.concatenate([first, slot_dst], axis=0).reshape(n_blocks + 1, 1, MOE_BLOCK)
    return pl.pallas_call(
        _moe_kernel,
        grid_spec=grid_spec,
        out_shape=jax.ShapeDtypeStruct((n_rows, n_tiles, LANES), F32),
        compiler_params=_cparams(1),
        name="moe_experts",
    )(blk_e, tok3, tok3, tok3, dst3, dst3, f_rows, w1, w3, w2)


def _moe_plan(route, t):
    n_assign = TOP_K * t
    experts = jnp.arange(N_EXPERTS, dtype=jnp.int32)
    e_flat = route[:, :TOP_K].astype(jnp.int32).reshape(n_assign)
    order = jnp.argsort(e_flat, stable=True).astype(jnp.int32)
    counts = jnp.sum((e_flat[:, None] == experts[None, :]).astype(jnp.int32), axis=0)
    starts = jnp.cumsum(counts) - counts
    padded = (counts + MOE_BLOCK - 1) // MOE_BLOCK * MOE_BLOCK
    p_ends = jnp.cumsum(padded)
    p_starts = p_ends - padded
    n_blocks = (n_assign + MOE_BLOCK - 1) // MOE_BLOCK + N_EXPERTS
    blk_start = jnp.arange(n_blocks, dtype=jnp.int32) * MOE_BLOCK
    blk_e = jnp.minimum(jnp.sum((p_ends[None, :] <= blk_start[:, None]).astype(jnp.int32), axis=1), N_EXPERTS - 1)
    of_block = lambda v: jnp.sum(jnp.where(blk_e[:, None] == experts[None, :], v[None, :], 0), axis=1)[:, None]
    slot = blk_start[:, None] + jnp.arange(MOE_BLOCK, dtype=jnp.int32)[None, :]
    j = slot - of_block(p_starts)
    valid = j < of_block(counts)
    a_idx = order[jnp.clip(of_block(starts) + j, 0, n_assign - 1)]
    slot_tok = jnp.where(valid, a_idx // TOP_K, 0).astype(jnp.int32)
    spare = n_assign + slot - of_block(starts + counts)
    slot_dst = jnp.where(valid, (a_idx % TOP_K) * t + a_idx // TOP_K, spare).astype(jnp.int32)
    return blk_e, slot_tok, slot_dst


def _combine_kernel(h_ref, y0_ref, y1_ref, route_ref, gate_ref, lng_ref, lnb_ref, o_ref, *, alpha):
    r = route_ref[...]
    y = r[:, 2:3] * _load_row_tiles(y0_ref) + r[:, 3:4] * _load_row_tiles(y1_ref)
    o_ref[...] = _layer_norm(alpha * h_ref[...] + gate_ref[...] * y, lng_ref[...], lnb_ref[...])


def _combine_call(h1, y, route, gate, ln_g, ln_b, *, s_len, alpha):
    b, n_rows, d = h1.shape
    n_lat = s_len // TM
    seg = lambda bb, j: (bb, jnp.where(j >= n_lat, 1, 0), 0, 0)
    row = lambda bb, j: (bb, j, 0)
    const = lambda bb, j: (0, 0)
    blocks_per_batch = n_rows // TM
    blocks_per_k = b * blocks_per_batch
    return pl.pallas_call(
        functools.partial(_combine_kernel, alpha=alpha),
        grid=(b, n_rows // TM),
        in_specs=[
            pl.BlockSpec((None, TM, d), row),
            pl.BlockSpec((TM, d // LANES, LANES), lambda bb, j: (bb * blocks_per_batch + j, 0, 0)),
            pl.BlockSpec((TM, d // LANES, LANES), lambda bb, j: (blocks_per_k + bb * blocks_per_batch + j, 0, 0)),
            pl.BlockSpec((None, TM, LANES), row),
            pl.BlockSpec((None, None, 1, d), seg),
            pl.BlockSpec((1, d), const),
            pl.BlockSpec((1, d), const),
        ],
        out_specs=pl.BlockSpec((None, TM, d), row),
        out_shape=jax.ShapeDtypeStruct((b, n_rows, d), F32),
        compiler_params=_cparams(2),
        name="combine",
    )(h1, y, y, route, gate, ln_g, ln_b)


def _pair_heads(w, axis):
    w = jnp.moveaxis(w, axis, -1)
    lead = w.shape[:-1]
    n_heads = w.shape[-1] // HEAD_DIM
    w = w.reshape(lead + (2, n_heads // 2, HEAD_DIM)).swapaxes(-3, -2).reshape(lead + (n_heads * HEAD_DIM,))
    return jnp.moveaxis(w, -1, axis)


def _rope_tables(s_len, n_ctx):
    t = jnp.arange(s_len, dtype=jnp.int32)
    row = (t // GRID_W).astype(F32)
    col = (t % GRID_W).astype(F32)
    n_freq = HEAD_DIM // 4
    inv = ROPE_THETA ** (-jnp.arange(n_freq, dtype=F32) / n_freq)
    ang = jnp.concatenate([row[:, None] * inv, col[:, None] * inv], -1)
    cos = jnp.concatenate([jnp.cos(ang), jnp.ones((n_ctx, HEAD_DIM // 2), F32)], 0)
    sin = jnp.concatenate([jnp.sin(ang), jnp.zeros((n_ctx, HEAD_DIM // 2), F32)], 0)
    return jnp.tile(cos, (1, 4)), jnp.concatenate([-sin, sin, -sin, sin], -1)


def _lambda_init(layer_idx):
    return 0.8 - 0.6 * float(np.exp(-0.3 * layer_idx))


def _router_params(w_group, b_group, w_router, b_router):
    d = w_group.shape[0]
    pad = LANES - N_GROUPS - N_EXPERTS
    w = jnp.concatenate([w_group, w_router, jnp.zeros((d, pad), F32)], axis=1)
    bb = jnp.concatenate([b_group, b_router, jnp.zeros((pad,), F32)])
    w_hi, w_lo = _split_hi_lo(w)
    return jnp.concatenate([w_hi, w_lo], axis=1), bb.reshape(1, LANES)


def _moe_layer(f, route, w1, w3, w2, layer):
    b, n_rows, n_tiles, _ = f.shape
    t = b * n_rows
    blk_e, slot_tok, slot_dst = _moe_plan(route.reshape(t, LANES), t)
    return _moe_call(f.reshape(t, n_tiles, LANES), blk_e, slot_tok, slot_dst, w1, w3, w2, layer)


def kernel(x, c, ctx, c_ctx, mod_w, mod_b, ln_g, ln_b, ab_w_in, ab_w_out, diff_lambda, diff_subln_g, gqa_qk_g,
           cd_w_in, cd_w_out, win_sink, na_rpb, moe_w_group, moe_b_group, moe_w_router, moe_b_router,
           moe_w1, moe_w3, moe_w2):
    b, s, d = x.shape
    n_ctx = ctx.shape[1]
    n = s + n_ctx
    depth = mod_w.shape[0]
    rows = s // GRID_W
    assert depth == 2, "layer pattern implemented: one differential/GQA layer, then one window/neighbourhood layer"
    assert s % TM == 0 and n_ctx % TM == 0 and s % TK == 0 and b + 1 <= 8
    assert n_ctx % TQ_DIFF == 0 and n_ctx % TQ_GQA == 0
    assert rows >= NA_SLAB and rows % NA_QROWS == 0 and s % GRID_W == 0
    alpha = (2.0 * depth) ** 0.25
    qw = d // 2
    kvw = qw // 4
    assert qw == 4 * LANES and kvw == LANES

    cos128, sin128 = _rope_tables(s, n_ctx)
    cvec = jnp.concatenate([c, c_ctx[None, :], jnp.zeros((8 - b - 1, d), F32)], axis=0)
    mods_all = _mods_call(cvec, mod_w, mod_b)

    def seg_mods(i):
        m = mods_all[i]
        lat = m[:b].reshape(b, 6, d)
        cm = jnp.broadcast_to(m[b].reshape(1, 6, d), (b, 6, d))
        ms = jnp.stack([lat, cm], axis=1)
        pick = lambda k, one: (ms[:, :, k] + one)[:, :, None, :]
        return pick(0, 0.0), pick(1, 1.0), pick(2, 0.0), pick(3, 0.0), pick(4, 1.0), pick(5, 0.0)

    h = jnp.concatenate([x, ctx], axis=1)

    shift1, scale1p, gate1, shift2, scale2p, gate2 = seg_mods(0)
    w_in = ab_w_in[0]
    q_d, q_g, k_d, v_d = (w_in[:, k * qw:(k + 1) * qw] for k in range(4))
    k_g, v_g = w_in[:, 4 * qw:4 * qw + kvw], w_in[:, 4 * qw + kvw:]
    w_qk = jnp.concatenate([q_d, _pair_heads(q_g, 1), k_d, k_g], axis=1).astype(BF16)
    roles0 = ((True, None, True),) * 4 + ((True, 0, True),) * 4 + ((True, None, False),) * 4 + ((True, 1, False),)
    w_vt = jnp.concatenate([v_d, v_g], axis=1).T.astype(BF16)
    g128 = jnp.tile(gqa_qk_g[0], (1, 2))
    p0, vt0 = _inproj_call(h, shift1, scale1p, w_qk, w_vt, cos128, sin128, g128, roles0, s)
    o_d = _diff_attn_call(p0, vt0, diff_lambda[0], diff_subln_g[0], s_len=s, n_ctx=n_ctx, n_heads=4,
                          q_blk0=0, k_blk0=8, v_blk0=0, lam_init=_lambda_init(0))
    o_g = _gqa_attn_call(p0, vt0, s_len=s, n_ctx=n_ctx, q_width=qw, q_col0=qw, k_blk=12, v_blk=4)
    w_out = jnp.concatenate([ab_w_out[0][:qw], _pair_heads(ab_w_out[0][qw:], 0)], axis=0).astype(BF16)
    w_route, b_route = _router_params(moe_w_group[0], moe_b_group[0], moe_w_router[0], moe_b_router[0])
    h1, f, route = _outproj_call(o_d, o_g, w_out, h, gate1, ln_g[0, 0:1], ln_b[0, 0:1], shift2, scale2p,
                                 w_route, b_route, n_rows=n, s_len=s, alpha=alpha)
    y = _moe_layer(f, route, moe_w1, moe_w3, moe_w2, 0)
    h = _combine_call(h1, y, route, gate2, ln_g[0, 1:2], ln_b[0, 1:2], s_len=s, alpha=alpha)

    shift1, scale1p, gate1, shift2, scale2p, gate2 = seg_mods(1)
    w_in = cd_w_in[0]
    q_w, q_n = w_in[:, :qw], w_in[:, qw:2 * qw]
    k_w, v_w = w_in[:, 2 * qw:2 * qw + kvw], w_in[:, 2 * qw + kvw:2 * qw + 2 * kvw]
    k_n, v_n = w_in[:, 2 * qw + 2 * kvw:3 * qw + 2 * kvw], w_in[:, 3 * qw + 2 * kvw:]
    w_qk = jnp.concatenate([_pair_heads(q_w, 1), q_n, k_w, k_n], axis=1).astype(BF16)
    roles1 = ((True, None, True),) * 4 + ((False, None, True),) * 4 + ((True, None, False),) \
        + ((False, None, False),) * 4
    w_vt = jnp.concatenate([v_w, v_n], axis=1).T.astype(BF16)
    p1, vt1 = _inproj_call(h, shift1, scale1p, w_qk, w_vt, cos128, sin128, jnp.ones((2, LANES), F32), roles1, s)
    sink_row = jnp.repeat(win_sink[0].reshape(2, 4).T.reshape(8), TQ_WIN).reshape(1, 8 * TQ_WIN)
    o_w = _win_attn_call(p1, vt1, sink_row, s_len=s, n_ctx=n_ctx, q_width=qw, k_blk=8, v_blk=0)
    o_n = _na_attn_call(p1, vt1, _na_bias_tables(na_rpb[0], rows), s_len=s, n_ctx=n_ctx, n_pairs=4,
                        q_blk0=4, k_blk0=9, v_blk0=1)
    w_out = jnp.concatenate([_pair_heads(cd_w_out[0][:qw], 0), cd_w_out[0][qw:]], axis=0).astype(BF16)
    w_route, b_route = _router_params(moe_w_group[1], moe_b_group[1], moe_w_router[1], moe_b_router[1])
    h1, f, route = _outproj_call(o_w, o_n, w_out, h, gate1, ln_g[1, 0:1], ln_b[1, 0:1], shift2, scale2p,
                                 w_route, b_route, n_rows=s, s_len=s, alpha=alpha)
    y = _moe_layer(f, route, moe_w1, moe_w3, moe_w2, 1)
    return _combine_call(h1, y, route, gate2, ln_g[1, 1:2], ln_b[1, 1:2], s_len=s, alpha=alpha)
```

```python
import functools

import numpy as np
import jax
import jax.numpy as jnp
from jax import lax
from jax.experimental import pallas as pl
from jax.experimental.pallas import tpu as pltpu

F32 = jnp.float32
BF16 = jnp.bfloat16
HIGHEST = lax.Precision.HIGHEST

GRID_W = 64
HEAD_DIM = 64
ROPE_THETA = 10000.0
WINDOW = 128
NA_ROWS_MAX = 8
NA_COLS = 16
N_GROUPS = 4
EXPERTS_PER_GROUP = 8
N_EXPERTS = N_GROUPS * EXPERTS_PER_GROUP
TOP_K = 2
MOE_BLOCK = 128
LN_EPS = 1e-5
RMS_EPS = 1e-6
ATTN_SCALE = HEAD_DIM ** -0.5
NEG_INF = -1e30
LOG2E = 1.4426950408889634
Q_SCALE = ATTN_SCALE * LOG2E

LANES = 128
PAIR = 2 * HEAD_DIM
assert PAIR == LANES

VMEM_LIMIT = 48 * 1024 * 1024

TM = 256
TQ_DIFF = 512
TQ_GQA = 128
TK = 1024
Q_STRIP = 256
TQ_WIN = 128
NA_QROWS = 4
NA_SLAB = NA_QROWS + NA_ROWS_MAX


def _cparams(n_axes):
    return pltpu.CompilerParams(dimension_semantics=("arbitrary",) * n_axes, vmem_limit_bytes=VMEM_LIMIT)


def _lane_iota():
    return lax.broadcasted_iota(jnp.int32, (1, LANES), 1)


def _mods_kernel(c_ref, w_ref, b_ref, o_ref):
    c = c_ref[...]
    sc = c / (1.0 + jnp.exp(-c))
    o_ref[...] = jnp.dot(sc, w_ref[...], precision=HIGHEST, preferred_element_type=F32) + b_ref[...]


def _mods_call(cvec, mod_w, mod_b):
    depth, d, n6 = mod_w.shape
    tn = n6 // 4
    return pl.pallas_call(
        _mods_kernel,
        grid=(depth, n6 // tn),
        in_specs=[
            pl.BlockSpec((8, d), lambda l, j: (0, 0)),
            pl.BlockSpec((None, d, tn), lambda l, j: (l, 0, j)),
            pl.BlockSpec((None, 1, tn), lambda l, j: (l, 0, j)),
        ],
        out_specs=pl.BlockSpec((None, 8, tn), lambda l, j: (l, 0, j)),
        out_shape=jax.ShapeDtypeStruct((depth, 8, n6), F32),
        compiler_params=_cparams(2),
        name="mods",
    )(cvec, mod_w, mod_b.reshape(depth, 1, n6))


def _inproj_kernel(h_ref, sh_ref, sc_ref, w_ref, wvt_ref, cos_ref, sin_ref, g_ref, o_ref, vt_ref, *, roles):
    a = (h_ref[...] * sc_ref[...] + sh_ref[...]).astype(BF16)
    vt = lax.dot_general(wvt_ref[...], a, (((1,), (1,)), ((), ())), preferred_element_type=F32)
    for j in range(vt_ref.shape[0]):
        vt_ref[j] = vt[j * LANES:(j + 1) * LANES, :].astype(BF16)
    lane = _lane_iota()
    rope_first = (lane & (HEAD_DIM - 1)) < (HEAD_DIM // 2)
    lo = lane < HEAD_DIM
    n_blocks = len(roles)
    for c0 in range(0, n_blocks, 2):
        width = min(2, n_blocks - c0)
        p = jnp.dot(a, w_ref[:, c0 * LANES:(c0 + width) * LANES], preferred_element_type=F32)
        for half in range(width):
            cb = c0 + half
            rope, norm, scale = roles[cb]
            blk = p[:, half * LANES:(half + 1) * LANES]
            if norm is not None:
                sq = blk * blk
                s_lo = jnp.sum(jnp.where(lo, sq, 0.0), axis=-1, keepdims=True)
                s_hi = jnp.sum(jnp.where(lo, 0.0, sq), axis=-1, keepdims=True)
                ms = jnp.where(lo, s_lo, s_hi) * (1.0 / HEAD_DIM)
                blk = blk * lax.rsqrt(ms + RMS_EPS) * g_ref[norm:norm + 1, :]
            if rope:
                partner = jnp.where(rope_first, pltpu.roll(blk, LANES - HEAD_DIM // 2, 1),
                                    pltpu.roll(blk, HEAD_DIM // 2, 1))
                blk = blk * cos_ref[...] + partner * sin_ref[...]
            if scale:
                blk = blk * Q_SCALE
            o_ref[:, cb * LANES:(cb + 1) * LANES] = blk.astype(BF16)


def _inproj_call(h, shift, scale1p, w16, wvt16, cos128, sin128, g128, roles, s_len):
    b, n, d = h.shape
    ncol = w16.shape[1]
    nvb = wvt16.shape[0] // LANES
    n_lat = s_len // TM
    seg = lambda bb, j: (bb, jnp.where(j >= n_lat, 1, 0), 0, 0)
    return pl.pallas_call(
        functools.partial(_inproj_kernel, roles=roles),
        grid=(b, n // TM),
        in_specs=[
            pl.BlockSpec((None, TM, d), lambda bb, j: (bb, j, 0)),
            pl.BlockSpec((None, None, 1, d), seg),
            pl.BlockSpec((None, None, 1, d), seg),
            pl.BlockSpec((d, ncol), lambda bb, j: (0, 0)),
            pl.BlockSpec((nvb * LANES, d), lambda bb, j: (0, 0)),
            pl.BlockSpec((TM, LANES), lambda bb, j: (j, 0)),
            pl.BlockSpec((TM, LANES), lambda bb, j: (j, 0)),
            pl.BlockSpec((2, LANES), lambda bb, j: (0, 0)),
        ],
        out_specs=[
            pl.BlockSpec((None, TM, ncol), lambda bb, j: (bb, j, 0)),
            pl.BlockSpec((None, nvb, LANES, TM), lambda bb, j: (bb, 0, 0, j)),
        ],
        out_shape=[
            jax.ShapeDtypeStruct((b, n, ncol), BF16),
            jax.ShapeDtypeStruct((b, nvb, LANES, n), BF16),
        ],
        compiler_params=_cparams(2),
        name="inproj",
    )(h, shift, scale1p, w16, wvt16, cos128, sin128, g128)


def _stack_q(q_ref, n_qblocks, row0=0, n_rows=None):
    n_rows = q_ref.shape[0] if n_rows is None else n_rows
    lo = _lane_iota() < HEAD_DIM
    parts = []
    for j in range(n_qblocks):
        q = q_ref[row0:row0 + n_rows, j * LANES:(j + 1) * LANES]
        zero = jnp.zeros_like(q)
        parts.append(jnp.where(lo, q, zero))
        parts.append(jnp.where(lo, zero, q))
    return jnp.concatenate(parts, axis=0)


def _scores_t(k, qs):
    return lax.dot_general(k, qs, (((1,), (1,)), ((), ())), preferred_element_type=F32)


def _flash_init(m_ref, l_ref, acc_ref):
    m_ref[...] = jnp.full(m_ref.shape, NEG_INF, F32)
    l_ref[...] = jnp.zeros(l_ref.shape, F32)
    acc_ref[...] = jnp.zeros(acc_ref.shape, F32)


def _flash_update(st, vt, m_ref, l_ref, acc_ref):
    m_prev = m_ref[...]
    m_new = jnp.maximum(m_prev, jnp.max(st, axis=0, keepdims=True))
    alpha = jnp.exp2(m_prev - m_new)
    pt = jnp.exp2(st - m_new)
    l_ref[...] = alpha * l_ref[...] + jnp.sum(pt, axis=0, keepdims=True)
    acc_ref[...] = alpha * acc_ref[...] + jnp.dot(vt, pt.astype(BF16), preferred_element_type=F32)
    m_ref[...] = m_new


def _flash_sweep(qs, k_ref, vt_ref, m_ref, l_ref, acc_ref, *, s_len, n_ctx):
    _flash_init(m_ref, l_ref, acc_ref)
    bounds = [(c * TK, (c + 1) * TK) for c in range(s_len // TK)] + [(s_len, s_len + n_ctx)]
    s_next = _scores_t(k_ref[bounds[0][0]:bounds[0][1], :], qs)
    for idx, (lo, hi) in enumerate(bounds):
        s_cur = s_next
        if idx + 1 < len(bounds):
            s_next = _scores_t(k_ref[bounds[idx + 1][0]:bounds[idx + 1][1], :], qs)
        _flash_update(s_cur, vt_ref[:, lo:hi], m_ref, l_ref, acc_ref)


def _merge_pairs(ot, tq, n_qblocks, o_ref, row0=0):
    for j in range(n_qblocks):
        o_lo = ot[:HEAD_DIM, (2 * j) * tq:(2 * j + 1) * tq]
        o_hi = ot[HEAD_DIM:, (2 * j + 1) * tq:(2 * j + 2) * tq]
        blk = jnp.concatenate([o_lo, o_hi], axis=0)
        o_ref[row0:row0 + tq, j * LANES:(j + 1) * LANES] = blk.T.astype(o_ref.dtype)


def _diff_attn_kernel(q_ref, k_ref, vt_ref, lam_ref, g_ref, o_ref, m_ref, l_ref, acc_ref, *, s_len, n_ctx, lam_init):
    tq = q_ref.shape[0]
    qs = _stack_q(q_ref, 1)
    _flash_sweep(qs, k_ref, vt_ref, m_ref, l_ref, acc_ref, s_len=s_len, n_ctx=n_ctx)
    lv = lam_ref[...]
    lam = (jnp.exp(jnp.sum(lv[0:1] * lv[1:2], axis=-1, keepdims=True))
           - jnp.exp(jnp.sum(lv[2:3] * lv[3:4], axis=-1, keepdims=True)) + lam_init)
    ot = acc_ref[...] / l_ref[...]
    od = ot[:, :tq] - lam * ot[:, tq:]
    ms = jnp.mean(od * od, axis=0, keepdims=True)
    od = od * lax.rsqrt(ms + RMS_EPS) * g_ref[...] * (1.0 - lam_init)
    o_ref[...] = od.T.astype(o_ref.dtype)


def _query_key_windows(n, s_len, n_ctx, tq, for_ctx):
    if for_ctx:
        assert s_len % n_ctx == 0 and n_ctx % tq == 0
        return n_ctx, s_len // tq, n_ctx, s_len // n_ctx, 0
    assert s_len % tq == 0
    return s_len, 0, n, 0, s_len


def _diff_attn_call(p, vt, lam_vecs, subln_g, *, s_len, n_ctx, n_heads, q_blk0, k_blk0, v_blk0, lam_init, for_ctx):
    b, n, _ = p.shape
    tq = min(TQ_DIFF, n_ctx) if for_ctx else TQ_DIFF
    n_q, q_i0, n_k, k_i, s_keys = _query_key_windows(n, s_len, n_ctx, tq, for_ctx)
    return pl.pallas_call(
        functools.partial(_diff_attn_kernel, s_len=s_keys, n_ctx=n_ctx, lam_init=lam_init),
        grid=(b, n_heads, n_q // tq),
        in_specs=[
            pl.BlockSpec((None, tq, LANES), lambda bb, hh, i: (bb, q_i0 + i, q_blk0 + hh)),
            pl.BlockSpec((None, n_k, LANES), lambda bb, hh, i: (bb, k_i, k_blk0 + hh)),
            pl.BlockSpec((None, None, LANES, n_k), lambda bb, hh, i: (bb, v_blk0 + hh, 0, k_i)),
            pl.BlockSpec((4, HEAD_DIM), lambda bb, hh, i: (0, 0)),
            pl.BlockSpec((LANES, 1), lambda bb, hh, i: (0, 0)),
        ],
        out_specs=pl.BlockSpec((None, tq, LANES), lambda bb, hh, i: (bb, i, hh)),
        out_shape=jax.ShapeDtypeStruct((b, n_q, n_heads * LANES), BF16),
        scratch_shapes=[pltpu.VMEM((1, 2 * tq), F32), pltpu.VMEM((1, 2 * tq), F32), pltpu.VMEM((LANES, 2 * tq), F32)],
        compiler_params=_cparams(3),
        name="diff_attn",
    )(p, p, vt, lam_vecs, subln_g.reshape(LANES, 1))


def _gqa_attn_kernel(q_ref, k_ref, vt_ref, o_ref, m_ref, l_ref, acc_ref, *, s_len, n_ctx):
    tq = q_ref.shape[0]
    n_qblocks = q_ref.shape[1] // LANES
    qs = _stack_q(q_ref, n_qblocks)
    _flash_sweep(qs, k_ref, vt_ref, m_ref, l_ref, acc_ref, s_len=s_len, n_ctx=n_ctx)
    _merge_pairs(acc_ref[...] / l_ref[...], tq, n_qblocks, o_ref)


def _gqa_attn_call(p, vt, *, s_len, n_ctx, q_width, q_col0, k_blk, v_blk, for_ctx):
    b, n, _ = p.shape
    tq = TQ_GQA
    n_q, q_i0, n_k, k_i, s_keys = _query_key_windows(n, s_len, n_ctx, tq, for_ctx)
    m_rows = 2 * (q_width // LANES) * tq
    return pl.pallas_call(
        functools.partial(_gqa_attn_kernel, s_len=s_keys, n_ctx=n_ctx),
        grid=(b, n_q // tq),
        in_specs=[
            pl.BlockSpec((None, tq, q_width), lambda bb, i: (bb, q_i0 + i, q_col0 // q_width)),
            pl.BlockSpec((None, n_k, LANES), lambda bb, i: (bb, k_i, k_blk)),
            pl.BlockSpec((None, None, LANES, n_k), lambda bb, i: (bb, v_blk, 0, k_i)),
        ],
        out_specs=pl.BlockSpec((None, tq, q_width), lambda bb, i: (bb, i, 0)),
        out_shape=jax.ShapeDtypeStruct((b, n_q, q_width), BF16),
        scratch_shapes=[pltpu.VMEM((1, m_rows), F32), pltpu.VMEM((1, m_rows), F32), pltpu.VMEM((LANES, m_rows), F32)],
        compiler_params=_cparams(2),
        name="gqa_attn",
    )(p, p, vt)


BLOCKS_PER_STEP = 2


def _win_attn_kernel(q_ref, k_ref, vt_ref, sink_ref, o_ref, *, s_len, n_ctx):
    tq = q_ref.shape[0] // BLOCKS_PER_STEP
    n_qblocks = q_ref.shape[1] // LANES
    span = tq + 2 * WINDOW
    step = pl.program_id(1)
    k_ctx = k_ref[s_len:s_len + n_ctx, :]
    vt_ctx = vt_ref[:, s_len:s_len + n_ctx]
    sink = sink_ref[...] * LOG2E

    def scores(blk):
        i = BLOCKS_PER_STEP * step + blk
        ws = pl.multiple_of(jnp.clip(i * tq - WINDOW, 0, s_len - span), LANES)
        qs = _stack_q(q_ref, n_qblocks, blk * tq, tq)
        m_rows = qs.shape[0]
        s_loc = _scores_t(k_ref[pl.ds(ws, span), :], qs)
        kpos = ws + lax.broadcasted_iota(jnp.int32, (span, m_rows), 0)
        qpos = i * tq + (lax.broadcasted_iota(jnp.int32, (span, m_rows), 1) & (tq - 1))
        s_loc = jnp.where(jnp.abs(qpos - kpos) <= WINDOW, s_loc, NEG_INF)
        return ws, s_loc, _scores_t(k_ctx, qs)

    def finish(blk, ws, s_loc, s_ctx):
        m = jnp.maximum(jnp.maximum(jnp.max(s_loc, axis=0, keepdims=True), jnp.max(s_ctx, axis=0, keepdims=True)),
                        sink)
        p_loc = jnp.exp2(s_loc - m)
        p_ctx = jnp.exp2(s_ctx - m)
        denom = jnp.sum(p_loc, axis=0, keepdims=True) + jnp.sum(p_ctx, axis=0, keepdims=True) + jnp.exp2(sink - m)
        acc = (jnp.dot(vt_ref[:, pl.ds(ws, span)], p_loc.astype(BF16), preferred_element_type=F32)
               + jnp.dot(vt_ctx, p_ctx.astype(BF16), preferred_element_type=F32))
        _merge_pairs(acc / denom, tq, n_qblocks, o_ref, blk * tq)

    pending = [scores(blk) for blk in range(BLOCKS_PER_STEP)]
    for blk, args in enumerate(pending):
        finish(blk, *args)


def _win_attn_call(p, vt, sink_row, *, s_len, n_ctx, q_width, k_blk, v_blk):
    b, n, _ = p.shape
    tq = BLOCKS_PER_STEP * TQ_WIN
    m_rows = sink_row.shape[1]
    return pl.pallas_call(
        functools.partial(_win_attn_kernel, s_len=s_len, n_ctx=n_ctx),
        grid=(b, s_len // tq),
        in_specs=[
            pl.BlockSpec((None, tq, q_width), lambda bb, i: (bb, i, 0)),
            pl.BlockSpec((None, n, LANES), lambda bb, i: (bb, 0, k_blk)),
            pl.BlockSpec((None, None, LANES, n), lambda bb, i: (bb, v_blk, 0, 0)),
            pl.BlockSpec((1, m_rows), lambda bb, i: (0, 0)),
        ],
        out_specs=pl.BlockSpec((None, tq, q_width), lambda bb, i: (bb, i, 0)),
        out_shape=jax.ShapeDtypeStruct((b, s_len, q_width), BF16),
        compiler_params=_cparams(2),
        name="win_attn",
    )(p, p, vt, sink_row)


def _na_attn_kernel(q_ref, k_ref, vt_ref, *rest, s_len, n_ctx):
    bias_refs, o_ref = rest[:BLOCKS_PER_STEP], rest[BLOCKS_PER_STEP]
    tq = q_ref.shape[0] // BLOCKS_PER_STEP
    slab = NA_SLAB * GRID_W
    rows = s_len // GRID_W
    step = pl.program_id(2)
    k_cx = k_ref[s_len:s_len + n_ctx, :]
    vt_cx = vt_ref[:, s_len:s_len + n_ctx]

    def scores(blk):
        mi = BLOCKS_PER_STEP * step + blk
        ss = pl.multiple_of(jnp.clip(NA_QROWS * mi - NA_ROWS_MAX // 2, 0, rows - NA_SLAB) * GRID_W, LANES)
        qs = _stack_q(q_ref, 1, blk * tq, tq)
        s_nb = _scores_t(k_ref[pl.ds(ss, slab), :], qs) + bias_refs[blk][...]
        return ss, s_nb, _scores_t(k_cx, qs)

    def finish(blk, ss, s_nb, s_cx):
        m = jnp.maximum(jnp.max(s_nb, axis=0, keepdims=True), jnp.max(s_cx, axis=0, keepdims=True))
        p_nb = jnp.exp2(s_nb - m)
        p_cx = jnp.exp2(s_cx - m)
        denom = jnp.sum(p_nb, axis=0, keepdims=True) + jnp.sum(p_cx, axis=0, keepdims=True)
        acc = (jnp.dot(vt_ref[:, pl.ds(ss, slab)], p_nb.astype(BF16), preferred_element_type=F32)
               + jnp.dot(vt_cx, p_cx.astype(BF16), preferred_element_type=F32))
        _merge_pairs(acc / denom, tq, 1, o_ref, blk * tq)

    pending = [scores(blk) for blk in range(BLOCKS_PER_STEP)]
    for blk, args in enumerate(pending):
        finish(blk, *args)


def _na_attn_call(p, vt, bias_tab, *, s_len, n_ctx, n_pairs, q_blk0, k_blk0, v_blk0):
    b, n, _ = p.shape
    tq = NA_QROWS * GRID_W
    n_blocks = s_len // tq
    assert n_blocks % BLOCKS_PER_STEP == 0
    tbl = lambda mi: jnp.where(mi == 0, 0, jnp.where(mi == n_blocks - 1, 2, 1))
    bias_spec = lambda blk: pl.BlockSpec((None, None, NA_SLAB * GRID_W, 2 * tq),
                                         lambda bb, j, st: (tbl(BLOCKS_PER_STEP * st + blk), j, 0, 0))
    return pl.pallas_call(
        functools.partial(_na_attn_kernel, s_len=s_len, n_ctx=n_ctx),
        grid=(b, n_pairs, n_blocks // BLOCKS_PER_STEP),
        in_specs=[
            pl.BlockSpec((None, BLOCKS_PER_STEP * tq, LANES), lambda bb, j, st: (bb, st, q_blk0 + j)),
            pl.BlockSpec((None, n, LANES), lambda bb, j, st: (bb, 0, k_blk0 + j)),
            pl.BlockSpec((None, None, LANES, n), lambda bb, j, st: (bb, v_blk0 + j, 0, 0)),
        ] + [bias_spec(blk) for blk in range(BLOCKS_PER_STEP)],
        out_specs=pl.BlockSpec((None, BLOCKS_PER_STEP * tq, LANES), lambda bb, j, st: (bb, st, j)),
        out_shape=jax.ShapeDtypeStruct((b, s_len, n_pairs * LANES), BF16),
        compiler_params=_cparams(3),
        name="na_attn",
    )(p, p, vt, *([bias_tab] * BLOCKS_PER_STEP))


def _na_bias_tables(rpb, rows):
    n_heads = rpb.shape[0]
    rpb = rpb.astype(F32)
    pad = GRID_W - NA_COLS
    rpb_p = jnp.pad(rpb, ((0, 0), (0, 0), (pad, pad)))
    col_tab = jnp.stack([rpb_p[:, :, pad + NA_COLS - 1 - qc: pad + NA_COLS - 1 - qc + GRID_W] for qc in range(GRID_W)],
                        axis=2)
    qr = np.arange(NA_QROWS)[:, None, None, None]
    qc = np.arange(GRID_W)[None, :, None, None]
    kr = np.arange(NA_SLAB)[None, None, :, None]
    kc = np.arange(GRID_W)[None, None, None, :]
    full = (NA_QROWS, GRID_W, NA_SLAB, GRID_W)
    flat = (NA_QROWS * GRID_W, NA_SLAB * GRID_W)
    tabs = []
    for q0, s0 in ((0, 0), (NA_QROWS, NA_QROWS - NA_ROWS_MAX // 2), (rows - NA_QROWS, rows - NA_SLAB)):
        q_row = q0 + qr
        k_row = s0 + kr
        r0 = np.clip(q_row - NA_ROWS_MAX // 2, 0, rows - NA_ROWS_MAX)
        c0 = np.clip(qc - NA_COLS // 2, 0, GRID_W - NA_COLS)
        valid = (k_row >= r0) & (k_row < r0 + NA_ROWS_MAX) & (kc >= c0) & (kc < c0 + NA_COLS)
        valid = np.broadcast_to(valid, full).reshape(flat)
        r_off = np.clip(k_row - q_row + NA_ROWS_MAX - 1, 0, 2 * NA_ROWS_MAX - 2)[:, 0, :, 0]
        bias = jnp.stack([jnp.stack([col_tab[:, int(r_off[a, c])] for c in range(NA_SLAB)], axis=2)
                          for a in range(NA_QROWS)], axis=1)
        tabs.append(jnp.where(valid[None], bias.reshape((n_heads,) + flat) * LOG2E, NEG_INF))
    tab = jnp.stack(tabs)
    return jnp.swapaxes(tab.reshape(3, n_heads // 2, 2 * flat[0], flat[1]), -1, -2)


def _split_hi_lo(x):
    c = x * 65537.0
    hi = c - (c - x)
    return hi.astype(BF16), (x - hi).astype(BF16)


def _layer_norm(y, g, b):
    mu = jnp.mean(y, axis=-1, keepdims=True)
    yc = y - mu
    var = jnp.mean(yc * yc, axis=-1, keepdims=True)
    return yc * lax.rsqrt(var + LN_EPS) * g + b


def _route(logits):
    lane = _lane_iota().astype(F32)
    big = float(LANES)
    is_g = lane < N_GROUPS
    gl = jnp.where(is_g, logits, NEG_INF)
    g_max = jnp.max(gl, axis=-1, keepdims=True)
    g_idx = jnp.min(jnp.where(gl == g_max, lane, big), axis=-1, keepdims=True)
    g_w = 1.0 / jnp.sum(jnp.where(is_g, jnp.exp(gl - g_max), 0.0), axis=-1, keepdims=True)
    base = N_GROUPS + EXPERTS_PER_GROUP * g_idx
    el = jnp.where((lane >= base) & (lane < base + EXPERTS_PER_GROUP), logits, NEG_INF)
    v1 = jnp.max(el, axis=-1, keepdims=True)
    i1 = jnp.min(jnp.where(el == v1, lane, big), axis=-1, keepdims=True)
    el2 = jnp.where(lane == i1, NEG_INF, el)
    v2 = jnp.max(el2, axis=-1, keepdims=True)
    i2 = jnp.min(jnp.where(el2 == v2, lane, big), axis=-1, keepdims=True)
    t = jnp.exp(v2 - v1)
    w1 = g_w / (1.0 + t)
    w2 = g_w * t / (1.0 + t)
    return jnp.where(lane == 0, i1 - N_GROUPS,
                     jnp.where(lane == 1, i2 - N_GROUPS, jnp.where(lane == 2, w1, jnp.where(lane == 3, w2, 0.0))))


def _outproj_kernel(oa_ref, ob_ref, w_ref, h_ref, gate_ref, lng_ref, lnb_ref, sh_ref, sc_ref, wr_ref, br_ref,
                    h1_ref, f_ref, route_ref, *, alpha):
    ka = oa_ref.shape[1]
    o = (jnp.dot(oa_ref[...], w_ref[:ka, :], preferred_element_type=F32)
         + jnp.dot(ob_ref[...], w_ref[ka:, :], preferred_element_type=F32))
    h1 = _layer_norm(alpha * h_ref[...] + gate_ref[...] * o, lng_ref[...], lnb_ref[...])
    h1_ref[...] = h1
    f = h1 * sc_ref[...] + sh_ref[...]
    f_ref[...] = f.astype(f_ref.dtype)
    f_hi, f_lo = _split_hi_lo(f)
    hi_prod = jnp.dot(f_hi, wr_ref[...], preferred_element_type=F32)
    lo_prod = jnp.dot(f_lo, wr_ref[...], preferred_element_type=F32)
    logits = (hi_prod[:, :LANES] + hi_prod[:, LANES:]) + (lo_prod[:, :LANES] + lo_prod[:, LANES:]) + br_ref[...]
    route_ref[...] = _route(logits)


def _outproj_call(oa, ob, w16, h, gate, ln_g, ln_b, shift, scale1p, w_route, b_route, *, n_rows, s_len, alpha):
    b, _, d = h.shape
    ka, kb = oa.shape[2], ob.shape[2]
    n_lat = s_len // TM
    seg = lambda bb, j: (bb, jnp.where(j >= n_lat, 1, 0), 0, 0)
    row = lambda bb, j: (bb, j, 0)
    const = lambda bb, j: (0, 0)
    return pl.pallas_call(
        functools.partial(_outproj_kernel, alpha=alpha),
        grid=(b, n_rows // TM),
        in_specs=[
            pl.BlockSpec((None, TM, ka), row),
            pl.BlockSpec((None, TM, kb), row),
            pl.BlockSpec((ka + kb, d), const),
            pl.BlockSpec((None, TM, d), row),
            pl.BlockSpec((None, None, 1, d), seg),
            pl.BlockSpec((1, d), const),
            pl.BlockSpec((1, d), const),
            pl.BlockSpec((None, None, 1, d), seg),
            pl.BlockSpec((None, None, 1, d), seg),
            pl.BlockSpec((d, 2 * LANES), const),
            pl.BlockSpec((1, LANES), const),
        ],
        out_specs=[
            pl.BlockSpec((None, TM, d), row),
            pl.BlockSpec((None, TM, d), row),
            pl.BlockSpec((None, TM, LANES), row),
        ],
        out_shape=[
            jax.ShapeDtypeStruct((b, n_rows, d), F32),
            jax.ShapeDtypeStruct((b, n_rows, d), BF16),
            jax.ShapeDtypeStruct((b, n_rows, LANES), F32),
        ],
        compiler_params=_cparams(2),
        name="outproj",
    )(oa, ob, w16, h, gate, ln_g, ln_b, shift, scale1p, w_route, b_route)


def _row_copy(src, src_row, dst, dst_row, sem):
    return pltpu.make_async_copy(src.at[pl.ds(src_row, 1), :], dst.at[pl.ds(dst_row, 1), :], sem)


def _block_copy_for_wait(src, dst, sem):
    return pltpu.make_async_copy(src, dst, sem)


MOE_BUFS = 3
DMA_PRIORITIES = 2
MOE_VMEM_LIMIT = 57 * 1024 * 1024
ROW_TILE = 16


def _moe_kernel(blk_e_ref, tok_ref, dst_prev_ref, dst_last_ref, f_ref, w1_ref, w3_ref, w2_ref, y_hbm,
                tmp, xs, ybuf, w1b, w3b, w2b, ssem):
    i = pl.program_id(0)
    last = pl.num_programs(0) - 1
    cur = lax.rem(i, MOE_BUFS)
    prev = lax.rem(i + 2, MOE_BUFS)

    def wait_scatter(s):
        _block_copy_for_wait(ybuf.at[s], y_hbm.at[pl.ds(0, MOE_BLOCK), :], ssem.at[s]).wait()

    def start_scatter(dst_ref, s):
        for r in range(MOE_BLOCK):
            _row_copy(ybuf.at[s], r, y_hbm, dst_ref[0, r], ssem.at[s]).start(priority=r % DMA_PRIORITIES)

    @pl.when(i == 0)
    def _():
        ybuf[...] = jnp.zeros(ybuf.shape, ybuf.dtype)

    @pl.when(i >= 2)
    def _():
        wait_scatter(cur)

    @pl.when(jnp.logical_or(i == 0, blk_e_ref[i] != blk_e_ref[jnp.maximum(i - 1, 0)]))
    def _():
        w1b[...] = w1_ref[...].astype(BF16)
        w3b[...] = w3_ref[...].astype(BF16)
        w2b[...] = w2_ref[...].astype(BF16)

    for r in range(MOE_BLOCK):
        t = tok_ref[0, r]
        base = pl.multiple_of((t // ROW_TILE) * ROW_TILE, ROW_TILE)
        stage = tmp.at[r % tmp.shape[0]]
        stage[...] = f_ref[pl.ds(base, ROW_TILE), :].astype(F32)
        xs[pl.ds(r, 1), :] = stage[pl.ds(t - base, 1), :]
    x = xs[...].astype(BF16)
    start_scatter(dst_prev_ref, prev)
    h1 = jnp.dot(x, w1b[...], preferred_element_type=F32)
    h3 = jnp.dot(x, w3b[...], preferred_element_type=F32)
    act = (h1 / (1.0 + jnp.exp(-h1))) * h3
    ybuf[cur] = jnp.dot(act.astype(BF16), w2b[...], preferred_element_type=F32)

    @pl.when(i == last)
    def _():
        start_scatter(dst_last_ref, cur)
        for s in range(MOE_BUFS):
            wait_scatter(s)


def _moe_call(f_rows, blk_e, slot_tok, slot_dst, w1, w3, w2, layer):
    t, d = f_rows.shape
    assert t % ROW_TILE == 0
    n_blocks = blk_e.shape[0]
    n_slots = n_blocks * MOE_BLOCK
    de = w1.shape[3]
    assert n_blocks >= MOE_BUFS
    smem_block = lambda index_map: pl.BlockSpec((None, 1, MOE_BLOCK), index_map, memory_space=pltpu.SMEM)
    grid_spec = pltpu.PrefetchScalarGridSpec(
        num_scalar_prefetch=1,
        grid=(n_blocks,),
        in_specs=[
            smem_block(lambda i, be: (i, 0, 0)),
            smem_block(lambda i, be: (i, 0, 0)),
            smem_block(lambda i, be: (i + 1, 0, 0)),
            pl.BlockSpec(memory_space=pltpu.VMEM),
            pl.BlockSpec((None, None, d, de), lambda i, be: (layer, be[i], 0, 0)),
            pl.BlockSpec((None, None, d, de), lambda i, be: (layer, be[i], 0, 0)),
            pl.BlockSpec((None, None, de, d), lambda i, be: (layer, be[i], 0, 0)),
        ],
        out_specs=pl.BlockSpec(memory_space=pl.ANY),
        scratch_shapes=[
            pltpu.VMEM((4, ROW_TILE, d), F32),
            pltpu.VMEM((MOE_BLOCK, d), F32),
            pltpu.VMEM((MOE_BUFS, MOE_BLOCK, d), F32),
            pltpu.VMEM((d, de), BF16),
            pltpu.VMEM((d, de), BF16),
            pltpu.VMEM((de, d), BF16),
            pltpu.SemaphoreType.DMA((MOE_BUFS,)),
        ],
    )
    tok3 = slot_tok.reshape(n_blocks, 1, MOE_BLOCK)
    n_rows = n_slots + MOE_BLOCK
    first = (n_slots + jnp.arange(MOE_BLOCK, dtype=jnp.int32))[None, :]
    dst3 = jnp.concatenate([first, slot_dst], axis=0).reshape(n_blocks + 1, 1, MOE_BLOCK)
    return pl.pallas_call(
        _moe_kernel,
        grid_spec=grid_spec,
        out_shape=jax.ShapeDtypeStruct((n_rows, d), F32),
        compiler_params=pltpu.CompilerParams(dimension_semantics=("arbitrary",), vmem_limit_bytes=MOE_VMEM_LIMIT),
        name="moe_experts",
    )(blk_e, tok3, dst3, dst3, f_rows, w1, w3, w2)


def _moe_plan(route, t):
    n_assign = TOP_K * t
    experts = jnp.arange(N_EXPERTS, dtype=jnp.int32)
    e_flat = route[:, :TOP_K].astype(jnp.int32).reshape(n_assign)
    order = jnp.argsort(e_flat, stable=True).astype(jnp.int32)
    counts = jnp.sum((e_flat[:, None] == experts[None, :]).astype(jnp.int32), axis=0)
    starts = jnp.cumsum(counts) - counts
    padded = (counts + MOE_BLOCK - 1) // MOE_BLOCK * MOE_BLOCK
    p_ends = jnp.cumsum(padded)
    p_starts = p_ends - padded
    n_blocks = (n_assign + MOE_BLOCK - 1) // MOE_BLOCK + N_EXPERTS
    blk_start = jnp.arange(n_blocks, dtype=jnp.int32) * MOE_BLOCK
    blk_e = jnp.minimum(jnp.sum((p_ends[None, :] <= blk_start[:, None]).astype(jnp.int32), axis=1), N_EXPERTS - 1)
    of_block = lambda v: jnp.sum(jnp.where(blk_e[:, None] == experts[None, :], v[None, :], 0), axis=1)[:, None]
    slot = blk_start[:, None] + jnp.arange(MOE_BLOCK, dtype=jnp.int32)[None, :]
    j = slot - of_block(p_starts)
    valid = j < of_block(counts)
    a_idx = order[jnp.clip(of_block(starts) + j, 0, n_assign - 1)]
    slot_tok = jnp.where(valid, a_idx // TOP_K, 0).astype(jnp.int32)
    spare = n_assign + slot - of_block(starts + counts)
    slot_dst = jnp.where(valid, (a_idx % TOP_K) * t + a_idx // TOP_K, spare).astype(jnp.int32)
    return blk_e, slot_tok, slot_dst


def _combine_kernel(h_ref, y0_ref, y1_ref, route_ref, gate_ref, lng_ref, lnb_ref, o_ref, *, alpha):
    r = route_ref[...]
    y = r[:, 2:3] * y0_ref[...] + r[:, 3:4] * y1_ref[...]
    o_ref[...] = _layer_norm(alpha * h_ref[...] + gate_ref[...] * y, lng_ref[...], lnb_ref[...])


def _combine_call(h1, y, route, gate, ln_g, ln_b, *, s_len, alpha):
    b, n_rows, d = h1.shape
    n_lat = s_len // TM
    seg = lambda bb, j: (bb, jnp.where(j >= n_lat, 1, 0), 0, 0)
    row = lambda bb, j: (bb, j, 0)
    const = lambda bb, j: (0, 0)
    blocks_per_batch = n_rows // TM
    blocks_per_k = b * blocks_per_batch
    return pl.pallas_call(
        functools.partial(_combine_kernel, alpha=alpha),
        grid=(b, n_rows // TM),
        in_specs=[
            pl.BlockSpec((None, TM, d), row),
            pl.BlockSpec((TM, d), lambda bb, j: (bb * blocks_per_batch + j, 0)),
            pl.BlockSpec((TM, d), lambda bb, j: (blocks_per_k + bb * blocks_per_batch + j, 0)),
            pl.BlockSpec((None, TM, LANES), row),
            pl.BlockSpec((None, None, 1, d), seg),
            pl.BlockSpec((1, d), const),
            pl.BlockSpec((1, d), const),
        ],
        out_specs=pl.BlockSpec((None, TM, d), row),
        out_shape=jax.ShapeDtypeStruct((b, n_rows, d), F32),
        compiler_params=_cparams(2),
        name="combine",
    )(h1, y, y, route, gate, ln_g, ln_b)


def _pair_heads(w, axis):
    w = jnp.moveaxis(w, axis, -1)
    lead = w.shape[:-1]
    n_heads = w.shape[-1] // HEAD_DIM
    w = w.reshape(lead + (2, n_heads // 2, HEAD_DIM)).swapaxes(-3, -2).reshape(lead + (n_heads * HEAD_DIM,))
    return jnp.moveaxis(w, -1, axis)


def _rope_tables(s_len, n_ctx):
    t = jnp.arange(s_len, dtype=jnp.int32)
    row = (t // GRID_W).astype(F32)
    col = (t % GRID_W).astype(F32)
    n_freq = HEAD_DIM // 4
    inv = ROPE_THETA ** (-jnp.arange(n_freq, dtype=F32) / n_freq)
    ang = jnp.concatenate([row[:, None] * inv, col[:, None] * inv], -1)
    cos = jnp.concatenate([jnp.cos(ang), jnp.ones((n_ctx, HEAD_DIM // 2), F32)], 0)
    sin = jnp.concatenate([jnp.sin(ang), jnp.zeros((n_ctx, HEAD_DIM // 2), F32)], 0)
    return jnp.tile(cos, (1, 4)), jnp.concatenate([-sin, sin, -sin, sin], -1)


def _lambda_init(layer_idx):
    return 0.8 - 0.6 * float(np.exp(-0.3 * layer_idx))


def _router_params(w_group, b_group, w_router, b_router):
    d = w_group.shape[0]
    pad = LANES - N_GROUPS - N_EXPERTS
    w = jnp.concatenate([w_group, w_router, jnp.zeros((d, pad), F32)], axis=1)
    bb = jnp.concatenate([b_group, b_router, jnp.zeros((pad,), F32)])
    w_hi, w_lo = _split_hi_lo(w)
    return jnp.concatenate([w_hi, w_lo], axis=1), bb.reshape(1, LANES)


def _moe_layer(f, route, w1, w3, w2, layer):
    b, n_rows, d = f.shape
    t = b * n_rows
    blk_e, slot_tok, slot_dst = _moe_plan(route.reshape(t, LANES), t)
    return _moe_call(f.reshape(t, d), blk_e, slot_tok, slot_dst, w1, w3, w2, layer)


def kernel(x, c, ctx, c_ctx, mod_w, mod_b, ln_g, ln_b, ab_w_in, ab_w_out, diff_lambda, diff_subln_g, gqa_qk_g,
           cd_w_in, cd_w_out, win_sink, na_rpb, moe_w_group, moe_b_group, moe_w_router, moe_b_router,
           moe_w1, moe_w3, moe_w2):
    b, s, d = x.shape
    n_ctx = ctx.shape[1]
    n = s + n_ctx
    depth = mod_w.shape[0]
    rows = s // GRID_W
    assert depth == 2, "layer pattern implemented: one differential/GQA layer, then one window/neighbourhood layer"
    assert s % TM == 0 and n_ctx % TM == 0 and s % TK == 0 and b + 1 <= 8
    assert s % TQ_DIFF == 0 and s % TQ_GQA == 0 and n_ctx % TQ_GQA == 0
    assert rows >= NA_SLAB and rows % NA_QROWS == 0 and s % GRID_W == 0
    alpha = (2.0 * depth) ** 0.25
    qw = d // 2
    kvw = qw // 4
    assert qw == 4 * LANES and kvw == LANES

    cos128, sin128 = _rope_tables(s, n_ctx)
    cvec = jnp.concatenate([c, c_ctx[None, :], jnp.zeros((8 - b - 1, d), F32)], axis=0)
    mods_all = _mods_call(cvec, mod_w, mod_b)

    def seg_mods(i):
        m = mods_all[i]
        lat = m[:b].reshape(b, 6, d)
        cm = jnp.broadcast_to(m[b].reshape(1, 6, d), (b, 6, d))
        ms = jnp.stack([lat, cm], axis=1)
        pick = lambda k, one: (ms[:, :, k] + one)[:, :, None, :]
        return pick(0, 0.0), pick(1, 1.0), pick(2, 0.0), pick(3, 0.0), pick(4, 1.0), pick(5, 0.0)

    h = jnp.concatenate([x, ctx], axis=1)

    shift1, scale1p, gate1, shift2, scale2p, gate2 = seg_mods(0)
    w_in = ab_w_in[0]
    q_d, q_g, k_d, v_d = (w_in[:, k * qw:(k + 1) * qw] for k in range(4))
    k_g, v_g = w_in[:, 4 * qw:4 * qw + kvw], w_in[:, 4 * qw + kvw:]
    w_qk = jnp.concatenate([q_d, _pair_heads(q_g, 1), k_d, k_g], axis=1).astype(BF16)
    roles0 = ((True, None, True),) * 4 + ((True, 0, True),) * 4 + ((True, None, False),) * 4 + ((True, 1, False),)
    w_vt = jnp.concatenate([v_d, v_g], axis=1).T.astype(BF16)
    g128 = jnp.tile(gqa_qk_g[0], (1, 2))
    p0, vt0 = _inproj_call(h, shift1, scale1p, w_qk, w_vt, cos128, sin128, g128, roles0, s)
    diff_attn = functools.partial(_diff_attn_call, p0, vt0, diff_lambda[0], diff_subln_g[0], s_len=s, n_ctx=n_ctx,
                                  n_heads=4, q_blk0=0, k_blk0=8, v_blk0=0, lam_init=_lambda_init(0))
    gqa_attn = functools.partial(_gqa_attn_call, p0, vt0, s_len=s, n_ctx=n_ctx, q_width=qw, q_col0=qw,
                                 k_blk=12, v_blk=4)
    o_d = jnp.concatenate([diff_attn(for_ctx=False), diff_attn(for_ctx=True)], axis=1)
    o_g = jnp.concatenate([gqa_attn(for_ctx=False), gqa_attn(for_ctx=True)], axis=1)
    w_out = jnp.concatenate([ab_w_out[0][:qw], _pair_heads(ab_w_out[0][qw:], 0)], axis=0).astype(BF16)
    w_route, b_route = _router_params(moe_w_group[0], moe_b_group[0], moe_w_router[0], moe_b_router[0])
    h1, f, route = _outproj_call(o_d, o_g, w_out, h, gate1, ln_g[0, 0:1], ln_b[0, 0:1], shift2, scale2p,
                                 w_route, b_route, n_rows=n, s_len=s, alpha=alpha)
    y = _moe_layer(f, route, moe_w1, moe_w3, moe_w2, 0)
    h = _combine_call(h1, y, route, gate2, ln_g[0, 1:2], ln_b[0, 1:2], s_len=s, alpha=alpha)

    shift1, scale1p, gate1, shift2, scale2p, gate2 = seg_mods(1)
    w_in = cd_w_in[0]
    q_w, q_n = w_in[:, :qw], w_in[:, qw:2 * qw]
    k_w, v_w = w_in[:, 2 * qw:2 * qw + kvw], w_in[:, 2 * qw + kvw:2 * qw + 2 * kvw]
    k_n, v_n = w_in[:, 2 * qw + 2 * kvw:3 * qw + 2 * kvw], w_in[:, 3 * qw + 2 * kvw:]
    w_qk = jnp.concatenate([_pair_heads(q_w, 1), q_n, k_w, k_n], axis=1).astype(BF16)
    roles1 = ((True, None, True),) * 4 + ((False, None, True),) * 4 + ((True, None, False),) \
        + ((False, None, False),) * 4
    w_vt = jnp.concatenate([v_w, v_n], axis=1).T.astype(BF16)
    p1, vt1 = _inproj_call(h, shift1, scale1p, w_qk, w_vt, cos128, sin128, jnp.ones((2, LANES), F32), roles1, s)
    sink_row = jnp.repeat(win_sink[0].reshape(2, 4).T.reshape(8), TQ_WIN).reshape(1, 8 * TQ_WIN)
    o_w = _win_attn_call(p1, vt1, sink_row, s_len=s, n_ctx=n_ctx, q_width=qw, k_blk=8, v_blk=0)
    o_n = _na_attn_call(p1, vt1, _na_bias_tables(na_rpb[0], rows), s_len=s, n_ctx=n_ctx, n_pairs=4,
                        q_blk0=4, k_blk0=9, v_blk0=1)
    w_out = jnp.concatenate([_pair_heads(cd_w_out[0][:qw], 0), cd_w_out[0][qw:]], axis=0).astype(BF16)
    w_route, b_route = _router_params(moe_w_group[1], moe_b_group[1], moe_w_router[1], moe_b_router[1])
    h1, f, route = _outproj_call(o_w, o_n, w_out, h, gate1, ln_g[1, 0:1], ln_b[1, 0:1], shift2, scale2p,
                                 w_route, b_route, n_rows=s, s_len=s, alpha=alpha)
    y = _moe_layer(f, route, moe_w1, moe_w3, moe_w2, 1)
    return _combine_call(h1, y, route, gate2, ln_g[1, 1:2], ln_b[1, 1:2], s_len=s, alpha=alpha)
```

```python
import functools

import numpy as np
import jax
import jax.numpy as jnp
from jax import lax
from jax.experimental import pallas as pl
from jax.experimental.pallas import tpu as pltpu

F32 = jnp.float32
BF16 = jnp.bfloat16
HIGHEST = lax.Precision.HIGHEST

GRID_W = 64
HEAD_DIM = 64
ROPE_THETA = 10000.0
WINDOW = 128
NA_ROWS_MAX = 8
NA_COLS = 16
N_GROUPS = 4
EXPERTS_PER_GROUP = 8
N_EXPERTS = N_GROUPS * EXPERTS_PER_GROUP
TOP_K = 2
MOE_BLOCK = 128
LN_EPS = 1e-5
RMS_EPS = 1e-6
ATTN_SCALE = HEAD_DIM ** -0.5
NEG_INF = -1e30
LOG2E = 1.4426950408889634
Q_SCALE = ATTN_SCALE * LOG2E

LANES = 128
PAIR = 2 * HEAD_DIM
assert PAIR == LANES

VMEM_LIMIT = 48 * 1024 * 1024

TM = 256
TQ_DIFF = 512
TQ_GQA = 128
TK = 1024
Q_STRIP = 256
TQ_WIN = 128
NA_QROWS = 4
NA_SLAB = NA_QROWS + NA_ROWS_MAX


def _cparams(n_axes):
    return pltpu.CompilerParams(dimension_semantics=("arbitrary",) * n_axes, vmem_limit_bytes=VMEM_LIMIT)


def _lane_iota():
    return lax.broadcasted_iota(jnp.int32, (1, LANES), 1)


def _mods_kernel(c_ref, w_ref, b_ref, o_ref):
    c = c_ref[...]
    sc = c / (1.0 + jnp.exp(-c))
    o_ref[...] = jnp.dot(sc, w_ref[...], precision=HIGHEST, preferred_element_type=F32) + b_ref[...]


def _mods_call(cvec, mod_w, mod_b):
    depth, d, n6 = mod_w.shape
    tn = n6 // 4
    return pl.pallas_call(
        _mods_kernel,
        grid=(depth, n6 // tn),
        in_specs=[
            pl.BlockSpec((8, d), lambda l, j: (0, 0)),
            pl.BlockSpec((None, d, tn), lambda l, j: (l, 0, j)),
            pl.BlockSpec((None, 1, tn), lambda l, j: (l, 0, j)),
        ],
        out_specs=pl.BlockSpec((None, 8, tn), lambda l, j: (l, 0, j)),
        out_shape=jax.ShapeDtypeStruct((depth, 8, n6), F32),
        compiler_params=_cparams(2),
        name="mods",
    )(cvec, mod_w, mod_b.reshape(depth, 1, n6))


def _inproj_kernel(h_ref, sh_ref, sc_ref, w_ref, wvt_ref, cos_ref, sin_ref, g_ref, o_ref, vt_ref, *, roles):
    a = (h_ref[...] * sc_ref[...] + sh_ref[...]).astype(BF16)
    vt = lax.dot_general(wvt_ref[...], a, (((1,), (1,)), ((), ())), preferred_element_type=F32)
    for j in range(vt_ref.shape[0]):
        vt_ref[j] = vt[j * LANES:(j + 1) * LANES, :].astype(BF16)
    lane = _lane_iota()
    rope_first = (lane & (HEAD_DIM - 1)) < (HEAD_DIM // 2)
    lo = lane < HEAD_DIM
    n_blocks = len(roles)
    for c0 in range(0, n_blocks, 2):
        width = min(2, n_blocks - c0)
        p = jnp.dot(a, w_ref[:, c0 * LANES:(c0 + width) * LANES], preferred_element_type=F32)
        for half in range(width):
            cb = c0 + half
            rope, norm, scale = roles[cb]
            blk = p[:, half * LANES:(half + 1) * LANES]
            if norm is not None:
                sq = blk * blk
                s_lo = jnp.sum(jnp.where(lo, sq, 0.0), axis=-1, keepdims=True)
                s_hi = jnp.sum(jnp.where(lo, 0.0, sq), axis=-1, keepdims=True)
                ms = jnp.where(lo, s_lo, s_hi) * (1.0 / HEAD_DIM)
                blk = blk * lax.rsqrt(ms + RMS_EPS) * g_ref[norm:norm + 1, :]
            if rope:
                partner = jnp.where(rope_first, pltpu.roll(blk, LANES - HEAD_DIM // 2, 1),
                                    pltpu.roll(blk, HEAD_DIM // 2, 1))
                blk = blk * cos_ref[...] + partner * sin_ref[...]
            if scale:
                blk = blk * Q_SCALE
            o_ref[:, cb * LANES:(cb + 1) * LANES] = blk.astype(BF16)


def _inproj_call(h, shift, scale1p, w16, wvt16, cos128, sin128, g128, roles, s_len):
    b, n, d = h.shape
    ncol = w16.shape[1]
    nvb = wvt16.shape[0] // LANES
    n_lat = s_len // TM
    seg = lambda bb, j: (bb, jnp.where(j >= n_lat, 1, 0), 0, 0)
    return pl.pallas_call(
        functools.partial(_inproj_kernel, roles=roles),
        grid=(b, n // TM),
        in_specs=[
            pl.BlockSpec((None, TM, d), lambda bb, j: (bb, j, 0)),
            pl.BlockSpec((None, None, 1, d), seg),
            pl.BlockSpec((None, None, 1, d), seg),
            pl.BlockSpec((d, ncol), lambda bb, j: (0, 0)),
            pl.BlockSpec((nvb * LANES, d), lambda bb, j: (0, 0)),
            pl.BlockSpec((TM, LANES), lambda bb, j: (j, 0)),
            pl.BlockSpec((TM, LANES), lambda bb, j: (j, 0)),
            pl.BlockSpec((2, LANES), lambda bb, j: (0, 0)),
        ],
        out_specs=[
            pl.BlockSpec((None, TM, ncol), lambda bb, j: (bb, j, 0)),
            pl.BlockSpec((None, nvb, LANES, TM), lambda bb, j: (bb, 0, 0, j)),
        ],
        out_shape=[
            jax.ShapeDtypeStruct((b, n, ncol), BF16),
            jax.ShapeDtypeStruct((b, nvb, LANES, n), BF16),
        ],
        compiler_params=_cparams(2),
        name="inproj",
    )(h, shift, scale1p, w16, wvt16, cos128, sin128, g128)


def _stack_q(q_ref, n_qblocks, row0=0, n_rows=None):
    n_rows = q_ref.shape[0] if n_rows is None else n_rows
    lo = _lane_iota() < HEAD_DIM
    parts = []
    for j in range(n_qblocks):
        q = q_ref[row0:row0 + n_rows, j * LANES:(j + 1) * LANES]
        zero = jnp.zeros_like(q)
        parts.append(jnp.where(lo, q, zero))
        parts.append(jnp.where(lo, zero, q))
    return jnp.concatenate(parts, axis=0)


def _scores_t(k, qs):
    return lax.dot_general(k, qs, (((1,), (1,)), ((), ())), preferred_element_type=F32)


def _flash_init(m_ref, l_ref, acc_ref):
    m_ref[...] = jnp.full(m_ref.shape, NEG_INF, F32)
    l_ref[...] = jnp.zeros(l_ref.shape, F32)
    acc_ref[...] = jnp.zeros(acc_ref.shape, F32)


def _flash_update(st, vt, m_ref, l_ref, acc_ref):
    m_prev = m_ref[...]
    m_new = jnp.maximum(m_prev, jnp.max(st, axis=0, keepdims=True))
    alpha = jnp.exp2(m_prev - m_new)
    pt = jnp.exp2(st - m_new)
    l_ref[...] = alpha * l_ref[...] + jnp.sum(pt, axis=0, keepdims=True)
    acc_ref[...] = alpha * acc_ref[...] + jnp.dot(vt, pt.astype(BF16), preferred_element_type=F32)
    m_ref[...] = m_new


def _flash_sweep(qs, k_ref, vt_ref, m_ref, l_ref, acc_ref, *, s_len, n_ctx):
    _flash_init(m_ref, l_ref, acc_ref)
    bounds = [(c * TK, (c + 1) * TK) for c in range(s_len // TK)] + [(s_len, s_len + n_ctx)]
    s_next = _scores_t(k_ref[bounds[0][0]:bounds[0][1], :], qs)
    for idx, (lo, hi) in enumerate(bounds):
        s_cur = s_next
        if idx + 1 < len(bounds):
            s_next = _scores_t(k_ref[bounds[idx + 1][0]:bounds[idx + 1][1], :], qs)
        _flash_update(s_cur, vt_ref[:, lo:hi], m_ref, l_ref, acc_ref)


def _merge_pairs(ot, tq, n_qblocks, o_ref, row0=0):
    for j in range(n_qblocks):
        o_lo = ot[:HEAD_DIM, (2 * j) * tq:(2 * j + 1) * tq]
        o_hi = ot[HEAD_DIM:, (2 * j + 1) * tq:(2 * j + 2) * tq]
        blk = jnp.concatenate([o_lo, o_hi], axis=0)
        o_ref[row0:row0 + tq, j * LANES:(j + 1) * LANES] = blk.T.astype(o_ref.dtype)


def _diff_attn_kernel(q_ref, k_ref, vt_ref, lam_ref, g_ref, o_ref, m_ref, l_ref, acc_ref, *, s_len, n_ctx, lam_init):
    tq = q_ref.shape[0]
    qs = _stack_q(q_ref, 1)
    _flash_sweep(qs, k_ref, vt_ref, m_ref, l_ref, acc_ref, s_len=s_len, n_ctx=n_ctx)
    lv = lam_ref[...]
    lam = (jnp.exp(jnp.sum(lv[0:1] * lv[1:2], axis=-1, keepdims=True))
           - jnp.exp(jnp.sum(lv[2:3] * lv[3:4], axis=-1, keepdims=True)) + lam_init)
    ot = acc_ref[...] / l_ref[...]
    od = ot[:, :tq] - lam * ot[:, tq:]
    ms = jnp.mean(od * od, axis=0, keepdims=True)
    od = od * lax.rsqrt(ms + RMS_EPS) * g_ref[...] * (1.0 - lam_init)
    o_ref[...] = od.T.astype(o_ref.dtype)


def _query_key_windows(n, s_len, n_ctx, tq, for_ctx):
    if for_ctx:
        assert s_len % n_ctx == 0 and n_ctx % tq == 0
        return n_ctx, s_len // tq, n_ctx, s_len // n_ctx, 0
    assert s_len % tq == 0
    return s_len, 0, n, 0, s_len


def _diff_attn_call(p, vt, lam_vecs, subln_g, *, s_len, n_ctx, n_heads, q_blk0, k_blk0, v_blk0, lam_init, for_ctx):
    b, n, _ = p.shape
    tq = min(TQ_DIFF, n_ctx) if for_ctx else TQ_DIFF
    n_q, q_i0, n_k, k_i, s_keys = _query_key_windows(n, s_len, n_ctx, tq, for_ctx)
    return pl.pallas_call(
        functools.partial(_diff_attn_kernel, s_len=s_keys, n_ctx=n_ctx, lam_init=lam_init),
        grid=(b, n_heads, n_q // tq),
        in_specs=[
            pl.BlockSpec((None, tq, LANES), lambda bb, hh, i: (bb, q_i0 + i, q_blk0 + hh)),
            pl.BlockSpec((None, n_k, LANES), lambda bb, hh, i: (bb, k_i, k_blk0 + hh)),
            pl.BlockSpec((None, None, LANES, n_k), lambda bb, hh, i: (bb, v_blk0 + hh, 0, k_i)),
            pl.BlockSpec((4, HEAD_DIM), lambda bb, hh, i: (0, 0)),
            pl.BlockSpec((LANES, 1), lambda bb, hh, i: (0, 0)),
        ],
        out_specs=pl.BlockSpec((None, tq, LANES), lambda bb, hh, i: (bb, i, hh)),
        out_shape=jax.ShapeDtypeStruct((b, n_q, n_heads * LANES), BF16),
        scratch_shapes=[pltpu.VMEM((1, 2 * tq), F32), pltpu.VMEM((1, 2 * tq), F32), pltpu.VMEM((LANES, 2 * tq), F32)],
        compiler_params=_cparams(3),
        name="diff_attn",
    )(p, p, vt, lam_vecs, subln_g.reshape(LANES, 1))


def _gqa_attn_kernel(q_ref, k_ref, vt_ref, o_ref, m_ref, l_ref, acc_ref, *, s_len, n_ctx):
    tq = q_ref.shape[0]
    n_qblocks = q_ref.shape[1] // LANES
    qs = _stack_q(q_ref, n_qblocks)
    _flash_sweep(qs, k_ref, vt_ref, m_ref, l_ref, acc_ref, s_len=s_len, n_ctx=n_ctx)
    _merge_pairs(acc_ref[...] / l_ref[...], tq, n_qblocks, o_ref)


def _gqa_attn_call(p, vt, *, s_len, n_ctx, q_width, q_col0, k_blk, v_blk, for_ctx):
    b, n, _ = p.shape
    tq = TQ_GQA
    n_q, q_i0, n_k, k_i, s_keys = _query_key_windows(n, s_len, n_ctx, tq, for_ctx)
    m_rows = 2 * (q_width // LANES) * tq
    return pl.pallas_call(
        functools.partial(_gqa_attn_kernel, s_len=s_keys, n_ctx=n_ctx),
        grid=(b, n_q // tq),
        in_specs=[
            pl.BlockSpec((None, tq, q_width), lambda bb, i: (bb, q_i0 + i, q_col0 // q_width)),
            pl.BlockSpec((None, n_k, LANES), lambda bb, i: (bb, k_i, k_blk)),
            pl.BlockSpec((None, None, LANES, n_k), lambda bb, i: (bb, v_blk, 0, k_i)),
        ],
        out_specs=pl.BlockSpec((None, tq, q_width), lambda bb, i: (bb, i, 0)),
        out_shape=jax.ShapeDtypeStruct((b, n_q, q_width), BF16),
        scratch_shapes=[pltpu.VMEM((1, m_rows), F32), pltpu.VMEM((1, m_rows), F32), pltpu.VMEM((LANES, m_rows), F32)],
        compiler_params=_cparams(2),
        name="gqa_attn",
    )(p, p, vt)


BLOCKS_PER_STEP = 2


def _win_attn_kernel(q_ref, k_ref, vt_ref, sink_ref, o_ref, *, s_len, n_ctx):
    tq = q_ref.shape[0] // BLOCKS_PER_STEP
    n_qblocks = q_ref.shape[1] // LANES
    span = tq + 2 * WINDOW
    step = pl.program_id(1)
    k_ctx = k_ref[s_len:s_len + n_ctx, :]
    vt_ctx = vt_ref[:, s_len:s_len + n_ctx]
    sink = sink_ref[...] * LOG2E

    def scores(blk):
        i = BLOCKS_PER_STEP * step + blk
        ws = pl.multiple_of(jnp.clip(i * tq - WINDOW, 0, s_len - span), LANES)
        qs = _stack_q(q_ref, n_qblocks, blk * tq, tq)
        m_rows = qs.shape[0]
        s_loc = _scores_t(k_ref[pl.ds(ws, span), :], qs)
        kpos = ws + lax.broadcasted_iota(jnp.int32, (span, m_rows), 0)
        qpos = i * tq + (lax.broadcasted_iota(jnp.int32, (span, m_rows), 1) & (tq - 1))
        s_loc = jnp.where(jnp.abs(qpos - kpos) <= WINDOW, s_loc, NEG_INF)
        return ws, s_loc, _scores_t(k_ctx, qs)

    def finish(blk, ws, s_loc, s_ctx):
        m = jnp.maximum(jnp.maximum(jnp.max(s_loc, axis=0, keepdims=True), jnp.max(s_ctx, axis=0, keepdims=True)),
                        sink)
        p_loc = jnp.exp2(s_loc - m)
        p_ctx = jnp.exp2(s_ctx - m)
        denom = jnp.sum(p_loc, axis=0, keepdims=True) + jnp.sum(p_ctx, axis=0, keepdims=True) + jnp.exp2(sink - m)
        acc = (jnp.dot(vt_ref[:, pl.ds(ws, span)], p_loc.astype(BF16), preferred_element_type=F32)
               + jnp.dot(vt_ctx, p_ctx.astype(BF16), preferred_element_type=F32))
        _merge_pairs(acc / denom, tq, n_qblocks, o_ref, blk * tq)

    pending = [scores(blk) for blk in range(BLOCKS_PER_STEP)]
    for blk, args in enumerate(pending):
        finish(blk, *args)


def _win_attn_call(p, vt, sink_row, *, s_len, n_ctx, q_width, k_blk, v_blk):
    b, n, _ = p.shape
    tq = BLOCKS_PER_STEP * TQ_WIN
    m_rows = sink_row.shape[1]
    return pl.pallas_call(
        functools.partial(_win_attn_kernel, s_len=s_len, n_ctx=n_ctx),
        grid=(b, s_len // tq),
        in_specs=[
            pl.BlockSpec((None, tq, q_width), lambda bb, i: (bb, i, 0)),
            pl.BlockSpec((None, n, LANES), lambda bb, i: (bb, 0, k_blk)),
            pl.BlockSpec((None, None, LANES, n), lambda bb, i: (bb, v_blk, 0, 0)),
            pl.BlockSpec((1, m_rows), lambda bb, i: (0, 0)),
        ],
        out_specs=pl.BlockSpec((None, tq, q_width), lambda bb, i: (bb, i, 0)),
        out_shape=jax.ShapeDtypeStruct((b, s_len, q_width), BF16),
        compiler_params=_cparams(2),
        name="win_attn",
    )(p, p, vt, sink_row)


def _na_attn_kernel(q_ref, k_ref, vt_ref, *rest, s_len, n_ctx):
    bias_refs, o_ref = rest[:BLOCKS_PER_STEP], rest[BLOCKS_PER_STEP]
    tq = q_ref.shape[0] // BLOCKS_PER_STEP
    slab = NA_SLAB * GRID_W
    rows = s_len // GRID_W
    step = pl.program_id(2)
    k_cx = k_ref[s_len:s_len + n_ctx, :]
    vt_cx = vt_ref[:, s_len:s_len + n_ctx]

    def scores(blk):
        mi = BLOCKS_PER_STEP * step + blk
        ss = pl.multiple_of(jnp.clip(NA_QROWS * mi - NA_ROWS_MAX // 2, 0, rows - NA_SLAB) * GRID_W, LANES)
        qs = _stack_q(q_ref, 1, blk * tq, tq)
        s_nb = _scores_t(k_ref[pl.ds(ss, slab), :], qs) + bias_refs[blk][...]
        return ss, s_nb, _scores_t(k_cx, qs)

    def finish(blk, ss, s_nb, s_cx):
        m = jnp.maximum(jnp.max(s_nb, axis=0, keepdims=True), jnp.max(s_cx, axis=0, keepdims=True))
        p_nb = jnp.exp2(s_nb - m)
        p_cx = jnp.exp2(s_cx - m)
        denom = jnp.sum(p_nb, axis=0, keepdims=True) + jnp.sum(p_cx, axis=0, keepdims=True)
        acc = (jnp.dot(vt_ref[:, pl.ds(ss, slab)], p_nb.astype(BF16), preferred_element_type=F32)
               + jnp.dot(vt_cx, p_cx.astype(BF16), preferred_element_type=F32))
        _merge_pairs(acc / denom, tq, 1, o_ref, blk * tq)

    pending = [scores(blk) for blk in range(BLOCKS_PER_STEP)]
    for blk, args in enumerate(pending):
        finish(blk, *args)


def _na_attn_call(p, vt, bias_tab, *, s_len, n_ctx, n_pairs, q_blk0, k_blk0, v_blk0):
    b, n, _ = p.shape
    tq = NA_QROWS * GRID_W
    n_blocks = s_len // tq
    assert n_blocks % BLOCKS_PER_STEP == 0
    tbl = lambda mi: jnp.where(mi == 0, 0, jnp.where(mi == n_blocks - 1, 2, 1))
    bias_spec = lambda blk: pl.BlockSpec((None, None, NA_SLAB * GRID_W, 2 * tq),
                                         lambda bb, j, st: (tbl(BLOCKS_PER_STEP * st + blk), j, 0, 0))
    return pl.pallas_call(
        functools.partial(_na_attn_kernel, s_len=s_len, n_ctx=n_ctx),
        grid=(b, n_pairs, n_blocks // BLOCKS_PER_STEP),
        in_specs=[
            pl.BlockSpec((None, BLOCKS_PER_STEP * tq, LANES), lambda bb, j, st: (bb, st, q_blk0 + j)),
            pl.BlockSpec((None, n, LANES), lambda bb, j, st: (bb, 0, k_blk0 + j)),
            pl.BlockSpec((None, None, LANES, n), lambda bb, j, st: (bb, v_blk0 + j, 0, 0)),
        ] + [bias_spec(blk) for blk in range(BLOCKS_PER_STEP)],
        out_specs=pl.BlockSpec((None, BLOCKS_PER_STEP * tq, LANES), lambda bb, j, st: (bb, st, j)),
        out_shape=jax.ShapeDtypeStruct((b, s_len, n_pairs * LANES), BF16),
        compiler_params=_cparams(3),
        name="na_attn",
    )(p, p, vt, *([bias_tab] * BLOCKS_PER_STEP))


def _na_bias_tables(rpb, rows):
    n_heads = rpb.shape[0]
    rpb = rpb.astype(F32)
    pad = GRID_W - NA_COLS
    rpb_p = jnp.pad(rpb, ((0, 0), (0, 0), (pad, pad)))
    col_tab = jnp.stack([rpb_p[:, :, pad + NA_COLS - 1 - qc: pad + NA_COLS - 1 - qc + GRID_W] for qc in range(GRID_W)],
                        axis=2)
    qr = np.arange(NA_QROWS)[:, None, None, None]
    qc = np.arange(GRID_W)[None, :, None, None]
    kr = np.arange(NA_SLAB)[None, None, :, None]
    kc = np.arange(GRID_W)[None, None, None, :]
    full = (NA_QROWS, GRID_W, NA_SLAB, GRID_W)
    flat = (NA_QROWS * GRID_W, NA_SLAB * GRID_W)
    tabs = []
    for q0, s0 in ((0, 0), (NA_QROWS, NA_QROWS - NA_ROWS_MAX // 2), (rows - NA_QROWS, rows - NA_SLAB)):
        q_row = q0 + qr
        k_row = s0 + kr
        r0 = np.clip(q_row - NA_ROWS_MAX // 2, 0, rows - NA_ROWS_MAX)
        c0 = np.clip(qc - NA_COLS // 2, 0, GRID_W - NA_COLS)
        valid = (k_row >= r0) & (k_row < r0 + NA_ROWS_MAX) & (kc >= c0) & (kc < c0 + NA_COLS)
        valid = np.broadcast_to(valid, full).reshape(flat)
        r_off = np.clip(k_row - q_row + NA_ROWS_MAX - 1, 0, 2 * NA_ROWS_MAX - 2)[:, 0, :, 0]
        bias = jnp.stack([jnp.stack([col_tab[:, int(r_off[a, c])] for c in range(NA_SLAB)], axis=2)
                          for a in range(NA_QROWS)], axis=1)
        tabs.append(jnp.where(valid[None], bias.reshape((n_heads,) + flat) * LOG2E, NEG_INF))
    tab = jnp.stack(tabs)
    return jnp.swapaxes(tab.reshape(3, n_heads // 2, 2 * flat[0], flat[1]), -1, -2)


def _split_hi_lo(x):
    c = x * 65537.0
    hi = c - (c - x)
    return hi.astype(BF16), (x - hi).astype(BF16)


def _layer_norm(y, g, b):
    mu = jnp.mean(y, axis=-1, keepdims=True)
    yc = y - mu
    var = jnp.mean(yc * yc, axis=-1, keepdims=True)
    return yc * lax.rsqrt(var + LN_EPS) * g + b


def _route(logits):
    lane = _lane_iota().astype(F32)
    big = float(LANES)
    is_g = lane < N_GROUPS
    gl = jnp.where(is_g, logits, NEG_INF)
    g_max = jnp.max(gl, axis=-1, keepdims=True)
    g_idx = jnp.min(jnp.where(gl == g_max, lane, big), axis=-1, keepdims=True)
    g_w = 1.0 / jnp.sum(jnp.where(is_g, jnp.exp(gl - g_max), 0.0), axis=-1, keepdims=True)
    base = N_GROUPS + EXPERTS_PER_GROUP * g_idx
    el = jnp.where((lane >= base) & (lane < base + EXPERTS_PER_GROUP), logits, NEG_INF)
    v1 = jnp.max(el, axis=-1, keepdims=True)
    i1 = jnp.min(jnp.where(el == v1, lane, big), axis=-1, keepdims=True)
    el2 = jnp.where(lane == i1, NEG_INF, el)
    v2 = jnp.max(el2, axis=-1, keepdims=True)
    i2 = jnp.min(jnp.where(el2 == v2, lane, big), axis=-1, keepdims=True)
    t = jnp.exp(v2 - v1)
    w1 = g_w / (1.0 + t)
    w2 = g_w * t / (1.0 + t)
    return jnp.where(lane == 0, i1 - N_GROUPS,
                     jnp.where(lane == 1, i2 - N_GROUPS, jnp.where(lane == 2, w1, jnp.where(lane == 3, w2, 0.0))))


def _outproj_kernel(oa_ref, ob_ref, w_ref, h_ref, gate_ref, lng_ref, lnb_ref, sh_ref, sc_ref, wr_ref, br_ref,
                    h1_ref, f_ref, route_ref, *, alpha):
    ka = oa_ref.shape[1]
    o = (jnp.dot(oa_ref[...], w_ref[:ka, :], preferred_element_type=F32)
         + jnp.dot(ob_ref[...], w_ref[ka:, :], preferred_element_type=F32))
    h1 = _layer_norm(alpha * h_ref[...] + gate_ref[...] * o, lng_ref[...], lnb_ref[...])
    h1_ref[...] = h1
    f = h1 * sc_ref[...] + sh_ref[...]
    f_ref[...] = f
    f_hi, f_lo = _split_hi_lo(f)
    hi_prod = jnp.dot(f_hi, wr_ref[...], preferred_element_type=F32)
    lo_prod = jnp.dot(f_lo, wr_ref[...], preferred_element_type=F32)
    logits = (hi_prod[:, :LANES] + hi_prod[:, LANES:]) + (lo_prod[:, :LANES] + lo_prod[:, LANES:]) + br_ref[...]
    route_ref[...] = _route(logits)


def _outproj_call(oa, ob, w16, h, gate, ln_g, ln_b, shift, scale1p, w_route, b_route, *, n_rows, s_len, alpha):
    b, _, d = h.shape
    ka, kb = oa.shape[2], ob.shape[2]
    n_lat = s_len // TM
    seg = lambda bb, j: (bb, jnp.where(j >= n_lat, 1, 0), 0, 0)
    row = lambda bb, j: (bb, j, 0)
    const = lambda bb, j: (0, 0)
    return pl.pallas_call(
        functools.partial(_outproj_kernel, alpha=alpha),
        grid=(b, n_rows // TM),
        in_specs=[
            pl.BlockSpec((None, TM, ka), row),
            pl.BlockSpec((None, TM, kb), row),
            pl.BlockSpec((ka + kb, d), const),
            pl.BlockSpec((None, TM, d), row),
            pl.BlockSpec((None, None, 1, d), seg),
            pl.BlockSpec((1, d), const),
            pl.BlockSpec((1, d), const),
            pl.BlockSpec((None, None, 1, d), seg),
            pl.BlockSpec((None, None, 1, d), seg),
            pl.BlockSpec((d, 2 * LANES), const),
            pl.BlockSpec((1, LANES), const),
        ],
        out_specs=[
            pl.BlockSpec((None, TM, d), row),
            pl.BlockSpec((None, TM, d), row),
            pl.BlockSpec((None, TM, LANES), row),
        ],
        out_shape=[
            jax.ShapeDtypeStruct((b, n_rows, d), F32),
            jax.ShapeDtypeStruct((b, n_rows, d), F32),
            jax.ShapeDtypeStruct((b, n_rows, LANES), F32),
        ],
        compiler_params=_cparams(2),
        name="outproj",
    )(oa, ob, w16, h, gate, ln_g, ln_b, shift, scale1p, w_route, b_route)


def _row_copy(src, src_row, dst, dst_row, sem):
    return pltpu.make_async_copy(src.at[pl.ds(src_row, 1), :], dst.at[pl.ds(dst_row, 1), :], sem)


def _block_copy_for_wait(src, dst, sem):
    return pltpu.make_async_copy(src, dst, sem)


MOE_BUFS = 3
DMA_PRIORITIES = 2


def _moe_kernel(blk_e_ref, tok0_ref, tok1_ref, tok_ahead_ref, dst_prev_ref, dst_last_ref, f_hbm,
                w1_ref, w3_ref, w2_ref, y_hbm, xbuf, ybuf, w1b, w3b, w2b, gsem, ssem):
    i = pl.program_id(0)
    last = pl.num_programs(0) - 1
    cur = lax.rem(i, MOE_BUFS)
    ahead = lax.rem(i + 2, MOE_BUFS)

    def wait_gather(s):
        _block_copy_for_wait(f_hbm.at[pl.ds(0, MOE_BLOCK), :], xbuf.at[s], gsem.at[s]).wait()

    def start_gather(tok_ref, s):
        for r in range(MOE_BLOCK):
            _row_copy(f_hbm, tok_ref[0, r], xbuf.at[s], r, gsem.at[s]).start(priority=r % DMA_PRIORITIES)

    def wait_scatter(s):
        _block_copy_for_wait(ybuf.at[s], y_hbm.at[pl.ds(0, MOE_BLOCK), :], ssem.at[s]).wait()

    def start_scatter(dst_ref, s):
        for r in range(MOE_BLOCK):
            _row_copy(ybuf.at[s], r, y_hbm, dst_ref[0, r], ssem.at[s]).start(priority=r % DMA_PRIORITIES)

    @pl.when(i == 0)
    def _():
        ybuf[...] = jnp.zeros(ybuf.shape, ybuf.dtype)
        start_gather(tok0_ref, 0)
        start_gather(tok1_ref, 1)

    wait_gather(cur)

    @pl.when(i >= 2)
    def _():
        wait_scatter(cur)

    @pl.when(jnp.logical_or(i == 0, blk_e_ref[i] != blk_e_ref[jnp.maximum(i - 1, 0)]))
    def _():
        w1b[...] = w1_ref[...].astype(BF16)
        w3b[...] = w3_ref[...].astype(BF16)
        w2b[...] = w2_ref[...].astype(BF16)

    x = xbuf[cur].astype(BF16)
    start_gather(tok_ahead_ref, ahead)
    start_scatter(dst_prev_ref, ahead)
    h1 = jnp.dot(x, w1b[...], preferred_element_type=F32)
    h3 = jnp.dot(x, w3b[...], preferred_element_type=F32)
    act = (h1 / (1.0 + jnp.exp(-h1))) * h3
    ybuf[cur] = jnp.dot(act.astype(BF16), w2b[...], preferred_element_type=F32)

    @pl.when(i == last)
    def _():
        start_scatter(dst_last_ref, cur)
        for s in range(MOE_BUFS):
            wait_scatter(s)
        wait_gather(lax.rem(i + 1, MOE_BUFS))
        wait_gather(ahead)


def _moe_call(f_rows, blk_e, slot_tok, slot_dst, w1, w3, w2, layer):
    t, d = f_rows.shape
    n_blocks = blk_e.shape[0]
    n_slots = n_blocks * MOE_BLOCK
    de = w1.shape[3]
    assert n_blocks >= MOE_BUFS
    smem_block = lambda index_map: pl.BlockSpec((None, 1, MOE_BLOCK), index_map, memory_space=pltpu.SMEM)
    grid_spec = pltpu.PrefetchScalarGridSpec(
        num_scalar_prefetch=1,
        grid=(n_blocks,),
        in_specs=[
            smem_block(lambda i, be: (0, 0, 0)),
            smem_block(lambda i, be: (1, 0, 0)),
            smem_block(lambda i, be: (jnp.minimum(i + 2, n_blocks - 1), 0, 0)),
            smem_block(lambda i, be: (i, 0, 0)),
            smem_block(lambda i, be: (i + 1, 0, 0)),
            pl.BlockSpec(memory_space=pl.ANY),
            pl.BlockSpec((None, None, d, de), lambda i, be: (layer, be[i], 0, 0)),
            pl.BlockSpec((None, None, d, de), lambda i, be: (layer, be[i], 0, 0)),
            pl.BlockSpec((None, None, de, d), lambda i, be: (layer, be[i], 0, 0)),
        ],
        out_specs=pl.BlockSpec(memory_space=pl.ANY),
        scratch_shapes=[
            pltpu.VMEM((MOE_BUFS, MOE_BLOCK, d), F32),
            pltpu.VMEM((MOE_BUFS, MOE_BLOCK, d), F32),
            pltpu.VMEM((d, de), BF16),
            pltpu.VMEM((d, de), BF16),
            pltpu.VMEM((de, d), BF16),
            pltpu.SemaphoreType.DMA((MOE_BUFS,)),
            pltpu.SemaphoreType.DMA((MOE_BUFS,)),
        ],
    )
    tok3 = slot_tok.reshape(n_blocks, 1, MOE_BLOCK)
    n_rows = n_slots + MOE_BLOCK
    first = (n_slots + jnp.arange(MOE_BLOCK, dtype=jnp.int32))[None, :]
    dst3 = jnp.concatenate([first, slot_dst], axis=0).reshape(n_blocks + 1, 1, MOE_BLOCK)
    return pl.pallas_call(
        _moe_kernel,
        grid_spec=grid_spec,
        out_shape=jax.ShapeDtypeStruct((n_rows, d), F32),
        compiler_params=_cparams(1),
        name="moe_experts",
    )(blk_e, tok3, tok3, tok3, dst3, dst3, f_rows, w1, w3, w2)


def _moe_plan(route, t):
    n_assign = TOP_K * t
    experts = jnp.arange(N_EXPERTS, dtype=jnp.int32)
    e_flat = route[:, :TOP_K].astype(jnp.int32).reshape(n_assign)
    order = jnp.argsort(e_flat, stable=True).astype(jnp.int32)
    counts = jnp.sum((e_flat[:, None] == experts[None, :]).astype(jnp.int32), axis=0)
    starts = jnp.cumsum(counts) - counts
    padded = (counts + MOE_BLOCK - 1) // MOE_BLOCK * MOE_BLOCK
    p_ends = jnp.cumsum(padded)
    p_starts = p_ends - padded
    n_blocks = (n_assign + MOE_BLOCK - 1) // MOE_BLOCK + N_EXPERTS
    blk_start = jnp.arange(n_blocks, dtype=jnp.int32) * MOE_BLOCK
    blk_e = jnp.minimum(jnp.sum((p_ends[None, :] <= blk_start[:, None]).astype(jnp.int32), axis=1), N_EXPERTS - 1)
    of_block = lambda v: jnp.sum(jnp.where(blk_e[:, None] == experts[None, :], v[None, :], 0), axis=1)[:, None]
    slot = blk_start[:, None] + jnp.arange(MOE_BLOCK, dtype=jnp.int32)[None, :]
    j = slot - of_block(p_starts)
    valid = j < of_block(counts)
    a_idx = order[jnp.clip(of_block(starts) + j, 0, n_assign - 1)]
    slot_tok = jnp.where(valid, a_idx // TOP_K, 0).astype(jnp.int32)
    spare = n_assign + slot - of_block(starts + counts)
    slot_dst = jnp.where(valid, (a_idx % TOP_K) * t + a_idx // TOP_K, spare).astype(jnp.int32)
    return blk_e, slot_tok, slot_dst


def _combine_kernel(h_ref, y0_ref, y1_ref, route_ref, gate_ref, lng_ref, lnb_ref, o_ref, *, alpha):
    r = route_ref[...]
    y = r[:, 2:3] * y0_ref[...] + r[:, 3:4] * y1_ref[...]
    o_ref[...] = _layer_norm(alpha * h_ref[...] + gate_ref[...] * y, lng_ref[...], lnb_ref[...])


def _combine_call(h1, y, route, gate, ln_g, ln_b, *, s_len, alpha):
    b, n_rows, d = h1.shape
    n_lat = s_len // TM
    seg = lambda bb, j: (bb, jnp.where(j >= n_lat, 1, 0), 0, 0)
    row = lambda bb, j: (bb, j, 0)
    const = lambda bb, j: (0, 0)
    blocks_per_batch = n_rows // TM
    blocks_per_k = b * blocks_per_batch
    return pl.pallas_call(
        functools.partial(_combine_kernel, alpha=alpha),
        grid=(b, n_rows // TM),
        in_specs=[
            pl.BlockSpec((None, TM, d), row),
            pl.BlockSpec((TM, d), lambda bb, j: (bb * blocks_per_batch + j, 0)),
            pl.BlockSpec((TM, d), lambda bb, j: (blocks_per_k + bb * blocks_per_batch + j, 0)),
            pl.BlockSpec((None, TM, LANES), row),
            pl.BlockSpec((None, None, 1, d), seg),
            pl.BlockSpec((1, d), const),
            pl.BlockSpec((1, d), const),
        ],
        out_specs=pl.BlockSpec((None, TM, d), row),
        out_shape=jax.ShapeDtypeStruct((b, n_rows, d), F32),
        compiler_params=_cparams(2),
        name="combine",
    )(h1, y, y, route, gate, ln_g, ln_b)


def _pair_heads(w, axis):
    w = jnp.moveaxis(w, axis, -1)
    lead = w.shape[:-1]
    n_heads = w.shape[-1] // HEAD_DIM
    w = w.reshape(lead + (2, n_heads // 2, HEAD_DIM)).swapaxes(-3, -2).reshape(lead + (n_heads * HEAD_DIM,))
    return jnp.moveaxis(w, -1, axis)


def _rope_tables(s_len, n_ctx):
    t = jnp.arange(s_len, dtype=jnp.int32)
    row = (t // GRID_W).astype(F32)
    col = (t % GRID_W).astype(F32)
    n_freq = HEAD_DIM // 4
    inv = ROPE_THETA ** (-jnp.arange(n_freq, dtype=F32) / n_freq)
    ang = jnp.concatenate([row[:, None] * inv, col[:, None] * inv], -1)
    cos = jnp.concatenate([jnp.cos(ang), jnp.ones((n_ctx, HEAD_DIM // 2), F32)], 0)
    sin = jnp.concatenate([jnp.sin(ang), jnp.zeros((n_ctx, HEAD_DIM // 2), F32)], 0)
    return jnp.tile(cos, (1, 4)), jnp.concatenate([-sin, sin, -sin, sin], -1)


def _lambda_init(layer_idx):
    return 0.8 - 0.6 * float(np.exp(-0.3 * layer_idx))


def _router_params(w_group, b_group, w_router, b_router):
    d = w_group.shape[0]
    pad = LANES - N_GROUPS - N_EXPERTS
    w = jnp.concatenate([w_group, w_router, jnp.zeros((d, pad), F32)], axis=1)
    bb = jnp.concatenate([b_group, b_router, jnp.zeros((pad,), F32)])
    w_hi, w_lo = _split_hi_lo(w)
    return jnp.concatenate([w_hi, w_lo], axis=1), bb.reshape(1, LANES)


def _moe_layer(f, route, w1, w3, w2, layer):
    b, n_rows, d = f.shape
    t = b * n_rows
    blk_e, slot_tok, slot_dst = _moe_plan(route.reshape(t, LANES), t)
    return _moe_call(f.reshape(t, d), blk_e, slot_tok, slot_dst, w1, w3, w2, layer)


def kernel(x, c, ctx, c_ctx, mod_w, mod_b, ln_g, ln_b, ab_w_in, ab_w_out, diff_lambda, diff_subln_g, gqa_qk_g,
           cd_w_in, cd_w_out, win_sink, na_rpb, moe_w_group, moe_b_group, moe_w_router, moe_b_router,
           moe_w1, moe_w3, moe_w2):
    b, s, d = x.shape
    n_ctx = ctx.shape[1]
    n = s + n_ctx
    depth = mod_w.shape[0]
    rows = s // GRID_W
    assert depth == 2, "layer pattern implemented: one differential/GQA layer, then one window/neighbourhood layer"
    assert s % TM == 0 and n_ctx % TM == 0 and s % TK == 0 and b + 1 <= 8
    assert s % TQ_DIFF == 0 and s % TQ_GQA == 0 and n_ctx % TQ_GQA == 0
    assert rows >= NA_SLAB and rows % NA_QROWS == 0 and s % GRID_W == 0
    alpha = (2.0 * depth) ** 0.25
    qw = d // 2
    kvw = qw // 4
    assert qw == 4 * LANES and kvw == LANES

    cos128, sin128 = _rope_tables(s, n_ctx)
    cvec = jnp.concatenate([c, c_ctx[None, :], jnp.zeros((8 - b - 1, d), F32)], axis=0)
    mods_all = _mods_call(cvec, mod_w, mod_b)

    def seg_mods(i):
        m = mods_all[i]
        lat = m[:b].reshape(b, 6, d)
        cm = jnp.broadcast_to(m[b].reshape(1, 6, d), (b, 6, d))
        ms = jnp.stack([lat, cm], axis=1)
        pick = lambda k, one: (ms[:, :, k] + one)[:, :, None, :]
        return pick(0, 0.0), pick(1, 1.0), pick(2, 0.0), pick(3, 0.0), pick(4, 1.0), pick(5, 0.0)

    h = jnp.concatenate([x, ctx], axis=1)

    shift1, scale1p, gate1, shift2, scale2p, gate2 = seg_mods(0)
    w_in = ab_w_in[0]
    q_d, q_g, k_d, v_d = (w_in[:, k * qw:(k + 1) * qw] for k in range(4))
    k_g, v_g = w_in[:, 4 * qw:4 * qw + kvw], w_in[:, 4 * qw + kvw:]
    w_qk = jnp.concatenate([q_d, _pair_heads(q_g, 1), k_d, k_g], axis=1).astype(BF16)
    roles0 = ((True, None, True),) * 4 + ((True, 0, True),) * 4 + ((True, None, False),) * 4 + ((True, 1, False),)
    w_vt = jnp.concatenate([v_d, v_g], axis=1).T.astype(BF16)
    g128 = jnp.tile(gqa_qk_g[0], (1, 2))
    p0, vt0 = _inproj_call(h, shift1, scale1p, w_qk, w_vt, cos128, sin128, g128, roles0, s)
    diff_attn = functools.partial(_diff_attn_call, p0, vt0, diff_lambda[0], diff_subln_g[0], s_len=s, n_ctx=n_ctx,
                                  n_heads=4, q_blk0=0, k_blk0=8, v_blk0=0, lam_init=_lambda_init(0))
    gqa_attn = functools.partial(_gqa_attn_call, p0, vt0, s_len=s, n_ctx=n_ctx, q_width=qw, q_col0=qw,
                                 k_blk=12, v_blk=4)
    o_d = jnp.concatenate([diff_attn(for_ctx=False), diff_attn(for_ctx=True)], axis=1)
    o_g = jnp.concatenate([gqa_attn(for_ctx=False), gqa_attn(for_ctx=True)], axis=1)
    w_out = jnp.concatenate([ab_w_out[0][:qw], _pair_heads(ab_w_out[0][qw:], 0)], axis=0).astype(BF16)
    w_route, b_route = _router_params(moe_w_group[0], moe_b_group[0], moe_w_router[0], moe_b_router[0])
    h1, f, route = _outproj_call(o_d, o_g, w_out, h, gate1, ln_g[0, 0:1], ln_b[0, 0:1], shift2, scale2p,
                                 w_route, b_route, n_rows=n, s_len=s, alpha=alpha)
    y = _moe_layer(f, route, moe_w1, moe_w3, moe_w2, 0)
    h = _combine_call(h1, y, route, gate2, ln_g[0, 1:2], ln_b[0, 1:2], s_len=s, alpha=alpha)

    shift1, scale1p, gate1, shift2, scale2p, gate2 = seg_mods(1)
    w_in = cd_w_in[0]
    q_w, q_n = w_in[:, :qw], w_in[:, qw:2 * qw]
    k_w, v_w = w_in[:, 2 * qw:2 * qw + kvw], w_in[:, 2 * qw + kvw:2 * qw + 2 * kvw]
    k_n, v_n = w_in[:, 2 * qw + 2 * kvw:3 * qw + 2 * kvw], w_in[:, 3 * qw + 2 * kvw:]
    w_qk = jnp.concatenate([_pair_heads(q_w, 1), q_n, k_w, k_n], axis=1).astype(BF16)
    roles1 = ((True, None, True),) * 4 + ((False, None, True),) * 4 + ((True, None, False),) \
        + ((False, None, False),) * 4
    w_vt = jnp.concatenate([v_w, v_n], axis=1).T.astype(BF16)
    p1, vt1 = _inproj_call(h, shift1, scale1p, w_qk, w_vt, cos128, sin128, jnp.ones((2, LANES), F32), roles1, s)
    sink_row = jnp.repeat(win_sink[0].reshape(2, 4).T.reshape(8), TQ_WIN).reshape(1, 8 * TQ_WIN)
    o_w = _win_attn_call(p1, vt1, sink_row, s_len=s, n_ctx=n_ctx, q_width=qw, k_blk=8, v_blk=0)
    o_n = _na_attn_call(p1, vt1, _na_bias_tables(na_rpb[0], rows), s_len=s, n_ctx=n_ctx, n_pairs=4,
                        q_blk0=4, k_blk0=9, v_blk0=1)
    w_out = jnp.concatenate([_pair_heads(cd_w_out[0][:qw], 0), cd_w_out[0][qw:]], axis=0).astype(BF16)
    w_route, b_route = _router_params(moe_w_group[1], moe_b_group[1], moe_w_router[1], moe_b_router[1])
    h1, f, route = _outproj_call(o_w, o_n, w_out, h, gate1, ln_g[1, 0:1], ln_b[1, 0:1], shift2, scale2p,
                                 w_route, b_route, n_rows=s, s_len=s, alpha=alpha)
    y = _moe_layer(f, route, moe_w1, moe_w3, moe_w2, 1)
    return _combine_call(h1, y, route, gate2, ln_g[1, 1:2], ln_b[1, 1:2], s_len=s, alpha=alpha)
```

```python
import functools

import numpy as np
import jax
import jax.numpy as jnp
from jax import lax
from jax.experimental import pallas as pl
from jax.experimental.pallas import tpu as pltpu

F32 = jnp.float32
BF16 = jnp.bfloat16
HIGHEST = lax.Precision.HIGHEST

GRID_W = 64
HEAD_DIM = 64
ROPE_THETA = 10000.0
WINDOW = 128
NA_ROWS_MAX = 8
NA_COLS = 16
N_GROUPS = 4
EXPERTS_PER_GROUP = 8
N_EXPERTS = N_GROUPS * EXPERTS_PER_GROUP
TOP_K = 2
MOE_BLOCK = 128
LN_EPS = 1e-5
RMS_EPS = 1e-6
ATTN_SCALE = HEAD_DIM ** -0.5
NEG_INF = -1e30
LOG2E = 1.4426950408889634
Q_SCALE = ATTN_SCALE * LOG2E

LANES = 128
PAIR = 2 * HEAD_DIM
assert PAIR == LANES

VMEM_LIMIT = 48 * 1024 * 1024

TM = 256
TQ_DIFF = 512
TQ_GQA = 128
TK = 1024
Q_STRIP = 256
TQ_WIN = 128
NA_QROWS = 4
NA_SLAB = NA_QROWS + NA_ROWS_MAX


def _cparams(n_axes):
    return pltpu.CompilerParams(dimension_semantics=("arbitrary",) * n_axes, vmem_limit_bytes=VMEM_LIMIT)


def _lane_iota():
    return lax.broadcasted_iota(jnp.int32, (1, LANES), 1)


def _mods_kernel(c_ref, w_ref, b_ref, o_ref):
    c = c_ref[...]
    sc = c / (1.0 + jnp.exp(-c))
    o_ref[...] = jnp.dot(sc, w_ref[...], precision=HIGHEST, preferred_element_type=F32) + b_ref[...]


def _mods_call(cvec, mod_w, mod_b):
    depth, d, n6 = mod_w.shape
    tn = n6 // 4
    return pl.pallas_call(
        _mods_kernel,
        grid=(depth, n6 // tn),
        in_specs=[
            pl.BlockSpec((8, d), lambda l, j: (0, 0)),
            pl.BlockSpec((None, d, tn), lambda l, j: (l, 0, j)),
            pl.BlockSpec((None, 1, tn), lambda l, j: (l, 0, j)),
        ],
        out_specs=pl.BlockSpec((None, 8, tn), lambda l, j: (l, 0, j)),
        out_shape=jax.ShapeDtypeStruct((depth, 8, n6), F32),
        compiler_params=_cparams(2),
        name="mods",
    )(cvec, mod_w, mod_b.reshape(depth, 1, n6))


def _inproj_kernel(h_ref, sh_ref, sc_ref, w_ref, wvt_ref, cos_ref, sin_ref, g_ref, o_ref, vt_ref, *, roles):
    a = (h_ref[...] * sc_ref[...] + sh_ref[...]).astype(BF16)
    vt = lax.dot_general(wvt_ref[...], a, (((1,), (1,)), ((), ())), preferred_element_type=F32)
    for j in range(vt_ref.shape[0]):
        vt_ref[j] = vt[j * LANES:(j + 1) * LANES, :].astype(BF16)
    lane = _lane_iota()
    rope_first = (lane & (HEAD_DIM - 1)) < (HEAD_DIM // 2)
    lo = lane < HEAD_DIM
    n_blocks = len(roles)
    for c0 in range(0, n_blocks, 2):
        width = min(2, n_blocks - c0)
        p = jnp.dot(a, w_ref[:, c0 * LANES:(c0 + width) * LANES], preferred_element_type=F32)
        for half in range(width):
            cb = c0 + half
            rope, norm, scale = roles[cb]
            blk = p[:, half * LANES:(half + 1) * LANES]
            if norm is not None:
                sq = blk * blk
                s_lo = jnp.sum(jnp.where(lo, sq, 0.0), axis=-1, keepdims=True)
                s_hi = jnp.sum(jnp.where(lo, 0.0, sq), axis=-1, keepdims=True)
                ms = jnp.where(lo, s_lo, s_hi) * (1.0 / HEAD_DIM)
                blk = blk * lax.rsqrt(ms + RMS_EPS) * g_ref[norm:norm + 1, :]
            if rope:
                partner = jnp.where(rope_first, pltpu.roll(blk, LANES - HEAD_DIM // 2, 1),
                                    pltpu.roll(blk, HEAD_DIM // 2, 1))
                blk = blk * cos_ref[...] + partner * sin_ref[...]
            if scale:
                blk = blk * Q_SCALE
            o_ref[:, cb * LANES:(cb + 1) * LANES] = blk.astype(BF16)


def _inproj_call(h, shift, scale1p, w16, wvt16, cos128, sin128, g128, roles, s_len):
    b, n, d = h.shape
    ncol = w16.shape[1]
    nvb = wvt16.shape[0] // LANES
    n_lat = s_len // TM
    seg = lambda bb, j: (bb, jnp.where(j >= n_lat, 1, 0), 0, 0)
    return pl.pallas_call(
        functools.partial(_inproj_kernel, roles=roles),
        grid=(b, n // TM),
        in_specs=[
            pl.BlockSpec((None, TM, d), lambda bb, j: (bb, j, 0)),
            pl.BlockSpec((None, None, 1, d), seg),
            pl.BlockSpec((None, None, 1, d), seg),
            pl.BlockSpec((d, ncol), lambda bb, j: (0, 0)),
            pl.BlockSpec((nvb * LANES, d), lambda bb, j: (0, 0)),
            pl.BlockSpec((TM, LANES), lambda bb, j: (j, 0)),
            pl.BlockSpec((TM, LANES), lambda bb, j: (j, 0)),
            pl.BlockSpec((2, LANES), lambda bb, j: (0, 0)),
        ],
        out_specs=[
            pl.BlockSpec((None, TM, ncol), lambda bb, j: (bb, j, 0)),
            pl.BlockSpec((None, nvb, LANES, TM), lambda bb, j: (bb, 0, 0, j)),
        ],
        out_shape=[
            jax.ShapeDtypeStruct((b, n, ncol), BF16),
            jax.ShapeDtypeStruct((b, nvb, LANES, n), BF16),
        ],
        compiler_params=_cparams(2),
        name="inproj",
    )(h, shift, scale1p, w16, wvt16, cos128, sin128, g128)


def _stack_q(q_ref, n_qblocks, row0=0, n_rows=None):
    n_rows = q_ref.shape[0] if n_rows is None else n_rows
    lo = _lane_iota() < HEAD_DIM
    parts = []
    for j in range(n_qblocks):
        q = q_ref[row0:row0 + n_rows, j * LANES:(j + 1) * LANES]
        zero = jnp.zeros_like(q)
        parts.append(jnp.where(lo, q, zero))
        parts.append(jnp.where(lo, zero, q))
    return jnp.concatenate(parts, axis=0)


def _scores_t(k, qs):
    return lax.dot_general(k, qs, (((1,), (1,)), ((), ())), preferred_element_type=F32)


def _flash_init(m_ref, l_ref, acc_ref):
    m_ref[...] = jnp.full(m_ref.shape, NEG_INF, F32)
    l_ref[...] = jnp.zeros(l_ref.shape, F32)
    acc_ref[...] = jnp.zeros(acc_ref.shape, F32)


def _flash_update(st, vt, m_ref, l_ref, acc_ref):
    m_prev = m_ref[...]
    m_new = jnp.maximum(m_prev, jnp.max(st, axis=0, keepdims=True))
    alpha = jnp.exp2(m_prev - m_new)
    pt = jnp.exp2(st - m_new)
    l_ref[...] = alpha * l_ref[...] + jnp.sum(pt, axis=0, keepdims=True)
    acc_ref[...] = alpha * acc_ref[...] + jnp.dot(vt, pt.astype(BF16), preferred_element_type=F32)
    m_ref[...] = m_new


def _flash_sweep(qs, k_ref, vt_ref, m_ref, l_ref, acc_ref, *, s_len, n_ctx):
    _flash_init(m_ref, l_ref, acc_ref)
    bounds = [(c * TK, (c + 1) * TK) for c in range(s_len // TK)] + [(s_len, s_len + n_ctx)]
    s_next = _scores_t(k_ref[bounds[0][0]:bounds[0][1], :], qs)
    for idx, (lo, hi) in enumerate(bounds):
        s_cur = s_next
        if idx + 1 < len(bounds):
            s_next = _scores_t(k_ref[bounds[idx + 1][0]:bounds[idx + 1][1], :], qs)
        _flash_update(s_cur, vt_ref[:, lo:hi], m_ref, l_ref, acc_ref)


def _merge_pairs(ot, tq, n_qblocks, o_ref, row0=0):
    for j in range(n_qblocks):
        o_lo = ot[:HEAD_DIM, (2 * j) * tq:(2 * j + 1) * tq]
        o_hi = ot[HEAD_DIM:, (2 * j + 1) * tq:(2 * j + 2) * tq]
        blk = jnp.concatenate([o_lo, o_hi], axis=0)
        o_ref[row0:row0 + tq, j * LANES:(j + 1) * LANES] = blk.T.astype(o_ref.dtype)


def _diff_attn_kernel(q_ref, k_ref, vt_ref, lam_ref, g_ref, o_ref, m_ref, l_ref, acc_ref, *, s_len, n_ctx, lam_init):
    tq = q_ref.shape[0]
    qs = _stack_q(q_ref, 1)
    _flash_sweep(qs, k_ref, vt_ref, m_ref, l_ref, acc_ref, s_len=s_len, n_ctx=n_ctx)
    lv = lam_ref[...]
    lam = (jnp.exp(jnp.sum(lv[0:1] * lv[1:2], axis=-1, keepdims=True))
           - jnp.exp(jnp.sum(lv[2:3] * lv[3:4], axis=-1, keepdims=True)) + lam_init)
    ot = acc_ref[...] / l_ref[...]
    od = ot[:, :tq] - lam * ot[:, tq:]
    ms = jnp.mean(od * od, axis=0, keepdims=True)
    od = od * lax.rsqrt(ms + RMS_EPS) * g_ref[...] * (1.0 - lam_init)
    o_ref[...] = od.T.astype(o_ref.dtype)


def _query_key_windows(n, s_len, n_ctx, tq, for_ctx):
    if for_ctx:
        assert s_len % n_ctx == 0 and n_ctx % tq == 0
        return n_ctx, s_len // tq, n_ctx, s_len // n_ctx, 0
    assert s_len % tq == 0
    return s_len, 0, n, 0, s_len


def _diff_attn_call(p, vt, lam_vecs, subln_g, *, s_len, n_ctx, n_heads, q_blk0, k_blk0, v_blk0, lam_init, for_ctx):
    b, n, _ = p.shape
    tq = min(TQ_DIFF, n_ctx) if for_ctx else TQ_DIFF
    n_q, q_i0, n_k, k_i, s_keys = _query_key_windows(n, s_len, n_ctx, tq, for_ctx)
    return pl.pallas_call(
        functools.partial(_diff_attn_kernel, s_len=s_keys, n_ctx=n_ctx, lam_init=lam_init),
        grid=(b, n_heads, n_q // tq),
        in_specs=[
            pl.BlockSpec((None, tq, LANES), lambda bb, hh, i: (bb, q_i0 + i, q_blk0 + hh)),
            pl.BlockSpec((None, n_k, LANES), lambda bb, hh, i: (bb, k_i, k_blk0 + hh)),
            pl.BlockSpec((None, None, LANES, n_k), lambda bb, hh, i: (bb, v_blk0 + hh, 0, k_i)),
            pl.BlockSpec((4, HEAD_DIM), lambda bb, hh, i: (0, 0)),
            pl.BlockSpec((LANES, 1), lambda bb, hh, i: (0, 0)),
        ],
        out_specs=pl.BlockSpec((None, tq, LANES), lambda bb, hh, i: (bb, i, hh)),
        out_shape=jax.ShapeDtypeStruct((b, n_q, n_heads * LANES), BF16),
        scratch_shapes=[pltpu.VMEM((1, 2 * tq), F32), pltpu.VMEM((1, 2 * tq), F32), pltpu.VMEM((LANES, 2 * tq), F32)],
        compiler_params=_cparams(3),
        name="diff_attn",
    )(p, p, vt, lam_vecs, subln_g.reshape(LANES, 1))


def _gqa_attn_kernel(q_ref, k_ref, vt_ref, o_ref, m_ref, l_ref, acc_ref, *, s_len, n_ctx):
    tq = q_ref.shape[0]
    n_qblocks = q_ref.shape[1] // LANES
    qs = _stack_q(q_ref, n_qblocks)
    _flash_sweep(qs, k_ref, vt_ref, m_ref, l_ref, acc_ref, s_len=s_len, n_ctx=n_ctx)
    _merge_pairs(acc_ref[...] / l_ref[...], tq, n_qblocks, o_ref)


def _gqa_attn_call(p, vt, *, s_len, n_ctx, q_width, q_col0, k_blk, v_blk, for_ctx):
    b, n, _ = p.shape
    tq = TQ_GQA
    n_q, q_i0, n_k, k_i, s_keys = _query_key_windows(n, s_len, n_ctx, tq, for_ctx)
    m_rows = 2 * (q_width // LANES) * tq
    return pl.pallas_call(
        functools.partial(_gqa_attn_kernel, s_len=s_keys, n_ctx=n_ctx),
        grid=(b, n_q // tq),
        in_specs=[
            pl.BlockSpec((None, tq, q_width), lambda bb, i: (bb, q_i0 + i, q_col0 // q_width)),
            pl.BlockSpec((None, n_k, LANES), lambda bb, i: (bb, k_i, k_blk)),
            pl.BlockSpec((None, None, LANES, n_k), lambda bb, i: (bb, v_blk, 0, k_i)),
        ],
        out_specs=pl.BlockSpec((None, tq, q_width), lambda bb, i: (bb, i, 0)),
        out_shape=jax.ShapeDtypeStruct((b, n_q, q_width), BF16),
        scratch_shapes=[pltpu.VMEM((1, m_rows), F32), pltpu.VMEM((1, m_rows), F32), pltpu.VMEM((LANES, m_rows), F32)],
        compiler_params=_cparams(2),
        name="gqa_attn",
    )(p, p, vt)


BLOCKS_PER_STEP = 2


def _win_attn_kernel(q_ref, k_ref, vt_ref, sink_ref, o_ref, *, s_len, n_ctx):
    tq = q_ref.shape[0] // BLOCKS_PER_STEP
    n_qblocks = q_ref.shape[1] // LANES
    span = tq + 2 * WINDOW
    step = pl.program_id(1)
    k_ctx = k_ref[s_len:s_len + n_ctx, :]
    vt_ctx = vt_ref[:, s_len:s_len + n_ctx]
    sink = sink_ref[...] * LOG2E

    def scores(blk):
        i = BLOCKS_PER_STEP * step + blk
        ws = pl.multiple_of(jnp.clip(i * tq - WINDOW, 0, s_len - span), LANES)
        qs = _stack_q(q_ref, n_qblocks, blk * tq, tq)
        m_rows = qs.shape[0]
        s_loc = _scores_t(k_ref[pl.ds(ws, span), :], qs)
        kpos = ws + lax.broadcasted_iota(jnp.int32, (span, m_rows), 0)
        qpos = i * tq + (lax.broadcasted_iota(jnp.int32, (span, m_rows), 1) & (tq - 1))
        s_loc = jnp.where(jnp.abs(qpos - kpos) <= WINDOW, s_loc, NEG_INF)
        return ws, s_loc, _scores_t(k_ctx, qs)

    def finish(blk, ws, s_loc, s_ctx):
        m = jnp.maximum(jnp.maximum(jnp.max(s_loc, axis=0, keepdims=True), jnp.max(s_ctx, axis=0, keepdims=True)),
                        sink)
        p_loc = jnp.exp2(s_loc - m)
        p_ctx = jnp.exp2(s_ctx - m)
        denom = jnp.sum(p_loc, axis=0, keepdims=True) + jnp.sum(p_ctx, axis=0, keepdims=True) + jnp.exp2(sink - m)
        acc = (jnp.dot(vt_ref[:, pl.ds(ws, span)], p_loc.astype(BF16), preferred_element_type=F32)
               + jnp.dot(vt_ctx, p_ctx.astype(BF16), preferred_element_type=F32))
        _merge_pairs(acc / denom, tq, n_qblocks, o_ref, blk * tq)

    pending = [scores(blk) for blk in range(BLOCKS_PER_STEP)]
    for blk, args in enumerate(pending):
        finish(blk, *args)


def _win_attn_call(p, vt, sink_row, *, s_len, n_ctx, q_width, k_blk, v_blk):
    b, n, _ = p.shape
    tq = BLOCKS_PER_STEP * TQ_WIN
    m_rows = sink_row.shape[1]
    return pl.pallas_call(
        functools.partial(_win_attn_kernel, s_len=s_len, n_ctx=n_ctx),
        grid=(b, s_len // tq),
        in_specs=[
            pl.BlockSpec((None, tq, q_width), lambda bb, i: (bb, i, 0)),
            pl.BlockSpec((None, n, LANES), lambda bb, i: (bb, 0, k_blk)),
            pl.BlockSpec((None, None, LANES, n), lambda bb, i: (bb, v_blk, 0, 0)),
            pl.BlockSpec((1, m_rows), lambda bb, i: (0, 0)),
        ],
        out_specs=pl.BlockSpec((None, tq, q_width), lambda bb, i: (bb, i, 0)),
        out_shape=jax.ShapeDtypeStruct((b, s_len, q_width), BF16),
        compiler_params=_cparams(2),
        name="win_attn",
    )(p, p, vt, sink_row)


def _na_attn_kernel(q_ref, k_ref, vt_ref, *rest, s_len, n_ctx):
    bias_refs, o_ref = rest[:BLOCKS_PER_STEP], rest[BLOCKS_PER_STEP]
    tq = q_ref.shape[0] // BLOCKS_PER_STEP
    slab = NA_SLAB * GRID_W
    rows = s_len // GRID_W
    step = pl.program_id(2)
    k_cx = k_ref[s_len:s_len + n_ctx, :]
    vt_cx = vt_ref[:, s_len:s_len + n_ctx]

    def scores(blk):
        mi = BLOCKS_PER_STEP * step + blk
        ss = pl.multiple_of(jnp.clip(NA_QROWS * mi - NA_ROWS_MAX // 2, 0, rows - NA_SLAB) * GRID_W, LANES)
        qs = _stack_q(q_ref, 1, blk * tq, tq)
        s_nb = _scores_t(k_ref[pl.ds(ss, slab), :], qs) + bias_refs[blk][...]
        return ss, s_nb, _scores_t(k_cx, qs)

    def finish(blk, ss, s_nb, s_cx):
        m = jnp.maximum(jnp.max(s_nb, axis=0, keepdims=True), jnp.max(s_cx, axis=0, keepdims=True))
        p_nb = jnp.exp2(s_nb - m)
        p_cx = jnp.exp2(s_cx - m)
        denom = jnp.sum(p_nb, axis=0, keepdims=True) + jnp.sum(p_cx, axis=0, keepdims=True)
        acc = (jnp.dot(vt_ref[:, pl.ds(ss, slab)], p_nb.astype(BF16), preferred_element_type=F32)
               + jnp.dot(vt_cx, p_cx.astype(BF16), preferred_element_type=F32))
        _merge_pairs(acc / denom, tq, 1, o_ref, blk * tq)

    pending = [scores(blk) for blk in range(BLOCKS_PER_STEP)]
    for blk, args in enumerate(pending):
        finish(blk, *args)


def _na_attn_call(p, vt, bias_tab, *, s_len, n_ctx, n_pairs, q_blk0, k_blk0, v_blk0):
    b, n, _ = p.shape
    tq = NA_QROWS * GRID_W
    n_blocks = s_len // tq
    assert n_blocks % BLOCKS_PER_STEP == 0
    tbl = lambda mi: jnp.where(mi == 0, 0, jnp.where(mi == n_blocks - 1, 2, 1))
    bias_spec = lambda blk: pl.BlockSpec((None, None, NA_SLAB * GRID_W, 2 * tq),
                                         lambda bb, j, st: (tbl(BLOCKS_PER_STEP * st + blk), j, 0, 0))
    return pl.pallas_call(
        functools.partial(_na_attn_kernel, s_len=s_len, n_ctx=n_ctx),
        grid=(b, n_pairs, n_blocks // BLOCKS_PER_STEP),
        in_specs=[
            pl.BlockSpec((None, BLOCKS_PER_STEP * tq, LANES), lambda bb, j, st: (bb, st, q_blk0 + j)),
            pl.BlockSpec((None, n, LANES), lambda bb, j, st: (bb, 0, k_blk0 + j)),
            pl.BlockSpec((None, None, LANES, n), lambda bb, j, st: (bb, v_blk0 + j, 0, 0)),
        ] + [bias_spec(blk) for blk in range(BLOCKS_PER_STEP)],
        out_specs=pl.BlockSpec((None, BLOCKS_PER_STEP * tq, LANES), lambda bb, j, st: (bb, st, j)),
        out_shape=jax.ShapeDtypeStruct((b, s_len, n_pairs * LANES), BF16),
        compiler_params=_cparams(3),
        name="na_attn",
    )(p, p, vt, *([bias_tab] * BLOCKS_PER_STEP))


def _na_bias_tables(rpb, rows):
    n_heads = rpb.shape[0]
    rpb = rpb.astype(F32)
    pad = GRID_W - NA_COLS
    rpb_p = jnp.pad(rpb, ((0, 0), (0, 0), (pad, pad)))
    col_tab = jnp.stack([rpb_p[:, :, pad + NA_COLS - 1 - qc: pad + NA_COLS - 1 - qc + GRID_W] for qc in range(GRID_W)],
                        axis=2)
    qr = np.arange(NA_QROWS)[:, None, None, None]
    qc = np.arange(GRID_W)[None, :, None, None]
    kr = np.arange(NA_SLAB)[None, None, :, None]
    kc = np.arange(GRID_W)[None, None, None, :]
    full = (NA_QROWS, GRID_W, NA_SLAB, GRID_W)
    flat = (NA_QROWS * GRID_W, NA_SLAB * GRID_W)
    tabs = []
    for q0, s0 in ((0, 0), (NA_QROWS, NA_QROWS - NA_ROWS_MAX // 2), (rows - NA_QROWS, rows - NA_SLAB)):
        q_row = q0 + qr
        k_row = s0 + kr
        r0 = np.clip(q_row - NA_ROWS_MAX // 2, 0, rows - NA_ROWS_MAX)
        c0 = np.clip(qc - NA_COLS // 2, 0, GRID_W - NA_COLS)
        valid = (k_row >= r0) & (k_row < r0 + NA_ROWS_MAX) & (kc >= c0) & (kc < c0 + NA_COLS)
        valid = np.broadcast_to(valid, full).reshape(flat)
        r_off = np.clip(k_row - q_row + NA_ROWS_MAX - 1, 0, 2 * NA_ROWS_MAX - 2)[:, 0, :, 0]
        bias = jnp.stack([jnp.stack([col_tab[:, int(r_off[a, c])] for c in range(NA_SLAB)], axis=2)
                          for a in range(NA_QROWS)], axis=1)
        tabs.append(jnp.where(valid[None], bias.reshape((n_heads,) + flat) * LOG2E, NEG_INF))
    tab = jnp.stack(tabs)
    return jnp.swapaxes(tab.reshape(3, n_heads // 2, 2 * flat[0], flat[1]), -1, -2)


def _split_hi_lo(x):
    c = x * 65537.0
    hi = c - (c - x)
    return hi.astype(BF16), (x - hi).astype(BF16)


def _layer_norm(y, g, b):
    mu = jnp.mean(y, axis=-1, keepdims=True)
    yc = y - mu
    var = jnp.mean(yc * yc, axis=-1, keepdims=True)
    return yc * lax.rsqrt(var + LN_EPS) * g + b


def _route(logits):
    lane = _lane_iota().astype(F32)
    big = float(LANES)
    is_g = lane < N_GROUPS
    gl = jnp.where(is_g, logits, NEG_INF)
    g_max = jnp.max(gl, axis=-1, keepdims=True)
    g_idx = jnp.min(jnp.where(gl == g_max, lane, big), axis=-1, keepdims=True)
    g_w = 1.0 / jnp.sum(jnp.where(is_g, jnp.exp(gl - g_max), 0.0), axis=-1, keepdims=True)
    base = N_GROUPS + EXPERTS_PER_GROUP * g_idx
    el = jnp.where((lane >= base) & (lane < base + EXPERTS_PER_GROUP), logits, NEG_INF)
    v1 = jnp.max(el, axis=-1, keepdims=True)
    i1 = jnp.min(jnp.where(el == v1, lane, big), axis=-1, keepdims=True)
    el2 = jnp.where(lane == i1, NEG_INF, el)
    v2 = jnp.max(el2, axis=-1, keepdims=True)
    i2 = jnp.min(jnp.where(el2 == v2, lane, big), axis=-1, keepdims=True)
    t = jnp.exp(v2 - v1)
    w1 = g_w / (1.0 + t)
    w2 = g_w * t / (1.0 + t)
    return jnp.where(lane == 0, i1 - N_GROUPS,
                     jnp.where(lane == 1, i2 - N_GROUPS, jnp.where(lane == 2, w1, jnp.where(lane == 3, w2, 0.0))))


def _outproj_kernel(oa_ref, ob_ref, w_ref, h_ref, gate_ref, lng_ref, lnb_ref, sh_ref, sc_ref, wr_ref, br_ref,
                    h1_ref, f_ref, route_ref, *, alpha):
    ka = oa_ref.shape[1]
    o = (jnp.dot(oa_ref[...], w_ref[:ka, :], preferred_element_type=F32)
         + jnp.dot(ob_ref[...], w_ref[ka:, :], preferred_element_type=F32))
    h1 = _layer_norm(alpha * h_ref[...] + gate_ref[...] * o, lng_ref[...], lnb_ref[...])
    h1_ref[...] = h1
    f = h1 * sc_ref[...] + sh_ref[...]
    f_ref[:, 0, :] = f
    f_hi, f_lo = _split_hi_lo(f)
    hi_prod = jnp.dot(f_hi, wr_ref[...], preferred_element_type=F32)
    lo_prod = jnp.dot(f_lo, wr_ref[...], preferred_element_type=F32)
    logits = (hi_prod[:, :LANES] + hi_prod[:, LANES:]) + (lo_prod[:, :LANES] + lo_prod[:, LANES:]) + br_ref[...]
    route_ref[...] = _route(logits)


def _outproj_call(oa, ob, w16, h, gate, ln_g, ln_b, shift, scale1p, w_route, b_route, *, n_rows, s_len, alpha):
    b, _, d = h.shape
    ka, kb = oa.shape[2], ob.shape[2]
    n_lat = s_len // TM
    seg = lambda bb, j: (bb, jnp.where(j >= n_lat, 1, 0), 0, 0)
    row = lambda bb, j: (bb, j, 0)
    const = lambda bb, j: (0, 0)
    return pl.pallas_call(
        functools.partial(_outproj_kernel, alpha=alpha),
        grid=(b, n_rows // TM),
        in_specs=[
            pl.BlockSpec((None, TM, ka), row),
            pl.BlockSpec((None, TM, kb), row),
            pl.BlockSpec((ka + kb, d), const),
            pl.BlockSpec((None, TM, d), row),
            pl.BlockSpec((None, None, 1, d), seg),
            pl.BlockSpec((1, d), const),
            pl.BlockSpec((1, d), const),
            pl.BlockSpec((None, None, 1, d), seg),
            pl.BlockSpec((None, None, 1, d), seg),
            pl.BlockSpec((d, 2 * LANES), const),
            pl.BlockSpec((1, LANES), const),
        ],
        out_specs=[
            pl.BlockSpec((None, TM, d), row),
            pl.BlockSpec((None, TM, 1, d), lambda bb, j: (bb, j, 0, 0)),
            pl.BlockSpec((None, TM, LANES), row),
        ],
        out_shape=[
            jax.ShapeDtypeStruct((b, n_rows, d), F32),
            jax.ShapeDtypeStruct((b, n_rows, 1, d), F32),
            jax.ShapeDtypeStruct((b, n_rows, LANES), F32),
        ],
        compiler_params=_cparams(2),
        name="outproj",
    )(oa, ob, w16, h, gate, ln_g, ln_b, shift, scale1p, w_route, b_route)


def _row_copy(src, src_row, dst, dst_row, sem):
    return pltpu.make_async_copy(src.at[pl.ds(src_row, 1), :], dst.at[pl.ds(dst_row, 1), :], sem)


def _block_copy_for_wait(src, dst, sem):
    return pltpu.make_async_copy(src, dst, sem)


MOE_BUFS = 3
DMA_PRIORITIES = 2
MOE_VMEM_LIMIT = 57 * 1024 * 1024


def _moe_kernel(blk_e_ref, blk_b_ref, tok_ref, dst_prev_ref, dst_last_ref, f_ref, w1_ref, w3_ref, w2_ref, y_hbm,
                xs, ybuf, w1b, w3b, w2b, ssem):
    del blk_b_ref
    i = pl.program_id(0)
    last = pl.num_programs(0) - 1
    cur = lax.rem(i, MOE_BUFS)
    prev = lax.rem(i + 2, MOE_BUFS)

    def wait_scatter(s):
        _block_copy_for_wait(ybuf.at[s], y_hbm.at[pl.ds(0, MOE_BLOCK), :], ssem.at[s]).wait()

    def start_scatter(dst_ref, s):
        for r in range(MOE_BLOCK):
            _row_copy(ybuf.at[s], r, y_hbm, dst_ref[0, r], ssem.at[s]).start(priority=r % DMA_PRIORITIES)

    @pl.when(i == 0)
    def _():
        ybuf[...] = jnp.zeros(ybuf.shape, ybuf.dtype)

    @pl.when(i >= 2)
    def _():
        wait_scatter(cur)

    @pl.when(jnp.logical_or(i == 0, blk_e_ref[i] != blk_e_ref[jnp.maximum(i - 1, 0)]))
    def _():
        w1b[...] = w1_ref[...].astype(BF16)
        w3b[...] = w3_ref[...].astype(BF16)
        w2b[...] = w2_ref[...].astype(BF16)

    for r in range(MOE_BLOCK):
        xs[pl.ds(r, 1), :] = f_ref[tok_ref[0, r]]
    x = xs[...].astype(BF16)
    start_scatter(dst_prev_ref, prev)
    h1 = jnp.dot(x, w1b[...], preferred_element_type=F32)
    h3 = jnp.dot(x, w3b[...], preferred_element_type=F32)
    act = (h1 / (1.0 + jnp.exp(-h1))) * h3
    ybuf[cur] = jnp.dot(act.astype(BF16), w2b[...], preferred_element_type=F32)

    @pl.when(i == last)
    def _():
        start_scatter(dst_last_ref, cur)
        for s in range(MOE_BUFS):
            wait_scatter(s)


def _moe_call(f_rows, blk_e, blk_b, slot_tok, slot_dst, w1, w3, w2, layer):
    n_batch, per_batch, _, d = f_rows.shape
    n_blocks = blk_e.shape[0]
    n_slots = n_blocks * MOE_BLOCK
    de = w1.shape[3]
    assert n_blocks >= MOE_BUFS
    smem_block = lambda index_map: pl.BlockSpec((None, 1, MOE_BLOCK), index_map, memory_space=pltpu.SMEM)
    grid_spec = pltpu.PrefetchScalarGridSpec(
        num_scalar_prefetch=2,
        grid=(n_blocks,),
        in_specs=[
            smem_block(lambda i, be, bb: (i, 0, 0)),
            smem_block(lambda i, be, bb: (i, 0, 0)),
            smem_block(lambda i, be, bb: (i + 1, 0, 0)),
            pl.BlockSpec((None, per_batch, 1, d), lambda i, be, bb: (bb[i], 0, 0, 0), pipeline_mode=pl.Buffered(1)),
            pl.BlockSpec((None, None, d, de), lambda i, be, bb: (layer, be[i], 0, 0)),
            pl.BlockSpec((None, None, d, de), lambda i, be, bb: (layer, be[i], 0, 0)),
            pl.BlockSpec((None, None, de, d), lambda i, be, bb: (layer, be[i], 0, 0)),
        ],
        out_specs=pl.BlockSpec(memory_space=pl.ANY),
        scratch_shapes=[
            pltpu.VMEM((MOE_BLOCK, d), F32),
            pltpu.VMEM((MOE_BUFS, MOE_BLOCK, d), F32),
            pltpu.VMEM((d, de), BF16),
            pltpu.VMEM((d, de), BF16),
            pltpu.VMEM((de, d), BF16),
            pltpu.SemaphoreType.DMA((MOE_BUFS,)),
        ],
    )
    tok3 = slot_tok.reshape(n_blocks, 1, MOE_BLOCK)
    n_rows = n_slots + MOE_BLOCK
    first = (n_slots + jnp.arange(MOE_BLOCK, dtype=jnp.int32))[None, :]
    dst3 = jnp.concatenate([first, slot_dst], axis=0).reshape(n_blocks + 1, 1, MOE_BLOCK)
    return pl.pallas_call(
        _moe_kernel,
        grid_spec=grid_spec,
        out_shape=jax.ShapeDtypeStruct((n_rows, d), F32),
        compiler_params=pltpu.CompilerParams(dimension_semantics=("arbitrary",), vmem_limit_bytes=MOE_VMEM_LIMIT),
        name="moe_experts",
    )(blk_e, blk_b, tok3, dst3, dst3, f_rows, w1, w3, w2)


def _moe_plan(route, t, n_batch):
    n_assign = TOP_K * t
    per_batch = t // n_batch
    n_seg = n_batch * N_EXPERTS
    segs = jnp.arange(n_seg, dtype=jnp.int32)
    e_flat = route[:, :TOP_K].astype(jnp.int32).reshape(n_assign)
    tok_batch = jnp.arange(n_assign, dtype=jnp.int32) // (TOP_K * per_batch)
    seg_flat = tok_batch * N_EXPERTS + e_flat
    order = jnp.argsort(seg_flat, stable=True).astype(jnp.int32)
    counts = jnp.sum((seg_flat[:, None] == segs[None, :]).astype(jnp.int32), axis=0)
    starts = jnp.cumsum(counts) - counts
    padded = (counts + MOE_BLOCK - 1) // MOE_BLOCK * MOE_BLOCK
    p_ends = jnp.cumsum(padded)
    p_starts = p_ends - padded
    n_blocks = (n_assign + MOE_BLOCK - 1) // MOE_BLOCK + n_seg
    blk_start = jnp.arange(n_blocks, dtype=jnp.int32) * MOE_BLOCK
    blk_seg = jnp.minimum(jnp.sum((p_ends[None, :] <= blk_start[:, None]).astype(jnp.int32), axis=1), n_seg - 1)
    blk_b = blk_seg // N_EXPERTS
    of_block = lambda v: jnp.sum(jnp.where(blk_seg[:, None] == segs[None, :], v[None, :], 0), axis=1)[:, None]
    slot = blk_start[:, None] + jnp.arange(MOE_BLOCK, dtype=jnp.int32)[None, :]
    j = slot - of_block(p_starts)
    valid = j < of_block(counts)
    a_idx = order[jnp.clip(of_block(starts) + j, 0, n_assign - 1)]
    slot_tok = jnp.where(valid, a_idx // TOP_K - blk_b[:, None] * per_batch, 0).astype(jnp.int32)
    spare = n_assign + slot - of_block(starts + counts)
    slot_dst = jnp.where(valid, (a_idx % TOP_K) * t + a_idx // TOP_K, spare).astype(jnp.int32)
    return blk_seg % N_EXPERTS, blk_b, slot_tok, slot_dst


def _combine_kernel(h_ref, y0_ref, y1_ref, route_ref, gate_ref, lng_ref, lnb_ref, o_ref, *, alpha):
    r = route_ref[...]
    y = r[:, 2:3] * y0_ref[...] + r[:, 3:4] * y1_ref[...]
    o_ref[...] = _layer_norm(alpha * h_ref[...] + gate_ref[...] * y, lng_ref[...], lnb_ref[...])


def _combine_call(h1, y, route, gate, ln_g, ln_b, *, s_len, alpha):
    b, n_rows, d = h1.shape
    n_lat = s_len // TM
    seg = lambda bb, j: (bb, jnp.where(j >= n_lat, 1, 0), 0, 0)
    row = lambda bb, j: (bb, j, 0)
    const = lambda bb, j: (0, 0)
    blocks_per_batch = n_rows // TM
    blocks_per_k = b * blocks_per_batch
    return pl.pallas_call(
        functools.partial(_combine_kernel, alpha=alpha),
        grid=(b, n_rows // TM),
        in_specs=[
            pl.BlockSpec((None, TM, d), row),
            pl.BlockSpec((TM, d), lambda bb, j: (bb * blocks_per_batch + j, 0)),
            pl.BlockSpec((TM, d), lambda bb, j: (blocks_per_k + bb * blocks_per_batch + j, 0)),
            pl.BlockSpec((None, TM, LANES), row),
            pl.BlockSpec((None, None, 1, d), seg),
            pl.BlockSpec((1, d), const),
            pl.BlockSpec((1, d), const),
        ],
        out_specs=pl.BlockSpec((None, TM, d), row),
        out_shape=jax.ShapeDtypeStruct((b, n_rows, d), F32),
        compiler_params=_cparams(2),
        name="combine",
    )(h1, y, y, route, gate, ln_g, ln_b)


def _pair_heads(w, axis):
    w = jnp.moveaxis(w, axis, -1)
    lead = w.shape[:-1]
    n_heads = w.shape[-1] // HEAD_DIM
    w = w.reshape(lead + (2, n_heads // 2, HEAD_DIM)).swapaxes(-3, -2).reshape(lead + (n_heads * HEAD_DIM,))
    return jnp.moveaxis(w, -1, axis)


def _rope_tables(s_len, n_ctx):
    t = jnp.arange(s_len, dtype=jnp.int32)
    row = (t // GRID_W).astype(F32)
    col = (t % GRID_W).astype(F32)
    n_freq = HEAD_DIM // 4
    inv = ROPE_THETA ** (-jnp.arange(n_freq, dtype=F32) / n_freq)
    ang = jnp.concatenate([row[:, None] * inv, col[:, None] * inv], -1)
    cos = jnp.concatenate([jnp.cos(ang), jnp.ones((n_ctx, HEAD_DIM // 2), F32)], 0)
    sin = jnp.concatenate([jnp.sin(ang), jnp.zeros((n_ctx, HEAD_DIM // 2), F32)], 0)
    return jnp.tile(cos, (1, 4)), jnp.concatenate([-sin, sin, -sin, sin], -1)


def _lambda_init(layer_idx):
    return 0.8 - 0.6 * float(np.exp(-0.3 * layer_idx))


def _router_params(w_group, b_group, w_router, b_router):
    d = w_group.shape[0]
    pad = LANES - N_GROUPS - N_EXPERTS
    w = jnp.concatenate([w_group, w_router, jnp.zeros((d, pad), F32)], axis=1)
    bb = jnp.concatenate([b_group, b_router, jnp.zeros((pad,), F32)])
    w_hi, w_lo = _split_hi_lo(w)
    return jnp.concatenate([w_hi, w_lo], axis=1), bb.reshape(1, LANES)


def _moe_layer(f, route, w1, w3, w2, layer):
    b, n_rows = f.shape[:2]
    t = b * n_rows
    blk_e, blk_b, slot_tok, slot_dst = _moe_plan(route.reshape(t, LANES), t, b)
    return _moe_call(f, blk_e, blk_b, slot_tok, slot_dst, w1, w3, w2, layer)


def kernel(x, c, ctx, c_ctx, mod_w, mod_b, ln_g, ln_b, ab_w_in, ab_w_out, diff_lambda, diff_subln_g, gqa_qk_g,
           cd_w_in, cd_w_out, win_sink, na_rpb, moe_w_group, moe_b_group, moe_w_router, moe_b_router,
           moe_w1, moe_w3, moe_w2):
    b, s, d = x.shape
    n_ctx = ctx.shape[1]
    n = s + n_ctx
    depth = mod_w.shape[0]
    rows = s // GRID_W
    assert depth == 2, "layer pattern implemented: one differential/GQA layer, then one window/neighbourhood layer"
    assert s % TM == 0 and n_ctx % TM == 0 and s % TK == 0 and b + 1 <= 8
    assert s % TQ_DIFF == 0 and s % TQ_GQA == 0 and n_ctx % TQ_GQA == 0
    assert rows >= NA_SLAB and rows % NA_QROWS == 0 and s % GRID_W == 0
    alpha = (2.0 * depth) ** 0.25
    qw = d // 2
    kvw = qw // 4
    assert qw == 4 * LANES and kvw == LANES

    cos128, sin128 = _rope_tables(s, n_ctx)
    cvec = jnp.concatenate([c, c_ctx[None, :], jnp.zeros((8 - b - 1, d), F32)], axis=0)
    mods_all = _mods_call(cvec, mod_w, mod_b)

    def seg_mods(i):
        m = mods_all[i]
        lat = m[:b].reshape(b, 6, d)
        cm = jnp.broadcast_to(m[b].reshape(1, 6, d), (b, 6, d))
        ms = jnp.stack([lat, cm], axis=1)
        pick = lambda k, one: (ms[:, :, k] + one)[:, :, None, :]
        return pick(0, 0.0), pick(1, 1.0), pick(2, 0.0), pick(3, 0.0), pick(4, 1.0), pick(5, 0.0)

    h = jnp.concatenate([x, ctx], axis=1)

    shift1, scale1p, gate1, shift2, scale2p, gate2 = seg_mods(0)
    w_in = ab_w_in[0]
    q_d, q_g, k_d, v_d = (w_in[:, k * qw:(k + 1) * qw] for k in range(4))
    k_g, v_g = w_in[:, 4 * qw:4 * qw + kvw], w_in[:, 4 * qw + kvw:]
    w_qk = jnp.concatenate([q_d, _pair_heads(q_g, 1), k_d, k_g], axis=1).astype(BF16)
    roles0 = ((True, None, True),) * 4 + ((True, 0, True),) * 4 + ((True, None, False),) * 4 + ((True, 1, False),)
    w_vt = jnp.concatenate([v_d, v_g], axis=1).T.astype(BF16)
    g128 = jnp.tile(gqa_qk_g[0], (1, 2))
    p0, vt0 = _inproj_call(h, shift1, scale1p, w_qk, w_vt, cos128, sin128, g128, roles0, s)
    diff_attn = functools.partial(_diff_attn_call, p0, vt0, diff_lambda[0], diff_subln_g[0], s_len=s, n_ctx=n_ctx,
                                  n_heads=4, q_blk0=0, k_blk0=8, v_blk0=0, lam_init=_lambda_init(0))
    gqa_attn = functools.partial(_gqa_attn_call, p0, vt0, s_len=s, n_ctx=n_ctx, q_width=qw, q_col0=qw,
                                 k_blk=12, v_blk=4)
    o_d = jnp.concatenate([diff_attn(for_ctx=False), diff_attn(for_ctx=True)], axis=1)
    o_g = jnp.concatenate([gqa_attn(for_ctx=False), gqa_attn(for_ctx=True)], axis=1)
    w_out = jnp.concatenate([ab_w_out[0][:qw], _pair_heads(ab_w_out[0][qw:], 0)], axis=0).astype(BF16)
    w_route, b_route = _router_params(moe_w_group[0], moe_b_group[0], moe_w_router[0], moe_b_router[0])
    h1, f, route = _outproj_call(o_d, o_g, w_out, h, gate1, ln_g[0, 0:1], ln_b[0, 0:1], shift2, scale2p,
                                 w_route, b_route, n_rows=n, s_len=s, alpha=alpha)
    y = _moe_layer(f, route, moe_w1, moe_w3, moe_w2, 0)
    h = _combine_call(h1, y, route, gate2, ln_g[0, 1:2], ln_b[0, 1:2], s_len=s, alpha=alpha)

    shift1, scale1p, gate1, shift2, scale2p, gate2 = seg_mods(1)
    w_in = cd_w_in[0]
    q_w, q_n = w_in[:, :qw], w_in[:, qw:2 * qw]
    k_w, v_w = w_in[:, 2 * qw:2 * qw + kvw], w_in[:, 2 * qw + kvw:2 * qw + 2 * kvw]
    k_n, v_n = w_in[:, 2 * qw + 2 * kvw:3 * qw + 2 * kvw], w_in[:, 3 * qw + 2 * kvw:]
    w_qk = jnp.concatenate([_pair_heads(q_w, 1), q_n, k_w, k_n], axis=1).astype(BF16)
    roles1 = ((True, None, True),) * 4 + ((False, None, True),) * 4 + ((True, None, False),) \
        + ((False, None, False),) * 4
    w_vt = jnp.concatenate([v_w, v_n], axis=1).T.astype(BF16)
    p1, vt1 = _inproj_call(h, shift1, scale1p, w_qk, w_vt, cos128, sin128, jnp.ones((2, LANES), F32), roles1, s)
    sink_row = jnp.repeat(win_sink[0].reshape(2, 4).T.reshape(8), TQ_WIN).reshape(1, 8 * TQ_WIN)
    o_w = _win_attn_call(p1, vt1, sink_row, s_len=s, n_ctx=n_ctx, q_width=qw, k_blk=8, v_blk=0)
    o_n = _na_attn_call(p1, vt1, _na_bias_tables(na_rpb[0], rows), s_len=s, n_ctx=n_ctx, n_pairs=4,
                        q_blk0=4, k_blk0=9, v_blk0=1)
    w_out = jnp.concatenate([_pair_heads(cd_w_out[0][:qw], 0), cd_w_out[0][qw:]], axis=0).astype(BF16)
    w_route, b_route = _router_params(moe_w_group[1], moe_b_group[1], moe_w_router[1], moe_b_router[1])
    h1, f, route = _outproj_call(o_w, o_n, w_out, h, gate1, ln_g[1, 0:1], ln_b[1, 0:1], shift2, scale2p,
                                 w_route, b_route, n_rows=s, s_len=s, alpha=alpha)
    y = _moe_layer(f, route, moe_w1, moe_w3, moe_w2, 1)
    return _combine_call(h1, y, route, gate2, ln_g[1, 1:2], ln_b[1, 1:2], s_len=s, alpha=alpha)
```

```python
import functools

import numpy as np
import jax
import jax.numpy as jnp
from jax import lax
from jax.experimental import pallas as pl
from jax.experimental.pallas import tpu as pltpu

F32 = jnp.float32
BF16 = jnp.bfloat16
HIGHEST = lax.Precision.HIGHEST

GRID_W = 64
HEAD_DIM = 64
ROPE_THETA = 10000.0
WINDOW = 128
NA_ROWS_MAX = 8
NA_COLS = 16
N_GROUPS = 4
EXPERTS_PER_GROUP = 8
N_EXPERTS = N_GROUPS * EXPERTS_PER_GROUP
TOP_K = 2
MOE_BLOCK = 128
LN_EPS = 1e-5
RMS_EPS = 1e-6
ATTN_SCALE = HEAD_DIM ** -0.5
NEG_INF = -1e30
LOG2E = 1.4426950408889634
Q_SCALE = ATTN_SCALE * LOG2E

LANES = 128
PAIR = 2 * HEAD_DIM
assert PAIR == LANES

VMEM_LIMIT = 48 * 1024 * 1024

TM = 256
TQ_DIFF = 512
TQ_GQA = 128
TK = 1024
Q_STRIP = 256
TQ_WIN = 128
NA_QROWS = 4
NA_SLAB = NA_QROWS + NA_ROWS_MAX


def _cparams(n_axes):
    return pltpu.CompilerParams(dimension_semantics=("arbitrary",) * n_axes, vmem_limit_bytes=VMEM_LIMIT)


def _lane_iota():
    return lax.broadcasted_iota(jnp.int32, (1, LANES), 1)


def _mods_kernel(c_ref, w_ref, b_ref, o_ref):
    c = c_ref[...]
    sc = c / (1.0 + jnp.exp(-c))
    o_ref[...] = jnp.dot(sc, w_ref[...], precision=HIGHEST, preferred_element_type=F32) + b_ref[...]


def _mods_call(cvec, mod_w, mod_b):
    depth, d, n6 = mod_w.shape
    tn = n6 // 4
    return pl.pallas_call(
        _mods_kernel,
        grid=(depth, n6 // tn),
        in_specs=[
            pl.BlockSpec((8, d), lambda l, j: (0, 0)),
            pl.BlockSpec((None, d, tn), lambda l, j: (l, 0, j)),
            pl.BlockSpec((None, 1, tn), lambda l, j: (l, 0, j)),
        ],
        out_specs=pl.BlockSpec((None, 8, tn), lambda l, j: (l, 0, j)),
        out_shape=jax.ShapeDtypeStruct((depth, 8, n6), F32),
        compiler_params=_cparams(2),
        name="mods",
    )(cvec, mod_w, mod_b.reshape(depth, 1, n6))


def _inproj_kernel(h_ref, sh_ref, sc_ref, w_ref, wvt_ref, cos_ref, sin_ref, g_ref, o_ref, vt_ref, *, roles):
    a = (h_ref[...] * sc_ref[...] + sh_ref[...]).astype(BF16)
    vt = lax.dot_general(wvt_ref[...], a, (((1,), (1,)), ((), ())), preferred_element_type=F32)
    for j in range(vt_ref.shape[0]):
        vt_ref[j] = vt[j * LANES:(j + 1) * LANES, :].astype(BF16)
    lane = _lane_iota()
    rope_first = (lane & (HEAD_DIM - 1)) < (HEAD_DIM // 2)
    lo = lane < HEAD_DIM
    n_blocks = len(roles)
    for c0 in range(0, n_blocks, 2):
        width = min(2, n_blocks - c0)
        p = jnp.dot(a, w_ref[:, c0 * LANES:(c0 + width) * LANES], preferred_element_type=F32)
        for half in range(width):
            cb = c0 + half
            rope, norm, scale = roles[cb]
            blk = p[:, half * LANES:(half + 1) * LANES]
            if norm is not None:
                sq = blk * blk
                s_lo = jnp.sum(jnp.where(lo, sq, 0.0), axis=-1, keepdims=True)
                s_hi = jnp.sum(jnp.where(lo, 0.0, sq), axis=-1, keepdims=True)
                ms = jnp.where(lo, s_lo, s_hi) * (1.0 / HEAD_DIM)
                blk = blk * lax.rsqrt(ms + RMS_EPS) * g_ref[norm:norm + 1, :]
            if rope:
                partner = jnp.where(rope_first, pltpu.roll(blk, LANES - HEAD_DIM // 2, 1),
                                    pltpu.roll(blk, HEAD_DIM // 2, 1))
                blk = blk * cos_ref[...] + partner * sin_ref[...]
            if scale:
                blk = blk * Q_SCALE
            o_ref[:, cb * LANES:(cb + 1) * LANES] = blk.astype(BF16)


def _inproj_call(h, shift, scale1p, w16, wvt16, cos128, sin128, g128, roles, s_len):
    b, n, d = h.shape
    ncol = w16.shape[1]
    nvb = wvt16.shape[0] // LANES
    n_lat = s_len // TM
    seg = lambda bb, j: (bb, jnp.where(j >= n_lat, 1, 0), 0, 0)
    return pl.pallas_call(
        functools.partial(_inproj_kernel, roles=roles),
        grid=(b, n // TM),
        in_specs=[
            pl.BlockSpec((None, TM, d), lambda bb, j: (bb, j, 0)),
            pl.BlockSpec((None, None, 1, d), seg),
            pl.BlockSpec((None, None, 1, d), seg),
            pl.BlockSpec((d, ncol), lambda bb, j: (0, 0)),
            pl.BlockSpec((nvb * LANES, d), lambda bb, j: (0, 0)),
            pl.BlockSpec((TM, LANES), lambda bb, j: (j, 0)),
            pl.BlockSpec((TM, LANES), lambda bb, j: (j, 0)),
            pl.BlockSpec((2, LANES), lambda bb, j: (0, 0)),
        ],
        out_specs=[
            pl.BlockSpec((None, TM, ncol), lambda bb, j: (bb, j, 0)),
            pl.BlockSpec((None, nvb, LANES, TM), lambda bb, j: (bb, 0, 0, j)),
        ],
        out_shape=[
            jax.ShapeDtypeStruct((b, n, ncol), BF16),
            jax.ShapeDtypeStruct((b, nvb, LANES, n), BF16),
        ],
        compiler_params=_cparams(2),
        name="inproj",
    )(h, shift, scale1p, w16, wvt16, cos128, sin128, g128)


def _stack_q(q_ref, n_qblocks, row0=0, n_rows=None):
    n_rows = q_ref.shape[0] if n_rows is None else n_rows
    lo = _lane_iota() < HEAD_DIM
    parts = []
    for j in range(n_qblocks):
        q = q_ref[row0:row0 + n_rows, j * LANES:(j + 1) * LANES]
        zero = jnp.zeros_like(q)
        parts.append(jnp.where(lo, q, zero))
        parts.append(jnp.where(lo, zero, q))
    return jnp.concatenate(parts, axis=0)


def _scores_t(k, qs):
    return lax.dot_general(k, qs, (((1,), (1,)), ((), ())), preferred_element_type=F32)


def _flash_init(m_ref, l_ref, acc_ref):
    m_ref[...] = jnp.full(m_ref.shape, NEG_INF, F32)
    l_ref[...] = jnp.zeros(l_ref.shape, F32)
    acc_ref[...] = jnp.zeros(acc_ref.shape, F32)


def _flash_update(st, vt, m_ref, l_ref, acc_ref):
    m_prev = m_ref[...]
    m_new = jnp.maximum(m_prev, jnp.max(st, axis=0, keepdims=True))
    alpha = jnp.exp2(m_prev - m_new)
    pt = jnp.exp2(st - m_new)
    l_ref[...] = alpha * l_ref[...] + jnp.sum(pt, axis=0, keepdims=True)
    acc_ref[...] = alpha * acc_ref[...] + jnp.dot(vt, pt.astype(BF16), preferred_element_type=F32)
    m_ref[...] = m_new


def _flash_sweep(qs, k_ref, vt_ref, m_ref, l_ref, acc_ref, *, s_len, n_ctx):
    _flash_init(m_ref, l_ref, acc_ref)
    bounds = [(c * TK, (c + 1) * TK) for c in range(s_len // TK)] + [(s_len, s_len + n_ctx)]
    s_next = _scores_t(k_ref[bounds[0][0]:bounds[0][1], :], qs)
    for idx, (lo, hi) in enumerate(bounds):
        s_cur = s_next
        if idx + 1 < len(bounds):
            s_next = _scores_t(k_ref[bounds[idx + 1][0]:bounds[idx + 1][1], :], qs)
        _flash_update(s_cur, vt_ref[:, lo:hi], m_ref, l_ref, acc_ref)


def _merge_pairs(ot, tq, n_qblocks, o_ref, row0=0):
    for j in range(n_qblocks):
        o_lo = ot[:HEAD_DIM, (2 * j) * tq:(2 * j + 1) * tq]
        o_hi = ot[HEAD_DIM:, (2 * j + 1) * tq:(2 * j + 2) * tq]
        blk = jnp.concatenate([o_lo, o_hi], axis=0)
        o_ref[row0:row0 + tq, j * LANES:(j + 1) * LANES] = blk.T.astype(o_ref.dtype)


def _diff_attn_kernel(q_ref, k_ref, vt_ref, lam_ref, g_ref, o_ref, m_ref, l_ref, acc_ref, *, s_len, n_ctx, lam_init):
    tq = q_ref.shape[0]
    qs = _stack_q(q_ref, 1)
    _flash_sweep(qs, k_ref, vt_ref, m_ref, l_ref, acc_ref, s_len=s_len, n_ctx=n_ctx)
    lv = lam_ref[...]
    lam = (jnp.exp(jnp.sum(lv[0:1] * lv[1:2], axis=-1, keepdims=True))
           - jnp.exp(jnp.sum(lv[2:3] * lv[3:4], axis=-1, keepdims=True)) + lam_init)
    ot = acc_ref[...] / l_ref[...]
    od = ot[:, :tq] - lam * ot[:, tq:]
    ms = jnp.mean(od * od, axis=0, keepdims=True)
    od = od * lax.rsqrt(ms + RMS_EPS) * g_ref[...] * (1.0 - lam_init)
    o_ref[...] = od.T.astype(o_ref.dtype)


def _query_key_windows(n, s_len, n_ctx, tq, for_ctx):
    if for_ctx:
        assert s_len % n_ctx == 0 and n_ctx % tq == 0
        return n_ctx, s_len // tq, n_ctx, s_len // n_ctx, 0
    assert s_len % tq == 0
    return s_len, 0, n, 0, s_len


def _diff_attn_call(p, vt, lam_vecs, subln_g, *, s_len, n_ctx, n_heads, q_blk0, k_blk0, v_blk0, lam_init, for_ctx):
    b, n, _ = p.shape
    tq = min(TQ_DIFF, n_ctx) if for_ctx else TQ_DIFF
    n_q, q_i0, n_k, k_i, s_keys = _query_key_windows(n, s_len, n_ctx, tq, for_ctx)
    return pl.pallas_call(
        functools.partial(_diff_attn_kernel, s_len=s_keys, n_ctx=n_ctx, lam_init=lam_init),
        grid=(b, n_heads, n_q // tq),
        in_specs=[
            pl.BlockSpec((None, tq, LANES), lambda bb, hh, i: (bb, q_i0 + i, q_blk0 + hh)),
            pl.BlockSpec((None, n_k, LANES), lambda bb, hh, i: (bb, k_i, k_blk0 + hh)),
            pl.BlockSpec((None, None, LANES, n_k), lambda bb, hh, i: (bb, v_blk0 + hh, 0, k_i)),
            pl.BlockSpec((4, HEAD_DIM), lambda bb, hh, i: (0, 0)),
            pl.BlockSpec((LANES, 1), lambda bb, hh, i: (0, 0)),
        ],
        out_specs=pl.BlockSpec((None, tq, LANES), lambda bb, hh, i: (bb, i, hh)),
        out_shape=jax.ShapeDtypeStruct((b, n_q, n_heads * LANES), BF16),
        scratch_shapes=[pltpu.VMEM((1, 2 * tq), F32), pltpu.VMEM((1, 2 * tq), F32), pltpu.VMEM((LANES, 2 * tq), F32)],
        compiler_params=_cparams(3),
        name="diff_attn",
    )(p, p, vt, lam_vecs, subln_g.reshape(LANES, 1))


def _gqa_attn_kernel(q_ref, k_ref, vt_ref, o_ref, m_ref, l_ref, acc_ref, *, s_len, n_ctx):
    tq = q_ref.shape[0]
    n_qblocks = q_ref.shape[1] // LANES
    qs = _stack_q(q_ref, n_qblocks)
    _flash_sweep(qs, k_ref, vt_ref, m_ref, l_ref, acc_ref, s_len=s_len, n_ctx=n_ctx)
    _merge_pairs(acc_ref[...] / l_ref[...], tq, n_qblocks, o_ref)


def _gqa_attn_call(p, vt, *, s_len, n_ctx, q_width, q_col0, k_blk, v_blk, for_ctx):
    b, n, _ = p.shape
    tq = TQ_GQA
    n_q, q_i0, n_k, k_i, s_keys = _query_key_windows(n, s_len, n_ctx, tq, for_ctx)
    m_rows = 2 * (q_width // LANES) * tq
    return pl.pallas_call(
        functools.partial(_gqa_attn_kernel, s_len=s_keys, n_ctx=n_ctx),
        grid=(b, n_q // tq),
        in_specs=[
            pl.BlockSpec((None, tq, q_width), lambda bb, i: (bb, q_i0 + i, q_col0 // q_width)),
            pl.BlockSpec((None, n_k, LANES), lambda bb, i: (bb, k_i, k_blk)),
            pl.BlockSpec((None, None, LANES, n_k), lambda bb, i: (bb, v_blk, 0, k_i)),
        ],
        out_specs=pl.BlockSpec((None, tq, q_width), lambda bb, i: (bb, i, 0)),
        out_shape=jax.ShapeDtypeStruct((b, n_q, q_width), BF16),
        scratch_shapes=[pltpu.VMEM((1, m_rows), F32), pltpu.VMEM((1, m_rows), F32), pltpu.VMEM((LANES, m_rows), F32)],
        compiler_params=_cparams(2),
        name="gqa_attn",
    )(p, p, vt)


BLOCKS_PER_STEP = 4


def _win_attn_kernel(q_ref, k_ref, vt_ref, sink_ref, o_ref, *, s_len, n_ctx):
    tq = q_ref.shape[0] // BLOCKS_PER_STEP
    n_qblocks = q_ref.shape[1] // LANES
    span = tq + 2 * WINDOW
    step = pl.program_id(1)
    k_ctx = k_ref[s_len:s_len + n_ctx, :]
    vt_ctx = vt_ref[:, s_len:s_len + n_ctx]
    sink = sink_ref[...] * LOG2E

    def scores(blk):
        i = BLOCKS_PER_STEP * step + blk
        ws = pl.multiple_of(jnp.clip(i * tq - WINDOW, 0, s_len - span), LANES)
        qs = _stack_q(q_ref, n_qblocks, blk * tq, tq)
        m_rows = qs.shape[0]
        s_loc = _scores_t(k_ref[pl.ds(ws, span), :], qs)
        kpos = ws + lax.broadcasted_iota(jnp.int32, (span, m_rows), 0)
        qpos = i * tq + (lax.broadcasted_iota(jnp.int32, (span, m_rows), 1) & (tq - 1))
        s_loc = jnp.where(jnp.abs(qpos - kpos) <= WINDOW, s_loc, NEG_INF)
        return ws, s_loc, _scores_t(k_ctx, qs)

    def finish(blk, ws, s_loc, s_ctx):
        m = jnp.maximum(jnp.maximum(jnp.max(s_loc, axis=0, keepdims=True), jnp.max(s_ctx, axis=0, keepdims=True)),
                        sink)
        p_loc = jnp.exp2(s_loc - m)
        p_ctx = jnp.exp2(s_ctx - m)
        denom = jnp.sum(p_loc, axis=0, keepdims=True) + jnp.sum(p_ctx, axis=0, keepdims=True) + jnp.exp2(sink - m)
        acc = (jnp.dot(vt_ref[:, pl.ds(ws, span)], p_loc.astype(BF16), preferred_element_type=F32)
               + jnp.dot(vt_ctx, p_ctx.astype(BF16), preferred_element_type=F32))
        _merge_pairs(acc / denom, tq, n_qblocks, o_ref, blk * tq)

    pending = [scores(blk) for blk in range(BLOCKS_PER_STEP)]
    for blk, args in enumerate(pending):
        finish(blk, *args)


def _win_attn_call(p, vt, sink_row, *, s_len, n_ctx, q_width, k_blk, v_blk):
    b, n, _ = p.shape
    tq = BLOCKS_PER_STEP * TQ_WIN
    m_rows = sink_row.shape[1]
    return pl.pallas_call(
        functools.partial(_win_attn_kernel, s_len=s_len, n_ctx=n_ctx),
        grid=(b, s_len // tq),
        in_specs=[
            pl.BlockSpec((None, tq, q_width), lambda bb, i: (bb, i, 0)),
            pl.BlockSpec((None, n, LANES), lambda bb, i: (bb, 0, k_blk)),
            pl.BlockSpec((None, None, LANES, n), lambda bb, i: (bb, v_blk, 0, 0)),
            pl.BlockSpec((1, m_rows), lambda bb, i: (0, 0)),
        ],
        out_specs=pl.BlockSpec((None, tq, q_width), lambda bb, i: (bb, i, 0)),
        out_shape=jax.ShapeDtypeStruct((b, s_len, q_width), BF16),
        compiler_params=_cparams(2),
        name="win_attn",
    )(p, p, vt, sink_row)


def _na_attn_kernel(q_ref, k_ref, vt_ref, *rest, s_len, n_ctx):
    bias_refs, o_ref = rest[:BLOCKS_PER_STEP], rest[BLOCKS_PER_STEP]
    tq = q_ref.shape[0] // BLOCKS_PER_STEP
    slab = NA_SLAB * GRID_W
    rows = s_len // GRID_W
    step = pl.program_id(2)
    k_cx = k_ref[s_len:s_len + n_ctx, :]
    vt_cx = vt_ref[:, s_len:s_len + n_ctx]

    def scores(blk):
        mi = BLOCKS_PER_STEP * step + blk
        ss = pl.multiple_of(jnp.clip(NA_QROWS * mi - NA_ROWS_MAX // 2, 0, rows - NA_SLAB) * GRID_W, LANES)
        qs = _stack_q(q_ref, 1, blk * tq, tq)
        s_nb = _scores_t(k_ref[pl.ds(ss, slab), :], qs) + bias_refs[blk][...]
        return ss, s_nb, _scores_t(k_cx, qs)

    def finish(blk, ss, s_nb, s_cx):
        m = jnp.maximum(jnp.max(s_nb, axis=0, keepdims=True), jnp.max(s_cx, axis=0, keepdims=True))
        p_nb = jnp.exp2(s_nb - m)
        p_cx = jnp.exp2(s_cx - m)
        denom = jnp.sum(p_nb, axis=0, keepdims=True) + jnp.sum(p_cx, axis=0, keepdims=True)
        acc = (jnp.dot(vt_ref[:, pl.ds(ss, slab)], p_nb.astype(BF16), preferred_element_type=F32)
               + jnp.dot(vt_cx, p_cx.astype(BF16), preferred_element_type=F32))
        _merge_pairs(acc / denom, tq, 1, o_ref, blk * tq)

    pending = [scores(blk) for blk in range(BLOCKS_PER_STEP)]
    for blk, args in enumerate(pending):
        finish(blk, *args)


def _na_attn_call(p, vt, bias_tab, *, s_len, n_ctx, n_pairs, q_blk0, k_blk0, v_blk0):
    b, n, _ = p.shape
    tq = NA_QROWS * GRID_W
    n_blocks = s_len // tq
    assert n_blocks % BLOCKS_PER_STEP == 0
    tbl = lambda mi: jnp.where(mi == 0, 0, jnp.where(mi == n_blocks - 1, 2, 1))
    bias_spec = lambda blk: pl.BlockSpec((None, None, NA_SLAB * GRID_W, 2 * tq),
                                         lambda bb, j, st: (tbl(BLOCKS_PER_STEP * st + blk), j, 0, 0))
    return pl.pallas_call(
        functools.partial(_na_attn_kernel, s_len=s_len, n_ctx=n_ctx),
        grid=(b, n_pairs, n_blocks // BLOCKS_PER_STEP),
        in_specs=[
            pl.BlockSpec((None, BLOCKS_PER_STEP * tq, LANES), lambda bb, j, st: (bb, st, q_blk0 + j)),
            pl.BlockSpec((None, n, LANES), lambda bb, j, st: (bb, 0, k_blk0 + j)),
            pl.BlockSpec((None, None, LANES, n), lambda bb, j, st: (bb, v_blk0 + j, 0, 0)),
        ] + [bias_spec(blk) for blk in range(BLOCKS_PER_STEP)],
        out_specs=pl.BlockSpec((None, BLOCKS_PER_STEP * tq, LANES), lambda bb, j, st: (bb, st, j)),
        out_shape=jax.ShapeDtypeStruct((b, s_len, n_pairs * LANES), BF16),
        compiler_params=_cparams(3),
        name="na_attn",
    )(p, p, vt, *([bias_tab] * BLOCKS_PER_STEP))


def _na_bias_tables(rpb, rows):
    n_heads = rpb.shape[0]
    rpb = rpb.astype(F32)
    pad = GRID_W - NA_COLS
    rpb_p = jnp.pad(rpb, ((0, 0), (0, 0), (pad, pad)))
    col_tab = jnp.stack([rpb_p[:, :, pad + NA_COLS - 1 - qc: pad + NA_COLS - 1 - qc + GRID_W] for qc in range(GRID_W)],
                        axis=2)
    qr = np.arange(NA_QROWS)[:, None, None, None]
    qc = np.arange(GRID_W)[None, :, None, None]
    kr = np.arange(NA_SLAB)[None, None, :, None]
    kc = np.arange(GRID_W)[None, None, None, :]
    full = (NA_QROWS, GRID_W, NA_SLAB, GRID_W)
    flat = (NA_QROWS * GRID_W, NA_SLAB * GRID_W)
    tabs = []
    for q0, s0 in ((0, 0), (NA_QROWS, NA_QROWS - NA_ROWS_MAX // 2), (rows - NA_QROWS, rows - NA_SLAB)):
        q_row = q0 + qr
        k_row = s0 + kr
        r0 = np.clip(q_row - NA_ROWS_MAX // 2, 0, rows - NA_ROWS_MAX)
        c0 = np.clip(qc - NA_COLS // 2, 0, GRID_W - NA_COLS)
        valid = (k_row >= r0) & (k_row < r0 + NA_ROWS_MAX) & (kc >= c0) & (kc < c0 + NA_COLS)
        valid = np.broadcast_to(valid, full).reshape(flat)
        r_off = np.clip(k_row - q_row + NA_ROWS_MAX - 1, 0, 2 * NA_ROWS_MAX - 2)[:, 0, :, 0]
        bias = jnp.stack([jnp.stack([col_tab[:, int(r_off[a, c])] for c in range(NA_SLAB)], axis=2)
                          for a in range(NA_QROWS)], axis=1)
        tabs.append(jnp.where(valid[None], bias.reshape((n_heads,) + flat) * LOG2E, NEG_INF))
    tab = jnp.stack(tabs)
    return jnp.swapaxes(tab.reshape(3, n_heads // 2, 2 * flat[0], flat[1]), -1, -2)


def _split_hi_lo(x):
    c = x * 65537.0
    hi = c - (c - x)
    return hi.astype(BF16), (x - hi).astype(BF16)


def _layer_norm(y, g, b):
    mu = jnp.mean(y, axis=-1, keepdims=True)
    yc = y - mu
    var = jnp.mean(yc * yc, axis=-1, keepdims=True)
    return yc * lax.rsqrt(var + LN_EPS) * g + b


def _route(logits):
    lane = _lane_iota().astype(F32)
    big = float(LANES)
    is_g = lane < N_GROUPS
    gl = jnp.where(is_g, logits, NEG_INF)
    g_max = jnp.max(gl, axis=-1, keepdims=True)
    g_idx = jnp.min(jnp.where(gl == g_max, lane, big), axis=-1, keepdims=True)
    g_w = 1.0 / jnp.sum(jnp.where(is_g, jnp.exp(gl - g_max), 0.0), axis=-1, keepdims=True)
    base = N_GROUPS + EXPERTS_PER_GROUP * g_idx
    el = jnp.where((lane >= base) & (lane < base + EXPERTS_PER_GROUP), logits, NEG_INF)
    v1 = jnp.max(el, axis=-1, keepdims=True)
    i1 = jnp.min(jnp.where(el == v1, lane, big), axis=-1, keepdims=True)
    el2 = jnp.where(lane == i1, NEG_INF, el)
    v2 = jnp.max(el2, axis=-1, keepdims=True)
    i2 = jnp.min(jnp.where(el2 == v2, lane, big), axis=-1, keepdims=True)
    t = jnp.exp(v2 - v1)
    w1 = g_w / (1.0 + t)
    w2 = g_w * t / (1.0 + t)
    return jnp.where(lane == 0, i1 - N_GROUPS,
                     jnp.where(lane == 1, i2 - N_GROUPS, jnp.where(lane == 2, w1, jnp.where(lane == 3, w2, 0.0))))


def _outproj_kernel(oa_ref, ob_ref, w_ref, h_ref, gate_ref, lng_ref, lnb_ref, sh_ref, sc_ref, wr_ref, br_ref,
                    h1_ref, f_ref, route_ref, *, alpha):
    ka = oa_ref.shape[1]
    o = (jnp.dot(oa_ref[...], w_ref[:ka, :], preferred_element_type=F32)
         + jnp.dot(ob_ref[...], w_ref[ka:, :], preferred_element_type=F32))
    h1 = _layer_norm(alpha * h_ref[...] + gate_ref[...] * o, lng_ref[...], lnb_ref[...])
    h1_ref[...] = h1
    f = h1 * sc_ref[...] + sh_ref[...]
    f_ref[:, 0, :] = f
    f_hi, f_lo = _split_hi_lo(f)
    hi_prod = jnp.dot(f_hi, wr_ref[...], preferred_element_type=F32)
    lo_prod = jnp.dot(f_lo, wr_ref[...], preferred_element_type=F32)
    logits = (hi_prod[:, :LANES] + hi_prod[:, LANES:]) + (lo_prod[:, :LANES] + lo_prod[:, LANES:]) + br_ref[...]
    route_ref[...] = _route(logits)


def _outproj_call(oa, ob, w16, h, gate, ln_g, ln_b, shift, scale1p, w_route, b_route, *, n_rows, s_len, alpha):
    b, _, d = h.shape
    ka, kb = oa.shape[2], ob.shape[2]
    n_lat = s_len // TM
    seg = lambda bb, j: (bb, jnp.where(j >= n_lat, 1, 0), 0, 0)
    row = lambda bb, j: (bb, j, 0)
    const = lambda bb, j: (0, 0)
    return pl.pallas_call(
        functools.partial(_outproj_kernel, alpha=alpha),
        grid=(b, n_rows // TM),
        in_specs=[
            pl.BlockSpec((None, TM, ka), row),
            pl.BlockSpec((None, TM, kb), row),
            pl.BlockSpec((ka + kb, d), const),
            pl.BlockSpec((None, TM, d), row),
            pl.BlockSpec((None, None, 1, d), seg),
            pl.BlockSpec((1, d), const),
            pl.BlockSpec((1, d), const),
            pl.BlockSpec((None, None, 1, d), seg),
            pl.BlockSpec((None, None, 1, d), seg),
            pl.BlockSpec((d, 2 * LANES), const),
            pl.BlockSpec((1, LANES), const),
        ],
        out_specs=[
            pl.BlockSpec((None, TM, d), row),
            pl.BlockSpec((None, TM, 1, d), lambda bb, j: (bb, j, 0, 0)),
            pl.BlockSpec((None, TM, LANES), row),
        ],
        out_shape=[
            jax.ShapeDtypeStruct((b, n_rows, d), F32),
            jax.ShapeDtypeStruct((b, n_rows, 1, d), F32),
            jax.ShapeDtypeStruct((b, n_rows, LANES), F32),
        ],
        compiler_params=_cparams(2),
        name="outproj",
    )(oa, ob, w16, h, gate, ln_g, ln_b, shift, scale1p, w_route, b_route)


def _row_copy(src, src_row, dst, dst_row, sem):
    return pltpu.make_async_copy(src.at[pl.ds(src_row, 1), :], dst.at[pl.ds(dst_row, 1), :], sem)


def _block_copy_for_wait(src, dst, sem):
    return pltpu.make_async_copy(src, dst, sem)


MOE_BUFS = 3
DMA_PRIORITIES = 2
MOE_VMEM_LIMIT = 57 * 1024 * 1024


def _moe_kernel(blk_e_ref, blk_b_ref, blk_used_ref, tok_ref, dst_prev_ref, dst_last_ref, f_ref,
                w1_ref, w3_ref, w2_ref, y_hbm, xs, zbuf, ybuf, w1b, w3b, w2b, ssem):
    del blk_b_ref
    i = pl.program_id(0)
    last = pl.num_programs(0) - 1
    cur = lax.rem(i, MOE_BUFS)
    prev = lax.rem(i + 2, MOE_BUFS)
    used = blk_used_ref[i] > 0
    prev_used = jnp.logical_or(i == 0, blk_used_ref[jnp.maximum(i - 1, 0)] > 0)

    def wait_scatter(s):
        _block_copy_for_wait(ybuf.at[s], y_hbm.at[pl.ds(0, MOE_BLOCK), :], ssem.at[s]).wait()

    def start_scatter(dst_ref, s):
        for r in range(MOE_BLOCK):
            _row_copy(ybuf.at[s], r, y_hbm, dst_ref[0, r], ssem.at[s]).start(priority=r % DMA_PRIORITIES)

    def start_zero_fill(dst_ref, s):
        row0 = pl.multiple_of(dst_ref[0, 0], MOE_BLOCK)
        pltpu.make_async_copy(zbuf, y_hbm.at[pl.ds(row0, MOE_BLOCK), :], ssem.at[s]).start()

    @pl.when(i == 0)
    def _():
        ybuf[...] = jnp.zeros(ybuf.shape, ybuf.dtype)
        zbuf[...] = jnp.zeros(zbuf.shape, zbuf.dtype)

    @pl.when(i >= 2)
    def _():
        wait_scatter(cur)

    @pl.when(jnp.logical_or(i == 0, blk_e_ref[i] != blk_e_ref[jnp.maximum(i - 1, 0)]))
    def _():
        w1b[...] = w1_ref[...].astype(BF16)
        w3b[...] = w3_ref[...].astype(BF16)
        w2b[...] = w2_ref[...].astype(BF16)

    @pl.when(used)
    def _():
        for r in range(MOE_BLOCK):
            xs[pl.ds(r, 1), :] = f_ref[tok_ref[0, r]]
        x = xs[...].astype(BF16)
        start_scatter(dst_prev_ref, prev)
        h1 = jnp.dot(x, w1b[...], preferred_element_type=F32)
        h3 = jnp.dot(x, w3b[...], preferred_element_type=F32)
        act = (h1 / (1.0 + jnp.exp(-h1))) * h3
        ybuf[cur] = jnp.dot(act.astype(BF16), w2b[...], preferred_element_type=F32)

    @pl.when(jnp.logical_and(jnp.logical_not(used), prev_used))
    def _():
        start_scatter(dst_prev_ref, prev)

    @pl.when(jnp.logical_and(jnp.logical_not(used), jnp.logical_not(prev_used)))
    def _():
        start_zero_fill(dst_prev_ref, prev)

    @pl.when(i == last)
    def _():
        @pl.when(used)
        def _():
            start_scatter(dst_last_ref, cur)

        @pl.when(jnp.logical_not(used))
        def _():
            start_zero_fill(dst_last_ref, cur)

        for s in range(MOE_BUFS):
            wait_scatter(s)


def _moe_call(f_rows, blk_e, blk_b, blk_used, slot_tok, slot_dst, w1, w3, w2, layer):
    n_batch, per_batch, _, d = f_rows.shape
    n_blocks = blk_e.shape[0]
    n_slots = n_blocks * MOE_BLOCK
    de = w1.shape[3]
    assert n_blocks >= MOE_BUFS
    smem_block = lambda index_map: pl.BlockSpec((None, 1, MOE_BLOCK), index_map, memory_space=pltpu.SMEM)
    grid_spec = pltpu.PrefetchScalarGridSpec(
        num_scalar_prefetch=3,
        grid=(n_blocks,),
        in_specs=[
            smem_block(lambda i, be, bb, bu: (i, 0, 0)),
            smem_block(lambda i, be, bb, bu: (i, 0, 0)),
            smem_block(lambda i, be, bb, bu: (i + 1, 0, 0)),
            pl.BlockSpec((None, per_batch, 1, d), lambda i, be, bb, bu: (bb[i], 0, 0, 0), pipeline_mode=pl.Buffered(1)),
            pl.BlockSpec((None, None, d, de), lambda i, be, bb, bu: (layer, be[i], 0, 0)),
            pl.BlockSpec((None, None, d, de), lambda i, be, bb, bu: (layer, be[i], 0, 0)),
            pl.BlockSpec((None, None, de, d), lambda i, be, bb, bu: (layer, be[i], 0, 0)),
        ],
        out_specs=pl.BlockSpec(memory_space=pl.ANY),
        scratch_shapes=[
            pltpu.VMEM((MOE_BLOCK, d), F32),
            pltpu.VMEM((MOE_BLOCK, d), F32),
            pltpu.VMEM((MOE_BUFS, MOE_BLOCK, d), F32),
            pltpu.VMEM((d, de), BF16),
            pltpu.VMEM((d, de), BF16),
            pltpu.VMEM((de, d), BF16),
            pltpu.SemaphoreType.DMA((MOE_BUFS,)),
        ],
    )
    tok3 = slot_tok.reshape(n_blocks, 1, MOE_BLOCK)
    n_rows = n_slots + MOE_BLOCK
    first = (n_slots + jnp.arange(MOE_BLOCK, dtype=jnp.int32))[None, :]
    dst3 = jnp.concatenate([first, slot_dst], axis=0).reshape(n_blocks + 1, 1, MOE_BLOCK)
    return pl.pallas_call(
        _moe_kernel,
        grid_spec=grid_spec,
        out_shape=jax.ShapeDtypeStruct((n_rows, d), F32),
        compiler_params=pltpu.CompilerParams(dimension_semantics=("arbitrary",), vmem_limit_bytes=MOE_VMEM_LIMIT),
        name="moe_experts",
    )(blk_e, blk_b, blk_used, tok3, dst3, dst3, f_rows, w1, w3, w2)


def _moe_plan(route, t, n_batch):
    n_assign = TOP_K * t
    per_batch = t // n_batch
    n_seg = n_batch * N_EXPERTS
    segs = jnp.arange(n_seg, dtype=jnp.int32)
    e_flat = route[:, :TOP_K].astype(jnp.int32).reshape(n_assign)
    tok_batch = jnp.arange(n_assign, dtype=jnp.int32) // (TOP_K * per_batch)
    seg_flat = tok_batch * N_EXPERTS + e_flat
    order = jnp.argsort(seg_flat, stable=True).astype(jnp.int32)
    counts = jnp.sum((seg_flat[:, None] == segs[None, :]).astype(jnp.int32), axis=0)
    starts = jnp.cumsum(counts) - counts
    padded = (counts + MOE_BLOCK - 1) // MOE_BLOCK * MOE_BLOCK
    p_ends = jnp.cumsum(padded)
    p_starts = p_ends - padded
    n_blocks = (n_assign + MOE_BLOCK - 1) // MOE_BLOCK + n_seg
    blk_start = jnp.arange(n_blocks, dtype=jnp.int32) * MOE_BLOCK
    blk_seg = jnp.minimum(jnp.sum((p_ends[None, :] <= blk_start[:, None]).astype(jnp.int32), axis=1), n_seg - 1)
    blk_b = blk_seg // N_EXPERTS
    of_block = lambda v: jnp.sum(jnp.where(blk_seg[:, None] == segs[None, :], v[None, :], 0), axis=1)[:, None]
    slot = blk_start[:, None] + jnp.arange(MOE_BLOCK, dtype=jnp.int32)[None, :]
    j = slot - of_block(p_starts)
    valid = j < of_block(counts)
    a_idx = order[jnp.clip(of_block(starts) + j, 0, n_assign - 1)]
    slot_tok = jnp.where(valid, a_idx // TOP_K - blk_b[:, None] * per_batch, 0).astype(jnp.int32)
    spare = n_assign + slot - of_block(starts + counts)
    slot_dst = jnp.where(valid, (a_idx % TOP_K) * t + a_idx // TOP_K, spare).astype(jnp.int32)
    blk_used = (blk_start < p_ends[n_seg - 1]).astype(jnp.int32)
    return blk_seg % N_EXPERTS, blk_b, blk_used, slot_tok, slot_dst


def _combine_kernel(h_ref, y0_ref, y1_ref, route_ref, gate_ref, lng_ref, lnb_ref, o_ref, *, alpha):
    r = route_ref[...]
    y = r[:, 2:3] * y0_ref[...] + r[:, 3:4] * y1_ref[...]
    o_ref[...] = _layer_norm(alpha * h_ref[...] + gate_ref[...] * y, lng_ref[...], lnb_ref[...])


def _combine_call(h1, y, route, gate, ln_g, ln_b, *, s_len, alpha):
    b, n_rows, d = h1.shape
    n_lat = s_len // TM
    seg = lambda bb, j: (bb, jnp.where(j >= n_lat, 1, 0), 0, 0)
    row = lambda bb, j: (bb, j, 0)
    const = lambda bb, j: (0, 0)
    blocks_per_batch = n_rows // TM
    blocks_per_k = b * blocks_per_batch
    return pl.pallas_call(
        functools.partial(_combine_kernel, alpha=alpha),
        grid=(b, n_rows // TM),
        in_specs=[
            pl.BlockSpec((None, TM, d), row),
            pl.BlockSpec((TM, d), lambda bb, j: (bb * blocks_per_batch + j, 0)),
            pl.BlockSpec((TM, d), lambda bb, j: (blocks_per_k + bb * blocks_per_batch + j, 0)),
            pl.BlockSpec((None, TM, LANES), row),
            pl.BlockSpec((None, None, 1, d), seg),
            pl.BlockSpec((1, d), const),
            pl.BlockSpec((1, d), const),
        ],
        out_specs=pl.BlockSpec((None, TM, d), row),
        out_shape=jax.ShapeDtypeStruct((b, n_rows, d), F32),
        compiler_params=_cparams(2),
        name="combine",
    )(h1, y, y, route, gate, ln_g, ln_b)


def _pair_heads(w, axis):
    w = jnp.moveaxis(w, axis, -1)
    lead = w.shape[:-1]
    n_heads = w.shape[-1] // HEAD_DIM
    w = w.reshape(lead + (2, n_heads // 2, HEAD_DIM)).swapaxes(-3, -2).reshape(lead + (n_heads * HEAD_DIM,))
    return jnp.moveaxis(w, -1, axis)


def _rope_tables(s_len, n_ctx):
    t = jnp.arange(s_len, dtype=jnp.int32)
    row = (t // GRID_W).astype(F32)
    col = (t % GRID_W).astype(F32)
    n_freq = HEAD_DIM // 4
    inv = ROPE_THETA ** (-jnp.arange(n_freq, dtype=F32) / n_freq)
    ang = jnp.concatenate([row[:, None] * inv, col[:, None] * inv], -1)
    cos = jnp.concatenate([jnp.cos(ang), jnp.ones((n_ctx, HEAD_DIM // 2), F32)], 0)
    sin = jnp.concatenate([jnp.sin(ang), jnp.zeros((n_ctx, HEAD_DIM // 2), F32)], 0)
    return jnp.tile(cos, (1, 4)), jnp.concatenate([-sin, sin, -sin, sin], -1)


def _lambda_init(layer_idx):
    return 0.8 - 0.6 * float(np.exp(-0.3 * layer_idx))


def _router_params(w_group, b_group, w_router, b_router):
    d = w_group.shape[0]
    pad = LANES - N_GROUPS - N_EXPERTS
    w = jnp.concatenate([w_group, w_router, jnp.zeros((d, pad), F32)], axis=1)
    bb = jnp.concatenate([b_group, b_router, jnp.zeros((pad,), F32)])
    w_hi, w_lo = _split_hi_lo(w)
    return jnp.concatenate([w_hi, w_lo], axis=1), bb.reshape(1, LANES)


def _moe_layer(f, route, w1, w3, w2, layer):
    b, n_rows = f.shape[:2]
    t = b * n_rows
    blk_e, blk_b, blk_used, slot_tok, slot_dst = _moe_plan(route.reshape(t, LANES), t, b)
    return _moe_call(f, blk_e, blk_b, blk_used, slot_tok, slot_dst, w1, w3, w2, layer)


def kernel(x, c, ctx, c_ctx, mod_w, mod_b, ln_g, ln_b, ab_w_in, ab_w_out, diff_lambda, diff_subln_g, gqa_qk_g,
           cd_w_in, cd_w_out, win_sink, na_rpb, moe_w_group, moe_b_group, moe_w_router, moe_b_router,
           moe_w1, moe_w3, moe_w2):
    b, s, d = x.shape
    n_ctx = ctx.shape[1]
    n = s + n_ctx
    depth = mod_w.shape[0]
    rows = s // GRID_W
    assert depth == 2, "layer pattern implemented: one differential/GQA layer, then one window/neighbourhood layer"
    assert s % TM == 0 and n_ctx % TM == 0 and s % TK == 0 and b + 1 <= 8
    assert s % TQ_DIFF == 0 and s % TQ_GQA == 0 and n_ctx % TQ_GQA == 0
    assert rows >= NA_SLAB and rows % NA_QROWS == 0 and s % GRID_W == 0
    alpha = (2.0 * depth) ** 0.25
    qw = d // 2
    kvw = qw // 4
    assert qw == 4 * LANES and kvw == LANES

    cos128, sin128 = _rope_tables(s, n_ctx)
    cvec = jnp.concatenate([c, c_ctx[None, :], jnp.zeros((8 - b - 1, d), F32)], axis=0)
    mods_all = _mods_call(cvec, mod_w, mod_b)

    def seg_mods(i):
        m = mods_all[i]
        lat = m[:b].reshape(b, 6, d)
        cm = jnp.broadcast_to(m[b].reshape(1, 6, d), (b, 6, d))
        ms = jnp.stack([lat, cm], axis=1)
        pick = lambda k, one: (ms[:, :, k] + one)[:, :, None, :]
        return pick(0, 0.0), pick(1, 1.0), pick(2, 0.0), pick(3, 0.0), pick(4, 1.0), pick(5, 0.0)

    h = jnp.concatenate([x, ctx], axis=1)

    shift1, scale1p, gate1, shift2, scale2p, gate2 = seg_mods(0)
    w_in = ab_w_in[0]
    q_d, q_g, k_d, v_d = (w_in[:, k * qw:(k + 1) * qw] for k in range(4))
    k_g, v_g = w_in[:, 4 * qw:4 * qw + kvw], w_in[:, 4 * qw + kvw:]
    w_qk = jnp.concatenate([q_d, _pair_heads(q_g, 1), k_d, k_g], axis=1).astype(BF16)
    roles0 = ((True, None, True),) * 4 + ((True, 0, True),) * 4 + ((True, None, False),) * 4 + ((True, 1, False),)
    w_vt = jnp.concatenate([v_d, v_g], axis=1).T.astype(BF16)
    g128 = jnp.tile(gqa_qk_g[0], (1, 2))
    p0, vt0 = _inproj_call(h, shift1, scale1p, w_qk, w_vt, cos128, sin128, g128, roles0, s)
    diff_attn = functools.partial(_diff_attn_call, p0, vt0, diff_lambda[0], diff_subln_g[0], s_len=s, n_ctx=n_ctx,
                                  n_heads=4, q_blk0=0, k_blk0=8, v_blk0=0, lam_init=_lambda_init(0))
    gqa_attn = functools.partial(_gqa_attn_call, p0, vt0, s_len=s, n_ctx=n_ctx, q_width=qw, q_col0=qw,
                                 k_blk=12, v_blk=4)
    o_d = jnp.concatenate([diff_attn(for_ctx=False), diff_attn(for_ctx=True)], axis=1)
    o_g = jnp.concatenate([gqa_attn(for_ctx=False), gqa_attn(for_ctx=True)], axis=1)
    w_out = jnp.concatenate([ab_w_out[0][:qw], _pair_heads(ab_w_out[0][qw:], 0)], axis=0).astype(BF16)
    w_route, b_route = _router_params(moe_w_group[0], moe_b_group[0], moe_w_router[0], moe_b_router[0])
    h1, f, route = _outproj_call(o_d, o_g, w_out, h, gate1, ln_g[0, 0:1], ln_b[0, 0:1], shift2, scale2p,
                                 w_route, b_route, n_rows=n, s_len=s, alpha=alpha)
    y = _moe_layer(f, route, moe_w1, moe_w3, moe_w2, 0)
    h = _combine_call(h1, y, route, gate2, ln_g[0, 1:2], ln_b[0, 1:2], s_len=s, alpha=alpha)

    shift1, scale1p, gate1, shift2, scale2p, gate2 = seg_mods(1)
    w_in = cd_w_in[0]
    q_w, q_n = w_in[:, :qw], w_in[:, qw:2 * qw]
    k_w, v_w = w_in[:, 2 * qw:2 * qw + kvw], w_in[:, 2 * qw + kvw:2 * qw + 2 * kvw]
    k_n, v_n = w_in[:, 2 * qw + 2 * kvw:3 * qw + 2 * kvw], w_in[:, 3 * qw + 2 * kvw:]
    w_qk = jnp.concatenate([_pair_heads(q_w, 1), q_n, k_w, k_n], axis=1).astype(BF16)
    roles1 = ((True, None, True),) * 4 + ((False, None, True),) * 4 + ((True, None, False),) \
        + ((False, None, False),) * 4
    w_vt = jnp.concatenate([v_w, v_n], axis=1).T.astype(BF16)
    p1, vt1 = _inproj_call(h, shift1, scale1p, w_qk, w_vt, cos128, sin128, jnp.ones((2, LANES), F32), roles1, s)
    sink_row = jnp.repeat(win_sink[0].reshape(2, 4).T.reshape(8), TQ_WIN).reshape(1, 8 * TQ_WIN)
    o_w = _win_attn_call(p1, vt1, sink_row, s_len=s, n_ctx=n_ctx, q_width=qw, k_blk=8, v_blk=0)
    o_n = _na_attn_call(p1, vt1, _na_bias_tables(na_rpb[0], rows), s_len=s, n_ctx=n_ctx, n_pairs=4,
                        q_blk0=4, k_blk0=9, v_blk0=1)
    w_out = jnp.concatenate([_pair_heads(cd_w_out[0][:qw], 0), cd_w_out[0][qw:]], axis=0).astype(BF16)
    w_route, b_route = _router_params(moe_w_group[1], moe_b_group[1], moe_w_router[1], moe_b_router[1])
    h1, f, route = _outproj_call(o_w, o_n, w_out, h, gate1, ln_g[1, 0:1], ln_b[1, 0:1], shift2, scale2p,
                                 w_route, b_route, n_rows=s, s_len=s, alpha=alpha)
    y = _moe_layer(f, route, moe_w1, moe_w3, moe_w2, 1)
    return _combine_call(h1, y, route, gate2, ln_g[1, 1:2], ln_b[1, 1:2], s_len=s, alpha=alpha)
```

```python
import functools

import numpy as np
import jax
import jax.numpy as jnp
from jax import lax
from jax.experimental import pallas as pl
from jax.experimental.pallas import tpu as pltpu

F32 = jnp.float32
BF16 = jnp.bfloat16
HIGHEST = lax.Precision.HIGHEST

GRID_W = 64
HEAD_DIM = 64
ROPE_THETA = 10000.0
WINDOW = 128
NA_ROWS_MAX = 8
NA_COLS = 16
N_GROUPS = 4
EXPERTS_PER_GROUP = 8
N_EXPERTS = N_GROUPS * EXPERTS_PER_GROUP
TOP_K = 2
MOE_BLOCK = 128
LN_EPS = 1e-5
RMS_EPS = 1e-6
ATTN_SCALE = HEAD_DIM ** -0.5
NEG_INF = -1e30
LOG2E = 1.4426950408889634
Q_SCALE = ATTN_SCALE * LOG2E

LANES = 128
PAIR = 2 * HEAD_DIM
assert PAIR == LANES

VMEM_LIMIT = 48 * 1024 * 1024

TM = 256
TQ_DIFF = 512
TQ_GQA = 128
TK = 1024
TQ_WIN = 128
NA_QROWS = 4
NA_SLAB = NA_QROWS + NA_ROWS_MAX


def _cparams(n_axes):
    return pltpu.CompilerParams(dimension_semantics=("arbitrary",) * n_axes, vmem_limit_bytes=VMEM_LIMIT)


def _lane_iota():
    return lax.broadcasted_iota(jnp.int32, (1, LANES), 1)


def _mods_kernel(c_ref, w_ref, b_ref, o_ref):
    c = c_ref[...]
    sc = c / (1.0 + jnp.exp(-c))
    o_ref[...] = jnp.dot(sc, w_ref[...], precision=HIGHEST, preferred_element_type=F32) + b_ref[...]


def _mods_call(cvec, mod_w, mod_b):
    depth, d, n6 = mod_w.shape
    tn = n6 // 4
    return pl.pallas_call(
        _mods_kernel,
        grid=(depth, n6 // tn),
        in_specs=[
            pl.BlockSpec((8, d), lambda l, j: (0, 0)),
            pl.BlockSpec((None, d, tn), lambda l, j: (l, 0, j)),
            pl.BlockSpec((None, 1, tn), lambda l, j: (l, 0, j)),
        ],
        out_specs=pl.BlockSpec((None, 8, tn), lambda l, j: (l, 0, j)),
        out_shape=jax.ShapeDtypeStruct((depth, 8, n6), F32),
        compiler_params=_cparams(2),
        name="mods",
    )(cvec, mod_w, mod_b.reshape(depth, 1, n6))


def _split_stream_specs(lat, ctx, ctx_block0, n_lat):
    width = lat.shape[2]
    return [
        pl.BlockSpec((None, TM, width), lambda bb, j: (bb, jnp.minimum(j, n_lat - 1), 0)),
        pl.BlockSpec((None, TM, width), lambda bb, j: (bb, ctx_block0 + jnp.maximum(j - n_lat, 0), 0)),
    ]


def _read_split_stream(lat_ref, ctx_ref, n_lat):
    return jnp.where(pl.program_id(1) >= n_lat, ctx_ref[...], lat_ref[...])


def _inproj_kernel(hl_ref, hc_ref, sh_ref, sc_ref, w_ref, wvt_ref, cos_ref, sin_ref, g_ref, o_ref, vt_ref, *,
                   roles, n_lat):
    a = (_read_split_stream(hl_ref, hc_ref, n_lat) * sc_ref[...] + sh_ref[...]).astype(BF16)
    vt = lax.dot_general(wvt_ref[...], a, (((1,), (1,)), ((), ())), preferred_element_type=F32)
    for j in range(vt_ref.shape[0]):
        vt_ref[j] = vt[j * LANES:(j + 1) * LANES, :].astype(BF16)
    lane = _lane_iota()
    rope_first = (lane & (HEAD_DIM - 1)) < (HEAD_DIM // 2)
    lo = lane < HEAD_DIM
    n_blocks = len(roles)
    for c0 in range(0, n_blocks, 2):
        width = min(2, n_blocks - c0)
        p = jnp.dot(a, w_ref[:, c0 * LANES:(c0 + width) * LANES], preferred_element_type=F32)
        for half in range(width):
            cb = c0 + half
            rope, norm, scale = roles[cb]
            blk = p[:, half * LANES:(half + 1) * LANES]
            if norm is not None:
                sq = blk * blk
                s_lo = jnp.sum(jnp.where(lo, sq, 0.0), axis=-1, keepdims=True)
                s_hi = jnp.sum(jnp.where(lo, 0.0, sq), axis=-1, keepdims=True)
                ms = jnp.where(lo, s_lo, s_hi) * (1.0 / HEAD_DIM)
                blk = blk * lax.rsqrt(ms + RMS_EPS) * g_ref[norm:norm + 1, :]
            if rope:
                partner = jnp.where(rope_first, pltpu.roll(blk, LANES - HEAD_DIM // 2, 1),
                                    pltpu.roll(blk, HEAD_DIM // 2, 1))
                blk = blk * cos_ref[...] + partner * sin_ref[...]
            if scale:
                blk = blk * Q_SCALE
            o_ref[:, cb * LANES:(cb + 1) * LANES] = blk.astype(BF16)


def _inproj_call(h_lat, h_ctx, ctx_block0, n, shift, scale1p, w16, wvt16, cos128, sin128, g128, roles, s_len):
    b, _, d = h_lat.shape
    ncol = w16.shape[1]
    nvb = wvt16.shape[0] // LANES
    n_lat = s_len // TM
    seg = lambda bb, j: (bb, jnp.where(j >= n_lat, 1, 0), 0, 0)
    return pl.pallas_call(
        functools.partial(_inproj_kernel, roles=roles, n_lat=n_lat),
        grid=(b, n // TM),
        in_specs=_split_stream_specs(h_lat, h_ctx, ctx_block0, n_lat) + [
            pl.BlockSpec((None, None, 1, d), seg),
            pl.BlockSpec((None, None, 1, d), seg),
            pl.BlockSpec((d, ncol), lambda bb, j: (0, 0)),
            pl.BlockSpec((nvb * LANES, d), lambda bb, j: (0, 0)),
            pl.BlockSpec((TM, LANES), lambda bb, j: (j, 0)),
            pl.BlockSpec((TM, LANES), lambda bb, j: (j, 0)),
            pl.BlockSpec((2, LANES), lambda bb, j: (0, 0)),
        ],
        out_specs=[
            pl.BlockSpec((None, TM, ncol), lambda bb, j: (bb, j, 0)),
            pl.BlockSpec((None, nvb, LANES, TM), lambda bb, j: (bb, 0, 0, j)),
        ],
        out_shape=[
            jax.ShapeDtypeStruct((b, n, ncol), BF16),
            jax.ShapeDtypeStruct((b, nvb, LANES, n), BF16),
        ],
        compiler_params=_cparams(2),
        name="inproj",
    )(h_lat, h_ctx, shift, scale1p, w16, wvt16, cos128, sin128, g128)


def _stack_q(q_ref, n_qblocks, row0=0, n_rows=None):
    n_rows = q_ref.shape[0] if n_rows is None else n_rows
    lo = _lane_iota() < HEAD_DIM
    parts = []
    for j in range(n_qblocks):
        q = q_ref[row0:row0 + n_rows, j * LANES:(j + 1) * LANES]
        zero = jnp.zeros_like(q)
        parts.append(jnp.where(lo, q, zero))
        parts.append(jnp.where(lo, zero, q))
    return jnp.concatenate(parts, axis=0)


def _scores_t(k, qs):
    return lax.dot_general(k, qs, (((1,), (1,)), ((), ())), preferred_element_type=F32)


def _flash_init(m_ref, l_ref, acc_ref):
    m_ref[...] = jnp.full(m_ref.shape, NEG_INF, F32)
    l_ref[...] = jnp.zeros(l_ref.shape, F32)
    acc_ref[...] = jnp.zeros(acc_ref.shape, F32)


def _flash_update(st, vt, m_ref, l_ref, acc_ref):
    m_prev = m_ref[...]
    m_new = jnp.maximum(m_prev, jnp.max(st, axis=0, keepdims=True))
    alpha = jnp.exp2(m_prev - m_new)
    pt = jnp.exp2(st - m_new)
    l_ref[...] = alpha * l_ref[...] + jnp.sum(pt, axis=0, keepdims=True)
    acc_ref[...] = alpha * acc_ref[...] + jnp.dot(vt, pt.astype(BF16), preferred_element_type=F32)
    m_ref[...] = m_new


def _flash_sweep(qs, k_ref, vt_ref, m_ref, l_ref, acc_ref, *, s_len, n_ctx):
    _flash_init(m_ref, l_ref, acc_ref)
    bounds = [(c * TK, (c + 1) * TK) for c in range(s_len // TK)] + [(s_len, s_len + n_ctx)]
    s_next = _scores_t(k_ref[bounds[0][0]:bounds[0][1], :], qs)
    for idx, (lo, hi) in enumerate(bounds):
        s_cur = s_next
        if idx + 1 < len(bounds):
            s_next = _scores_t(k_ref[bounds[idx + 1][0]:bounds[idx + 1][1], :], qs)
        _flash_update(s_cur, vt_ref[:, lo:hi], m_ref, l_ref, acc_ref)


def _merge_pairs(ot, tq, n_qblocks, o_ref, row0=0):
    for j in range(n_qblocks):
        o_lo = ot[:HEAD_DIM, (2 * j) * tq:(2 * j + 1) * tq]
        o_hi = ot[HEAD_DIM:, (2 * j + 1) * tq:(2 * j + 2) * tq]
        blk = jnp.concatenate([o_lo, o_hi], axis=0)
        o_ref[row0:row0 + tq, j * LANES:(j + 1) * LANES] = blk.T.astype(o_ref.dtype)


def _diff_attn_kernel(q_ref, k_ref, vt_ref, lam_ref, g_ref, o_ref, m_ref, l_ref, acc_ref, *, s_len, n_ctx, lam_init):
    tq = q_ref.shape[0]
    qs = _stack_q(q_ref, 1)
    _flash_sweep(qs, k_ref, vt_ref, m_ref, l_ref, acc_ref, s_len=s_len, n_ctx=n_ctx)
    lv = lam_ref[...]
    lam = (jnp.exp(jnp.sum(lv[0:1] * lv[1:2], axis=-1, keepdims=True))
           - jnp.exp(jnp.sum(lv[2:3] * lv[3:4], axis=-1, keepdims=True)) + lam_init)
    ot = acc_ref[...] / l_ref[...]
    od = ot[:, :tq] - lam * ot[:, tq:]
    ms = jnp.mean(od * od, axis=0, keepdims=True)
    od = od * lax.rsqrt(ms + RMS_EPS) * g_ref[...] * (1.0 - lam_init)
    o_ref[...] = od.T.astype(o_ref.dtype)


def _query_key_windows(n, s_len, n_ctx, tq, for_ctx):
    if for_ctx:
        assert s_len % n_ctx == 0 and n_ctx % tq == 0
        return n_ctx, s_len // tq, n_ctx, s_len // n_ctx, 0
    assert s_len % tq == 0
    return s_len, 0, n, 0, s_len


def _diff_attn_call(p, vt, lam_vecs, subln_g, *, s_len, n_ctx, n_heads, q_blk0, k_blk0, v_blk0, lam_init, for_ctx):
    b, n, _ = p.shape
    tq = min(TQ_DIFF, n_ctx) if for_ctx else TQ_DIFF
    n_q, q_i0, n_k, k_i, s_keys = _query_key_windows(n, s_len, n_ctx, tq, for_ctx)
    return pl.pallas_call(
        functools.partial(_diff_attn_kernel, s_len=s_keys, n_ctx=n_ctx, lam_init=lam_init),
        grid=(b, n_heads, n_q // tq),
        in_specs=[
            pl.BlockSpec((None, tq, LANES), lambda bb, hh, i: (bb, q_i0 + i, q_blk0 + hh)),
            pl.BlockSpec((None, n_k, LANES), lambda bb, hh, i: (bb, k_i, k_blk0 + hh)),
            pl.BlockSpec((None, None, LANES, n_k), lambda bb, hh, i: (bb, v_blk0 + hh, 0, k_i)),
            pl.BlockSpec((4, HEAD_DIM), lambda bb, hh, i: (0, 0)),
            pl.BlockSpec((LANES, 1), lambda bb, hh, i: (0, 0)),
        ],
        out_specs=pl.BlockSpec((None, tq, LANES), lambda bb, hh, i: (bb, i, hh)),
        out_shape=jax.ShapeDtypeStruct((b, n_q, n_heads * LANES), BF16),
        scratch_shapes=[pltpu.VMEM((1, 2 * tq), F32), pltpu.VMEM((1, 2 * tq), F32), pltpu.VMEM((LANES, 2 * tq), F32)],
        compiler_params=_cparams(3),
        name="diff_attn",
    )(p, p, vt, lam_vecs, subln_g.reshape(LANES, 1))


def _gqa_attn_kernel(q_ref, k_ref, vt_ref, o_ref, m_ref, l_ref, acc_ref, *, s_len, n_ctx):
    tq = q_ref.shape[0]
    n_qblocks = q_ref.shape[1] // LANES
    qs = _stack_q(q_ref, n_qblocks)
    _flash_sweep(qs, k_ref, vt_ref, m_ref, l_ref, acc_ref, s_len=s_len, n_ctx=n_ctx)
    _merge_pairs(acc_ref[...] / l_ref[...], tq, n_qblocks, o_ref)


def _gqa_attn_call(p, vt, *, s_len, n_ctx, q_width, q_col0, k_blk, v_blk, for_ctx):
    b, n, _ = p.shape
    tq = TQ_GQA
    n_q, q_i0, n_k, k_i, s_keys = _query_key_windows(n, s_len, n_ctx, tq, for_ctx)
    m_rows = 2 * (q_width // LANES) * tq
    return pl.pallas_call(
        functools.partial(_gqa_attn_kernel, s_len=s_keys, n_ctx=n_ctx),
        grid=(b, n_q // tq),
        in_specs=[
            pl.BlockSpec((None, tq, q_width), lambda bb, i: (bb, q_i0 + i, q_col0 // q_width)),
            pl.BlockSpec((None, n_k, LANES), lambda bb, i: (bb, k_i, k_blk)),
            pl.BlockSpec((None, None, LANES, n_k), lambda bb, i: (bb, v_blk, 0, k_i)),
        ],
        out_specs=pl.BlockSpec((None, tq, q_width), lambda bb, i: (bb, i, 0)),
        out_shape=jax.ShapeDtypeStruct((b, n_q, q_width), BF16),
        scratch_shapes=[pltpu.VMEM((1, m_rows), F32), pltpu.VMEM((1, m_rows), F32), pltpu.VMEM((LANES, m_rows), F32)],
        compiler_params=_cparams(2),
        name="gqa_attn",
    )(p, p, vt)


BLOCKS_PER_STEP = 4


def _win_attn_kernel(q_ref, k_ref, vt_ref, sink_ref, o_ref, *, s_len, n_ctx):
    tq = q_ref.shape[0] // BLOCKS_PER_STEP
    n_qblocks = q_ref.shape[1] // LANES
    span = tq + 2 * WINDOW
    step = pl.program_id(1)
    k_ctx = k_ref[s_len:s_len + n_ctx, :]
    vt_ctx = vt_ref[:, s_len:s_len + n_ctx]
    sink = sink_ref[...] * LOG2E

    def scores(blk):
        i = BLOCKS_PER_STEP * step + blk
        ws = pl.multiple_of(jnp.clip(i * tq - WINDOW, 0, s_len - span), LANES)
        qs = _stack_q(q_ref, n_qblocks, blk * tq, tq)
        m_rows = qs.shape[0]
        s_loc = _scores_t(k_ref[pl.ds(ws, span), :], qs)
        kpos = ws + lax.broadcasted_iota(jnp.int32, (span, m_rows), 0)
        qpos = i * tq + (lax.broadcasted_iota(jnp.int32, (span, m_rows), 1) & (tq - 1))
        s_loc = jnp.where(jnp.abs(qpos - kpos) <= WINDOW, s_loc, NEG_INF)
        return ws, s_loc, _scores_t(k_ctx, qs)

    def finish(blk, ws, s_loc, s_ctx):
        m = jnp.maximum(jnp.maximum(jnp.max(s_loc, axis=0, keepdims=True), jnp.max(s_ctx, axis=0, keepdims=True)),
                        sink)
        p_loc = jnp.exp2(s_loc - m)
        p_ctx = jnp.exp2(s_ctx - m)
        denom = jnp.sum(p_loc, axis=0, keepdims=True) + jnp.sum(p_ctx, axis=0, keepdims=True) + jnp.exp2(sink - m)
        acc = (jnp.dot(vt_ref[:, pl.ds(ws, span)], p_loc.astype(BF16), preferred_element_type=F32)
               + jnp.dot(vt_ctx, p_ctx.astype(BF16), preferred_element_type=F32))
        _merge_pairs(acc / denom, tq, n_qblocks, o_ref, blk * tq)

    pending = [scores(blk) for blk in range(BLOCKS_PER_STEP)]
    for blk, args in enumerate(pending):
        finish(blk, *args)


def _win_attn_call(p, vt, sink_row, *, s_len, n_ctx, q_width, k_blk, v_blk):
    b, n, _ = p.shape
    tq = BLOCKS_PER_STEP * TQ_WIN
    m_rows = sink_row.shape[1]
    return pl.pallas_call(
        functools.partial(_win_attn_kernel, s_len=s_len, n_ctx=n_ctx),
        grid=(b, s_len // tq),
        in_specs=[
            pl.BlockSpec((None, tq, q_width), lambda bb, i: (bb, i, 0)),
            pl.BlockSpec((None, n, LANES), lambda bb, i: (bb, 0, k_blk)),
            pl.BlockSpec((None, None, LANES, n), lambda bb, i: (bb, v_blk, 0, 0)),
            pl.BlockSpec((1, m_rows), lambda bb, i: (0, 0)),
        ],
        out_specs=pl.BlockSpec((None, tq, q_width), lambda bb, i: (bb, i, 0)),
        out_shape=jax.ShapeDtypeStruct((b, s_len, q_width), BF16),
        compiler_params=_cparams(2),
        name="win_attn",
    )(p, p, vt, sink_row)


def _na_attn_kernel(q_ref, k_ref, vt_ref, *rest, s_len, n_ctx):
    bias_refs, o_ref = rest[:BLOCKS_PER_STEP], rest[BLOCKS_PER_STEP]
    tq = q_ref.shape[0] // BLOCKS_PER_STEP
    slab = NA_SLAB * GRID_W
    rows = s_len // GRID_W
    step = pl.program_id(2)
    k_cx = k_ref[s_len:s_len + n_ctx, :]
    vt_cx = vt_ref[:, s_len:s_len + n_ctx]

    def scores(blk):
        mi = BLOCKS_PER_STEP * step + blk
        ss = pl.multiple_of(jnp.clip(NA_QROWS * mi - NA_ROWS_MAX // 2, 0, rows - NA_SLAB) * GRID_W, LANES)
        qs = _stack_q(q_ref, 1, blk * tq, tq)
        s_nb = _scores_t(k_ref[pl.ds(ss, slab), :], qs) + bias_refs[blk][...]
        return ss, s_nb, _scores_t(k_cx, qs)

    def finish(blk, ss, s_nb, s_cx):
        m = jnp.maximum(jnp.max(s_nb, axis=0, keepdims=True), jnp.max(s_cx, axis=0, keepdims=True))
        p_nb = jnp.exp2(s_nb - m)
        p_cx = jnp.exp2(s_cx - m)
        denom = jnp.sum(p_nb, axis=0, keepdims=True) + jnp.sum(p_cx, axis=0, keepdims=True)
        acc = (jnp.dot(vt_ref[:, pl.ds(ss, slab)], p_nb.astype(BF16), preferred_element_type=F32)
               + jnp.dot(vt_cx, p_cx.astype(BF16), preferred_element_type=F32))
        _merge_pairs(acc / denom, tq, 1, o_ref, blk * tq)

    pending = [scores(blk) for blk in range(BLOCKS_PER_STEP)]
    for blk, args in enumerate(pending):
        finish(blk, *args)


def _na_attn_call(p, vt, bias_tab, *, s_len, n_ctx, n_pairs, q_blk0, k_blk0, v_blk0):
    b, n, _ = p.shape
    tq = NA_QROWS * GRID_W
    n_blocks = s_len // tq
    assert n_blocks % BLOCKS_PER_STEP == 0
    tbl = lambda mi: jnp.where(mi == 0, 0, jnp.where(mi == n_blocks - 1, 2, 1))
    bias_spec = lambda blk: pl.BlockSpec((None, None, NA_SLAB * GRID_W, 2 * tq),
                                         lambda bb, j, st: (tbl(BLOCKS_PER_STEP * st + blk), j, 0, 0))
    return pl.pallas_call(
        functools.partial(_na_attn_kernel, s_len=s_len, n_ctx=n_ctx),
        grid=(b, n_pairs, n_blocks // BLOCKS_PER_STEP),
        in_specs=[
            pl.BlockSpec((None, BLOCKS_PER_STEP * tq, LANES), lambda bb, j, st: (bb, st, q_blk0 + j)),
            pl.BlockSpec((None, n, LANES), lambda bb, j, st: (bb, 0, k_blk0 + j)),
            pl.BlockSpec((None, None, LANES, n), lambda bb, j, st: (bb, v_blk0 + j, 0, 0)),
        ] + [bias_spec(blk) for blk in range(BLOCKS_PER_STEP)],
        out_specs=pl.BlockSpec((None, BLOCKS_PER_STEP * tq, LANES), lambda bb, j, st: (bb, st, j)),
        out_shape=jax.ShapeDtypeStruct((b, s_len, n_pairs * LANES), BF16),
        compiler_params=_cparams(3),
        name="na_attn",
    )(p, p, vt, *([bias_tab] * BLOCKS_PER_STEP))


def _na_bias_tables(rpb, rows):
    n_heads = rpb.shape[0]
    rpb = rpb.astype(F32)
    pad = GRID_W - NA_COLS
    rpb_p = jnp.pad(rpb, ((0, 0), (0, 0), (pad, pad)))
    col_tab = jnp.stack([rpb_p[:, :, pad + NA_COLS - 1 - qc: pad + NA_COLS - 1 - qc + GRID_W] for qc in range(GRID_W)],
                        axis=2)
    qr = np.arange(NA_QROWS)[:, None, None, None]
    qc = np.arange(GRID_W)[None, :, None, None]
    kr = np.arange(NA_SLAB)[None, None, :, None]
    kc = np.arange(GRID_W)[None, None, None, :]
    full = (NA_QROWS, GRID_W, NA_SLAB, GRID_W)
    flat = (NA_QROWS * GRID_W, NA_SLAB * GRID_W)
    tabs = []
    for q0, s0 in ((0, 0), (NA_QROWS, NA_QROWS - NA_ROWS_MAX // 2), (rows - NA_QROWS, rows - NA_SLAB)):
        q_row = q0 + qr
        k_row = s0 + kr
        r0 = np.clip(q_row - NA_ROWS_MAX // 2, 0, rows - NA_ROWS_MAX)
        c0 = np.clip(qc - NA_COLS // 2, 0, GRID_W - NA_COLS)
        valid = (k_row >= r0) & (k_row < r0 + NA_ROWS_MAX) & (kc >= c0) & (kc < c0 + NA_COLS)
        valid = np.broadcast_to(valid, full).reshape(flat)
        r_off = np.clip(k_row - q_row + NA_ROWS_MAX - 1, 0, 2 * NA_ROWS_MAX - 2)[:, 0, :, 0]
        bias = jnp.stack([jnp.stack([col_tab[:, int(r_off[a, c])] for c in range(NA_SLAB)], axis=2)
                          for a in range(NA_QROWS)], axis=1)
        tabs.append(jnp.where(valid[None], bias.reshape((n_heads,) + flat) * LOG2E, NEG_INF))
    tab = jnp.stack(tabs)
    return jnp.swapaxes(tab.reshape(3, n_heads // 2, 2 * flat[0], flat[1]), -1, -2)


def _split_hi_lo(x):
    c = x * 65537.0
    hi = c - (c - x)
    return hi.astype(BF16), (x - hi).astype(BF16)


def _layer_norm(y, g, b):
    mu = jnp.mean(y, axis=-1, keepdims=True)
    yc = y - mu
    var = jnp.mean(yc * yc, axis=-1, keepdims=True)
    return yc * lax.rsqrt(var + LN_EPS) * g + b


def _route(logits):
    lane = _lane_iota().astype(F32)
    big = float(LANES)
    is_g = lane < N_GROUPS
    gl = jnp.where(is_g, logits, NEG_INF)
    g_max = jnp.max(gl, axis=-1, keepdims=True)
    g_idx = jnp.min(jnp.where(gl == g_max, lane, big), axis=-1, keepdims=True)
    g_w = 1.0 / jnp.sum(jnp.where(is_g, jnp.exp(gl - g_max), 0.0), axis=-1, keepdims=True)
    base = N_GROUPS + EXPERTS_PER_GROUP * g_idx
    el = jnp.where((lane >= base) & (lane < base + EXPERTS_PER_GROUP), logits, NEG_INF)
    v1 = jnp.max(el, axis=-1, keepdims=True)
    i1 = jnp.min(jnp.where(el == v1, lane, big), axis=-1, keepdims=True)
    el2 = jnp.where(lane == i1, NEG_INF, el)
    v2 = jnp.max(el2, axis=-1, keepdims=True)
    i2 = jnp.min(jnp.where(el2 == v2, lane, big), axis=-1, keepdims=True)
    t = jnp.exp(v2 - v1)
    w1 = g_w / (1.0 + t)
    w2 = g_w * t / (1.0 + t)
    return jnp.where(lane == 0, i1 - N_GROUPS,
                     jnp.where(lane == 1, i2 - N_GROUPS, jnp.where(lane == 2, w1, jnp.where(lane == 3, w2, 0.0))))


def _outproj_kernel(oal_ref, oac_ref, obl_ref, obc_ref, hl_ref, hc_ref, w_ref, gate_ref, lng_ref, lnb_ref,
                    sh_ref, sc_ref, wr_ref, br_ref, h1_ref, f_ref, route_ref, *, alpha, n_lat):
    ka = oal_ref.shape[1]
    oa = _read_split_stream(oal_ref, oac_ref, n_lat)
    ob = _read_split_stream(obl_ref, obc_ref, n_lat)
    h = _read_split_stream(hl_ref, hc_ref, n_lat)
    o = (jnp.dot(oa, w_ref[:ka, :], preferred_element_type=F32)
         + jnp.dot(ob, w_ref[ka:, :], preferred_element_type=F32))
    h1 = _layer_norm(alpha * h + gate_ref[...] * o, lng_ref[...], lnb_ref[...])
    h1_ref[...] = h1
    f = h1 * sc_ref[...] + sh_ref[...]
    f_ref[:, 0, :] = f
    f_hi, f_lo = _split_hi_lo(f)
    hi_prod = jnp.dot(f_hi, wr_ref[...], preferred_element_type=F32)
    lo_prod = jnp.dot(f_lo, wr_ref[...], preferred_element_type=F32)
    logits = (hi_prod[:, :LANES] + hi_prod[:, LANES:]) + (lo_prod[:, :LANES] + lo_prod[:, LANES:]) + br_ref[...]
    route_ref[...] = _route(logits)


def _outproj_call(oa, ob, h, w16, gate, ln_g, ln_b, shift, scale1p, w_route, b_route, *, n_rows, s_len, alpha):
    b, _, d = h[0].shape
    ka, kb = oa[0].shape[2], ob[0].shape[2]
    n_lat = s_len // TM
    seg = lambda bb, j: (bb, jnp.where(j >= n_lat, 1, 0), 0, 0)
    row = lambda bb, j: (bb, j, 0)
    const = lambda bb, j: (0, 0)
    return pl.pallas_call(
        functools.partial(_outproj_kernel, alpha=alpha, n_lat=n_lat),
        grid=(b, n_rows // TM),
        in_specs=_split_stream_specs(*oa, n_lat) + _split_stream_specs(*ob, n_lat) + _split_stream_specs(*h, n_lat) + [
            pl.BlockSpec((ka + kb, d), const),
            pl.BlockSpec((None, None, 1, d), seg),
            pl.BlockSpec((1, d), const),
            pl.BlockSpec((1, d), const),
            pl.BlockSpec((None, None, 1, d), seg),
            pl.BlockSpec((None, None, 1, d), seg),
            pl.BlockSpec((d, 2 * LANES), const),
            pl.BlockSpec((1, LANES), const),
        ],
        out_specs=[
            pl.BlockSpec((None, TM, d), row),
            pl.BlockSpec((None, TM, 1, d), lambda bb, j: (bb, j, 0, 0)),
            pl.BlockSpec((None, TM, LANES), row),
        ],
        out_shape=[
            jax.ShapeDtypeStruct((b, n_rows, d), F32),
            jax.ShapeDtypeStruct((b, n_rows, 1, d), F32),
            jax.ShapeDtypeStruct((b, n_rows, LANES), F32),
        ],
        compiler_params=_cparams(2),
        name="outproj",
    )(oa[0], oa[1], ob[0], ob[1], h[0], h[1], w16, gate, ln_g, ln_b, shift, scale1p, w_route, b_route)


def _row_copy(src, src_row, dst, dst_row, sem):
    return pltpu.make_async_copy(src.at[pl.ds(src_row, 1), :], dst.at[pl.ds(dst_row, 1), :], sem)


def _block_copy_for_wait(src, dst, sem):
    return pltpu.make_async_copy(src, dst, sem)


MOE_BUFS = 3
DMA_PRIORITIES = 2
MOE_VMEM_LIMIT = 57 * 1024 * 1024


def _moe_kernel(blk_e_ref, blk_b_ref, blk_used_ref, tok_ref, dst_prev_ref, dst_last_ref, f_ref,
                w1_ref, w3_ref, w2_ref, y_hbm, xs, zbuf, ybuf, w1b, w3b, w2b, ssem):
    del blk_b_ref
    i = pl.program_id(0)
    last = pl.num_programs(0) - 1
    cur = lax.rem(i, MOE_BUFS)
    prev = lax.rem(i + 2, MOE_BUFS)
    used = blk_used_ref[i] > 0
    prev_used = jnp.logical_or(i == 0, blk_used_ref[jnp.maximum(i - 1, 0)] > 0)

    def wait_scatter(s):
        _block_copy_for_wait(ybuf.at[s], y_hbm.at[pl.ds(0, MOE_BLOCK), :], ssem.at[s]).wait()

    def start_scatter(dst_ref, s):
        for r in range(MOE_BLOCK):
            _row_copy(ybuf.at[s], r, y_hbm, dst_ref[0, r], ssem.at[s]).start(priority=r % DMA_PRIORITIES)

    def start_zero_fill(dst_ref, s):
        row0 = pl.multiple_of(dst_ref[0, 0], MOE_BLOCK)
        pltpu.make_async_copy(zbuf, y_hbm.at[pl.ds(row0, MOE_BLOCK), :], ssem.at[s]).start()

    @pl.when(i == 0)
    def _():
        ybuf[...] = jnp.zeros(ybuf.shape, ybuf.dtype)
        zbuf[...] = jnp.zeros(zbuf.shape, zbuf.dtype)

    @pl.when(i >= 2)
    def _():
        wait_scatter(cur)

    @pl.when(jnp.logical_or(i == 0, blk_e_ref[i] != blk_e_ref[jnp.maximum(i - 1, 0)]))
    def _():
        w1b[...] = w1_ref[...].astype(BF16)
        w3b[...] = w3_ref[...].astype(BF16)
        w2b[...] = w2_ref[...].astype(BF16)

    @pl.when(used)
    def _():
        for r in range(MOE_BLOCK):
            xs[pl.ds(r, 1), :] = f_ref[tok_ref[0, r]]
        x = xs[...].astype(BF16)
        start_scatter(dst_prev_ref, prev)
        h1 = jnp.dot(x, w1b[...], preferred_element_type=F32)
        h3 = jnp.dot(x, w3b[...], preferred_element_type=F32)
        act = (h1 / (1.0 + jnp.exp(-h1))) * h3
        ybuf[cur] = jnp.dot(act.astype(BF16), w2b[...], preferred_element_type=F32)

    @pl.when(jnp.logical_and(jnp.logical_not(used), prev_used))
    def _():
        start_scatter(dst_prev_ref, prev)

    @pl.when(jnp.logical_and(jnp.logical_not(used), jnp.logical_not(prev_used)))
    def _():
        start_zero_fill(dst_prev_ref, prev)

    @pl.when(i == last)
    def _():
        @pl.when(used)
        def _():
            start_scatter(dst_last_ref, cur)

        @pl.when(jnp.logical_not(used))
        def _():
            start_zero_fill(dst_last_ref, cur)

        for s in range(MOE_BUFS):
            wait_scatter(s)


def _moe_call(f_rows, blk_e, blk_b, blk_used, slot_tok, slot_dst, w1, w3, w2, layer):
    n_batch, per_batch, _, d = f_rows.shape
    n_blocks = blk_e.shape[0]
    n_slots = n_blocks * MOE_BLOCK
    de = w1.shape[3]
    assert n_blocks >= MOE_BUFS
    smem_block = lambda index_map: pl.BlockSpec((None, 1, MOE_BLOCK), index_map, memory_space=pltpu.SMEM)
    grid_spec = pltpu.PrefetchScalarGridSpec(
        num_scalar_prefetch=3,
        grid=(n_blocks,),
        in_specs=[
            smem_block(lambda i, be, bb, bu: (i, 0, 0)),
            smem_block(lambda i, be, bb, bu: (i, 0, 0)),
            smem_block(lambda i, be, bb, bu: (i + 1, 0, 0)),
            pl.BlockSpec((None, per_batch, 1, d), lambda i, be, bb, bu: (bb[i], 0, 0, 0), pipeline_mode=pl.Buffered(1)),
            pl.BlockSpec((None, None, d, de), lambda i, be, bb, bu: (layer, be[i], 0, 0)),
            pl.BlockSpec((None, None, d, de), lambda i, be, bb, bu: (layer, be[i], 0, 0)),
            pl.BlockSpec((None, None, de, d), lambda i, be, bb, bu: (layer, be[i], 0, 0)),
        ],
        out_specs=pl.BlockSpec(memory_space=pl.ANY),
        scratch_shapes=[
            pltpu.VMEM((MOE_BLOCK, d), F32),
            pltpu.VMEM((MOE_BLOCK, d), F32),
            pltpu.VMEM((MOE_BUFS, MOE_BLOCK, d), F32),
            pltpu.VMEM((d, de), BF16),
            pltpu.VMEM((d, de), BF16),
            pltpu.VMEM((de, d), BF16),
            pltpu.SemaphoreType.DMA((MOE_BUFS,)),
        ],
    )
    tok3 = slot_tok.reshape(n_blocks, 1, MOE_BLOCK)
    n_rows = n_slots + MOE_BLOCK
    first = (n_slots + jnp.arange(MOE_BLOCK, dtype=jnp.int32))[None, :]
    dst3 = jnp.concatenate([first, slot_dst], axis=0).reshape(n_blocks + 1, 1, MOE_BLOCK)
    return pl.pallas_call(
        _moe_kernel,
        grid_spec=grid_spec,
        out_shape=jax.ShapeDtypeStruct((n_rows, d), F32),
        compiler_params=pltpu.CompilerParams(dimension_semantics=("arbitrary",), vmem_limit_bytes=MOE_VMEM_LIMIT),
        name="moe_experts",
    )(blk_e, blk_b, blk_used, tok3, dst3, dst3, f_rows, w1, w3, w2)


def _moe_plan(route, t, n_batch):
    n_assign = TOP_K * t
    per_batch = t // n_batch
    n_seg = n_batch * N_EXPERTS
    segs = jnp.arange(n_seg, dtype=jnp.int32)
    e_flat = route[:, :TOP_K].astype(jnp.int32).reshape(n_assign)
    tok_batch = jnp.arange(n_assign, dtype=jnp.int32) // (TOP_K * per_batch)
    seg_flat = tok_batch * N_EXPERTS + e_flat
    order = jnp.argsort(seg_flat, stable=True).astype(jnp.int32)
    counts = jnp.sum((seg_flat[:, None] == segs[None, :]).astype(jnp.int32), axis=0)
    starts = jnp.cumsum(counts) - counts
    padded = (counts + MOE_BLOCK - 1) // MOE_BLOCK * MOE_BLOCK
    p_ends = jnp.cumsum(padded)
    p_starts = p_ends - padded
    n_blocks = (n_assign + MOE_BLOCK - 1) // MOE_BLOCK + n_seg
    blk_start = jnp.arange(n_blocks, dtype=jnp.int32) * MOE_BLOCK
    blk_seg = jnp.minimum(jnp.sum((p_ends[None, :] <= blk_start[:, None]).astype(jnp.int32), axis=1), n_seg - 1)
    blk_b = blk_seg // N_EXPERTS
    of_block = lambda v: jnp.sum(jnp.where(blk_seg[:, None] == segs[None, :], v[None, :], 0), axis=1)[:, None]
    slot = blk_start[:, None] + jnp.arange(MOE_BLOCK, dtype=jnp.int32)[None, :]
    j = slot - of_block(p_starts)
    valid = j < of_block(counts)
    a_idx = order[jnp.clip(of_block(starts) + j, 0, n_assign - 1)]
    slot_tok = jnp.where(valid, a_idx // TOP_K - blk_b[:, None] * per_batch, 0).astype(jnp.int32)
    spare = n_assign + slot - of_block(starts + counts)
    slot_dst = jnp.where(valid, (a_idx % TOP_K) * t + a_idx // TOP_K, spare).astype(jnp.int32)
    blk_used = (blk_start < p_ends[n_seg - 1]).astype(jnp.int32)
    return blk_seg % N_EXPERTS, blk_b, blk_used, slot_tok, slot_dst


def _combine_kernel(h_ref, y0_ref, y1_ref, route_ref, gate_ref, lng_ref, lnb_ref, o_ref, *, alpha):
    r = route_ref[...]
    y = r[:, 2:3] * y0_ref[...] + r[:, 3:4] * y1_ref[...]
    o_ref[...] = _layer_norm(alpha * h_ref[...] + gate_ref[...] * y, lng_ref[...], lnb_ref[...])


def _combine_call(h1, y, route, gate, ln_g, ln_b, *, s_len, alpha):
    b, n_rows, d = h1.shape
    n_lat = s_len // TM
    seg = lambda bb, j: (bb, jnp.where(j >= n_lat, 1, 0), 0, 0)
    row = lambda bb, j: (bb, j, 0)
    const = lambda bb, j: (0, 0)
    blocks_per_batch = n_rows // TM
    blocks_per_k = b * blocks_per_batch
    return pl.pallas_call(
        functools.partial(_combine_kernel, alpha=alpha),
        grid=(b, n_rows // TM),
        in_specs=[
            pl.BlockSpec((None, TM, d), row),
            pl.BlockSpec((TM, d), lambda bb, j: (bb * blocks_per_batch + j, 0)),
            pl.BlockSpec((TM, d), lambda bb, j: (blocks_per_k + bb * blocks_per_batch + j, 0)),
            pl.BlockSpec((None, TM, LANES), row),
            pl.BlockSpec((None, None, 1, d), seg),
            pl.BlockSpec((1, d), const),
            pl.BlockSpec((1, d), const),
        ],
        out_specs=pl.BlockSpec((None, TM, d), row),
        out_shape=jax.ShapeDtypeStruct((b, n_rows, d), F32),
        compiler_params=_cparams(2),
        name="combine",
    )(h1, y, y, route, gate, ln_g, ln_b)


def _pair_heads(w, axis):
    w = jnp.moveaxis(w, axis, -1)
    lead = w.shape[:-1]
    n_heads = w.shape[-1] // HEAD_DIM
    w = w.reshape(lead + (2, n_heads // 2, HEAD_DIM)).swapaxes(-3, -2).reshape(lead + (n_heads * HEAD_DIM,))
    return jnp.moveaxis(w, -1, axis)


def _rope_tables(s_len, n_ctx):
    t = jnp.arange(s_len, dtype=jnp.int32)
    row = (t // GRID_W).astype(F32)
    col = (t % GRID_W).astype(F32)
    n_freq = HEAD_DIM // 4
    inv = ROPE_THETA ** (-jnp.arange(n_freq, dtype=F32) / n_freq)
    ang = jnp.concatenate([row[:, None] * inv, col[:, None] * inv], -1)
    cos = jnp.concatenate([jnp.cos(ang), jnp.ones((n_ctx, HEAD_DIM // 2), F32)], 0)
    sin = jnp.concatenate([jnp.sin(ang), jnp.zeros((n_ctx, HEAD_DIM // 2), F32)], 0)
    return jnp.tile(cos, (1, 4)), jnp.concatenate([-sin, sin, -sin, sin], -1)


def _lambda_init(layer_idx):
    return 0.8 - 0.6 * float(np.exp(-0.3 * layer_idx))


def _router_params(w_group, b_group, w_router, b_router):
    d = w_group.shape[0]
    pad = LANES - N_GROUPS - N_EXPERTS
    w = jnp.concatenate([w_group, w_router, jnp.zeros((d, pad), F32)], axis=1)
    bb = jnp.concatenate([b_group, b_router, jnp.zeros((pad,), F32)])
    w_hi, w_lo = _split_hi_lo(w)
    return jnp.concatenate([w_hi, w_lo], axis=1), bb.reshape(1, LANES)


def _moe_layer(f, route, w1, w3, w2, layer):
    b, n_rows = f.shape[:2]
    t = b * n_rows
    blk_e, blk_b, blk_used, slot_tok, slot_dst = _moe_plan(route.reshape(t, LANES), t, b)
    return _moe_call(f, blk_e, blk_b, blk_used, slot_tok, slot_dst, w1, w3, w2, layer)


def kernel(x, c, ctx, c_ctx, mod_w, mod_b, ln_g, ln_b, ab_w_in, ab_w_out, diff_lambda, diff_subln_g, gqa_qk_g,
           cd_w_in, cd_w_out, win_sink, na_rpb, moe_w_group, moe_b_group, moe_w_router, moe_b_router,
           moe_w1, moe_w3, moe_w2):
    b, s, d = x.shape
    n_ctx = ctx.shape[1]
    n = s + n_ctx
    depth = mod_w.shape[0]
    rows = s // GRID_W
    assert depth == 2, "layer pattern implemented: one differential/GQA layer, then one window/neighbourhood layer"
    assert s % TM == 0 and n_ctx % TM == 0 and s % TK == 0 and b + 1 <= 8
    assert s % TQ_DIFF == 0 and s % TQ_GQA == 0 and n_ctx % TQ_GQA == 0
    assert rows >= NA_SLAB and rows % NA_QROWS == 0 and s % GRID_W == 0
    alpha = (2.0 * depth) ** 0.25
    qw = d // 2
    kvw = qw // 4
    assert qw == 4 * LANES and kvw == LANES

    cos128, sin128 = _rope_tables(s, n_ctx)
    cvec = jnp.concatenate([c, c_ctx[None, :], jnp.zeros((8 - b - 1, d), F32)], axis=0)
    mods_all = _mods_call(cvec, mod_w, mod_b)

    def seg_mods(i):
        m = mods_all[i]
        lat = m[:b].reshape(b, 6, d)
        cm = jnp.broadcast_to(m[b].reshape(1, 6, d), (b, 6, d))
        ms = jnp.stack([lat, cm], axis=1)
        pick = lambda k, one: (ms[:, :, k] + one)[:, :, None, :]
        return pick(0, 0.0), pick(1, 1.0), pick(2, 0.0), pick(3, 0.0), pick(4, 1.0), pick(5, 0.0)

    n_lat = s // TM
    stream0 = (x, ctx, 0)

    shift1, scale1p, gate1, shift2, scale2p, gate2 = seg_mods(0)
    w_in = ab_w_in[0]
    q_d, q_g, k_d, v_d = (w_in[:, k * qw:(k + 1) * qw] for k in range(4))
    k_g, v_g = w_in[:, 4 * qw:4 * qw + kvw], w_in[:, 4 * qw + kvw:]
    w_qk = jnp.concatenate([q_d, _pair_heads(q_g, 1), k_d, k_g], axis=1).astype(BF16)
    roles0 = ((True, None, True),) * 4 + ((True, 0, True),) * 4 + ((True, None, False),) * 4 + ((True, 1, False),)
    w_vt = jnp.concatenate([v_d, v_g], axis=1).T.astype(BF16)
    g128 = jnp.tile(gqa_qk_g[0], (1, 2))
    p0, vt0 = _inproj_call(*stream0, n, shift1, scale1p, w_qk, w_vt, cos128, sin128, g128, roles0, s)
    diff_attn = functools.partial(_diff_attn_call, p0, vt0, diff_lambda[0], diff_subln_g[0], s_len=s, n_ctx=n_ctx,
                                  n_heads=4, q_blk0=0, k_blk0=8, v_blk0=0, lam_init=_lambda_init(0))
    gqa_attn = functools.partial(_gqa_attn_call, p0, vt0, s_len=s, n_ctx=n_ctx, q_width=qw, q_col0=qw,
                                 k_blk=12, v_blk=4)
    o_d = (diff_attn(for_ctx=False), diff_attn(for_ctx=True), 0)
    o_g = (gqa_attn(for_ctx=False), gqa_attn(for_ctx=True), 0)
    w_out = jnp.concatenate([ab_w_out[0][:qw], _pair_heads(ab_w_out[0][qw:], 0)], axis=0).astype(BF16)
    w_route, b_route = _router_params(moe_w_group[0], moe_b_group[0], moe_w_router[0], moe_b_router[0])
    h1, f, route = _outproj_call(o_d, o_g, stream0, w_out, gate1, ln_g[0, 0:1], ln_b[0, 0:1], shift2, scale2p,
                                 w_route, b_route, n_rows=n, s_len=s, alpha=alpha)
    y = _moe_layer(f, route, moe_w1, moe_w3, moe_w2, 0)
    h = _combine_call(h1, y, route, gate2, ln_g[0, 1:2], ln_b[0, 1:2], s_len=s, alpha=alpha)

    shift1, scale1p, gate1, shift2, scale2p, gate2 = seg_mods(1)
    w_in = cd_w_in[0]
    q_w, q_n = w_in[:, :qw], w_in[:, qw:2 * qw]
    k_w, v_w = w_in[:, 2 * qw:2 * qw + kvw], w_in[:, 2 * qw + kvw:2 * qw + 2 * kvw]
    k_n, v_n = w_in[:, 2 * qw + 2 * kvw:3 * qw + 2 * kvw], w_in[:, 3 * qw + 2 * kvw:]
    w_qk = jnp.concatenate([_pair_heads(q_w, 1), q_n, k_w, k_n], axis=1).astype(BF16)
    roles1 = ((True, None, True),) * 4 + ((False, None, True),) * 4 + ((True, None, False),) \
        + ((False, None, False),) * 4
    w_vt = jnp.concatenate([v_w, v_n], axis=1).T.astype(BF16)
    stream1 = (h, h, n_lat)
    p1, vt1 = _inproj_call(*stream1, n, shift1, scale1p, w_qk, w_vt, cos128, sin128, jnp.ones((2, LANES), F32),
                           roles1, s)
    sink_row = jnp.repeat(win_sink[0].reshape(2, 4).T.reshape(8), TQ_WIN).reshape(1, 8 * TQ_WIN)
    o_w = _win_attn_call(p1, vt1, sink_row, s_len=s, n_ctx=n_ctx, q_width=qw, k_blk=8, v_blk=0)
    o_n = _na_attn_call(p1, vt1, _na_bias_tables(na_rpb[0], rows), s_len=s, n_ctx=n_ctx, n_pairs=4,
                        q_blk0=4, k_blk0=9, v_blk0=1)
    w_out = jnp.concatenate([_pair_heads(cd_w_out[0][:qw], 0), cd_w_out[0][qw:]], axis=0).astype(BF16)
    w_route, b_route = _router_params(moe_w_group[1], moe_b_group[1], moe_w_router[1], moe_b_router[1])
    h1, f, route = _outproj_call((o_w, o_w, 0), (o_n, o_n, 0), stream1, w_out, gate1, ln_g[1, 0:1], ln_b[1, 0:1],
                                 shift2, scale2p, w_route, b_route, n_rows=s, s_len=s, alpha=alpha)
    y = _moe_layer(f, route, moe_w1, moe_w3, moe_w2, 1)
    return _combine_call(h1, y, route, gate2, ln_g[1, 1:2], ln_b[1, 1:2], s_len=s, alpha=alpha)
```

```python
import functools

import numpy as np
import jax
import jax.numpy as jnp
from jax import lax
from jax.experimental import pallas as pl
from jax.experimental.pallas import tpu as pltpu

F32 = jnp.float32
BF16 = jnp.bfloat16
HIGHEST = lax.Precision.HIGHEST

GRID_W = 64
HEAD_DIM = 64
ROPE_THETA = 10000.0
WINDOW = 128
NA_ROWS_MAX = 8
NA_COLS = 16
N_GROUPS = 4
EXPERTS_PER_GROUP = 8
N_EXPERTS = N_GROUPS * EXPERTS_PER_GROUP
TOP_K = 2
MOE_BLOCK = 128
LN_EPS = 1e-5
RMS_EPS = 1e-6
ATTN_SCALE = HEAD_DIM ** -0.5
NEG_INF = -1e30
LOG2E = 1.4426950408889634
Q_SCALE = ATTN_SCALE * LOG2E

LANES = 128
PAIR = 2 * HEAD_DIM
assert PAIR == LANES

VMEM_LIMIT = 48 * 1024 * 1024

TM = 256
TQ_DIFF = 512
TQ_GQA = 128
TK = 1024
TQ_WIN = 128
NA_QROWS = 4
NA_SLAB = NA_QROWS + NA_ROWS_MAX


def _cparams(n_axes):
    return pltpu.CompilerParams(dimension_semantics=("arbitrary",) * n_axes, vmem_limit_bytes=VMEM_LIMIT)


def _lane_iota():
    return lax.broadcasted_iota(jnp.int32, (1, LANES), 1)


def _mods_kernel(c_ref, w_ref, b_ref, o_ref):
    c = c_ref[...]
    sc = c / (1.0 + jnp.exp(-c))
    o_ref[...] = jnp.dot(sc, w_ref[...], precision=HIGHEST, preferred_element_type=F32) + b_ref[...]


def _mods_call(cvec, mod_w, mod_b):
    depth, d, n6 = mod_w.shape
    tn = n6 // 4
    return pl.pallas_call(
        _mods_kernel,
        grid=(depth, n6 // tn),
        in_specs=[
            pl.BlockSpec((8, d), lambda l, j: (0, 0)),
            pl.BlockSpec((None, d, tn), lambda l, j: (l, 0, j)),
            pl.BlockSpec((None, 1, tn), lambda l, j: (l, 0, j)),
        ],
        out_specs=pl.BlockSpec((None, 8, tn), lambda l, j: (l, 0, j)),
        out_shape=jax.ShapeDtypeStruct((depth, 8, n6), F32),
        compiler_params=_cparams(2),
        name="mods",
    )(cvec, mod_w, mod_b.reshape(depth, 1, n6))


def _split_stream_specs(lat, ctx, ctx_block0, n_lat):
    width = lat.shape[2]
    return [
        pl.BlockSpec((None, TM, width), lambda bb, j: (bb, jnp.minimum(j, n_lat - 1), 0)),
        pl.BlockSpec((None, TM, width), lambda bb, j: (bb, ctx_block0 + jnp.maximum(j - n_lat, 0), 0)),
    ]


def _read_split_stream(lat_ref, ctx_ref, n_lat):
    return jnp.where(pl.program_id(1) >= n_lat, ctx_ref[...], lat_ref[...])


def _inproj_kernel(hl_ref, hc_ref, sh_ref, sc_ref, w_ref, wvt_ref, cos_ref, sin_ref, g_ref, o_ref, vt_ref, *,
                   roles, n_lat):
    a = (_read_split_stream(hl_ref, hc_ref, n_lat) * sc_ref[...] + sh_ref[...]).astype(BF16)
    vt = lax.dot_general(wvt_ref[...], a, (((1,), (1,)), ((), ())), preferred_element_type=F32)
    for j in range(vt_ref.shape[0]):
        vt_ref[j] = vt[j * LANES:(j + 1) * LANES, :].astype(BF16)
    lane = _lane_iota()
    rope_first = (lane & (HEAD_DIM - 1)) < (HEAD_DIM // 2)
    lo = lane < HEAD_DIM
    n_blocks = len(roles)
    for c0 in range(0, n_blocks, 2):
        width = min(2, n_blocks - c0)
        p = jnp.dot(a, w_ref[:, c0 * LANES:(c0 + width) * LANES], preferred_element_type=F32)
        for half in range(width):
            cb = c0 + half
            rope, norm, scale = roles[cb]
            blk = p[:, half * LANES:(half + 1) * LANES]
            if norm is not None:
                sq = blk * blk
                s_lo = jnp.sum(jnp.where(lo, sq, 0.0), axis=-1, keepdims=True)
                s_hi = jnp.sum(jnp.where(lo, 0.0, sq), axis=-1, keepdims=True)
                ms = jnp.where(lo, s_lo, s_hi) * (1.0 / HEAD_DIM)
                blk = blk * lax.rsqrt(ms + RMS_EPS) * g_ref[norm:norm + 1, :]
            if rope:
                partner = jnp.where(rope_first, pltpu.roll(blk, LANES - HEAD_DIM // 2, 1),
                                    pltpu.roll(blk, HEAD_DIM // 2, 1))
                blk = blk * cos_ref[...] + partner * sin_ref[...]
            if scale:
                blk = blk * Q_SCALE
            o_ref[:, cb * LANES:(cb + 1) * LANES] = blk.astype(BF16)


def _inproj_call(h_lat, h_ctx, ctx_block0, n, shift, scale1p, w16, wvt16, cos128, sin128, g128, roles, s_len):
    b, _, d = h_lat.shape
    ncol = w16.shape[1]
    nvb = wvt16.shape[0] // LANES
    n_lat = s_len // TM
    seg = lambda bb, j: (bb, jnp.where(j >= n_lat, 1, 0), 0, 0)
    return pl.pallas_call(
        functools.partial(_inproj_kernel, roles=roles, n_lat=n_lat),
        grid=(b, n // TM),
        in_specs=_split_stream_specs(h_lat, h_ctx, ctx_block0, n_lat) + [
            pl.BlockSpec((None, None, 1, d), seg),
            pl.BlockSpec((None, None, 1, d), seg),
            pl.BlockSpec((d, ncol), lambda bb, j: (0, 0)),
            pl.BlockSpec((nvb * LANES, d), lambda bb, j: (0, 0)),
            pl.BlockSpec((TM, LANES), lambda bb, j: (j, 0)),
            pl.BlockSpec((TM, LANES), lambda bb, j: (j, 0)),
            pl.BlockSpec((2, LANES), lambda bb, j: (0, 0)),
        ],
        out_specs=[
            pl.BlockSpec((None, TM, ncol), lambda bb, j: (bb, j, 0)),
            pl.BlockSpec((None, nvb, LANES, TM), lambda bb, j: (bb, 0, 0, j)),
        ],
        out_shape=[
            jax.ShapeDtypeStruct((b, n, ncol), BF16),
            jax.ShapeDtypeStruct((b, nvb, LANES, n), BF16),
        ],
        compiler_params=_cparams(2),
        name="inproj",
    )(h_lat, h_ctx, shift, scale1p, w16, wvt16, cos128, sin128, g128)


def _stack_q(q_ref, n_qblocks, row0=0, n_rows=None):
    n_rows = q_ref.shape[0] if n_rows is None else n_rows
    lo = _lane_iota() < HEAD_DIM
    parts = []
    for j in range(n_qblocks):
        q = q_ref[row0:row0 + n_rows, j * LANES:(j + 1) * LANES]
        zero = jnp.zeros_like(q)
        parts.append(jnp.where(lo, q, zero))
        parts.append(jnp.where(lo, zero, q))
    return jnp.concatenate(parts, axis=0)


def _scores_t(k, qs):
    return lax.dot_general(k, qs, (((1,), (1,)), ((), ())), preferred_element_type=F32)


def _flash_init(m_ref, l_ref, acc_ref):
    m_ref[...] = jnp.full(m_ref.shape, NEG_INF, F32)
    l_ref[...] = jnp.zeros(l_ref.shape, F32)
    acc_ref[...] = jnp.zeros(acc_ref.shape, F32)


def _flash_update(st, vt, m_ref, l_ref, acc_ref):
    m_prev = m_ref[...]
    m_new = jnp.maximum(m_prev, jnp.max(st, axis=0, keepdims=True))
    alpha = jnp.exp2(m_prev - m_new)
    pt = jnp.exp2(st - m_new)
    l_ref[...] = alpha * l_ref[...] + jnp.sum(pt, axis=0, keepdims=True)
    acc_ref[...] = alpha * acc_ref[...] + jnp.dot(vt, pt.astype(BF16), preferred_element_type=F32)
    m_ref[...] = m_new


def _flash_sweep(qs, k_ref, vt_ref, m_ref, l_ref, acc_ref, *, s_len, n_ctx):
    _flash_init(m_ref, l_ref, acc_ref)
    bounds = [(c * TK, (c + 1) * TK) for c in range(s_len // TK)] + [(s_len, s_len + n_ctx)]
    s_next = _scores_t(k_ref[bounds[0][0]:bounds[0][1], :], qs)
    for idx, (lo, hi) in enumerate(bounds):
        s_cur = s_next
        if idx + 1 < len(bounds):
            s_next = _scores_t(k_ref[bounds[idx + 1][0]:bounds[idx + 1][1], :], qs)
        _flash_update(s_cur, vt_ref[:, lo:hi], m_ref, l_ref, acc_ref)


def _merge_pairs(ot, tq, n_qblocks, o_ref, row0=0):
    for j in range(n_qblocks):
        o_lo = ot[:HEAD_DIM, (2 * j) * tq:(2 * j + 1) * tq]
        o_hi = ot[HEAD_DIM:, (2 * j + 1) * tq:(2 * j + 2) * tq]
        blk = jnp.concatenate([o_lo, o_hi], axis=0)
        o_ref[row0:row0 + tq, j * LANES:(j + 1) * LANES] = blk.T.astype(o_ref.dtype)


def _diff_attn_kernel(q_ref, k_ref, vt_ref, lam_ref, g_ref, o_ref, m_ref, l_ref, acc_ref, *, s_len, n_ctx, lam_init):
    tq = q_ref.shape[0]
    qs = _stack_q(q_ref, 1)
    _flash_sweep(qs, k_ref, vt_ref, m_ref, l_ref, acc_ref, s_len=s_len, n_ctx=n_ctx)
    lv = lam_ref[...]
    lam = (jnp.exp(jnp.sum(lv[0:1] * lv[1:2], axis=-1, keepdims=True))
           - jnp.exp(jnp.sum(lv[2:3] * lv[3:4], axis=-1, keepdims=True)) + lam_init)
    ot = acc_ref[...] / l_ref[...]
    od = ot[:, :tq] - lam * ot[:, tq:]
    ms = jnp.mean(od * od, axis=0, keepdims=True)
    od = od * lax.rsqrt(ms + RMS_EPS) * g_ref[...] * (1.0 - lam_init)
    o_ref[...] = od.T.astype(o_ref.dtype)


def _query_key_windows(n, s_len, n_ctx, tq, for_ctx):
    if for_ctx:
        assert s_len % n_ctx == 0 and n_ctx % tq == 0
        return n_ctx, s_len // tq, n_ctx, s_len // n_ctx, 0
    assert s_len % tq == 0
    return s_len, 0, n, 0, s_len


def _diff_attn_call(p, vt, lam_vecs, subln_g, *, s_len, n_ctx, n_heads, q_blk0, k_blk0, v_blk0, lam_init, for_ctx):
    b, n, _ = p.shape
    tq = min(TQ_DIFF, n_ctx) if for_ctx else TQ_DIFF
    n_q, q_i0, n_k, k_i, s_keys = _query_key_windows(n, s_len, n_ctx, tq, for_ctx)
    return pl.pallas_call(
        functools.partial(_diff_attn_kernel, s_len=s_keys, n_ctx=n_ctx, lam_init=lam_init),
        grid=(b, n_heads, n_q // tq),
        in_specs=[
            pl.BlockSpec((None, tq, LANES), lambda bb, hh, i: (bb, q_i0 + i, q_blk0 + hh)),
            pl.BlockSpec((None, n_k, LANES), lambda bb, hh, i: (bb, k_i, k_blk0 + hh)),
            pl.BlockSpec((None, None, LANES, n_k), lambda bb, hh, i: (bb, v_blk0 + hh, 0, k_i)),
            pl.BlockSpec((4, HEAD_DIM), lambda bb, hh, i: (0, 0)),
            pl.BlockSpec((LANES, 1), lambda bb, hh, i: (0, 0)),
        ],
        out_specs=pl.BlockSpec((None, tq, LANES), lambda bb, hh, i: (bb, i, hh)),
        out_shape=jax.ShapeDtypeStruct((b, n_q, n_heads * LANES), BF16),
        scratch_shapes=[pltpu.VMEM((1, 2 * tq), F32), pltpu.VMEM((1, 2 * tq), F32), pltpu.VMEM((LANES, 2 * tq), F32)],
        compiler_params=_cparams(3),
        name="diff_attn",
    )(p, p, vt, lam_vecs, subln_g.reshape(LANES, 1))


def _gqa_attn_kernel(q_ref, k_ref, vt_ref, o_ref, m_ref, l_ref, acc_ref, *, s_len, n_ctx):
    tq = q_ref.shape[0]
    n_qblocks = q_ref.shape[1] // LANES
    qs = _stack_q(q_ref, n_qblocks)
    _flash_sweep(qs, k_ref, vt_ref, m_ref, l_ref, acc_ref, s_len=s_len, n_ctx=n_ctx)
    _merge_pairs(acc_ref[...] / l_ref[...], tq, n_qblocks, o_ref)


def _gqa_attn_call(p, vt, *, s_len, n_ctx, q_width, q_col0, k_blk, v_blk, for_ctx):
    b, n, _ = p.shape
    tq = TQ_GQA
    n_q, q_i0, n_k, k_i, s_keys = _query_key_windows(n, s_len, n_ctx, tq, for_ctx)
    m_rows = 2 * (q_width // LANES) * tq
    return pl.pallas_call(
        functools.partial(_gqa_attn_kernel, s_len=s_keys, n_ctx=n_ctx),
        grid=(b, n_q // tq),
        in_specs=[
            pl.BlockSpec((None, tq, q_width), lambda bb, i: (bb, q_i0 + i, q_col0 // q_width)),
            pl.BlockSpec((None, n_k, LANES), lambda bb, i: (bb, k_i, k_blk)),
            pl.BlockSpec((None, None, LANES, n_k), lambda bb, i: (bb, v_blk, 0, k_i)),
        ],
        out_specs=pl.BlockSpec((None, tq, q_width), lambda bb, i: (bb, i, 0)),
        out_shape=jax.ShapeDtypeStruct((b, n_q, q_width), BF16),
        scratch_shapes=[pltpu.VMEM((1, m_rows), F32), pltpu.VMEM((1, m_rows), F32), pltpu.VMEM((LANES, m_rows), F32)],
        compiler_params=_cparams(2),
        name="gqa_attn",
    )(p, p, vt)


BLOCKS_PER_STEP = 4


def _win_attn_kernel(q_ref, k_ref, vt_ref, sink_ref, o_ref, *, s_len, n_ctx):
    tq = q_ref.shape[0] // BLOCKS_PER_STEP
    n_qblocks = q_ref.shape[1] // LANES
    span = tq + 2 * WINDOW
    step = pl.program_id(1)
    k_ctx = k_ref[s_len:s_len + n_ctx, :]
    vt_ctx = vt_ref[:, s_len:s_len + n_ctx]
    sink = sink_ref[...] * LOG2E

    def scores(blk):
        i = BLOCKS_PER_STEP * step + blk
        ws = pl.multiple_of(jnp.clip(i * tq - WINDOW, 0, s_len - span), LANES)
        qs = _stack_q(q_ref, n_qblocks, blk * tq, tq)
        m_rows = qs.shape[0]
        s_loc = _scores_t(k_ref[pl.ds(ws, span), :], qs)
        kpos = ws + lax.broadcasted_iota(jnp.int32, (span, m_rows), 0)
        qpos = i * tq + (lax.broadcasted_iota(jnp.int32, (span, m_rows), 1) & (tq - 1))
        s_loc = jnp.where(jnp.abs(qpos - kpos) <= WINDOW, s_loc, NEG_INF)
        return ws, s_loc, _scores_t(k_ctx, qs)

    def finish(blk, ws, s_loc, s_ctx):
        m = jnp.maximum(jnp.maximum(jnp.max(s_loc, axis=0, keepdims=True), jnp.max(s_ctx, axis=0, keepdims=True)),
                        sink)
        p_loc = jnp.exp2(s_loc - m)
        p_ctx = jnp.exp2(s_ctx - m)
        denom = jnp.sum(p_loc, axis=0, keepdims=True) + jnp.sum(p_ctx, axis=0, keepdims=True) + jnp.exp2(sink - m)
        acc = (jnp.dot(vt_ref[:, pl.ds(ws, span)], p_loc.astype(BF16), preferred_element_type=F32)
               + jnp.dot(vt_ctx, p_ctx.astype(BF16), preferred_element_type=F32))
        _merge_pairs(acc / denom, tq, n_qblocks, o_ref, blk * tq)

    pending = [scores(blk) for blk in range(BLOCKS_PER_STEP)]
    for blk, args in enumerate(pending):
        finish(blk, *args)


def _win_attn_call(p, vt, sink_row, *, s_len, n_ctx, q_width, k_blk, v_blk):
    b, n, _ = p.shape
    tq = BLOCKS_PER_STEP * TQ_WIN
    m_rows = sink_row.shape[1]
    return pl.pallas_call(
        functools.partial(_win_attn_kernel, s_len=s_len, n_ctx=n_ctx),
        grid=(b, s_len // tq),
        in_specs=[
            pl.BlockSpec((None, tq, q_width), lambda bb, i: (bb, i, 0)),
            pl.BlockSpec((None, n, LANES), lambda bb, i: (bb, 0, k_blk)),
            pl.BlockSpec((None, None, LANES, n), lambda bb, i: (bb, v_blk, 0, 0)),
            pl.BlockSpec((1, m_rows), lambda bb, i: (0, 0)),
        ],
        out_specs=pl.BlockSpec((None, tq, q_width), lambda bb, i: (bb, i, 0)),
        out_shape=jax.ShapeDtypeStruct((b, s_len, q_width), BF16),
        compiler_params=_cparams(2),
        name="win_attn",
    )(p, p, vt, sink_row)


def _na_attn_kernel(q_ref, k_ref, vt_ref, *rest, s_len, n_ctx):
    bias_refs, o_ref = rest[:BLOCKS_PER_STEP], rest[BLOCKS_PER_STEP]
    tq = q_ref.shape[0] // BLOCKS_PER_STEP
    slab = NA_SLAB * GRID_W
    rows = s_len // GRID_W
    step = pl.program_id(2)
    k_cx = k_ref[s_len:s_len + n_ctx, :]
    vt_cx = vt_ref[:, s_len:s_len + n_ctx]

    def scores(blk):
        mi = BLOCKS_PER_STEP * step + blk
        ss = pl.multiple_of(jnp.clip(NA_QROWS * mi - NA_ROWS_MAX // 2, 0, rows - NA_SLAB) * GRID_W, LANES)
        qs = _stack_q(q_ref, 1, blk * tq, tq)
        s_nb = _scores_t(k_ref[pl.ds(ss, slab), :], qs) + bias_refs[blk][...]
        return ss, s_nb, _scores_t(k_cx, qs)

    def finish(blk, ss, s_nb, s_cx):
        m = jnp.maximum(jnp.max(s_nb, axis=0, keepdims=True), jnp.max(s_cx, axis=0, keepdims=True))
        p_nb = jnp.exp2(s_nb - m)
        p_cx = jnp.exp2(s_cx - m)
        denom = jnp.sum(p_nb, axis=0, keepdims=True) + jnp.sum(p_cx, axis=0, keepdims=True)
        acc = (jnp.dot(vt_ref[:, pl.ds(ss, slab)], p_nb.astype(BF16), preferred_element_type=F32)
               + jnp.dot(vt_cx, p_cx.astype(BF16), preferred_element_type=F32))
        _merge_pairs(acc / denom, tq, 1, o_ref, blk * tq)

    pending = [scores(blk) for blk in range(BLOCKS_PER_STEP)]
    for blk, args in enumerate(pending):
        finish(blk, *args)


def _na_attn_call(p, vt, bias_tab, *, s_len, n_ctx, n_pairs, q_blk0, k_blk0, v_blk0):
    b, n, _ = p.shape
    tq = NA_QROWS * GRID_W
    n_blocks = s_len // tq
    assert n_blocks % BLOCKS_PER_STEP == 0
    tbl = lambda mi: jnp.where(mi == 0, 0, jnp.where(mi == n_blocks - 1, 2, 1))
    bias_spec = lambda blk: pl.BlockSpec((None, None, NA_SLAB * GRID_W, 2 * tq),
                                         lambda bb, j, st: (tbl(BLOCKS_PER_STEP * st + blk), j, 0, 0))
    return pl.pallas_call(
        functools.partial(_na_attn_kernel, s_len=s_len, n_ctx=n_ctx),
        grid=(b, n_pairs, n_blocks // BLOCKS_PER_STEP),
        in_specs=[
            pl.BlockSpec((None, BLOCKS_PER_STEP * tq, LANES), lambda bb, j, st: (bb, st, q_blk0 + j)),
            pl.BlockSpec((None, n, LANES), lambda bb, j, st: (bb, 0, k_blk0 + j)),
            pl.BlockSpec((None, None, LANES, n), lambda bb, j, st: (bb, v_blk0 + j, 0, 0)),
        ] + [bias_spec(blk) for blk in range(BLOCKS_PER_STEP)],
        out_specs=pl.BlockSpec((None, BLOCKS_PER_STEP * tq, LANES), lambda bb, j, st: (bb, st, j)),
        out_shape=jax.ShapeDtypeStruct((b, s_len, n_pairs * LANES), BF16),
        compiler_params=_cparams(3),
        name="na_attn",
    )(p, p, vt, *([bias_tab] * BLOCKS_PER_STEP))


def _na_bias_tables(rpb, rows):
    n_heads = rpb.shape[0]
    pad = GRID_W - NA_COLS
    rev_p = jnp.pad(rpb.astype(F32)[:, :, ::-1], ((0, 0), (0, 0), (pad, pad)))
    col_tab = jnp.stack([rev_p[:, :, pad + NA_COLS - 1 - kc: pad + NA_COLS - 1 - kc + GRID_W] for kc in range(GRID_W)],
                        axis=2)
    kr = np.arange(NA_SLAB)[:, None, None, None]
    kc = np.arange(GRID_W)[None, :, None, None]
    qr = np.arange(NA_QROWS)[None, None, :, None]
    qc = np.arange(GRID_W)[None, None, None, :]
    full = (NA_SLAB, GRID_W, NA_QROWS, GRID_W)
    flat = (NA_SLAB * GRID_W, NA_QROWS * GRID_W)
    tabs = []
    for q0, s0 in ((0, 0), (NA_QROWS, NA_QROWS - NA_ROWS_MAX // 2), (rows - NA_QROWS, rows - NA_SLAB)):
        q_row = q0 + qr
        k_row = s0 + kr
        r0 = np.clip(q_row - NA_ROWS_MAX // 2, 0, rows - NA_ROWS_MAX)
        c0 = np.clip(qc - NA_COLS // 2, 0, GRID_W - NA_COLS)
        valid = (k_row >= r0) & (k_row < r0 + NA_ROWS_MAX) & (kc >= c0) & (kc < c0 + NA_COLS)
        valid = np.broadcast_to(valid, full).reshape(flat)
        r_off = np.clip(k_row - q_row + NA_ROWS_MAX - 1, 0, 2 * NA_ROWS_MAX - 2)[:, 0, :, 0]
        bias = jnp.stack([jnp.stack([col_tab[:, int(r_off[c, a])] for a in range(NA_QROWS)], axis=2)
                          for c in range(NA_SLAB)], axis=1)
        tab = jnp.where(valid[None], bias.reshape((n_heads,) + flat) * LOG2E, NEG_INF)
        tab = tab.reshape((n_heads // 2, 2) + flat)
        tabs.append(jnp.concatenate([tab[:, 0], tab[:, 1]], axis=-1))
    return jnp.stack(tabs)


def _split_hi_lo(x):
    c = x * 65537.0
    hi = c - (c - x)
    return hi.astype(BF16), (x - hi).astype(BF16)


def _layer_norm(y, g, b):
    mu = jnp.mean(y, axis=-1, keepdims=True)
    yc = y - mu
    var = jnp.mean(yc * yc, axis=-1, keepdims=True)
    return yc * lax.rsqrt(var + LN_EPS) * g + b


def _route(logits):
    lane = _lane_iota().astype(F32)
    big = float(LANES)
    is_g = lane < N_GROUPS
    gl = jnp.where(is_g, logits, NEG_INF)
    g_max = jnp.max(gl, axis=-1, keepdims=True)
    g_idx = jnp.min(jnp.where(gl == g_max, lane, big), axis=-1, keepdims=True)
    g_w = 1.0 / jnp.sum(jnp.where(is_g, jnp.exp(gl - g_max), 0.0), axis=-1, keepdims=True)
    base = N_GROUPS + EXPERTS_PER_GROUP * g_idx
    el = jnp.where((lane >= base) & (lane < base + EXPERTS_PER_GROUP), logits, NEG_INF)
    v1 = jnp.max(el, axis=-1, keepdims=True)
    i1 = jnp.min(jnp.where(el == v1, lane, big), axis=-1, keepdims=True)
    el2 = jnp.where(lane == i1, NEG_INF, el)
    v2 = jnp.max(el2, axis=-1, keepdims=True)
    i2 = jnp.min(jnp.where(el2 == v2, lane, big), axis=-1, keepdims=True)
    t = jnp.exp(v2 - v1)
    w1 = g_w / (1.0 + t)
    w2 = g_w * t / (1.0 + t)
    return jnp.where(lane == 0, i1 - N_GROUPS,
                     jnp.where(lane == 1, i2 - N_GROUPS, jnp.where(lane == 2, w1, jnp.where(lane == 3, w2, 0.0))))


def _outproj_kernel(oal_ref, oac_ref, obl_ref, obc_ref, hl_ref, hc_ref, w_ref, gate_ref, lng_ref, lnb_ref,
                    sh_ref, sc_ref, wr_ref, br_ref, h1_ref, f_ref, route_ref, *, alpha, n_lat):
    ka = oal_ref.shape[1]
    oa = _read_split_stream(oal_ref, oac_ref, n_lat)
    ob = _read_split_stream(obl_ref, obc_ref, n_lat)
    h = _read_split_stream(hl_ref, hc_ref, n_lat)
    o = (jnp.dot(oa, w_ref[:ka, :], preferred_element_type=F32)
         + jnp.dot(ob, w_ref[ka:, :], preferred_element_type=F32))
    h1 = _layer_norm(alpha * h + gate_ref[...] * o, lng_ref[...], lnb_ref[...])
    h1_ref[...] = h1
    f = h1 * sc_ref[...] + sh_ref[...]
    f_ref[:, 0, :] = f
    f_hi, f_lo = _split_hi_lo(f)
    hi_prod = jnp.dot(f_hi, wr_ref[...], preferred_element_type=F32)
    lo_prod = jnp.dot(f_lo, wr_ref[...], preferred_element_type=F32)
    logits = (hi_prod[:, :LANES] + hi_prod[:, LANES:]) + (lo_prod[:, :LANES] + lo_prod[:, LANES:]) + br_ref[...]
    route_ref[...] = _route(logits)


def _outproj_call(oa, ob, h, w16, gate, ln_g, ln_b, shift, scale1p, w_route, b_route, *, n_rows, s_len, alpha):
    b, _, d = h[0].shape
    ka, kb = oa[0].shape[2], ob[0].shape[2]
    n_lat = s_len // TM
    seg = lambda bb, j: (bb, jnp.where(j >= n_lat, 1, 0), 0, 0)
    row = lambda bb, j: (bb, j, 0)
    const = lambda bb, j: (0, 0)
    return pl.pallas_call(
        functools.partial(_outproj_kernel, alpha=alpha, n_lat=n_lat),
        grid=(b, n_rows // TM),
        in_specs=_split_stream_specs(*oa, n_lat) + _split_stream_specs(*ob, n_lat) + _split_stream_specs(*h, n_lat) + [
            pl.BlockSpec((ka + kb, d), const),
            pl.BlockSpec((None, None, 1, d), seg),
            pl.BlockSpec((1, d), const),
            pl.BlockSpec((1, d), const),
            pl.BlockSpec((None, None, 1, d), seg),
            pl.BlockSpec((None, None, 1, d), seg),
            pl.BlockSpec((d, 2 * LANES), const),
            pl.BlockSpec((1, LANES), const),
        ],
        out_specs=[
            pl.BlockSpec((None, TM, d), row),
            pl.BlockSpec((None, TM, 1, d), lambda bb, j: (bb, j, 0, 0)),
            pl.BlockSpec((None, TM, LANES), row),
        ],
        out_shape=[
            jax.ShapeDtypeStruct((b, n_rows, d), F32),
            jax.ShapeDtypeStruct((b, n_rows, 1, d), F32),
            jax.ShapeDtypeStruct((b, n_rows, LANES), F32),
        ],
        compiler_params=_cparams(2),
        name="outproj",
    )(oa[0], oa[1], ob[0], ob[1], h[0], h[1], w16, gate, ln_g, ln_b, shift, scale1p, w_route, b_route)


def _row_copy(src, src_row, dst, dst_row, sem):
    return pltpu.make_async_copy(src.at[pl.ds(src_row, 1), :], dst.at[pl.ds(dst_row, 1), :], sem)


def _block_copy_for_wait(src, dst, sem):
    return pltpu.make_async_copy(src, dst, sem)


MOE_BUFS = 3
DMA_PRIORITIES = 2
MOE_VMEM_LIMIT = 57 * 1024 * 1024


def _moe_kernel(blk_e_ref, blk_b_ref, blk_used_ref, tok_ref, dst_prev_ref, dst_last_ref, f_ref,
                w1_ref, w3_ref, w2_ref, y_hbm, xs, zbuf, ybuf, w1b, w3b, w2b, ssem):
    del blk_b_ref
    i = pl.program_id(0)
    last = pl.num_programs(0) - 1
    cur = lax.rem(i, MOE_BUFS)
    prev = lax.rem(i + 2, MOE_BUFS)
    used = blk_used_ref[i] > 0
    prev_used = jnp.logical_or(i == 0, blk_used_ref[jnp.maximum(i - 1, 0)] > 0)

    def wait_scatter(s):
        _block_copy_for_wait(ybuf.at[s], y_hbm.at[pl.ds(0, MOE_BLOCK), :], ssem.at[s]).wait()

    def start_scatter(dst_ref, s):
        for r in range(MOE_BLOCK):
            _row_copy(ybuf.at[s], r, y_hbm, dst_ref[0, r], ssem.at[s]).start(priority=r % DMA_PRIORITIES)

    def start_zero_fill(dst_ref, s):
        row0 = pl.multiple_of(dst_ref[0, 0], MOE_BLOCK)
        pltpu.make_async_copy(zbuf, y_hbm.at[pl.ds(row0, MOE_BLOCK), :], ssem.at[s]).start()

    @pl.when(i == 0)
    def _():
        ybuf[...] = jnp.zeros(ybuf.shape, ybuf.dtype)
        zbuf[...] = jnp.zeros(zbuf.shape, zbuf.dtype)

    @pl.when(i >= 2)
    def _():
        wait_scatter(cur)

    @pl.when(jnp.logical_or(i == 0, blk_e_ref[i] != blk_e_ref[jnp.maximum(i - 1, 0)]))
    def _():
        w1b[...] = w1_ref[...].astype(BF16)
        w3b[...] = w3_ref[...].astype(BF16)
        w2b[...] = w2_ref[...].astype(BF16)

    @pl.when(used)
    def _():
        for r in range(MOE_BLOCK):
            xs[pl.ds(r, 1), :] = f_ref[tok_ref[0, r]]
        x = xs[...].astype(BF16)
        start_scatter(dst_prev_ref, prev)
        h1 = jnp.dot(x, w1b[...], preferred_element_type=F32)
        h3 = jnp.dot(x, w3b[...], preferred_element_type=F32)
        act = (h1 / (1.0 + jnp.exp(-h1))) * h3
        ybuf[cur] = jnp.dot(act.astype(BF16), w2b[...], preferred_element_type=F32)

    @pl.when(jnp.logical_and(jnp.logical_not(used), prev_used))
    def _():
        start_scatter(dst_prev_ref, prev)

    @pl.when(jnp.logical_and(jnp.logical_not(used), jnp.logical_not(prev_used)))
    def _():
        start_zero_fill(dst_prev_ref, prev)

    @pl.when(i == last)
    def _():
        @pl.when(used)
        def _():
            start_scatter(dst_last_ref, cur)

        @pl.when(jnp.logical_not(used))
        def _():
            start_zero_fill(dst_last_ref, cur)

        for s in range(MOE_BUFS):
            wait_scatter(s)


def _moe_call(f_rows, blk_e, blk_b, blk_used, slot_tok, slot_dst, w1, w3, w2, layer):
    n_batch, per_batch, _, d = f_rows.shape
    n_blocks = blk_e.shape[0]
    n_slots = n_blocks * MOE_BLOCK
    de = w1.shape[3]
    assert n_blocks >= MOE_BUFS
    smem_block = lambda index_map: pl.BlockSpec((None, 1, MOE_BLOCK), index_map, memory_space=pltpu.SMEM)
    grid_spec = pltpu.PrefetchScalarGridSpec(
        num_scalar_prefetch=3,
        grid=(n_blocks,),
        in_specs=[
            smem_block(lambda i, be, bb, bu: (i, 0, 0)),
            smem_block(lambda i, be, bb, bu: (i, 0, 0)),
            smem_block(lambda i, be, bb, bu: (i + 1, 0, 0)),
            pl.BlockSpec((None, per_batch, 1, d), lambda i, be, bb, bu: (bb[i], 0, 0, 0), pipeline_mode=pl.Buffered(1)),
            pl.BlockSpec((None, None, d, de), lambda i, be, bb, bu: (layer, be[i], 0, 0)),
            pl.BlockSpec((None, None, d, de), lambda i, be, bb, bu: (layer, be[i], 0, 0)),
            pl.BlockSpec((None, None, de, d), lambda i, be, bb, bu: (layer, be[i], 0, 0)),
        ],
        out_specs=pl.BlockSpec(memory_space=pl.ANY),
        scratch_shapes=[
            pltpu.VMEM((MOE_BLOCK, d), F32),
            pltpu.VMEM((MOE_BLOCK, d), F32),
            pltpu.VMEM((MOE_BUFS, MOE_BLOCK, d), F32),
            pltpu.VMEM((d, de), BF16),
            pltpu.VMEM((d, de), BF16),
            pltpu.VMEM((de, d), BF16),
            pltpu.SemaphoreType.DMA((MOE_BUFS,)),
        ],
    )
    tok3 = slot_tok.reshape(n_blocks, 1, MOE_BLOCK)
    n_rows = n_slots + MOE_BLOCK
    first = (n_slots + jnp.arange(MOE_BLOCK, dtype=jnp.int32))[None, :]
    dst3 = jnp.concatenate([first, slot_dst], axis=0).reshape(n_blocks + 1, 1, MOE_BLOCK)
    return pl.pallas_call(
        _moe_kernel,
        grid_spec=grid_spec,
        out_shape=jax.ShapeDtypeStruct((n_rows, d), F32),
        compiler_params=pltpu.CompilerParams(dimension_semantics=("arbitrary",), vmem_limit_bytes=MOE_VMEM_LIMIT),
        name="moe_experts",
    )(blk_e, blk_b, blk_used, tok3, dst3, dst3, f_rows, w1, w3, w2)


def _moe_plan(route, t, n_batch):
    n_assign = TOP_K * t
    per_batch = t // n_batch
    n_seg = n_batch * N_EXPERTS
    segs = jnp.arange(n_seg, dtype=jnp.int32)
    e_flat = route[:, :TOP_K].astype(jnp.int32).reshape(n_assign)
    tok_batch = jnp.arange(n_assign, dtype=jnp.int32) // (TOP_K * per_batch)
    seg_flat = tok_batch * N_EXPERTS + e_flat
    order = jnp.argsort(seg_flat, stable=True).astype(jnp.int32)
    counts = jnp.sum((seg_flat[:, None] == segs[None, :]).astype(jnp.int32), axis=0)
    starts = jnp.cumsum(counts) - counts
    padded = (counts + MOE_BLOCK - 1) // MOE_BLOCK * MOE_BLOCK
    p_ends = jnp.cumsum(padded)
    p_starts = p_ends - padded
    n_blocks = (n_assign + MOE_BLOCK - 1) // MOE_BLOCK + n_seg
    blk_start = jnp.arange(n_blocks, dtype=jnp.int32) * MOE_BLOCK
    blk_seg = jnp.minimum(jnp.sum((p_ends[None, :] <= blk_start[:, None]).astype(jnp.int32), axis=1), n_seg - 1)
    blk_b = blk_seg // N_EXPERTS
    of_block = lambda v: jnp.sum(jnp.where(blk_seg[:, None] == segs[None, :], v[None, :], 0), axis=1)[:, None]
    slot = blk_start[:, None] + jnp.arange(MOE_BLOCK, dtype=jnp.int32)[None, :]
    j = slot - of_block(p_starts)
    valid = j < of_block(counts)
    a_idx = order[jnp.clip(of_block(starts) + j, 0, n_assign - 1)]
    slot_tok = jnp.where(valid, a_idx // TOP_K - blk_b[:, None] * per_batch, 0).astype(jnp.int32)
    spare = n_assign + slot - of_block(starts + counts)
    slot_dst = jnp.where(valid, (a_idx % TOP_K) * t + a_idx // TOP_K, spare).astype(jnp.int32)
    blk_used = (blk_start < p_ends[n_seg - 1]).astype(jnp.int32)
    return blk_seg % N_EXPERTS, blk_b, blk_used, slot_tok, slot_dst


def _combine_kernel(h_ref, y0_ref, y1_ref, route_ref, gate_ref, lng_ref, lnb_ref, o_ref, *, alpha):
    r = route_ref[...]
    y = r[:, 2:3] * y0_ref[...] + r[:, 3:4] * y1_ref[...]
    o_ref[...] = _layer_norm(alpha * h_ref[...] + gate_ref[...] * y, lng_ref[...], lnb_ref[...])


def _combine_call(h1, y, route, gate, ln_g, ln_b, *, s_len, alpha):
    b, n_rows, d = h1.shape
    n_lat = s_len // TM
    seg = lambda bb, j: (bb, jnp.where(j >= n_lat, 1, 0), 0, 0)
    row = lambda bb, j: (bb, j, 0)
    const = lambda bb, j: (0, 0)
    blocks_per_batch = n_rows // TM
    blocks_per_k = b * blocks_per_batch
    return pl.pallas_call(
        functools.partial(_combine_kernel, alpha=alpha),
        grid=(b, n_rows // TM),
        in_specs=[
            pl.BlockSpec((None, TM, d), row),
            pl.BlockSpec((TM, d), lambda bb, j: (bb * blocks_per_batch + j, 0)),
            pl.BlockSpec((TM, d), lambda bb, j: (blocks_per_k + bb * blocks_per_batch + j, 0)),
            pl.BlockSpec((None, TM, LANES), row),
            pl.BlockSpec((None, None, 1, d), seg),
            pl.BlockSpec((1, d), const),
            pl.BlockSpec((1, d), const),
        ],
        out_specs=pl.BlockSpec((None, TM, d), row),
        out_shape=jax.ShapeDtypeStruct((b, n_rows, d), F32),
        compiler_params=_cparams(2),
        name="combine",
    )(h1, y, y, route, gate, ln_g, ln_b)


def _pair_heads(w, axis):
    w = jnp.moveaxis(w, axis, -1)
    lead = w.shape[:-1]
    n_heads = w.shape[-1] // HEAD_DIM
    w = w.reshape(lead + (2, n_heads // 2, HEAD_DIM)).swapaxes(-3, -2).reshape(lead + (n_heads * HEAD_DIM,))
    return jnp.moveaxis(w, -1, axis)


def _rope_tables(s_len, n_ctx):
    t = jnp.arange(s_len, dtype=jnp.int32)
    row = (t // GRID_W).astype(F32)
    col = (t % GRID_W).astype(F32)
    n_freq = HEAD_DIM // 4
    inv = ROPE_THETA ** (-jnp.arange(n_freq, dtype=F32) / n_freq)
    ang = jnp.concatenate([row[:, None] * inv, col[:, None] * inv], -1)
    cos = jnp.concatenate([jnp.cos(ang), jnp.ones((n_ctx, HEAD_DIM // 2), F32)], 0)
    sin = jnp.concatenate([jnp.sin(ang), jnp.zeros((n_ctx, HEAD_DIM // 2), F32)], 0)
    return jnp.tile(cos, (1, 4)), jnp.concatenate([-sin, sin, -sin, sin], -1)


def _lambda_init(layer_idx):
    return 0.8 - 0.6 * float(np.exp(-0.3 * layer_idx))


def _router_params(w_group, b_group, w_router, b_router):
    d = w_group.shape[0]
    pad = LANES - N_GROUPS - N_EXPERTS
    w = jnp.concatenate([w_group, w_router, jnp.zeros((d, pad), F32)], axis=1)
    bb = jnp.concatenate([b_group, b_router, jnp.zeros((pad,), F32)])
    w_hi, w_lo = _split_hi_lo(w)
    return jnp.concatenate([w_hi, w_lo], axis=1), bb.reshape(1, LANES)


def _moe_layer(f, route, w1, w3, w2, layer):
    b, n_rows = f.shape[:2]
    t = b * n_rows
    blk_e, blk_b, blk_used, slot_tok, slot_dst = _moe_plan(route.reshape(t, LANES), t, b)
    return _moe_call(f, blk_e, blk_b, blk_used, slot_tok, slot_dst, w1, w3, w2, layer)


def kernel(x, c, ctx, c_ctx, mod_w, mod_b, ln_g, ln_b, ab_w_in, ab_w_out, diff_lambda, diff_subln_g, gqa_qk_g,
           cd_w_in, cd_w_out, win_sink, na_rpb, moe_w_group, moe_b_group, moe_w_router, moe_b_router,
           moe_w1, moe_w3, moe_w2):
    b, s, d = x.shape
    n_ctx = ctx.shape[1]
    n = s + n_ctx
    depth = mod_w.shape[0]
    rows = s // GRID_W
    assert depth == 2, "layer pattern implemented: one differential/GQA layer, then one window/neighbourhood layer"
    assert s % TM == 0 and n_ctx % TM == 0 and s % TK == 0 and b + 1 <= 8
    assert s % TQ_DIFF == 0 and s % TQ_GQA == 0 and n_ctx % TQ_GQA == 0
    assert rows >= NA_SLAB and rows % NA_QROWS == 0 and s % GRID_W == 0
    alpha = (2.0 * depth) ** 0.25
    qw = d // 2
    kvw = qw // 4
    assert qw == 4 * LANES and kvw == LANES

    cos128, sin128 = _rope_tables(s, n_ctx)
    cvec = jnp.concatenate([c, c_ctx[None, :], jnp.zeros((8 - b - 1, d), F32)], axis=0)
    mods_all = _mods_call(cvec, mod_w, mod_b)

    def seg_mods(i):
        m = mods_all[i]
        lat = m[:b].reshape(b, 6, d)
        cm = jnp.broadcast_to(m[b].reshape(1, 6, d), (b, 6, d))
        ms = jnp.stack([lat, cm], axis=1)
        pick = lambda k, one: (ms[:, :, k] + one)[:, :, None, :]
        return pick(0, 0.0), pick(1, 1.0), pick(2, 0.0), pick(3, 0.0), pick(4, 1.0), pick(5, 0.0)

    n_lat = s // TM
    stream0 = (x, ctx, 0)

    shift1, scale1p, gate1, shift2, scale2p, gate2 = seg_mods(0)
    w_in = ab_w_in[0]
    q_d, q_g, k_d, v_d = (w_in[:, k * qw:(k + 1) * qw] for k in range(4))
    k_g, v_g = w_in[:, 4 * qw:4 * qw + kvw], w_in[:, 4 * qw + kvw:]
    w_qk = jnp.concatenate([q_d, _pair_heads(q_g, 1), k_d, k_g], axis=1).astype(BF16)
    roles0 = ((True, None, True),) * 4 + ((True, 0, True),) * 4 + ((True, None, False),) * 4 + ((True, 1, False),)
    w_vt = jnp.concatenate([v_d, v_g], axis=1).T.astype(BF16)
    g128 = jnp.tile(gqa_qk_g[0], (1, 2))
    p0, vt0 = _inproj_call(*stream0, n, shift1, scale1p, w_qk, w_vt, cos128, sin128, g128, roles0, s)
    diff_attn = functools.partial(_diff_attn_call, p0, vt0, diff_lambda[0], diff_subln_g[0], s_len=s, n_ctx=n_ctx,
                                  n_heads=4, q_blk0=0, k_blk0=8, v_blk0=0, lam_init=_lambda_init(0))
    gqa_attn = functools.partial(_gqa_attn_call, p0, vt0, s_len=s, n_ctx=n_ctx, q_width=qw, q_col0=qw,
                                 k_blk=12, v_blk=4)
    o_d = (diff_attn(for_ctx=False), diff_attn(for_ctx=True), 0)
    o_g = (gqa_attn(for_ctx=False), gqa_attn(for_ctx=True), 0)
    w_out = jnp.concatenate([ab_w_out[0][:qw], _pair_heads(ab_w_out[0][qw:], 0)], axis=0).astype(BF16)
    w_route, b_route = _router_params(moe_w_group[0], moe_b_group[0], moe_w_router[0], moe_b_router[0])
    h1, f, route = _outproj_call(o_d, o_g, stream0, w_out, gate1, ln_g[0, 0:1], ln_b[0, 0:1], shift2, scale2p,
                                 w_route, b_route, n_rows=n, s_len=s, alpha=alpha)
    y = _moe_layer(f, route, moe_w1, moe_w3, moe_w2, 0)
    h = _combine_call(h1, y, route, gate2, ln_g[0, 1:2], ln_b[0, 1:2], s_len=s, alpha=alpha)

    shift1, scale1p, gate1, shift2, scale2p, gate2 = seg_mods(1)
    w_in = cd_w_in[0]
    q_w, q_n = w_in[:, :qw], w_in[:, qw:2 * qw]
    k_w, v_w = w_in[:, 2 * qw:2 * qw + kvw], w_in[:, 2 * qw + kvw:2 * qw + 2 * kvw]
    k_n, v_n = w_in[:, 2 * qw + 2 * kvw:3 * qw + 2 * kvw], w_in[:, 3 * qw + 2 * kvw:]
    w_qk = jnp.concatenate([_pair_heads(q_w, 1), q_n, k_w, k_n], axis=1).astype(BF16)
    roles1 = ((True, None, True),) * 4 + ((False, None, True),) * 4 + ((True, None, False),) \
        + ((False, None, False),) * 4
    w_vt = jnp.concatenate([v_w, v_n], axis=1).T.astype(BF16)
    stream1 = (h, h, n_lat)
    p1, vt1 = _inproj_call(*stream1, n, shift1, scale1p, w_qk, w_vt, cos128, sin128, jnp.ones((2, LANES), F32),
                           roles1, s)
    sink_row = jnp.repeat(win_sink[0].reshape(2, 4).T.reshape(8), TQ_WIN).reshape(1, 8 * TQ_WIN)
    o_w = _win_attn_call(p1, vt1, sink_row, s_len=s, n_ctx=n_ctx, q_width=qw, k_blk=8, v_blk=0)
    o_n = _na_attn_call(p1, vt1, _na_bias_tables(na_rpb[0], rows), s_len=s, n_ctx=n_ctx, n_pairs=4,
                        q_blk0=4, k_blk0=9, v_blk0=1)
    w_out = jnp.concatenate([_pair_heads(cd_w_out[0][:qw], 0), cd_w_out[0][qw:]], axis=0).astype(BF16)
    w_route, b_route = _router_params(moe_w_group[1], moe_b_group[1], moe_w_router[1], moe_b_router[1])
    h1, f, route = _outproj_call((o_w, o_w, 0), (o_n, o_n, 0), stream1, w_out, gate1, ln_g[1, 0:1], ln_b[1, 0:1],
                                 shift2, scale2p, w_route, b_route, n_rows=s, s_len=s, alpha=alpha)
    y = _moe_layer(f, route, moe_w1, moe_w3, moe_w2, 1)
    return _combine_call(h1, y, route, gate2, ln_g[1, 1:2], ln_b[1, 1:2], s_len=s, alpha=alpha)
```

```python
import functools

import numpy as np
import jax
import jax.numpy as jnp
from jax import lax
from jax.experimental import pallas as pl
from jax.experimental.pallas import tpu as pltpu

F32 = jnp.float32
BF16 = jnp.bfloat16
HIGHEST = lax.Precision.HIGHEST

GRID_W = 64
HEAD_DIM = 64
ROPE_THETA = 10000.0
WINDOW = 128
NA_ROWS_MAX = 8
NA_COLS = 16
N_GROUPS = 4
EXPERTS_PER_GROUP = 8
N_EXPERTS = N_GROUPS * EXPERTS_PER_GROUP
TOP_K = 2
MOE_BLOCK = 128
LN_EPS = 1e-5
RMS_EPS = 1e-6
ATTN_SCALE = HEAD_DIM ** -0.5
NEG_INF = -1e30
LOG2E = 1.4426950408889634
Q_SCALE = ATTN_SCALE * LOG2E

LANES = 128
PAIR = 2 * HEAD_DIM
assert PAIR == LANES

VMEM_LIMIT = 48 * 1024 * 1024

TM = 256
TQ_DIFF = 512
TQ_GQA = 128
TK = 1024
TQ_WIN = 128
NA_QROWS = 4
NA_SLAB = NA_QROWS + NA_ROWS_MAX


def _cparams(n_axes):
    return pltpu.CompilerParams(dimension_semantics=("arbitrary",) * n_axes, vmem_limit_bytes=VMEM_LIMIT)


def _lane_iota():
    return lax.broadcasted_iota(jnp.int32, (1, LANES), 1)


def _mods_kernel(c_ref, w_ref, b_ref, o_ref):
    c = c_ref[...]
    sc = c / (1.0 + jnp.exp(-c))
    o_ref[...] = jnp.dot(sc, w_ref[...], precision=HIGHEST, preferred_element_type=F32) + b_ref[...]


def _mods_call(cvec, mod_w, mod_b):
    depth, d, n6 = mod_w.shape
    tn = n6 // 4
    return pl.pallas_call(
        _mods_kernel,
        grid=(depth, n6 // tn),
        in_specs=[
            pl.BlockSpec((8, d), lambda l, j: (0, 0)),
            pl.BlockSpec((None, d, tn), lambda l, j: (l, 0, j)),
            pl.BlockSpec((None, 1, tn), lambda l, j: (l, 0, j)),
        ],
        out_specs=pl.BlockSpec((None, 8, tn), lambda l, j: (l, 0, j)),
        out_shape=jax.ShapeDtypeStruct((depth, 8, n6), F32),
        compiler_params=_cparams(2),
        name="mods",
    )(cvec, mod_w, mod_b.reshape(depth, 1, n6))


def _split_stream_specs(lat, ctx, ctx_block0, n_lat):
    width = lat.shape[2]
    return [
        pl.BlockSpec((None, TM, width), lambda bb, j: (bb, jnp.minimum(j, n_lat - 1), 0)),
        pl.BlockSpec((None, TM, width), lambda bb, j: (bb, ctx_block0 + jnp.maximum(j - n_lat, 0), 0)),
    ]


def _read_split_stream(lat_ref, ctx_ref, n_lat):
    return jnp.where(pl.program_id(1) >= n_lat, ctx_ref[...], lat_ref[...])


def _inproj_kernel(hl_ref, hc_ref, sh_ref, sc_ref, w_ref, wvt_ref, cos_ref, sin_ref, g_ref, o_ref, vt_ref, *,
                   roles, n_lat):
    a = (_read_split_stream(hl_ref, hc_ref, n_lat) * sc_ref[...] + sh_ref[...]).astype(BF16)
    vt = lax.dot_general(wvt_ref[...], a, (((1,), (1,)), ((), ())), preferred_element_type=F32)
    for j in range(vt_ref.shape[0]):
        vt_ref[j] = vt[j * LANES:(j + 1) * LANES, :].astype(BF16)
    lane = _lane_iota()
    rope_first = (lane & (HEAD_DIM - 1)) < (HEAD_DIM // 2)
    lo = lane < HEAD_DIM
    n_blocks = len(roles)
    for c0 in range(0, n_blocks, 2):
        width = min(2, n_blocks - c0)
        p = jnp.dot(a, w_ref[:, c0 * LANES:(c0 + width) * LANES], preferred_element_type=F32)
        for half in range(width):
            cb = c0 + half
            rope, norm, scale = roles[cb]
            blk = p[:, half * LANES:(half + 1) * LANES]
            if norm is not None:
                sq = blk * blk
                s_lo = jnp.sum(jnp.where(lo, sq, 0.0), axis=-1, keepdims=True)
                s_hi = jnp.sum(jnp.where(lo, 0.0, sq), axis=-1, keepdims=True)
                ms = jnp.where(lo, s_lo, s_hi) * (1.0 / HEAD_DIM)
                blk = blk * lax.rsqrt(ms + RMS_EPS) * g_ref[norm:norm + 1, :]
            if rope:
                partner = jnp.where(rope_first, pltpu.roll(blk, LANES - HEAD_DIM // 2, 1),
                                    pltpu.roll(blk, HEAD_DIM // 2, 1))
                blk = blk * cos_ref[...] + partner * sin_ref[...]
            if scale:
                blk = blk * Q_SCALE
            o_ref[:, cb * LANES:(cb + 1) * LANES] = blk.astype(BF16)


def _inproj_call(h_lat, h_ctx, ctx_block0, n, shift, scale1p, w16, wvt16, cos128, sin128, g128, roles, s_len):
    b, _, d = h_lat.shape
    ncol = w16.shape[1]
    nvb = wvt16.shape[0] // LANES
    n_lat = s_len // TM
    seg = lambda bb, j: (bb, jnp.where(j >= n_lat, 1, 0), 0, 0)
    return pl.pallas_call(
        functools.partial(_inproj_kernel, roles=roles, n_lat=n_lat),
        grid=(b, n // TM),
        in_specs=_split_stream_specs(h_lat, h_ctx, ctx_block0, n_lat) + [
            pl.BlockSpec((None, None, 1, d), seg),
            pl.BlockSpec((None, None, 1, d), seg),
            pl.BlockSpec((d, ncol), lambda bb, j: (0, 0)),
            pl.BlockSpec((nvb * LANES, d), lambda bb, j: (0, 0)),
            pl.BlockSpec((TM, LANES), lambda bb, j: (j, 0)),
            pl.BlockSpec((TM, LANES), lambda bb, j: (j, 0)),
            pl.BlockSpec((2, LANES), lambda bb, j: (0, 0)),
        ],
        out_specs=[
            pl.BlockSpec((None, TM, ncol), lambda bb, j: (bb, j, 0)),
            pl.BlockSpec((None, nvb, LANES, TM), lambda bb, j: (bb, 0, 0, j)),
        ],
        out_shape=[
            jax.ShapeDtypeStruct((b, n, ncol), BF16),
            jax.ShapeDtypeStruct((b, nvb, LANES, n), BF16),
        ],
        compiler_params=_cparams(2),
        name="inproj",
    )(h_lat, h_ctx, shift, scale1p, w16, wvt16, cos128, sin128, g128)


def _stack_q(q_ref, n_qblocks, row0=0, n_rows=None):
    n_rows = q_ref.shape[0] if n_rows is None else n_rows
    lo = _lane_iota() < HEAD_DIM
    parts = []
    for j in range(n_qblocks):
        q = q_ref[row0:row0 + n_rows, j * LANES:(j + 1) * LANES]
        zero = jnp.zeros_like(q)
        parts.append(jnp.where(lo, q, zero))
        parts.append(jnp.where(lo, zero, q))
    return jnp.concatenate(parts, axis=0)


def _scores_t(k, qs):
    return lax.dot_general(k, qs, (((1,), (1,)), ((), ())), preferred_element_type=F32)


def _flash_init(m_ref, l_ref, acc_ref):
    m_ref[...] = jnp.full(m_ref.shape, NEG_INF, F32)
    l_ref[...] = jnp.zeros(l_ref.shape, F32)
    acc_ref[...] = jnp.zeros(acc_ref.shape, F32)


def _flash_update(st, vt, m_ref, l_ref, acc_ref):
    m_prev = m_ref[...]
    m_new = jnp.maximum(m_prev, jnp.max(st, axis=0, keepdims=True))
    alpha = jnp.exp2(m_prev - m_new)
    pt = jnp.exp2(st - m_new)
    l_ref[...] = alpha * l_ref[...] + jnp.sum(pt, axis=0, keepdims=True)
    acc_ref[...] = alpha * acc_ref[...] + jnp.dot(vt, pt.astype(BF16), preferred_element_type=F32)
    m_ref[...] = m_new


def _flash_sweep(qs, k_ref, vt_ref, m_ref, l_ref, acc_ref, *, s_len, n_ctx):
    _flash_init(m_ref, l_ref, acc_ref)
    bounds = [(c * TK, (c + 1) * TK) for c in range(s_len // TK)] + [(s_len, s_len + n_ctx)]
    s_next = _scores_t(k_ref[bounds[0][0]:bounds[0][1], :], qs)
    for idx, (lo, hi) in enumerate(bounds):
        s_cur = s_next
        if idx + 1 < len(bounds):
            s_next = _scores_t(k_ref[bounds[idx + 1][0]:bounds[idx + 1][1], :], qs)
        _flash_update(s_cur, vt_ref[:, lo:hi], m_ref, l_ref, acc_ref)


def _merge_pairs(ot, tq, n_qblocks, o_ref, row0=0):
    for j in range(n_qblocks):
        o_lo = ot[:HEAD_DIM, (2 * j) * tq:(2 * j + 1) * tq]
        o_hi = ot[HEAD_DIM:, (2 * j + 1) * tq:(2 * j + 2) * tq]
        blk = jnp.concatenate([o_lo, o_hi], axis=0)
        o_ref[row0:row0 + tq, j * LANES:(j + 1) * LANES] = blk.T.astype(o_ref.dtype)


def _diff_attn_kernel(q_ref, k_ref, vt_ref, lam_ref, g_ref, o_ref, m_ref, l_ref, acc_ref, *, s_len, n_ctx, lam_init):
    tq = q_ref.shape[0]
    qs = _stack_q(q_ref, 1)
    _flash_sweep(qs, k_ref, vt_ref, m_ref, l_ref, acc_ref, s_len=s_len, n_ctx=n_ctx)
    lv = lam_ref[...]
    lam = (jnp.exp(jnp.sum(lv[0:1] * lv[1:2], axis=-1, keepdims=True))
           - jnp.exp(jnp.sum(lv[2:3] * lv[3:4], axis=-1, keepdims=True)) + lam_init)
    ot = acc_ref[...] / l_ref[...]
    od = ot[:, :tq] - lam * ot[:, tq:]
    ms = jnp.mean(od * od, axis=0, keepdims=True)
    od = od * lax.rsqrt(ms + RMS_EPS) * g_ref[...] * (1.0 - lam_init)
    o_ref[...] = od.T.astype(o_ref.dtype)


def _query_key_windows(n, s_len, n_ctx, tq, for_ctx):
    if for_ctx:
        assert s_len % n_ctx == 0 and n_ctx % tq == 0
        return n_ctx, s_len // tq, n_ctx, s_len // n_ctx, 0
    assert s_len % tq == 0
    return s_len, 0, n, 0, s_len


def _diff_attn_call(p, vt, lam_vecs, subln_g, *, s_len, n_ctx, n_heads, q_blk0, k_blk0, v_blk0, lam_init, for_ctx):
    b, n, _ = p.shape
    tq = min(TQ_DIFF, n_ctx) if for_ctx else TQ_DIFF
    n_q, q_i0, n_k, k_i, s_keys = _query_key_windows(n, s_len, n_ctx, tq, for_ctx)
    return pl.pallas_call(
        functools.partial(_diff_attn_kernel, s_len=s_keys, n_ctx=n_ctx, lam_init=lam_init),
        grid=(b, n_heads, n_q // tq),
        in_specs=[
            pl.BlockSpec((None, tq, LANES), lambda bb, hh, i: (bb, q_i0 + i, q_blk0 + hh)),
            pl.BlockSpec((None, n_k, LANES), lambda bb, hh, i: (bb, k_i, k_blk0 + hh)),
            pl.BlockSpec((None, None, LANES, n_k), lambda bb, hh, i: (bb, v_blk0 + hh, 0, k_i)),
            pl.BlockSpec((4, HEAD_DIM), lambda bb, hh, i: (0, 0)),
            pl.BlockSpec((LANES, 1), lambda bb, hh, i: (0, 0)),
        ],
        out_specs=pl.BlockSpec((None, tq, LANES), lambda bb, hh, i: (bb, i, hh)),
        out_shape=jax.ShapeDtypeStruct((b, n_q, n_heads * LANES), BF16),
        scratch_shapes=[pltpu.VMEM((1, 2 * tq), F32), pltpu.VMEM((1, 2 * tq), F32), pltpu.VMEM((LANES, 2 * tq), F32)],
        compiler_params=_cparams(3),
        name="diff_attn",
    )(p, p, vt, lam_vecs, subln_g.reshape(LANES, 1))


def _gqa_attn_kernel(q_ref, k_ref, vt_ref, o_ref, m_ref, l_ref, acc_ref, *, s_len, n_ctx):
    tq = q_ref.shape[0]
    n_qblocks = q_ref.shape[1] // LANES
    qs = _stack_q(q_ref, n_qblocks)
    _flash_sweep(qs, k_ref, vt_ref, m_ref, l_ref, acc_ref, s_len=s_len, n_ctx=n_ctx)
    _merge_pairs(acc_ref[...] / l_ref[...], tq, n_qblocks, o_ref)


def _gqa_attn_call(p, vt, *, s_len, n_ctx, q_width, q_col0, k_blk, v_blk, for_ctx):
    b, n, _ = p.shape
    tq = TQ_GQA
    n_q, q_i0, n_k, k_i, s_keys = _query_key_windows(n, s_len, n_ctx, tq, for_ctx)
    m_rows = 2 * (q_width // LANES) * tq
    return pl.pallas_call(
        functools.partial(_gqa_attn_kernel, s_len=s_keys, n_ctx=n_ctx),
        grid=(b, n_q // tq),
        in_specs=[
            pl.BlockSpec((None, tq, q_width), lambda bb, i: (bb, q_i0 + i, q_col0 // q_width)),
            pl.BlockSpec((None, n_k, LANES), lambda bb, i: (bb, k_i, k_blk)),
            pl.BlockSpec((None, None, LANES, n_k), lambda bb, i: (bb, v_blk, 0, k_i)),
        ],
        out_specs=pl.BlockSpec((None, tq, q_width), lambda bb, i: (bb, i, 0)),
        out_shape=jax.ShapeDtypeStruct((b, n_q, q_width), BF16),
        scratch_shapes=[pltpu.VMEM((1, m_rows), F32), pltpu.VMEM((1, m_rows), F32), pltpu.VMEM((LANES, m_rows), F32)],
        compiler_params=_cparams(2),
        name="gqa_attn",
    )(p, p, vt)


BLOCKS_PER_STEP = 4


def _win_key_start(i, tq, s_len):
    return jnp.clip(i * tq - WINDOW, 0, s_len - (tq + 2 * WINDOW))


def _win_mask_tables(tq, n_stack, s_len):
    span = tq + 2 * WINDOW
    n_blocks = s_len // tq
    tabs = []
    for i in (0, 1, n_blocks - 1):
        ws = int(np.clip(i * tq - WINDOW, 0, s_len - span))
        kpos = ws + np.arange(span)[:, None]
        qpos = i * tq + (np.arange(n_stack * tq)[None, :] % tq)
        tabs.append(np.where(np.abs(qpos - kpos) <= WINDOW, 0.0, NEG_INF).astype(np.float32))
    return jnp.asarray(np.stack(tabs))


def _win_attn_kernel(q_ref, k_ref, vt_ref, sink_ref, *rest, s_len, n_ctx):
    mask_refs, o_ref = rest[:BLOCKS_PER_STEP], rest[BLOCKS_PER_STEP]
    tq = q_ref.shape[0] // BLOCKS_PER_STEP
    n_qblocks = q_ref.shape[1] // LANES
    span = tq + 2 * WINDOW
    step = pl.program_id(1)
    k_ctx = k_ref[s_len:s_len + n_ctx, :]
    vt_ctx = vt_ref[:, s_len:s_len + n_ctx]
    sink = sink_ref[...] * LOG2E

    def scores(blk):
        i = BLOCKS_PER_STEP * step + blk
        ws = pl.multiple_of(_win_key_start(i, tq, s_len), LANES)
        qs = _stack_q(q_ref, n_qblocks, blk * tq, tq)
        s_loc = _scores_t(k_ref[pl.ds(ws, span), :], qs) + mask_refs[blk][...]
        return ws, s_loc, _scores_t(k_ctx, qs)

    def finish(blk, ws, s_loc, s_ctx):
        m = jnp.maximum(jnp.maximum(jnp.max(s_loc, axis=0, keepdims=True), jnp.max(s_ctx, axis=0, keepdims=True)),
                        sink)
        p_loc = jnp.exp2(s_loc - m)
        p_ctx = jnp.exp2(s_ctx - m)
        denom = jnp.sum(p_loc, axis=0, keepdims=True) + jnp.sum(p_ctx, axis=0, keepdims=True) + jnp.exp2(sink - m)
        acc = (jnp.dot(vt_ref[:, pl.ds(ws, span)], p_loc.astype(BF16), preferred_element_type=F32)
               + jnp.dot(vt_ctx, p_ctx.astype(BF16), preferred_element_type=F32))
        _merge_pairs(acc / denom, tq, n_qblocks, o_ref, blk * tq)

    pending = [scores(blk) for blk in range(BLOCKS_PER_STEP)]
    for blk, args in enumerate(pending):
        finish(blk, *args)


def _win_attn_call(p, vt, sink_row, *, s_len, n_ctx, q_width, k_blk, v_blk):
    b, n, _ = p.shape
    tq = BLOCKS_PER_STEP * TQ_WIN
    m_rows = sink_row.shape[1]
    n_blocks = s_len // TQ_WIN
    assert n_blocks % BLOCKS_PER_STEP == 0 and n_blocks >= 3
    masks = _win_mask_tables(TQ_WIN, m_rows // TQ_WIN, s_len)
    tbl = lambda i: jnp.where(i == 0, 0, jnp.where(i == n_blocks - 1, 2, 1))
    mask_spec = lambda blk: pl.BlockSpec((None,) + masks.shape[1:],
                                         lambda bb, st: (tbl(BLOCKS_PER_STEP * st + blk), 0, 0))
    return pl.pallas_call(
        functools.partial(_win_attn_kernel, s_len=s_len, n_ctx=n_ctx),
        grid=(b, s_len // tq),
        in_specs=[
            pl.BlockSpec((None, tq, q_width), lambda bb, i: (bb, i, 0)),
            pl.BlockSpec((None, n, LANES), lambda bb, i: (bb, 0, k_blk)),
            pl.BlockSpec((None, None, LANES, n), lambda bb, i: (bb, v_blk, 0, 0)),
            pl.BlockSpec((1, m_rows), lambda bb, i: (0, 0)),
        ] + [mask_spec(blk) for blk in range(BLOCKS_PER_STEP)],
        out_specs=pl.BlockSpec((None, tq, q_width), lambda bb, i: (bb, i, 0)),
        out_shape=jax.ShapeDtypeStruct((b, s_len, q_width), BF16),
        compiler_params=_cparams(2),
        name="win_attn",
    )(p, p, vt, sink_row, *([masks] * BLOCKS_PER_STEP))


def _na_attn_kernel(q_ref, k_ref, vt_ref, *rest, s_len, n_ctx):
    bias_refs, o_ref = rest[:BLOCKS_PER_STEP], rest[BLOCKS_PER_STEP]
    tq = q_ref.shape[0] // BLOCKS_PER_STEP
    slab = NA_SLAB * GRID_W
    rows = s_len // GRID_W
    step = pl.program_id(2)
    k_cx = k_ref[s_len:s_len + n_ctx, :]
    vt_cx = vt_ref[:, s_len:s_len + n_ctx]

    def scores(blk):
        mi = BLOCKS_PER_STEP * step + blk
        ss = pl.multiple_of(jnp.clip(NA_QROWS * mi - NA_ROWS_MAX // 2, 0, rows - NA_SLAB) * GRID_W, LANES)
        qs = _stack_q(q_ref, 1, blk * tq, tq)
        s_nb = _scores_t(k_ref[pl.ds(ss, slab), :], qs) + bias_refs[blk][...]
        return ss, s_nb, _scores_t(k_cx, qs)

    def finish(blk, ss, s_nb, s_cx):
        m = jnp.maximum(jnp.max(s_nb, axis=0, keepdims=True), jnp.max(s_cx, axis=0, keepdims=True))
        p_nb = jnp.exp2(s_nb - m)
        p_cx = jnp.exp2(s_cx - m)
        denom = jnp.sum(p_nb, axis=0, keepdims=True) + jnp.sum(p_cx, axis=0, keepdims=True)
        acc = (jnp.dot(vt_ref[:, pl.ds(ss, slab)], p_nb.astype(BF16), preferred_element_type=F32)
               + jnp.dot(vt_cx, p_cx.astype(BF16), preferred_element_type=F32))
        _merge_pairs(acc / denom, tq, 1, o_ref, blk * tq)

    pending = [scores(blk) for blk in range(BLOCKS_PER_STEP)]
    for blk, args in enumerate(pending):
        finish(blk, *args)


def _na_attn_call(p, vt, bias_tab, *, s_len, n_ctx, n_pairs, q_blk0, k_blk0, v_blk0):
    b, n, _ = p.shape
    tq = NA_QROWS * GRID_W
    n_blocks = s_len // tq
    assert n_blocks % BLOCKS_PER_STEP == 0
    tbl = lambda mi: jnp.where(mi == 0, 0, jnp.where(mi == n_blocks - 1, 2, 1))
    bias_spec = lambda blk: pl.BlockSpec((None, None, NA_SLAB * GRID_W, 2 * tq),
                                         lambda bb, j, st: (tbl(BLOCKS_PER_STEP * st + blk), j, 0, 0))
    return pl.pallas_call(
        functools.partial(_na_attn_kernel, s_len=s_len, n_ctx=n_ctx),
        grid=(b, n_pairs, n_blocks // BLOCKS_PER_STEP),
        in_specs=[
            pl.BlockSpec((None, BLOCKS_PER_STEP * tq, LANES), lambda bb, j, st: (bb, st, q_blk0 + j)),
            pl.BlockSpec((None, n, LANES), lambda bb, j, st: (bb, 0, k_blk0 + j)),
            pl.BlockSpec((None, None, LANES, n), lambda bb, j, st: (bb, v_blk0 + j, 0, 0)),
        ] + [bias_spec(blk) for blk in range(BLOCKS_PER_STEP)],
        out_specs=pl.BlockSpec((None, BLOCKS_PER_STEP * tq, LANES), lambda bb, j, st: (bb, st, j)),
        out_shape=jax.ShapeDtypeStruct((b, s_len, n_pairs * LANES), BF16),
        compiler_params=_cparams(3),
        name="na_attn",
    )(p, p, vt, *([bias_tab] * BLOCKS_PER_STEP))


def _na_bias_tables(rpb, rows):
    n_heads = rpb.shape[0]
    rpb = rpb.astype(F32)
    pad = GRID_W - NA_COLS
    rpb_p = jnp.pad(rpb, ((0, 0), (0, 0), (pad, pad)))
    col_tab = jnp.stack([rpb_p[:, :, pad + NA_COLS - 1 - qc: pad + NA_COLS - 1 - qc + GRID_W] for qc in range(GRID_W)],
                        axis=2)
    qr = np.arange(NA_QROWS)[:, None, None, None]
    qc = np.arange(GRID_W)[None, :, None, None]
    kr = np.arange(NA_SLAB)[None, None, :, None]
    kc = np.arange(GRID_W)[None, None, None, :]
    full = (NA_QROWS, GRID_W, NA_SLAB, GRID_W)
    flat = (NA_QROWS * GRID_W, NA_SLAB * GRID_W)
    tabs = []
    for q0, s0 in ((0, 0), (NA_QROWS, NA_QROWS - NA_ROWS_MAX // 2), (rows - NA_QROWS, rows - NA_SLAB)):
        q_row = q0 + qr
        k_row = s0 + kr
        r0 = np.clip(q_row - NA_ROWS_MAX // 2, 0, rows - NA_ROWS_MAX)
        c0 = np.clip(qc - NA_COLS // 2, 0, GRID_W - NA_COLS)
        valid = (k_row >= r0) & (k_row < r0 + NA_ROWS_MAX) & (kc >= c0) & (kc < c0 + NA_COLS)
        valid = np.broadcast_to(valid, full).reshape(flat)
        r_off = np.clip(k_row - q_row + NA_ROWS_MAX - 1, 0, 2 * NA_ROWS_MAX - 2)[:, 0, :, 0]
        bias = jnp.stack([jnp.stack([col_tab[:, int(r_off[a, c])] for c in range(NA_SLAB)], axis=2)
                          for a in range(NA_QROWS)], axis=1)
        tabs.append(jnp.where(valid[None], bias.reshape((n_heads,) + flat) * LOG2E, NEG_INF))
    tab = jnp.stack(tabs)
    return jnp.swapaxes(tab.reshape(3, n_heads // 2, 2 * flat[0], flat[1]), -1, -2)


def _split_hi_lo(x):
    c = x * 65537.0
    hi = c - (c - x)
    return hi.astype(BF16), (x - hi).astype(BF16)


def _layer_norm(y, g, b):
    mu = jnp.mean(y, axis=-1, keepdims=True)
    yc = y - mu
    var = jnp.mean(yc * yc, axis=-1, keepdims=True)
    return yc * lax.rsqrt(var + LN_EPS) * g + b


def _route(logits):
    lane = _lane_iota().astype(F32)
    big = float(LANES)
    is_g = lane < N_GROUPS
    gl = jnp.where(is_g, logits, NEG_INF)
    g_max = jnp.max(gl, axis=-1, keepdims=True)
    g_idx = jnp.min(jnp.where(gl == g_max, lane, big), axis=-1, keepdims=True)
    g_w = 1.0 / jnp.sum(jnp.where(is_g, jnp.exp(gl - g_max), 0.0), axis=-1, keepdims=True)
    base = N_GROUPS + EXPERTS_PER_GROUP * g_idx
    el = jnp.where((lane >= base) & (lane < base + EXPERTS_PER_GROUP), logits, NEG_INF)
    v1 = jnp.max(el, axis=-1, keepdims=True)
    i1 = jnp.min(jnp.where(el == v1, lane, big), axis=-1, keepdims=True)
    el2 = jnp.where(lane == i1, NEG_INF, el)
    v2 = jnp.max(el2, axis=-1, keepdims=True)
    i2 = jnp.min(jnp.where(el2 == v2, lane, big), axis=-1, keepdims=True)
    t = jnp.exp(v2 - v1)
    w1 = g_w / (1.0 + t)
    w2 = g_w * t / (1.0 + t)
    return jnp.where(lane == 0, i1 - N_GROUPS,
                     jnp.where(lane == 1, i2 - N_GROUPS, jnp.where(lane == 2, w1, jnp.where(lane == 3, w2, 0.0))))


def _outproj_kernel(oal_ref, oac_ref, obl_ref, obc_ref, hl_ref, hc_ref, w_ref, gate_ref, lng_ref, lnb_ref,
                    sh_ref, sc_ref, wr_ref, br_ref, h1_ref, f_ref, route_ref, *, alpha, n_lat):
    ka = oal_ref.shape[1]
    oa = _read_split_stream(oal_ref, oac_ref, n_lat)
    ob = _read_split_stream(obl_ref, obc_ref, n_lat)
    h = _read_split_stream(hl_ref, hc_ref, n_lat)
    o = (jnp.dot(oa, w_ref[:ka, :], preferred_element_type=F32)
         + jnp.dot(ob, w_ref[ka:, :], preferred_element_type=F32))
    h1 = _layer_norm(alpha * h + gate_ref[...] * o, lng_ref[...], lnb_ref[...])
    h1_ref[...] = h1
    f = h1 * sc_ref[...] + sh_ref[...]
    f_ref[:, 0, :] = f
    f_hi, f_lo = _split_hi_lo(f)
    hi_prod = jnp.dot(f_hi, wr_ref[...], preferred_element_type=F32)
    lo_prod = jnp.dot(f_lo, wr_ref[...], preferred_element_type=F32)
    logits = (hi_prod[:, :LANES] + hi_prod[:, LANES:]) + (lo_prod[:, :LANES] + lo_prod[:, LANES:]) + br_ref[...]
    route_ref[...] = _route(logits)


def _outproj_call(oa, ob, h, w16, gate, ln_g, ln_b, shift, scale1p, w_route, b_route, *, n_rows, s_len, alpha):
    b, _, d = h[0].shape
    ka, kb = oa[0].shape[2], ob[0].shape[2]
    n_lat = s_len // TM
    seg = lambda bb, j: (bb, jnp.where(j >= n_lat, 1, 0), 0, 0)
    row = lambda bb, j: (bb, j, 0)
    const = lambda bb, j: (0, 0)
    return pl.pallas_call(
        functools.partial(_outproj_kernel, alpha=alpha, n_lat=n_lat),
        grid=(b, n_rows // TM),
        in_specs=_split_stream_specs(*oa, n_lat) + _split_stream_specs(*ob, n_lat) + _split_stream_specs(*h, n_lat) + [
            pl.BlockSpec((ka + kb, d), const),
            pl.BlockSpec((None, None, 1, d), seg),
            pl.BlockSpec((1, d), const),
            pl.BlockSpec((1, d), const),
            pl.BlockSpec((None, None, 1, d), seg),
            pl.BlockSpec((None, None, 1, d), seg),
            pl.BlockSpec((d, 2 * LANES), const),
            pl.BlockSpec((1, LANES), const),
        ],
        out_specs=[
            pl.BlockSpec((None, TM, d), row),
            pl.BlockSpec((None, TM, 1, d), lambda bb, j: (bb, j, 0, 0)),
            pl.BlockSpec((None, TM, LANES), row),
        ],
        out_shape=[
            jax.ShapeDtypeStruct((b, n_rows, d), F32),
            jax.ShapeDtypeStruct((b, n_rows, 1, d), F32),
            jax.ShapeDtypeStruct((b, n_rows, LANES), F32),
        ],
        compiler_params=_cparams(2),
        name="outproj",
    )(oa[0], oa[1], ob[0], ob[1], h[0], h[1], w16, gate, ln_g, ln_b, shift, scale1p, w_route, b_route)


def _row_copy(src, src_row, dst, dst_row, sem):
    return pltpu.make_async_copy(src.at[pl.ds(src_row, 1), :], dst.at[pl.ds(dst_row, 1), :], sem)


def _block_copy_for_wait(src, dst, sem):
    return pltpu.make_async_copy(src, dst, sem)


MOE_BUFS = 3
DMA_PRIORITIES = 2
MOE_VMEM_LIMIT = 57 * 1024 * 1024


def _moe_kernel(blk_e_ref, blk_b_ref, blk_used_ref, tok_ref, dst_prev_ref, dst_last_ref, f_ref,
                w1_ref, w3_ref, w2_ref, y_hbm, xs, zbuf, ybuf, w1b, w3b, w2b, ssem):
    del blk_b_ref
    i = pl.program_id(0)
    last = pl.num_programs(0) - 1
    cur = lax.rem(i, MOE_BUFS)
    prev = lax.rem(i + 2, MOE_BUFS)
    used = blk_used_ref[i] > 0
    prev_used = jnp.logical_or(i == 0, blk_used_ref[jnp.maximum(i - 1, 0)] > 0)

    def wait_scatter(s):
        _block_copy_for_wait(ybuf.at[s], y_hbm.at[pl.ds(0, MOE_BLOCK), :], ssem.at[s]).wait()

    def start_scatter(dst_ref, s):
        for r in range(MOE_BLOCK):
            _row_copy(ybuf.at[s], r, y_hbm, dst_ref[0, r], ssem.at[s]).start(priority=r % DMA_PRIORITIES)

    def start_zero_fill(dst_ref, s):
        row0 = pl.multiple_of(dst_ref[0, 0], MOE_BLOCK)
        pltpu.make_async_copy(zbuf, y_hbm.at[pl.ds(row0, MOE_BLOCK), :], ssem.at[s]).start()

    @pl.when(i == 0)
    def _():
        ybuf[...] = jnp.zeros(ybuf.shape, ybuf.dtype)
        zbuf[...] = jnp.zeros(zbuf.shape, zbuf.dtype)

    @pl.when(i >= 2)
    def _():
        wait_scatter(cur)

    @pl.when(jnp.logical_or(i == 0, blk_e_ref[i] != blk_e_ref[jnp.maximum(i - 1, 0)]))
    def _():
        w1b[...] = w1_ref[...].astype(BF16)
        w3b[...] = w3_ref[...].astype(BF16)
        w2b[...] = w2_ref[...].astype(BF16)

    @pl.when(used)
    def _():
        for r in range(MOE_BLOCK):
            xs[pl.ds(r, 1), :] = f_ref[tok_ref[0, r]]
        x = xs[...].astype(BF16)
        start_scatter(dst_prev_ref, prev)
        h1 = jnp.dot(x, w1b[...], preferred_element_type=F32)
        h3 = jnp.dot(x, w3b[...], preferred_element_type=F32)
        act = (h1 / (1.0 + jnp.exp(-h1))) * h3
        ybuf[cur] = jnp.dot(act.astype(BF16), w2b[...], preferred_element_type=F32)

    @pl.when(jnp.logical_and(jnp.logical_not(used), prev_used))
    def _():
        start_scatter(dst_prev_ref, prev)

    @pl.when(jnp.logical_and(jnp.logical_not(used), jnp.logical_not(prev_used)))
    def _():
        start_zero_fill(dst_prev_ref, prev)

    @pl.when(i == last)
    def _():
        @pl.when(used)
        def _():
            start_scatter(dst_last_ref, cur)

        @pl.when(jnp.logical_not(used))
        def _():
            start_zero_fill(dst_last_ref, cur)

        for s in range(MOE_BUFS):
            wait_scatter(s)


def _moe_call(f_rows, blk_e, blk_b, blk_used, slot_tok, slot_dst, w1, w3, w2, layer):
    n_batch, per_batch, _, d = f_rows.shape
    n_blocks = blk_e.shape[0]
    n_slots = n_blocks * MOE_BLOCK
    de = w1.shape[3]
    assert n_blocks >= MOE_BUFS
    smem_block = lambda index_map: pl.BlockSpec((None, 1, MOE_BLOCK), index_map, memory_space=pltpu.SMEM)
    grid_spec = pltpu.PrefetchScalarGridSpec(
        num_scalar_prefetch=3,
        grid=(n_blocks,),
        in_specs=[
            smem_block(lambda i, be, bb, bu: (i, 0, 0)),
            smem_block(lambda i, be, bb, bu: (i, 0, 0)),
            smem_block(lambda i, be, bb, bu: (i + 1, 0, 0)),
            pl.BlockSpec((None, per_batch, 1, d), lambda i, be, bb, bu: (bb[i], 0, 0, 0), pipeline_mode=pl.Buffered(1)),
            pl.BlockSpec((None, None, d, de), lambda i, be, bb, bu: (layer, be[i], 0, 0)),
            pl.BlockSpec((None, None, d, de), lambda i, be, bb, bu: (layer, be[i], 0, 0)),
            pl.BlockSpec((None, None, de, d), lambda i, be, bb, bu: (layer, be[i], 0, 0)),
        ],
        out_specs=pl.BlockSpec(memory_space=pl.ANY),
        scratch_shapes=[
            pltpu.VMEM((MOE_BLOCK, d), F32),
            pltpu.VMEM((MOE_BLOCK, d), F32),
            pltpu.VMEM((MOE_BUFS, MOE_BLOCK, d), F32),
            pltpu.VMEM((d, de), BF16),
            pltpu.VMEM((d, de), BF16),
            pltpu.VMEM((de, d), BF16),
            pltpu.SemaphoreType.DMA((MOE_BUFS,)),
        ],
    )
    tok3 = slot_tok.reshape(n_blocks, 1, MOE_BLOCK)
    n_rows = n_slots + MOE_BLOCK
    first = (n_slots + jnp.arange(MOE_BLOCK, dtype=jnp.int32))[None, :]
    dst3 = jnp.concatenate([first, slot_dst], axis=0).reshape(n_blocks + 1, 1, MOE_BLOCK)
    return pl.pallas_call(
        _moe_kernel,
        grid_spec=grid_spec,
        out_shape=jax.ShapeDtypeStruct((n_rows, d), F32),
        compiler_params=pltpu.CompilerParams(dimension_semantics=("arbitrary",), vmem_limit_bytes=MOE_VMEM_LIMIT),
        name="moe_experts",
    )(blk_e, blk_b, blk_used, tok3, dst3, dst3, f_rows, w1, w3, w2)


def _moe_plan(route, t, n_batch):
    n_assign = TOP_K * t
    per_batch = t // n_batch
    n_seg = n_batch * N_EXPERTS
    segs = jnp.arange(n_seg, dtype=jnp.int32)
    e_flat = route[:, :TOP_K].astype(jnp.int32).reshape(n_assign)
    tok_batch = jnp.arange(n_assign, dtype=jnp.int32) // (TOP_K * per_batch)
    seg_flat = tok_batch * N_EXPERTS + e_flat
    order = jnp.argsort(seg_flat, stable=True).astype(jnp.int32)
    counts = jnp.sum((seg_flat[:, None] == segs[None, :]).astype(jnp.int32), axis=0)
    starts = jnp.cumsum(counts) - counts
    padded = (counts + MOE_BLOCK - 1) // MOE_BLOCK * MOE_BLOCK
    p_ends = jnp.cumsum(padded)
    p_starts = p_ends - padded
    n_blocks = (n_assign + MOE_BLOCK - 1) // MOE_BLOCK + n_seg
    blk_start = jnp.arange(n_blocks, dtype=jnp.int32) * MOE_BLOCK
    blk_seg = jnp.minimum(jnp.sum((p_ends[None, :] <= blk_start[:, None]).astype(jnp.int32), axis=1), n_seg - 1)
    blk_b = blk_seg // N_EXPERTS
    of_block = lambda v: jnp.sum(jnp.where(blk_seg[:, None] == segs[None, :], v[None, :], 0), axis=1)[:, None]
    slot = blk_start[:, None] + jnp.arange(MOE_BLOCK, dtype=jnp.int32)[None, :]
    j = slot - of_block(p_starts)
    valid = j < of_block(counts)
    a_idx = order[jnp.clip(of_block(starts) + j, 0, n_assign - 1)]
    slot_tok = jnp.where(valid, a_idx // TOP_K - blk_b[:, None] * per_batch, 0).astype(jnp.int32)
    spare = n_assign + slot - of_block(starts + counts)
    slot_dst = jnp.where(valid, (a_idx % TOP_K) * t + a_idx // TOP_K, spare).astype(jnp.int32)
    blk_used = (blk_start < p_ends[n_seg - 1]).astype(jnp.int32)
    return blk_seg % N_EXPERTS, blk_b, blk_used, slot_tok, slot_dst


def _combine_kernel(h_ref, y0_ref, y1_ref, route_ref, gate_ref, lng_ref, lnb_ref, o_ref, *, alpha):
    r = route_ref[...]
    y = r[:, 2:3] * y0_ref[...] + r[:, 3:4] * y1_ref[...]
    o_ref[...] = _layer_norm(alpha * h_ref[...] + gate_ref[...] * y, lng_ref[...], lnb_ref[...])


def _combine_call(h1, y, route, gate, ln_g, ln_b, *, s_len, alpha):
    b, n_rows, d = h1.shape
    n_lat = s_len // TM
    seg = lambda bb, j: (bb, jnp.where(j >= n_lat, 1, 0), 0, 0)
    row = lambda bb, j: (bb, j, 0)
    const = lambda bb, j: (0, 0)
    blocks_per_batch = n_rows // TM
    blocks_per_k = b * blocks_per_batch
    return pl.pallas_call(
        functools.partial(_combine_kernel, alpha=alpha),
        grid=(b, n_rows // TM),
        in_specs=[
            pl.BlockSpec((None, TM, d), row),
            pl.BlockSpec((TM, d), lambda bb, j: (bb * blocks_per_batch + j, 0)),
            pl.BlockSpec((TM, d), lambda bb, j: (blocks_per_k + bb * blocks_per_batch + j, 0)),
            pl.BlockSpec((None, TM, LANES), row),
            pl.BlockSpec((None, None, 1, d), seg),
            pl.BlockSpec((1, d), const),
            pl.BlockSpec((1, d), const),
        ],
        out_specs=pl.BlockSpec((None, TM, d), row),
        out_shape=jax.ShapeDtypeStruct((b, n_rows, d), F32),
        compiler_params=_cparams(2),
        name="combine",
    )(h1, y, y, route, gate, ln_g, ln_b)


def _pair_heads(w, axis):
    w = jnp.moveaxis(w, axis, -1)
    lead = w.shape[:-1]
    n_heads = w.shape[-1] // HEAD_DIM
    w = w.reshape(lead + (2, n_heads // 2, HEAD_DIM)).swapaxes(-3, -2).reshape(lead + (n_heads * HEAD_DIM,))
    return jnp.moveaxis(w, -1, axis)


def _rope_tables(s_len, n_ctx):
    t = jnp.arange(s_len, dtype=jnp.int32)
    row = (t // GRID_W).astype(F32)
    col = (t % GRID_W).astype(F32)
    n_freq = HEAD_DIM // 4
    inv = ROPE_THETA ** (-jnp.arange(n_freq, dtype=F32) / n_freq)
    ang = jnp.concatenate([row[:, None] * inv, col[:, None] * inv], -1)
    cos = jnp.concatenate([jnp.cos(ang), jnp.ones((n_ctx, HEAD_DIM // 2), F32)], 0)
    sin = jnp.concatenate([jnp.sin(ang), jnp.zeros((n_ctx, HEAD_DIM // 2), F32)], 0)
    return jnp.tile(cos, (1, 4)), jnp.concatenate([-sin, sin, -sin, sin], -1)


def _lambda_init(layer_idx):
    return 0.8 - 0.6 * float(np.exp(-0.3 * layer_idx))


def _router_params(w_group, b_group, w_router, b_router):
    d = w_group.shape[0]
    pad = LANES - N_GROUPS - N_EXPERTS
    w = jnp.concatenate([w_group, w_router, jnp.zeros((d, pad), F32)], axis=1)
    bb = jnp.concatenate([b_group, b_router, jnp.zeros((pad,), F32)])
    w_hi, w_lo = _split_hi_lo(w)
    return jnp.concatenate([w_hi, w_lo], axis=1), bb.reshape(1, LANES)


def _moe_layer(f, route, w1, w3, w2, layer):
    b, n_rows = f.shape[:2]
    t = b * n_rows
    blk_e, blk_b, blk_used, slot_tok, slot_dst = _moe_plan(route.reshape(t, LANES), t, b)
    return _moe_call(f, blk_e, blk_b, blk_used, slot_tok, slot_dst, w1, w3, w2, layer)


def kernel(x, c, ctx, c_ctx, mod_w, mod_b, ln_g, ln_b, ab_w_in, ab_w_out, diff_lambda, diff_subln_g, gqa_qk_g,
           cd_w_in, cd_w_out, win_sink, na_rpb, moe_w_group, moe_b_group, moe_w_router, moe_b_router,
           moe_w1, moe_w3, moe_w2):
    b, s, d = x.shape
    n_ctx = ctx.shape[1]
    n = s + n_ctx
    depth = mod_w.shape[0]
    rows = s // GRID_W
    assert depth == 2, "layer pattern implemented: one differential/GQA layer, then one window/neighbourhood layer"
    assert s % TM == 0 and n_ctx % TM == 0 and s % TK == 0 and b + 1 <= 8
    assert s % TQ_DIFF == 0 and s % TQ_GQA == 0 and n_ctx % TQ_GQA == 0
    assert rows >= NA_SLAB and rows % NA_QROWS == 0 and s % GRID_W == 0
    alpha = (2.0 * depth) ** 0.25
    qw = d // 2
    kvw = qw // 4
    assert qw == 4 * LANES and kvw == LANES

    cos128, sin128 = _rope_tables(s, n_ctx)
    cvec = jnp.concatenate([c, c_ctx[None, :], jnp.zeros((8 - b - 1, d), F32)], axis=0)
    mods_all = _mods_call(cvec, mod_w, mod_b)

    def seg_mods(i):
        m = mods_all[i]
        lat = m[:b].reshape(b, 6, d)
        cm = jnp.broadcast_to(m[b].reshape(1, 6, d), (b, 6, d))
        ms = jnp.stack([lat, cm], axis=1)
        pick = lambda k, one: (ms[:, :, k] + one)[:, :, None, :]
        return pick(0, 0.0), pick(1, 1.0), pick(2, 0.0), pick(3, 0.0), pick(4, 1.0), pick(5, 0.0)

    n_lat = s // TM
    stream0 = (x, ctx, 0)

    shift1, scale1p, gate1, shift2, scale2p, gate2 = seg_mods(0)
    w_in = ab_w_in[0]
    q_d, q_g, k_d, v_d = (w_in[:, k * qw:(k + 1) * qw] for k in range(4))
    k_g, v_g = w_in[:, 4 * qw:4 * qw + kvw], w_in[:, 4 * qw + kvw:]
    w_qk = jnp.concatenate([q_d, _pair_heads(q_g, 1), k_d, k_g], axis=1).astype(BF16)
    roles0 = ((True, None, True),) * 4 + ((True, 0, True),) * 4 + ((True, None, False),) * 4 + ((True, 1, False),)
    w_vt = jnp.concatenate([v_d, v_g], axis=1).T.astype(BF16)
    g128 = jnp.tile(gqa_qk_g[0], (1, 2))
    p0, vt0 = _inproj_call(*stream0, n, shift1, scale1p, w_qk, w_vt, cos128, sin128, g128, roles0, s)
    diff_attn = functools.partial(_diff_attn_call, p0, vt0, diff_lambda[0], diff_subln_g[0], s_len=s, n_ctx=n_ctx,
                                  n_heads=4, q_blk0=0, k_blk0=8, v_blk0=0, lam_init=_lambda_init(0))
    gqa_attn = functools.partial(_gqa_attn_call, p0, vt0, s_len=s, n_ctx=n_ctx, q_width=qw, q_col0=qw,
                                 k_blk=12, v_blk=4)
    o_d = (diff_attn(for_ctx=False), diff_attn(for_ctx=True), 0)
    o_g = (gqa_attn(for_ctx=False), gqa_attn(for_ctx=True), 0)
    w_out = jnp.concatenate([ab_w_out[0][:qw], _pair_heads(ab_w_out[0][qw:], 0)], axis=0).astype(BF16)
    w_route, b_route = _router_params(moe_w_group[0], moe_b_group[0], moe_w_router[0], moe_b_router[0])
    h1, f, route = _outproj_call(o_d, o_g, stream0, w_out, gate1, ln_g[0, 0:1], ln_b[0, 0:1], shift2, scale2p,
                                 w_route, b_route, n_rows=n, s_len=s, alpha=alpha)
    y = _moe_layer(f, route, moe_w1, moe_w3, moe_w2, 0)
    h = _combine_call(h1, y, route, gate2, ln_g[0, 1:2], ln_b[0, 1:2], s_len=s, alpha=alpha)

    shift1, scale1p, gate1, shift2, scale2p, gate2 = seg_mods(1)
    w_in = cd_w_in[0]
    q_w, q_n = w_in[:, :qw], w_in[:, qw:2 * qw]
    k_w, v_w = w_in[:, 2 * qw:2 * qw + kvw], w_in[:, 2 * qw + kvw:2 * qw + 2 * kvw]
    k_n, v_n = w_in[:, 2 * qw + 2 * kvw:3 * qw + 2 * kvw], w_in[:, 3 * qw + 2 * kvw:]
    w_qk = jnp.concatenate([_pair_heads(q_w, 1), q_n, k_w, k_n], axis=1).astype(BF16)
    roles1 = ((True, None, True),) * 4 + ((False, None, True),) * 4 + ((True, None, False),) \
        + ((False, None, False),) * 4
    w_vt = jnp.concatenate([v_w, v_n], axis=1).T.astype(BF16)
    stream1 = (h, h, n_lat)
    p1, vt1 = _inproj_call(*stream1, n, shift1, scale1p, w_qk, w_vt, cos128, sin128, jnp.ones((2, LANES), F32),
                           roles1, s)
    sink_row = jnp.repeat(win_sink[0].reshape(2, 4).T.reshape(8), TQ_WIN).reshape(1, 8 * TQ_WIN)
    o_w = _win_attn_call(p1, vt1, sink_row, s_len=s, n_ctx=n_ctx, q_width=qw, k_blk=8, v_blk=0)
    o_n = _na_attn_call(p1, vt1, _na_bias_tables(na_rpb[0], rows), s_len=s, n_ctx=n_ctx, n_pairs=4,
                        q_blk0=4, k_blk0=9, v_blk0=1)
    w_out = jnp.concatenate([_pair_heads(cd_w_out[0][:qw], 0), cd_w_out[0][qw:]], axis=0).astype(BF16)
    w_route, b_route = _router_params(moe_w_group[1], moe_b_group[1], moe_w_router[1], moe_b_router[1])
    h1, f, route = _outproj_call((o_w, o_w, 0), (o_n, o_n, 0), stream1, w_out, gate1, ln_g[1, 0:1], ln_b[1, 0:1],
                                 shift2, scale2p, w_route, b_route, n_rows=s, s_len=s, alpha=alpha)
    y = _moe_layer(f, route, moe_w1, moe_w3, moe_w2, 1)
    return _combine_call(h1, y, route, gate2, ln_g[1, 1:2], ln_b[1, 1:2], s_len=s, alpha=alpha)
```

```python
import functools

import numpy as np
import jax
import jax.numpy as jnp
from jax import lax
from jax.experimental import pallas as pl
from jax.experimental.pallas import tpu as pltpu

F32 = jnp.float32
BF16 = jnp.bfloat16
HIGHEST = lax.Precision.HIGHEST

GRID_W = 64
HEAD_DIM = 64
ROPE_THETA = 10000.0
WINDOW = 128
NA_ROWS_MAX = 8
NA_COLS = 16
N_GROUPS = 4
EXPERTS_PER_GROUP = 8
N_EXPERTS = N_GROUPS * EXPERTS_PER_GROUP
TOP_K = 2
MOE_BLOCK = 128
LN_EPS = 1e-5
RMS_EPS = 1e-6
ATTN_SCALE = HEAD_DIM ** -0.5
NEG_INF = -1e30
LOG2E = 1.4426950408889634
Q_SCALE = ATTN_SCALE * LOG2E

LANES = 128
PAIR = 2 * HEAD_DIM
assert PAIR == LANES

VMEM_LIMIT = 48 * 1024 * 1024

TM = 256
TQ_DIFF = 512
TQ_GQA = 128
TK = 1024
TQ_WIN = 128
NA_QROWS = 4
NA_SLAB = NA_QROWS + NA_ROWS_MAX


def _cparams(n_axes):
    return pltpu.CompilerParams(dimension_semantics=("arbitrary",) * n_axes, vmem_limit_bytes=VMEM_LIMIT)


def _lane_iota():
    return lax.broadcasted_iota(jnp.int32, (1, LANES), 1)


def _mods_kernel(c_ref, w_ref, b_ref, o_ref):
    c = c_ref[...]
    sc = c / (1.0 + jnp.exp(-c))
    o_ref[...] = jnp.dot(sc, w_ref[...], precision=HIGHEST, preferred_element_type=F32) + b_ref[...]


def _mods_call(cvec, mod_w, mod_b):
    depth, d, n6 = mod_w.shape
    tn = n6 // 4
    return pl.pallas_call(
        _mods_kernel,
        grid=(depth, n6 // tn),
        in_specs=[
            pl.BlockSpec((8, d), lambda l, j: (0, 0)),
            pl.BlockSpec((None, d, tn), lambda l, j: (l, 0, j)),
            pl.BlockSpec((None, 1, tn), lambda l, j: (l, 0, j)),
        ],
        out_specs=pl.BlockSpec((None, 8, tn), lambda l, j: (l, 0, j)),
        out_shape=jax.ShapeDtypeStruct((depth, 8, n6), F32),
        compiler_params=_cparams(2),
        name="mods",
    )(cvec, mod_w, mod_b.reshape(depth, 1, n6))


def _split_stream_specs(lat, ctx, ctx_block0, n_lat):
    width = lat.shape[2]
    return [
        pl.BlockSpec((None, TM, width), lambda bb, j: (bb, jnp.minimum(j, n_lat - 1), 0)),
        pl.BlockSpec((None, TM, width), lambda bb, j: (bb, ctx_block0 + jnp.maximum(j - n_lat, 0), 0)),
    ]


def _read_split_stream(lat_ref, ctx_ref, n_lat):
    return jnp.where(pl.program_id(1) >= n_lat, ctx_ref[...], lat_ref[...])


def _inproj_kernel(hl_ref, hc_ref, sh_ref, sc_ref, w_ref, wvt_ref, cos_ref, sin_ref, g_ref, o_ref, vt_ref, *,
                   roles, n_lat):
    a = (_read_split_stream(hl_ref, hc_ref, n_lat) * sc_ref[...] + sh_ref[...]).astype(BF16)
    vt = lax.dot_general(wvt_ref[...], a, (((1,), (1,)), ((), ())), preferred_element_type=F32)
    for j in range(vt_ref.shape[0]):
        vt_ref[j] = vt[j * LANES:(j + 1) * LANES, :].astype(BF16)
    lane = _lane_iota()
    rope_first = (lane & (HEAD_DIM - 1)) < (HEAD_DIM // 2)
    lo = lane < HEAD_DIM
    n_blocks = len(roles)
    for c0 in range(0, n_blocks, 2):
        width = min(2, n_blocks - c0)
        p = jnp.dot(a, w_ref[:, c0 * LANES:(c0 + width) * LANES], preferred_element_type=F32)
        for half in range(width):
            cb = c0 + half
            rope, norm, scale = roles[cb]
            blk = p[:, half * LANES:(half + 1) * LANES]
            if norm is not None:
                sq = blk * blk
                s_lo = jnp.sum(jnp.where(lo, sq, 0.0), axis=-1, keepdims=True)
                s_hi = jnp.sum(jnp.where(lo, 0.0, sq), axis=-1, keepdims=True)
                ms = jnp.where(lo, s_lo, s_hi) * (1.0 / HEAD_DIM)
                blk = blk * lax.rsqrt(ms + RMS_EPS) * g_ref[norm:norm + 1, :]
            if rope:
                partner = jnp.where(rope_first, pltpu.roll(blk, LANES - HEAD_DIM // 2, 1),
                                    pltpu.roll(blk, HEAD_DIM // 2, 1))
                blk = blk * cos_ref[...] + partner * sin_ref[...]
            if scale:
                blk = blk * Q_SCALE
            o_ref[:, cb * LANES:(cb + 1) * LANES] = blk.astype(BF16)


def _inproj_call(h_lat, h_ctx, ctx_block0, n, shift, scale1p, w16, wvt16, cos128, sin128, g128, roles, s_len):
    b, _, d = h_lat.shape
    ncol = w16.shape[1]
    nvb = wvt16.shape[0] // LANES
    n_lat = s_len // TM
    seg = lambda bb, j: (bb, jnp.where(j >= n_lat, 1, 0), 0, 0)
    return pl.pallas_call(
        functools.partial(_inproj_kernel, roles=roles, n_lat=n_lat),
        grid=(b, n // TM),
        in_specs=_split_stream_specs(h_lat, h_ctx, ctx_block0, n_lat) + [
            pl.BlockSpec((None, None, 1, d), seg),
            pl.BlockSpec((None, None, 1, d), seg),
            pl.BlockSpec((d, ncol), lambda bb, j: (0, 0)),
            pl.BlockSpec((nvb * LANES, d), lambda bb, j: (0, 0)),
            pl.BlockSpec((TM, LANES), lambda bb, j: (j, 0)),
            pl.BlockSpec((TM, LANES), lambda bb, j: (j, 0)),
            pl.BlockSpec((2, LANES), lambda bb, j: (0, 0)),
        ],
        out_specs=[
            pl.BlockSpec((None, TM, ncol), lambda bb, j: (bb, j, 0)),
            pl.BlockSpec((None, nvb, LANES, TM), lambda bb, j: (bb, 0, 0, j)),
        ],
        out_shape=[
            jax.ShapeDtypeStruct((b, n, ncol), BF16),
            jax.ShapeDtypeStruct((b, nvb, LANES, n), BF16),
        ],
        compiler_params=_cparams(2),
        name="inproj",
    )(h_lat, h_ctx, shift, scale1p, w16, wvt16, cos128, sin128, g128)


def _stack_q(q_ref, n_qblocks, row0=0, n_rows=None):
    n_rows = q_ref.shape[0] if n_rows is None else n_rows
    lo = _lane_iota() < HEAD_DIM
    parts = []
    for j in range(n_qblocks):
        q = q_ref[row0:row0 + n_rows, j * LANES:(j + 1) * LANES]
        zero = jnp.zeros_like(q)
        parts.append(jnp.where(lo, q, zero))
        parts.append(jnp.where(lo, zero, q))
    return jnp.concatenate(parts, axis=0)


def _scores_t(k, qs):
    return lax.dot_general(k, qs, (((1,), (1,)), ((), ())), preferred_element_type=F32)


def _flash_init(m_ref, l_ref, acc_ref):
    m_ref[...] = jnp.full(m_ref.shape, NEG_INF, F32)
    l_ref[...] = jnp.zeros(l_ref.shape, F32)
    acc_ref[...] = jnp.zeros(acc_ref.shape, F32)


def _flash_update(st, vt, m_ref, l_ref, acc_ref):
    m_prev = m_ref[...]
    m_new = jnp.maximum(m_prev, jnp.max(st, axis=0, keepdims=True))
    alpha = jnp.exp2(m_prev - m_new)
    pt = jnp.exp2(st - m_new)
    l_ref[...] = alpha * l_ref[...] + jnp.sum(pt, axis=0, keepdims=True)
    acc_ref[...] = alpha * acc_ref[...] + jnp.dot(vt, pt.astype(BF16), preferred_element_type=F32)
    m_ref[...] = m_new


def _flash_sweep(qs, k_ref, vt_ref, m_ref, l_ref, acc_ref, *, s_len, n_ctx):
    _flash_init(m_ref, l_ref, acc_ref)
    bounds = [(c * TK, (c + 1) * TK) for c in range(s_len // TK)] + [(s_len, s_len + n_ctx)]
    s_next = _scores_t(k_ref[bounds[0][0]:bounds[0][1], :], qs)
    for idx, (lo, hi) in enumerate(bounds):
        s_cur = s_next
        if idx + 1 < len(bounds):
            s_next = _scores_t(k_ref[bounds[idx + 1][0]:bounds[idx + 1][1], :], qs)
        _flash_update(s_cur, vt_ref[:, lo:hi], m_ref, l_ref, acc_ref)


def _merge_pairs(ot, tq, n_qblocks, o_ref, row0=0):
    for j in range(n_qblocks):
        o_lo = ot[:HEAD_DIM, (2 * j) * tq:(2 * j + 1) * tq]
        o_hi = ot[HEAD_DIM:, (2 * j + 1) * tq:(2 * j + 2) * tq]
        blk = jnp.concatenate([o_lo, o_hi], axis=0)
        o_ref[row0:row0 + tq, j * LANES:(j + 1) * LANES] = blk.T.astype(o_ref.dtype)


def _diff_attn_kernel(q_ref, k_ref, vt_ref, lam_ref, g_ref, o_ref, m_ref, l_ref, acc_ref, *, s_len, n_ctx, lam_init):
    tq = q_ref.shape[0]
    qs = _stack_q(q_ref, 1)
    _flash_sweep(qs, k_ref, vt_ref, m_ref, l_ref, acc_ref, s_len=s_len, n_ctx=n_ctx)
    lv = lam_ref[...]
    lam = (jnp.exp(jnp.sum(lv[0:1] * lv[1:2], axis=-1, keepdims=True))
           - jnp.exp(jnp.sum(lv[2:3] * lv[3:4], axis=-1, keepdims=True)) + lam_init)
    ot = acc_ref[...] / l_ref[...]
    od = ot[:, :tq] - lam * ot[:, tq:]
    ms = jnp.mean(od * od, axis=0, keepdims=True)
    od = od * lax.rsqrt(ms + RMS_EPS) * g_ref[...] * (1.0 - lam_init)
    o_ref[...] = od.T.astype(o_ref.dtype)


def _query_key_windows(n, s_len, n_ctx, tq, for_ctx):
    if for_ctx:
        assert s_len % n_ctx == 0 and n_ctx % tq == 0
        return n_ctx, s_len // tq, n_ctx, s_len // n_ctx, 0
    assert s_len % tq == 0
    return s_len, 0, n, 0, s_len


def _diff_attn_call(p, vt, lam_vecs, subln_g, *, s_len, n_ctx, n_heads, q_blk0, k_blk0, v_blk0, lam_init, for_ctx):
    b, n, _ = p.shape
    tq = min(TQ_DIFF, n_ctx) if for_ctx else TQ_DIFF
    n_q, q_i0, n_k, k_i, s_keys = _query_key_windows(n, s_len, n_ctx, tq, for_ctx)
    return pl.pallas_call(
        functools.partial(_diff_attn_kernel, s_len=s_keys, n_ctx=n_ctx, lam_init=lam_init),
        grid=(b, n_heads, n_q // tq),
        in_specs=[
            pl.BlockSpec((None, tq, LANES), lambda bb, hh, i: (bb, q_i0 + i, q_blk0 + hh)),
            pl.BlockSpec((None, n_k, LANES), lambda bb, hh, i: (bb, k_i, k_blk0 + hh)),
            pl.BlockSpec((None, None, LANES, n_k), lambda bb, hh, i: (bb, v_blk0 + hh, 0, k_i)),
            pl.BlockSpec((4, HEAD_DIM), lambda bb, hh, i: (0, 0)),
            pl.BlockSpec((LANES, 1), lambda bb, hh, i: (0, 0)),
        ],
        out_specs=pl.BlockSpec((None, tq, LANES), lambda bb, hh, i: (bb, i, hh)),
        out_shape=jax.ShapeDtypeStruct((b, n_q, n_heads * LANES), BF16),
        scratch_shapes=[pltpu.VMEM((1, 2 * tq), F32), pltpu.VMEM((1, 2 * tq), F32), pltpu.VMEM((LANES, 2 * tq), F32)],
        compiler_params=_cparams(3),
        name="diff_attn",
    )(p, p, vt, lam_vecs, subln_g.reshape(LANES, 1))


def _gqa_attn_kernel(q_ref, k_ref, vt_ref, o_ref, m_ref, l_ref, acc_ref, *, s_len, n_ctx):
    tq = q_ref.shape[0]
    n_qblocks = q_ref.shape[1] // LANES
    qs = _stack_q(q_ref, n_qblocks)
    _flash_sweep(qs, k_ref, vt_ref, m_ref, l_ref, acc_ref, s_len=s_len, n_ctx=n_ctx)
    _merge_pairs(acc_ref[...] / l_ref[...], tq, n_qblocks, o_ref)


def _gqa_attn_call(p, vt, *, s_len, n_ctx, q_width, q_col0, k_blk, v_blk, for_ctx):
    b, n, _ = p.shape
    tq = TQ_GQA
    n_q, q_i0, n_k, k_i, s_keys = _query_key_windows(n, s_len, n_ctx, tq, for_ctx)
    m_rows = 2 * (q_width // LANES) * tq
    return pl.pallas_call(
        functools.partial(_gqa_attn_kernel, s_len=s_keys, n_ctx=n_ctx),
        grid=(b, n_q // tq),
        in_specs=[
            pl.BlockSpec((None, tq, q_width), lambda bb, i: (bb, q_i0 + i, q_col0 // q_width)),
            pl.BlockSpec((None, n_k, LANES), lambda bb, i: (bb, k_i, k_blk)),
            pl.BlockSpec((None, None, LANES, n_k), lambda bb, i: (bb, v_blk, 0, k_i)),
        ],
        out_specs=pl.BlockSpec((None, tq, q_width), lambda bb, i: (bb, i, 0)),
        out_shape=jax.ShapeDtypeStruct((b, n_q, q_width), BF16),
        scratch_shapes=[pltpu.VMEM((1, m_rows), F32), pltpu.VMEM((1, m_rows), F32), pltpu.VMEM((LANES, m_rows), F32)],
        compiler_params=_cparams(2),
        name="gqa_attn",
    )(p, p, vt)


BLOCKS_PER_STEP = 4


def _win_key_start(i, tq, s_len, clip=jnp.clip):
    return clip(i * tq - WINDOW, 0, s_len - (tq + 2 * WINDOW))


def _win_mask_tables(tq, n_stack, s_len):
    span = tq + 2 * WINDOW
    n_blocks = s_len // tq
    tabs = []
    for i in (0, 1, n_blocks - 1):
        kpos = int(_win_key_start(i, tq, s_len, np.clip)) + np.arange(span)[:, None]
        qpos = i * tq + (np.arange(n_stack * tq)[None, :] % tq)
        tabs.append(np.where(np.abs(qpos - kpos) <= WINDOW, 0.0, NEG_INF).astype(np.float32))
    return jnp.asarray(np.stack(tabs))


def _win_attn_kernel(q_ref, k_ref, vt_ref, sink_ref, *rest, s_len, n_ctx):
    mask_refs, o_ref = rest[:BLOCKS_PER_STEP], rest[BLOCKS_PER_STEP]
    tq = q_ref.shape[0] // BLOCKS_PER_STEP
    n_qblocks = q_ref.shape[1] // LANES
    span = tq + 2 * WINDOW
    step = pl.program_id(1)
    k_ctx = k_ref[s_len:s_len + n_ctx, :]
    vt_ctx = vt_ref[:, s_len:s_len + n_ctx]
    sink = sink_ref[...] * LOG2E

    def scores(blk):
        i = BLOCKS_PER_STEP * step + blk
        ws = pl.multiple_of(_win_key_start(i, tq, s_len), LANES)
        qs = _stack_q(q_ref, n_qblocks, blk * tq, tq)
        s_loc = _scores_t(k_ref[pl.ds(ws, span), :], qs) + mask_refs[blk][...]
        return ws, s_loc, _scores_t(k_ctx, qs)

    def finish(blk, ws, s_loc, s_ctx):
        m = jnp.maximum(jnp.maximum(jnp.max(s_loc, axis=0, keepdims=True), jnp.max(s_ctx, axis=0, keepdims=True)),
                        sink)
        p_loc = jnp.exp2(s_loc - m)
        p_ctx = jnp.exp2(s_ctx - m)
        denom = jnp.sum(p_loc, axis=0, keepdims=True) + jnp.sum(p_ctx, axis=0, keepdims=True) + jnp.exp2(sink - m)
        acc = (jnp.dot(vt_ref[:, pl.ds(ws, span)], p_loc.astype(BF16), preferred_element_type=F32)
               + jnp.dot(vt_ctx, p_ctx.astype(BF16), preferred_element_type=F32))
        _merge_pairs(acc / denom, tq, n_qblocks, o_ref, blk * tq)

    pending = [scores(blk) for blk in range(BLOCKS_PER_STEP)]
    for blk, args in enumerate(pending):
        finish(blk, *args)


def _win_attn_call(p, vt, sink_row, *, s_len, n_ctx, q_width, k_blk, v_blk):
    b, n, _ = p.shape
    tq = BLOCKS_PER_STEP * TQ_WIN
    m_rows = sink_row.shape[1]
    n_blocks = s_len // TQ_WIN
    assert n_blocks % BLOCKS_PER_STEP == 0 and n_blocks >= 3
    masks = _win_mask_tables(TQ_WIN, m_rows // TQ_WIN, s_len)
    tbl = lambda i: jnp.where(i == 0, 0, jnp.where(i == n_blocks - 1, 2, 1))
    mask_spec = lambda blk: pl.BlockSpec((None,) + masks.shape[1:],
                                         lambda bb, st: (tbl(BLOCKS_PER_STEP * st + blk), 0, 0))
    return pl.pallas_call(
        functools.partial(_win_attn_kernel, s_len=s_len, n_ctx=n_ctx),
        grid=(b, s_len // tq),
        in_specs=[
            pl.BlockSpec((None, tq, q_width), lambda bb, i: (bb, i, 0)),
            pl.BlockSpec((None, n, LANES), lambda bb, i: (bb, 0, k_blk)),
            pl.BlockSpec((None, None, LANES, n), lambda bb, i: (bb, v_blk, 0, 0)),
            pl.BlockSpec((1, m_rows), lambda bb, i: (0, 0)),
        ] + [mask_spec(blk) for blk in range(BLOCKS_PER_STEP)],
        out_specs=pl.BlockSpec((None, tq, q_width), lambda bb, i: (bb, i, 0)),
        out_shape=jax.ShapeDtypeStruct((b, s_len, q_width), BF16),
        compiler_params=_cparams(2),
        name="win_attn",
    )(p, p, vt, sink_row, *([masks] * BLOCKS_PER_STEP))


def _na_attn_kernel(q_ref, k_ref, vt_ref, *rest, s_len, n_ctx):
    bias_refs, o_ref = rest[:BLOCKS_PER_STEP], rest[BLOCKS_PER_STEP]
    tq = q_ref.shape[0] // BLOCKS_PER_STEP
    slab = NA_SLAB * GRID_W
    rows = s_len // GRID_W
    step = pl.program_id(2)
    k_cx = k_ref[s_len:s_len + n_ctx, :]
    vt_cx = vt_ref[:, s_len:s_len + n_ctx]

    def scores(blk):
        mi = BLOCKS_PER_STEP * step + blk
        ss = pl.multiple_of(jnp.clip(NA_QROWS * mi - NA_ROWS_MAX // 2, 0, rows - NA_SLAB) * GRID_W, LANES)
        qs = _stack_q(q_ref, 1, blk * tq, tq)
        s_nb = _scores_t(k_ref[pl.ds(ss, slab), :], qs) + bias_refs[blk][...]
        return ss, s_nb, _scores_t(k_cx, qs)

    def finish(blk, ss, s_nb, s_cx):
        m = jnp.maximum(jnp.max(s_nb, axis=0, keepdims=True), jnp.max(s_cx, axis=0, keepdims=True))
        p_nb = jnp.exp2(s_nb - m)
        p_cx = jnp.exp2(s_cx - m)
        denom = jnp.sum(p_nb, axis=0, keepdims=True) + jnp.sum(p_cx, axis=0, keepdims=True)
        acc = (jnp.dot(vt_ref[:, pl.ds(ss, slab)], p_nb.astype(BF16), preferred_element_type=F32)
               + jnp.dot(vt_cx, p_cx.astype(BF16), preferred_element_type=F32))
        _merge_pairs(acc / denom, tq, 1, o_ref, blk * tq)

    pending = [scores(blk) for blk in range(BLOCKS_PER_STEP)]
    for blk, args in enumerate(pending):
        finish(blk, *args)


def _na_attn_call(p, vt, bias_tab, *, s_len, n_ctx, n_pairs, q_blk0, k_blk0, v_blk0):
    b, n, _ = p.shape
    tq = NA_QROWS * GRID_W
    n_blocks = s_len // tq
    assert n_blocks % BLOCKS_PER_STEP == 0
    tbl = lambda mi: jnp.where(mi == 0, 0, jnp.where(mi == n_blocks - 1, 2, 1))
    bias_spec = lambda blk: pl.BlockSpec((None, None, NA_SLAB * GRID_W, 2 * tq),
                                         lambda bb, j, st: (tbl(BLOCKS_PER_STEP * st + blk), j, 0, 0))
    return pl.pallas_call(
        functools.partial(_na_attn_kernel, s_len=s_len, n_ctx=n_ctx),
        grid=(b, n_pairs, n_blocks // BLOCKS_PER_STEP),
        in_specs=[
            pl.BlockSpec((None, BLOCKS_PER_STEP * tq, LANES), lambda bb, j, st: (bb, st, q_blk0 + j)),
            pl.BlockSpec((None, n, LANES), lambda bb, j, st: (bb, 0, k_blk0 + j)),
            pl.BlockSpec((None, None, LANES, n), lambda bb, j, st: (bb, v_blk0 + j, 0, 0)),
        ] + [bias_spec(blk) for blk in range(BLOCKS_PER_STEP)],
        out_specs=pl.BlockSpec((None, BLOCKS_PER_STEP * tq, LANES), lambda bb, j, st: (bb, st, j)),
        out_shape=jax.ShapeDtypeStruct((b, s_len, n_pairs * LANES), BF16),
        compiler_params=_cparams(3),
        name="na_attn",
    )(p, p, vt, *([bias_tab] * BLOCKS_PER_STEP))


def _na_bias_tables(rpb, rows):
    n_heads = rpb.shape[0]
    rpb = rpb.astype(F32)
    pad = GRID_W - NA_COLS
    rpb_p = jnp.pad(rpb, ((0, 0), (0, 0), (pad, pad)))
    col_tab = jnp.stack([rpb_p[:, :, pad + NA_COLS - 1 - qc: pad + NA_COLS - 1 - qc + GRID_W] for qc in range(GRID_W)],
                        axis=2)
    qr = np.arange(NA_QROWS)[:, None, None, None]
    qc = np.arange(GRID_W)[None, :, None, None]
    kr = np.arange(NA_SLAB)[None, None, :, None]
    kc = np.arange(GRID_W)[None, None, None, :]
    full = (NA_QROWS, GRID_W, NA_SLAB, GRID_W)
    flat = (NA_QROWS * GRID_W, NA_SLAB * GRID_W)
    tabs = []
    for q0, s0 in ((0, 0), (NA_QROWS, NA_QROWS - NA_ROWS_MAX // 2), (rows - NA_QROWS, rows - NA_SLAB)):
        q_row = q0 + qr
        k_row = s0 + kr
        r0 = np.clip(q_row - NA_ROWS_MAX // 2, 0, rows - NA_ROWS_MAX)
        c0 = np.clip(qc - NA_COLS // 2, 0, GRID_W - NA_COLS)
        valid = (k_row >= r0) & (k_row < r0 + NA_ROWS_MAX) & (kc >= c0) & (kc < c0 + NA_COLS)
        valid = np.broadcast_to(valid, full).reshape(flat)
        r_off = np.clip(k_row - q_row + NA_ROWS_MAX - 1, 0, 2 * NA_ROWS_MAX - 2)[:, 0, :, 0]
        bias = jnp.stack([jnp.stack([col_tab[:, int(r_off[a, c])] for c in range(NA_SLAB)], axis=2)
                          for a in range(NA_QROWS)], axis=1)
        tabs.append(jnp.where(valid[None], bias.reshape((n_heads,) + flat) * LOG2E, NEG_INF))
    tab = jnp.stack(tabs)
    return jnp.swapaxes(tab.reshape(3, n_heads // 2, 2 * flat[0], flat[1]), -1, -2)


def _split_hi_lo(x):
    c = x * 65537.0
    hi = c - (c - x)
    return hi.astype(BF16), (x - hi).astype(BF16)


def _layer_norm(y, g, b):
    mu = jnp.mean(y, axis=-1, keepdims=True)
    yc = y - mu
    var = jnp.mean(yc * yc, axis=-1, keepdims=True)
    return yc * lax.rsqrt(var + LN_EPS) * g + b


def _route(logits):
    lane = _lane_iota().astype(F32)
    big = float(LANES)
    is_g = lane < N_GROUPS
    gl = jnp.where(is_g, logits, NEG_INF)
    g_max = jnp.max(gl, axis=-1, keepdims=True)
    g_idx = jnp.min(jnp.where(gl == g_max, lane, big), axis=-1, keepdims=True)
    g_w = 1.0 / jnp.sum(jnp.where(is_g, jnp.exp(gl - g_max), 0.0), axis=-1, keepdims=True)
    base = N_GROUPS + EXPERTS_PER_GROUP * g_idx
    el = jnp.where((lane >= base) & (lane < base + EXPERTS_PER_GROUP), logits, NEG_INF)
    v1 = jnp.max(el, axis=-1, keepdims=True)
    i1 = jnp.min(jnp.where(el == v1, lane, big), axis=-1, keepdims=True)
    el2 = jnp.where(lane == i1, NEG_INF, el)
    v2 = jnp.max(el2, axis=-1, keepdims=True)
    i2 = jnp.min(jnp.where(el2 == v2, lane, big), axis=-1, keepdims=True)
    t = jnp.exp(v2 - v1)
    w1 = g_w / (1.0 + t)
    w2 = g_w * t / (1.0 + t)
    return jnp.where(lane == 0, i1 - N_GROUPS,
                     jnp.where(lane == 1, i2 - N_GROUPS, jnp.where(lane == 2, w1, jnp.where(lane == 3, w2, 0.0))))


def _outproj_kernel(oal_ref, oac_ref, obl_ref, obc_ref, hl_ref, hc_ref, w_ref, gate_ref, lng_ref, lnb_ref,
                    sh_ref, sc_ref, wr_ref, br_ref, h1_ref, f_ref, route_ref, *, alpha, n_lat):
    ka = oal_ref.shape[1]
    oa = _read_split_stream(oal_ref, oac_ref, n_lat)
    ob = _read_split_stream(obl_ref, obc_ref, n_lat)
    h = _read_split_stream(hl_ref, hc_ref, n_lat)
    o = (jnp.dot(oa, w_ref[:ka, :], preferred_element_type=F32)
         + jnp.dot(ob, w_ref[ka:, :], preferred_element_type=F32))
    h1 = _layer_norm(alpha * h + gate_ref[...] * o, lng_ref[...], lnb_ref[...])
    h1_ref[...] = h1
    f = h1 * sc_ref[...] + sh_ref[...]
    f_ref[:, 0, :] = f
    f_hi, f_lo = _split_hi_lo(f)
    hi_prod = jnp.dot(f_hi, wr_ref[...], preferred_element_type=F32)
    lo_prod = jnp.dot(f_lo, wr_ref[...], preferred_element_type=F32)
    logits = (hi_prod[:, :LANES] + hi_prod[:, LANES:]) + (lo_prod[:, :LANES] + lo_prod[:, LANES:]) + br_ref[...]
    route_ref[...] = _route(logits)


def _outproj_call(oa, ob, h, w16, gate, ln_g, ln_b, shift, scale1p, w_route, b_route, *, n_rows, s_len, alpha):
    b, _, d = h[0].shape
    ka, kb = oa[0].shape[2], ob[0].shape[2]
    n_lat = s_len // TM
    seg = lambda bb, j: (bb, jnp.where(j >= n_lat, 1, 0), 0, 0)
    row = lambda bb, j: (bb, j, 0)
    const = lambda bb, j: (0, 0)
    return pl.pallas_call(
        functools.partial(_outproj_kernel, alpha=alpha, n_lat=n_lat),
        grid=(b, n_rows // TM),
        in_specs=_split_stream_specs(*oa, n_lat) + _split_stream_specs(*ob, n_lat) + _split_stream_specs(*h, n_lat) + [
            pl.BlockSpec((ka + kb, d), const),
            pl.BlockSpec((None, None, 1, d), seg),
            pl.BlockSpec((1, d), const),
            pl.BlockSpec((1, d), const),
            pl.BlockSpec((None, None, 1, d), seg),
            pl.BlockSpec((None, None, 1, d), seg),
            pl.BlockSpec((d, 2 * LANES), const),
            pl.BlockSpec((1, LANES), const),
        ],
        out_specs=[
            pl.BlockSpec((None, TM, d), row),
            pl.BlockSpec((None, TM, 1, d), lambda bb, j: (bb, j, 0, 0)),
            pl.BlockSpec((None, TM, LANES), row),
        ],
        out_shape=[
            jax.ShapeDtypeStruct((b, n_rows, d), F32),
            jax.ShapeDtypeStruct((b, n_rows, 1, d), F32),
            jax.ShapeDtypeStruct((b, n_rows, LANES), F32),
        ],
        compiler_params=_cparams(2),
        name="outproj",
    )(oa[0], oa[1], ob[0], ob[1], h[0], h[1], w16, gate, ln_g, ln_b, shift, scale1p, w_route, b_route)


def _row_copy(src, src_row, dst, dst_row, sem):
    return pltpu.make_async_copy(src.at[pl.ds(src_row, 1), :], dst.at[pl.ds(dst_row, 1), :], sem)


def _block_copy_for_wait(src, dst, sem):
    return pltpu.make_async_copy(src, dst, sem)


MOE_BUFS = 3
DMA_PRIORITIES = 2
MOE_VMEM_LIMIT = 57 * 1024 * 1024


def _moe_kernel(blk_e_ref, blk_b_ref, blk_used_ref, tok_ref, dst_prev_ref, dst_last_ref, f_ref,
                w1_ref, w3_ref, w2_ref, y_hbm, xs, zbuf, ybuf, w1b, w3b, w2b, ssem):
    del blk_b_ref
    i = pl.program_id(0)
    last = pl.num_programs(0) - 1
    cur = lax.rem(i, MOE_BUFS)
    prev = lax.rem(i + 2, MOE_BUFS)
    used = blk_used_ref[i] > 0
    prev_used = jnp.logical_or(i == 0, blk_used_ref[jnp.maximum(i - 1, 0)] > 0)

    def wait_scatter(s):
        _block_copy_for_wait(ybuf.at[s], y_hbm.at[pl.ds(0, MOE_BLOCK), :], ssem.at[s]).wait()

    def start_scatter(dst_ref, s):
        for r in range(MOE_BLOCK):
            _row_copy(ybuf.at[s], r, y_hbm, dst_ref[0, r], ssem.at[s]).start(priority=r % DMA_PRIORITIES)

    def start_zero_fill(dst_ref, s):
        row0 = pl.multiple_of(dst_ref[0, 0], MOE_BLOCK)
        pltpu.make_async_copy(zbuf, y_hbm.at[pl.ds(row0, MOE_BLOCK), :], ssem.at[s]).start()

    @pl.when(i == 0)
    def _():
        ybuf[...] = jnp.zeros(ybuf.shape, ybuf.dtype)
        zbuf[...] = jnp.zeros(zbuf.shape, zbuf.dtype)

    @pl.when(i >= 2)
    def _():
        wait_scatter(cur)

    @pl.when(jnp.logical_or(i == 0, blk_e_ref[i] != blk_e_ref[jnp.maximum(i - 1, 0)]))
    def _():
        w1b[...] = w1_ref[...].astype(BF16)
        w3b[...] = w3_ref[...].astype(BF16)
        w2b[...] = w2_ref[...].astype(BF16)

    @pl.when(used)
    def _():
        for r in range(MOE_BLOCK):
            xs[pl.ds(r, 1), :] = f_ref[tok_ref[0, r]]
        x = xs[...].astype(BF16)
        start_scatter(dst_prev_ref, prev)
        h1 = jnp.dot(x, w1b[...], preferred_element_type=F32)
        h3 = jnp.dot(x, w3b[...], preferred_element_type=F32)
        act = (h1 / (1.0 + jnp.exp(-h1))) * h3
        ybuf[cur] = jnp.dot(act.astype(BF16), w2b[...], preferred_element_type=F32)

    @pl.when(jnp.logical_and(jnp.logical_not(used), prev_used))
    def _():
        start_scatter(dst_prev_ref, prev)

    @pl.when(jnp.logical_and(jnp.logical_not(used), jnp.logical_not(prev_used)))
    def _():
        start_zero_fill(dst_prev_ref, prev)

    @pl.when(i == last)
    def _():
        @pl.when(used)
        def _():
            start_scatter(dst_last_ref, cur)

        @pl.when(jnp.logical_not(used))
        def _():
            start_zero_fill(dst_last_ref, cur)

        for s in range(MOE_BUFS):
            wait_scatter(s)


def _moe_call(f_rows, blk_e, blk_b, blk_used, slot_tok, slot_dst, w1, w3, w2, layer):
    n_batch, per_batch, _, d = f_rows.shape
    n_blocks = blk_e.shape[0]
    n_slots = n_blocks * MOE_BLOCK
    de = w1.shape[3]
    assert n_blocks >= MOE_BUFS
    smem_block = lambda index_map: pl.BlockSpec((None, 1, MOE_BLOCK), index_map, memory_space=pltpu.SMEM)
    grid_spec = pltpu.PrefetchScalarGridSpec(
        num_scalar_prefetch=3,
        grid=(n_blocks,),
        in_specs=[
            smem_block(lambda i, be, bb, bu: (i, 0, 0)),
            smem_block(lambda i, be, bb, bu: (i, 0, 0)),
            smem_block(lambda i, be, bb, bu: (i + 1, 0, 0)),
            pl.BlockSpec((None, per_batch, 1, d), lambda i, be, bb, bu: (bb[i], 0, 0, 0), pipeline_mode=pl.Buffered(1)),
            pl.BlockSpec((None, None, d, de), lambda i, be, bb, bu: (layer, be[i], 0, 0)),
            pl.BlockSpec((None, None, d, de), lambda i, be, bb, bu: (layer, be[i], 0, 0)),
            pl.BlockSpec((None, None, de, d), lambda i, be, bb, bu: (layer, be[i], 0, 0)),
        ],
        out_specs=pl.BlockSpec(memory_space=pl.ANY),
        scratch_shapes=[
            pltpu.VMEM((MOE_BLOCK, d), F32),
            pltpu.VMEM((MOE_BLOCK, d), F32),
            pltpu.VMEM((MOE_BUFS, MOE_BLOCK, d), F32),
            pltpu.VMEM((d, de), BF16),
            pltpu.VMEM((d, de), BF16),
            pltpu.VMEM((de, d), BF16),
            pltpu.SemaphoreType.DMA((MOE_BUFS,)),
        ],
    )
    tok3 = slot_tok.reshape(n_blocks, 1, MOE_BLOCK)
    n_rows = n_slots + MOE_BLOCK
    first = (n_slots + jnp.arange(MOE_BLOCK, dtype=jnp.int32))[None, :]
    dst3 = jnp.concatenate([first, slot_dst], axis=0).reshape(n_blocks + 1, 1, MOE_BLOCK)
    return pl.pallas_call(
        _moe_kernel,
        grid_spec=grid_spec,
        out_shape=jax.ShapeDtypeStruct((n_rows, d), F32),
        compiler_params=pltpu.CompilerParams(dimension_semantics=("arbitrary",), vmem_limit_bytes=MOE_VMEM_LIMIT),
        name="moe_experts",
    )(blk_e, blk_b, blk_used, tok3, dst3, dst3, f_rows, w1, w3, w2)


def _moe_plan(route, t, n_batch):
    n_assign = TOP_K * t
    per_batch = t // n_batch
    n_seg = n_batch * N_EXPERTS
    segs = jnp.arange(n_seg, dtype=jnp.int32)
    e_flat = route[:, :TOP_K].astype(jnp.int32).reshape(n_assign)
    tok_batch = jnp.arange(n_assign, dtype=jnp.int32) // (TOP_K * per_batch)
    seg_flat = tok_batch * N_EXPERTS + e_flat
    order = jnp.argsort(seg_flat, stable=True).astype(jnp.int32)
    counts = jnp.sum((seg_flat[:, None] == segs[None, :]).astype(jnp.int32), axis=0)
    starts = jnp.cumsum(counts) - counts
    padded = (counts + MOE_BLOCK - 1) // MOE_BLOCK * MOE_BLOCK
    p_ends = jnp.cumsum(padded)
    p_starts = p_ends - padded
    n_blocks = (n_assign + MOE_BLOCK - 1) // MOE_BLOCK + n_seg
    blk_start = jnp.arange(n_blocks, dtype=jnp.int32) * MOE_BLOCK
    blk_seg = jnp.minimum(jnp.sum((p_ends[None, :] <= blk_start[:, None]).astype(jnp.int32), axis=1), n_seg - 1)
    blk_b = blk_seg // N_EXPERTS
    of_block = lambda v: jnp.sum(jnp.where(blk_seg[:, None] == segs[None, :], v[None, :], 0), axis=1)[:, None]
    slot = blk_start[:, None] + jnp.arange(MOE_BLOCK, dtype=jnp.int32)[None, :]
    j = slot - of_block(p_starts)
    valid = j < of_block(counts)
    a_idx = order[jnp.clip(of_block(starts) + j, 0, n_assign - 1)]
    slot_tok = jnp.where(valid, a_idx // TOP_K - blk_b[:, None] * per_batch, 0).astype(jnp.int32)
    spare = n_assign + slot - of_block(starts + counts)
    slot_dst = jnp.where(valid, (a_idx % TOP_K) * t + a_idx // TOP_K, spare).astype(jnp.int32)
    blk_used = (blk_start < p_ends[n_seg - 1]).astype(jnp.int32)
    return blk_seg % N_EXPERTS, blk_b, blk_used, slot_tok, slot_dst


def _combine_kernel(h_ref, y0_ref, y1_ref, route_ref, gate_ref, lng_ref, lnb_ref, o_ref, *, alpha):
    r = route_ref[...]
    y = r[:, 2:3] * y0_ref[...] + r[:, 3:4] * y1_ref[...]
    o_ref[...] = _layer_norm(alpha * h_ref[...] + gate_ref[...] * y, lng_ref[...], lnb_ref[...])


def _combine_call(h1, y, route, gate, ln_g, ln_b, *, s_len, alpha):
    b, n_rows, d = h1.shape
    n_lat = s_len // TM
    seg = lambda bb, j: (bb, jnp.where(j >= n_lat, 1, 0), 0, 0)
    row = lambda bb, j: (bb, j, 0)
    const = lambda bb, j: (0, 0)
    blocks_per_batch = n_rows // TM
    blocks_per_k = b * blocks_per_batch
    return pl.pallas_call(
        functools.partial(_combine_kernel, alpha=alpha),
        grid=(b, n_rows // TM),
        in_specs=[
            pl.BlockSpec((None, TM, d), row),
            pl.BlockSpec((TM, d), lambda bb, j: (bb * blocks_per_batch + j, 0)),
            pl.BlockSpec((TM, d), lambda bb, j: (blocks_per_k + bb * blocks_per_batch + j, 0)),
            pl.BlockSpec((None, TM, LANES), row),
            pl.BlockSpec((None, None, 1, d), seg),
            pl.BlockSpec((1, d), const),
            pl.BlockSpec((1, d), const),
        ],
        out_specs=pl.BlockSpec((None, TM, d), row),
        out_shape=jax.ShapeDtypeStruct((b, n_rows, d), F32),
        compiler_params=_cparams(2),
        name="combine",
    )(h1, y, y, route, gate, ln_g, ln_b)


def _pair_heads(w, axis):
    w = jnp.moveaxis(w, axis, -1)
    lead = w.shape[:-1]
    n_heads = w.shape[-1] // HEAD_DIM
    w = w.reshape(lead + (2, n_heads // 2, HEAD_DIM)).swapaxes(-3, -2).reshape(lead + (n_heads * HEAD_DIM,))
    return jnp.moveaxis(w, -1, axis)


def _rope_tables(s_len, n_ctx):
    t = jnp.arange(s_len, dtype=jnp.int32)
    row = (t // GRID_W).astype(F32)
    col = (t % GRID_W).astype(F32)
    n_freq = HEAD_DIM // 4
    inv = ROPE_THETA ** (-jnp.arange(n_freq, dtype=F32) / n_freq)
    ang = jnp.concatenate([row[:, None] * inv, col[:, None] * inv], -1)
    cos = jnp.concatenate([jnp.cos(ang), jnp.ones((n_ctx, HEAD_DIM // 2), F32)], 0)
    sin = jnp.concatenate([jnp.sin(ang), jnp.zeros((n_ctx, HEAD_DIM // 2), F32)], 0)
    return jnp.tile(cos, (1, 4)), jnp.concatenate([-sin, sin, -sin, sin], -1)


def _lambda_init(layer_idx):
    return 0.8 - 0.6 * float(np.exp(-0.3 * layer_idx))


def _router_params(w_group, b_group, w_router, b_router):
    d = w_group.shape[0]
    pad = LANES - N_GROUPS - N_EXPERTS
    w = jnp.concatenate([w_group, w_router, jnp.zeros((d, pad), F32)], axis=1)
    bb = jnp.concatenate([b_group, b_router, jnp.zeros((pad,), F32)])
    w_hi, w_lo = _split_hi_lo(w)
    return jnp.concatenate([w_hi, w_lo], axis=1), bb.reshape(1, LANES)


def _moe_layer(f, route, w1, w3, w2, layer):
    b, n_rows = f.shape[:2]
    t = b * n_rows
    blk_e, blk_b, blk_used, slot_tok, slot_dst = _moe_plan(route.reshape(t, LANES), t, b)
    return _moe_call(f, blk_e, blk_b, blk_used, slot_tok, slot_dst, w1, w3, w2, layer)


def kernel(x, c, ctx, c_ctx, mod_w, mod_b, ln_g, ln_b, ab_w_in, ab_w_out, diff_lambda, diff_subln_g, gqa_qk_g,
           cd_w_in, cd_w_out, win_sink, na_rpb, moe_w_group, moe_b_group, moe_w_router, moe_b_router,
           moe_w1, moe_w3, moe_w2):
    b, s, d = x.shape
    n_ctx = ctx.shape[1]
    n = s + n_ctx
    depth = mod_w.shape[0]
    rows = s // GRID_W
    assert depth == 2, "layer pattern implemented: one differential/GQA layer, then one window/neighbourhood layer"
    assert s % TM == 0 and n_ctx % TM == 0 and s % TK == 0 and b + 1 <= 8
    assert s % TQ_DIFF == 0 and s % TQ_GQA == 0 and n_ctx % TQ_GQA == 0
    assert rows >= NA_SLAB and rows % NA_QROWS == 0 and s % GRID_W == 0
    alpha = (2.0 * depth) ** 0.25
    qw = d // 2
    kvw = qw // 4
    assert qw == 4 * LANES and kvw == LANES

    cos128, sin128 = _rope_tables(s, n_ctx)
    cvec = jnp.concatenate([c, c_ctx[None, :], jnp.zeros((8 - b - 1, d), F32)], axis=0)
    mods_all = _mods_call(cvec, mod_w, mod_b)

    def seg_mods(i):
        m = mods_all[i]
        lat = m[:b].reshape(b, 6, d)
        cm = jnp.broadcast_to(m[b].reshape(1, 6, d), (b, 6, d))
        ms = jnp.stack([lat, cm], axis=1)
        pick = lambda k, one: (ms[:, :, k] + one)[:, :, None, :]
        return pick(0, 0.0), pick(1, 1.0), pick(2, 0.0), pick(3, 0.0), pick(4, 1.0), pick(5, 0.0)

    n_lat = s // TM
    stream0 = (x, ctx, 0)

    shift1, scale1p, gate1, shift2, scale2p, gate2 = seg_mods(0)
    w_in = ab_w_in[0]
    q_d, q_g, k_d, v_d = (w_in[:, k * qw:(k + 1) * qw] for k in range(4))
    k_g, v_g = w_in[:, 4 * qw:4 * qw + kvw], w_in[:, 4 * qw + kvw:]
    w_qk = jnp.concatenate([q_d, _pair_heads(q_g, 1), k_d, k_g], axis=1).astype(BF16)
    roles0 = ((True, None, True),) * 4 + ((True, 0, True),) * 4 + ((True, None, False),) * 4 + ((True, 1, False),)
    w_vt = jnp.concatenate([v_d, v_g], axis=1).T.astype(BF16)
    g128 = jnp.tile(gqa_qk_g[0], (1, 2))
    p0, vt0 = _inproj_call(*stream0, n, shift1, scale1p, w_qk, w_vt, cos128, sin128, g128, roles0, s)
    diff_attn = functools.partial(_diff_attn_call, p0, vt0, diff_lambda[0], diff_subln_g[0], s_len=s, n_ctx=n_ctx,
                                  n_heads=4, q_blk0=0, k_blk0=8, v_blk0=0, lam_init=_lambda_init(0))
    gqa_attn = functools.partial(_gqa_attn_call, p0, vt0, s_len=s, n_ctx=n_ctx, q_width=qw, q_col0=qw,
                                 k_blk=12, v_blk=4)
    o_d = (diff_attn(for_ctx=False), diff_attn(for_ctx=True), 0)
    o_g = (gqa_attn(for_ctx=False), gqa_attn(for_ctx=True), 0)
    w_out = jnp.concatenate([ab_w_out[0][:qw], _pair_heads(ab_w_out[0][qw:], 0)], axis=0).astype(BF16)
    w_route, b_route = _router_params(moe_w_group[0], moe_b_group[0], moe_w_router[0], moe_b_router[0])
    h1, f, route = _outproj_call(o_d, o_g, stream0, w_out, gate1, ln_g[0, 0:1], ln_b[0, 0:1], shift2, scale2p,
                                 w_route, b_route, n_rows=n, s_len=s, alpha=alpha)
    y = _moe_layer(f, route, moe_w1, moe_w3, moe_w2, 0)
    h = _combine_call(h1, y, route, gate2, ln_g[0, 1:2], ln_b[0, 1:2], s_len=s, alpha=alpha)

    shift1, scale1p, gate1, shift2, scale2p, gate2 = seg_mods(1)
    w_in = cd_w_in[0]
    q_w, q_n = w_in[:, :qw], w_in[:, qw:2 * qw]
    k_w, v_w = w_in[:, 2 * qw:2 * qw + kvw], w_in[:, 2 * qw + kvw:2 * qw + 2 * kvw]
    k_n, v_n = w_in[:, 2 * qw + 2 * kvw:3 * qw + 2 * kvw], w_in[:, 3 * qw + 2 * kvw:]
    w_qk = jnp.concatenate([_pair_heads(q_w, 1), q_n, k_w, k_n], axis=1).astype(BF16)
    roles1 = ((True, None, True),) * 4 + ((False, None, True),) * 4 + ((True, None, False),) \
        + ((False, None, False),) * 4
    w_vt = jnp.concatenate([v_w, v_n], axis=1).T.astype(BF16)
    stream1 = (h, h, n_lat)
    p1, vt1 = _inproj_call(*stream1, n, shift1, scale1p, w_qk, w_vt, cos128, sin128, jnp.ones((2, LANES), F32),
                           roles1, s)
    sink_row = jnp.repeat(win_sink[0].reshape(2, 4).T.reshape(8), TQ_WIN).reshape(1, 8 * TQ_WIN)
    o_w = _win_attn_call(p1, vt1, sink_row, s_len=s, n_ctx=n_ctx, q_width=qw, k_blk=8, v_blk=0)
    o_n = _na_attn_call(p1, vt1, _na_bias_tables(na_rpb[0], rows), s_len=s, n_ctx=n_ctx, n_pairs=4,
                        q_blk0=4, k_blk0=9, v_blk0=1)
    w_out = jnp.concatenate([_pair_heads(cd_w_out[0][:qw], 0), cd_w_out[0][qw:]], axis=0).astype(BF16)
    w_route, b_route = _router_params(moe_w_group[1], moe_b_group[1], moe_w_router[1], moe_b_router[1])
    h1, f, route = _outproj_call((o_w, o_w, 0), (o_n, o_n, 0), stream1, w_out, gate1, ln_g[1, 0:1], ln_b[1, 0:1],
                                 shift2, scale2p, w_route, b_route, n_rows=s, s_len=s, alpha=alpha)
    y = _moe_layer(f, route, moe_w1, moe_w3, moe_w2, 1)
    return _combine_call(h1, y, route, gate2, ln_g[1, 1:2], ln_b[1, 1:2], s_len=s, alpha=alpha)
```

```python
import functools

import numpy as np
import jax
import jax.numpy as jnp
from jax import lax
from jax.experimental import pallas as pl
from jax.experimental.pallas import tpu as pltpu

F32 = jnp.float32
BF16 = jnp.bfloat16
HIGHEST = lax.Precision.HIGHEST

GRID_W = 64
HEAD_DIM = 64
ROPE_THETA = 10000.0
WINDOW = 128
NA_ROWS_MAX = 8
NA_COLS = 16
N_GROUPS = 4
EXPERTS_PER_GROUP = 8
N_EXPERTS = N_GROUPS * EXPERTS_PER_GROUP
TOP_K = 2
MOE_BLOCK = 128
LN_EPS = 1e-5
RMS_EPS = 1e-6
ATTN_SCALE = HEAD_DIM ** -0.5
NEG_INF = -1e30
LOG2E = 1.4426950408889634
Q_SCALE = ATTN_SCALE * LOG2E

LANES = 128
PAIR = 2 * HEAD_DIM
assert PAIR == LANES

VMEM_LIMIT = 48 * 1024 * 1024

TM = 256
TQ_DIFF = 512
TQ_GQA = 128
TK = 1024
TQ_WIN = 128
NA_QROWS = 4
NA_SLAB = NA_QROWS + NA_ROWS_MAX


def _cparams(n_axes):
    return pltpu.CompilerParams(dimension_semantics=("arbitrary",) * n_axes, vmem_limit_bytes=VMEM_LIMIT)


def _lane_iota():
    return lax.broadcasted_iota(jnp.int32, (1, LANES), 1)


def _mods_kernel(c_ref, w_ref, b_ref, o_ref):
    c = c_ref[...]
    sc = c / (1.0 + jnp.exp(-c))
    o_ref[...] = jnp.dot(sc, w_ref[...], precision=HIGHEST, preferred_element_type=F32) + b_ref[...]


def _mods_call(cvec, mod_w, mod_b):
    depth, d, n6 = mod_w.shape
    tn = n6 // 4
    return pl.pallas_call(
        _mods_kernel,
        grid=(depth, n6 // tn),
        in_specs=[
            pl.BlockSpec((8, d), lambda l, j: (0, 0)),
            pl.BlockSpec((None, d, tn), lambda l, j: (l, 0, j)),
            pl.BlockSpec((None, 1, tn), lambda l, j: (l, 0, j)),
        ],
        out_specs=pl.BlockSpec((None, 8, tn), lambda l, j: (l, 0, j)),
        out_shape=jax.ShapeDtypeStruct((depth, 8, n6), F32),
        compiler_params=_cparams(2),
        name="mods",
    )(cvec, mod_w, mod_b.reshape(depth, 1, n6))


def _split_stream_specs(lat, ctx, ctx_block0, n_lat):
    width = lat.shape[2]
    return [
        pl.BlockSpec((None, TM, width), lambda bb, j: (bb, jnp.minimum(j, n_lat - 1), 0)),
        pl.BlockSpec((None, TM, width), lambda bb, j: (bb, ctx_block0 + jnp.maximum(j - n_lat, 0), 0)),
    ]


def _read_split_stream(lat_ref, ctx_ref, n_lat):
    return jnp.where(pl.program_id(1) >= n_lat, ctx_ref[...], lat_ref[...])


def _inproj_kernel(hl_ref, hc_ref, sh_ref, sc_ref, w_ref, wvt_ref, cos_ref, sin_ref, g_ref, o_ref, vt_ref, *,
                   roles, n_lat):
    a = (_read_split_stream(hl_ref, hc_ref, n_lat) * sc_ref[...] + sh_ref[...]).astype(BF16)
    vt = lax.dot_general(wvt_ref[...], a, (((1,), (1,)), ((), ())), preferred_element_type=F32)
    for j in range(vt_ref.shape[0]):
        vt_ref[j] = vt[j * LANES:(j + 1) * LANES, :].astype(BF16)
    lane = _lane_iota()
    rope_first = (lane & (HEAD_DIM - 1)) < (HEAD_DIM // 2)
    lo = lane < HEAD_DIM
    n_blocks = len(roles)
    for c0 in range(0, n_blocks, 2):
        width = min(2, n_blocks - c0)
        p = jnp.dot(a, w_ref[:, c0 * LANES:(c0 + width) * LANES], preferred_element_type=F32)
        for half in range(width):
            cb = c0 + half
            rope, norm, scale = roles[cb]
            blk = p[:, half * LANES:(half + 1) * LANES]
            if norm is not None:
                sq = blk * blk
                s_lo = jnp.sum(jnp.where(lo, sq, 0.0), axis=-1, keepdims=True)
                s_hi = jnp.sum(jnp.where(lo, 0.0, sq), axis=-1, keepdims=True)
                ms = jnp.where(lo, s_lo, s_hi) * (1.0 / HEAD_DIM)
                blk = blk * lax.rsqrt(ms + RMS_EPS) * g_ref[norm:norm + 1, :]
            if rope:
                partner = jnp.where(rope_first, pltpu.roll(blk, LANES - HEAD_DIM // 2, 1),
                                    pltpu.roll(blk, HEAD_DIM // 2, 1))
                blk = blk * cos_ref[...] + partner * sin_ref[...]
            if scale:
                blk = blk * Q_SCALE
            o_ref[:, cb * LANES:(cb + 1) * LANES] = blk.astype(BF16)


def _inproj_call(h_lat, h_ctx, ctx_block0, n, shift, scale1p, w16, wvt16, cos128, sin128, g128, roles, s_len):
    b, _, d = h_lat.shape
    ncol = w16.shape[1]
    nvb = wvt16.shape[0] // LANES
    n_lat = s_len // TM
    seg = lambda bb, j: (bb, jnp.where(j >= n_lat, 1, 0), 0, 0)
    return pl.pallas_call(
        functools.partial(_inproj_kernel, roles=roles, n_lat=n_lat),
        grid=(b, n // TM),
        in_specs=_split_stream_specs(h_lat, h_ctx, ctx_block0, n_lat) + [
            pl.BlockSpec((None, None, 1, d), seg),
            pl.BlockSpec((None, None, 1, d), seg),
            pl.BlockSpec((d, ncol), lambda bb, j: (0, 0)),
            pl.BlockSpec((nvb * LANES, d), lambda bb, j: (0, 0)),
            pl.BlockSpec((TM, LANES), lambda bb, j: (j, 0)),
            pl.BlockSpec((TM, LANES), lambda bb, j: (j, 0)),
            pl.BlockSpec((2, LANES), lambda bb, j: (0, 0)),
        ],
        out_specs=[
            pl.BlockSpec((None, TM, ncol), lambda bb, j: (bb, j, 0)),
            pl.BlockSpec((None, nvb, LANES, TM), lambda bb, j: (bb, 0, 0, j)),
        ],
        out_shape=[
            jax.ShapeDtypeStruct((b, n, ncol), BF16),
            jax.ShapeDtypeStruct((b, nvb, LANES, n), BF16),
        ],
        compiler_params=_cparams(2),
        name="inproj",
    )(h_lat, h_ctx, shift, scale1p, w16, wvt16, cos128, sin128, g128)


def _stack_q(q_ref, n_qblocks, row0=0, n_rows=None):
    n_rows = q_ref.shape[0] if n_rows is None else n_rows
    lo = _lane_iota() < HEAD_DIM
    parts = []
    for j in range(n_qblocks):
        q = q_ref[row0:row0 + n_rows, j * LANES:(j + 1) * LANES]
        zero = jnp.zeros_like(q)
        parts.append(jnp.where(lo, q, zero))
        parts.append(jnp.where(lo, zero, q))
    return jnp.concatenate(parts, axis=0)


def _scores_t(k, qs):
    return lax.dot_general(k, qs, (((1,), (1,)), ((), ())), preferred_element_type=F32)


def _flash_init(m_ref, l_ref, acc_ref):
    m_ref[...] = jnp.full(m_ref.shape, NEG_INF, F32)
    l_ref[...] = jnp.zeros(l_ref.shape, F32)
    acc_ref[...] = jnp.zeros(acc_ref.shape, F32)


def _flash_update(st, vt, m_ref, l_ref, acc_ref):
    m_prev = m_ref[...]
    m_new = jnp.maximum(m_prev, jnp.max(st, axis=0, keepdims=True))
    alpha = jnp.exp2(m_prev - m_new)
    pt = jnp.exp2(st - m_new)
    l_ref[...] = alpha * l_ref[...] + jnp.sum(pt, axis=0, keepdims=True)
    acc_ref[...] = alpha * acc_ref[...] + jnp.dot(vt, pt.astype(BF16), preferred_element_type=F32)
    m_ref[...] = m_new


def _flash_sweep(qs, k_ref, vt_ref, m_ref, l_ref, acc_ref, *, s_len, n_ctx):
    _flash_init(m_ref, l_ref, acc_ref)
    bounds = [(c * TK, (c + 1) * TK) for c in range(s_len // TK)] + [(s_len, s_len + n_ctx)]
    s_next = _scores_t(k_ref[bounds[0][0]:bounds[0][1], :], qs)
    for idx, (lo, hi) in enumerate(bounds):
        s_cur = s_next
        if idx + 1 < len(bounds):
            s_next = _scores_t(k_ref[bounds[idx + 1][0]:bounds[idx + 1][1], :], qs)
        _flash_update(s_cur, vt_ref[:, lo:hi], m_ref, l_ref, acc_ref)


def _merge_pairs(ot, tq, n_qblocks, o_ref, row0=0):
    for j in range(n_qblocks):
        o_lo = ot[:HEAD_DIM, (2 * j) * tq:(2 * j + 1) * tq]
        o_hi = ot[HEAD_DIM:, (2 * j + 1) * tq:(2 * j + 2) * tq]
        blk = jnp.concatenate([o_lo, o_hi], axis=0)
        o_ref[row0:row0 + tq, j * LANES:(j + 1) * LANES] = blk.T.astype(o_ref.dtype)


def _diff_attn_kernel(q_ref, k_ref, vt_ref, lam_ref, g_ref, o_ref, m_ref, l_ref, acc_ref, *, s_len, n_ctx, lam_init):
    tq = q_ref.shape[0]
    qs = _stack_q(q_ref, 1)
    _flash_sweep(qs, k_ref, vt_ref, m_ref, l_ref, acc_ref, s_len=s_len, n_ctx=n_ctx)
    lv = lam_ref[...]
    lam = (jnp.exp(jnp.sum(lv[0:1] * lv[1:2], axis=-1, keepdims=True))
           - jnp.exp(jnp.sum(lv[2:3] * lv[3:4], axis=-1, keepdims=True)) + lam_init)
    ot = acc_ref[...] / l_ref[...]
    od = ot[:, :tq] - lam * ot[:, tq:]
    ms = jnp.mean(od * od, axis=0, keepdims=True)
    od = od * lax.rsqrt(ms + RMS_EPS) * g_ref[...] * (1.0 - lam_init)
    o_ref[...] = od.T.astype(o_ref.dtype)


def _query_key_windows(n, s_len, n_ctx, tq, for_ctx):
    if for_ctx:
        assert s_len % n_ctx == 0 and n_ctx % tq == 0
        return n_ctx, s_len // tq, n_ctx, s_len // n_ctx, 0
    assert s_len % tq == 0
    return s_len, 0, n, 0, s_len


def _diff_attn_call(p, vt, lam_vecs, subln_g, *, s_len, n_ctx, n_heads, q_blk0, k_blk0, v_blk0, lam_init, for_ctx):
    b, n, _ = p.shape
    tq = min(TQ_DIFF, n_ctx) if for_ctx else TQ_DIFF
    n_q, q_i0, n_k, k_i, s_keys = _query_key_windows(n, s_len, n_ctx, tq, for_ctx)
    return pl.pallas_call(
        functools.partial(_diff_attn_kernel, s_len=s_keys, n_ctx=n_ctx, lam_init=lam_init),
        grid=(b, n_heads, n_q // tq),
        in_specs=[
            pl.BlockSpec((None, tq, LANES), lambda bb, hh, i: (bb, q_i0 + i, q_blk0 + hh)),
            pl.BlockSpec((None, n_k, LANES), lambda bb, hh, i: (bb, k_i, k_blk0 + hh)),
            pl.BlockSpec((None, None, LANES, n_k), lambda bb, hh, i: (bb, v_blk0 + hh, 0, k_i)),
            pl.BlockSpec((4, HEAD_DIM), lambda bb, hh, i: (0, 0)),
            pl.BlockSpec((LANES, 1), lambda bb, hh, i: (0, 0)),
        ],
        out_specs=pl.BlockSpec((None, tq, LANES), lambda bb, hh, i: (bb, i, hh)),
        out_shape=jax.ShapeDtypeStruct((b, n_q, n_heads * LANES), BF16),
        scratch_shapes=[pltpu.VMEM((1, 2 * tq), F32), pltpu.VMEM((1, 2 * tq), F32), pltpu.VMEM((LANES, 2 * tq), F32)],
        compiler_params=_cparams(3),
        name="diff_attn",
    )(p, p, vt, lam_vecs, subln_g.reshape(LANES, 1))


def _gqa_attn_kernel(q_ref, k_ref, vt_ref, o_ref, m_ref, l_ref, acc_ref, *, s_len, n_ctx):
    tq = q_ref.shape[0]
    n_qblocks = q_ref.shape[1] // LANES
    qs = _stack_q(q_ref, n_qblocks)
    _flash_sweep(qs, k_ref, vt_ref, m_ref, l_ref, acc_ref, s_len=s_len, n_ctx=n_ctx)
    _merge_pairs(acc_ref[...] / l_ref[...], tq, n_qblocks, o_ref)


def _gqa_attn_call(p, vt, *, s_len, n_ctx, q_width, q_col0, k_blk, v_blk, for_ctx):
    b, n, _ = p.shape
    tq = TQ_GQA
    n_q, q_i0, n_k, k_i, s_keys = _query_key_windows(n, s_len, n_ctx, tq, for_ctx)
    m_rows = 2 * (q_width // LANES) * tq
    return pl.pallas_call(
        functools.partial(_gqa_attn_kernel, s_len=s_keys, n_ctx=n_ctx),
        grid=(b, n_q // tq),
        in_specs=[
            pl.BlockSpec((None, tq, q_width), lambda bb, i: (bb, q_i0 + i, q_col0 // q_width)),
            pl.BlockSpec((None, n_k, LANES), lambda bb, i: (bb, k_i, k_blk)),
            pl.BlockSpec((None, None, LANES, n_k), lambda bb, i: (bb, v_blk, 0, k_i)),
        ],
        out_specs=pl.BlockSpec((None, tq, q_width), lambda bb, i: (bb, i, 0)),
        out_shape=jax.ShapeDtypeStruct((b, n_q, q_width), BF16),
        scratch_shapes=[pltpu.VMEM((1, m_rows), F32), pltpu.VMEM((1, m_rows), F32), pltpu.VMEM((LANES, m_rows), F32)],
        compiler_params=_cparams(2),
        name="gqa_attn",
    )(p, p, vt)


BLOCKS_PER_STEP = 4


def _win_key_start(i, tq, s_len, clip=jnp.clip):
    return clip(i * tq - WINDOW, 0, s_len - (tq + 2 * WINDOW))


def _win_mask_tables(tq, n_stack, s_len):
    span = tq + 2 * WINDOW
    n_blocks = s_len // tq
    tabs = []
    for i in (0, 1, n_blocks - 1):
        kpos = int(_win_key_start(i, tq, s_len, np.clip)) + np.arange(span)[:, None]
        qpos = i * tq + (np.arange(n_stack * tq)[None, :] % tq)
        tabs.append(np.where(np.abs(qpos - kpos) <= WINDOW, 0.0, NEG_INF).astype(np.float32))
    return jnp.asarray(np.stack(tabs))


def _win_attn_kernel(q_ref, k_ref, vt_ref, sink_ref, *rest, s_len, n_ctx):
    mask_refs, o_ref = rest[:BLOCKS_PER_STEP], rest[BLOCKS_PER_STEP]
    tq = q_ref.shape[0] // BLOCKS_PER_STEP
    n_qblocks = q_ref.shape[1] // LANES
    span = tq + 2 * WINDOW
    step = pl.program_id(1)
    k_ctx = k_ref[s_len:s_len + n_ctx, :]
    vt_ctx = vt_ref[:, s_len:s_len + n_ctx]
    sink = sink_ref[...] * LOG2E

    def scores(blk):
        i = BLOCKS_PER_STEP * step + blk
        ws = pl.multiple_of(_win_key_start(i, tq, s_len), LANES)
        qs = _stack_q(q_ref, n_qblocks, blk * tq, tq)
        s_loc = _scores_t(k_ref[pl.ds(ws, span), :], qs) + mask_refs[blk][...]
        return ws, s_loc, _scores_t(k_ctx, qs)

    def finish(blk, ws, s_loc, s_ctx):
        m = jnp.maximum(jnp.maximum(jnp.max(s_loc, axis=0, keepdims=True), jnp.max(s_ctx, axis=0, keepdims=True)),
                        sink)
        p_loc = jnp.exp2(s_loc - m)
        p_ctx = jnp.exp2(s_ctx - m)
        denom = jnp.sum(p_loc, axis=0, keepdims=True) + jnp.sum(p_ctx, axis=0, keepdims=True) + jnp.exp2(sink - m)
        acc = (jnp.dot(vt_ref[:, pl.ds(ws, span)], p_loc.astype(BF16), preferred_element_type=F32)
               + jnp.dot(vt_ctx, p_ctx.astype(BF16), preferred_element_type=F32))
        _merge_pairs(acc / denom, tq, n_qblocks, o_ref, blk * tq)

    pending = [scores(blk) for blk in range(BLOCKS_PER_STEP)]
    for blk, args in enumerate(pending):
        finish(blk, *args)


def _win_attn_call(p, vt, sink_row, *, s_len, n_ctx, q_width, k_blk, v_blk):
    b, n, _ = p.shape
    tq = BLOCKS_PER_STEP * TQ_WIN
    m_rows = sink_row.shape[1]
    n_blocks = s_len // TQ_WIN
    assert n_blocks % BLOCKS_PER_STEP == 0 and n_blocks >= 3
    masks = _win_mask_tables(TQ_WIN, m_rows // TQ_WIN, s_len)
    tbl = lambda i: jnp.where(i == 0, 0, jnp.where(i == n_blocks - 1, 2, 1))
    mask_spec = lambda blk: pl.BlockSpec((None,) + masks.shape[1:],
                                         lambda bb, st: (tbl(BLOCKS_PER_STEP * st + blk), 0, 0))
    return pl.pallas_call(
        functools.partial(_win_attn_kernel, s_len=s_len, n_ctx=n_ctx),
        grid=(b, s_len // tq),
        in_specs=[
            pl.BlockSpec((None, tq, q_width), lambda bb, i: (bb, i, 0)),
            pl.BlockSpec((None, n, LANES), lambda bb, i: (bb, 0, k_blk)),
            pl.BlockSpec((None, None, LANES, n), lambda bb, i: (bb, v_blk, 0, 0)),
            pl.BlockSpec((1, m_rows), lambda bb, i: (0, 0)),
        ] + [mask_spec(blk) for blk in range(BLOCKS_PER_STEP)],
        out_specs=pl.BlockSpec((None, tq, q_width), lambda bb, i: (bb, i, 0)),
        out_shape=jax.ShapeDtypeStruct((b, s_len, q_width), BF16),
        compiler_params=_cparams(2),
        name="win_attn",
    )(p, p, vt, sink_row, *([masks] * BLOCKS_PER_STEP))


def _na_attn_kernel(q_ref, k_ref, vt_ref, *rest, s_len, n_ctx):
    bias_refs, o_ref = rest[:BLOCKS_PER_STEP], rest[BLOCKS_PER_STEP]
    tq = q_ref.shape[0] // BLOCKS_PER_STEP
    slab = NA_SLAB * GRID_W
    rows = s_len // GRID_W
    step = pl.program_id(2)
    k_cx = k_ref[s_len:s_len + n_ctx, :]
    vt_cx = vt_ref[:, s_len:s_len + n_ctx]

    def scores(blk):
        mi = BLOCKS_PER_STEP * step + blk
        ss = pl.multiple_of(jnp.clip(NA_QROWS * mi - NA_ROWS_MAX // 2, 0, rows - NA_SLAB) * GRID_W, LANES)
        qs = _stack_q(q_ref, 1, blk * tq, tq)
        s_nb = _scores_t(k_ref[pl.ds(ss, slab), :], qs) + bias_refs[blk][...]
        return ss, s_nb, _scores_t(k_cx, qs)

    def finish(blk, ss, s_nb, s_cx):
        m = jnp.maximum(jnp.max(s_nb, axis=0, keepdims=True), jnp.max(s_cx, axis=0, keepdims=True))
        p_nb = jnp.exp2(s_nb - m)
        p_cx = jnp.exp2(s_cx - m)
        denom = jnp.sum(p_nb, axis=0, keepdims=True) + jnp.sum(p_cx, axis=0, keepdims=True)
        acc = (jnp.dot(vt_ref[:, pl.ds(ss, slab)], p_nb.astype(BF16), preferred_element_type=F32)
               + jnp.dot(vt_cx, p_cx.astype(BF16), preferred_element_type=F32))
        _merge_pairs(acc / denom, tq, 1, o_ref, blk * tq)

    pending = [scores(blk) for blk in range(BLOCKS_PER_STEP)]
    for blk, args in enumerate(pending):
        finish(blk, *args)


def _na_attn_call(p, vt, bias_tab, *, s_len, n_ctx, n_pairs, q_blk0, k_blk0, v_blk0):
    b, n, _ = p.shape
    tq = NA_QROWS * GRID_W
    n_blocks = s_len // tq
    assert n_blocks % BLOCKS_PER_STEP == 0
    tbl = lambda mi: jnp.where(mi == 0, 0, jnp.where(mi == n_blocks - 1, 2, 1))
    bias_spec = lambda blk: pl.BlockSpec((None, None, NA_SLAB * GRID_W, 2 * tq),
                                         lambda bb, j, st: (tbl(BLOCKS_PER_STEP * st + blk), j, 0, 0))
    return pl.pallas_call(
        functools.partial(_na_attn_kernel, s_len=s_len, n_ctx=n_ctx),
        grid=(b, n_pairs, n_blocks // BLOCKS_PER_STEP),
        in_specs=[
            pl.BlockSpec((None, BLOCKS_PER_STEP * tq, LANES), lambda bb, j, st: (bb, st, q_blk0 + j)),
            pl.BlockSpec((None, n, LANES), lambda bb, j, st: (bb, 0, k_blk0 + j)),
            pl.BlockSpec((None, None, LANES, n), lambda bb, j, st: (bb, v_blk0 + j, 0, 0)),
        ] + [bias_spec(blk) for blk in range(BLOCKS_PER_STEP)],
        out_specs=pl.BlockSpec((None, BLOCKS_PER_STEP * tq, LANES), lambda bb, j, st: (bb, st, j)),
        out_shape=jax.ShapeDtypeStruct((b, s_len, n_pairs * LANES), BF16),
        compiler_params=_cparams(3),
        name="na_attn",
    )(p, p, vt, *([bias_tab] * BLOCKS_PER_STEP))


def _na_bias_tables(rpb, rows):
    n_heads = rpb.shape[0]
    rpb = rpb.astype(F32)
    pad = GRID_W - NA_COLS
    rpb_p = jnp.pad(rpb, ((0, 0), (0, 0), (pad, pad)))
    col_tab = jnp.stack([rpb_p[:, :, pad + NA_COLS - 1 - qc: pad + NA_COLS - 1 - qc + GRID_W] for qc in range(GRID_W)],
                        axis=2)
    qr = np.arange(NA_QROWS)[:, None, None, None]
    qc = np.arange(GRID_W)[None, :, None, None]
    kr = np.arange(NA_SLAB)[None, None, :, None]
    kc = np.arange(GRID_W)[None, None, None, :]
    full = (NA_QROWS, GRID_W, NA_SLAB, GRID_W)
    flat = (NA_QROWS * GRID_W, NA_SLAB * GRID_W)
    tabs = []
    for q0, s0 in ((0, 0), (NA_QROWS, NA_QROWS - NA_ROWS_MAX // 2), (rows - NA_QROWS, rows - NA_SLAB)):
        q_row = q0 + qr
        k_row = s0 + kr
        r0 = np.clip(q_row - NA_ROWS_MAX // 2, 0, rows - NA_ROWS_MAX)
        c0 = np.clip(qc - NA_COLS // 2, 0, GRID_W - NA_COLS)
        valid = (k_row >= r0) & (k_row < r0 + NA_ROWS_MAX) & (kc >= c0) & (kc < c0 + NA_COLS)
        valid = np.broadcast_to(valid, full).reshape(flat)
        r_off = np.clip(k_row - q_row + NA_ROWS_MAX - 1, 0, 2 * NA_ROWS_MAX - 2)[:, 0, :, 0]
        bias = jnp.stack([jnp.stack([col_tab[:, int(r_off[a, c])] for c in range(NA_SLAB)], axis=2)
                          for a in range(NA_QROWS)], axis=1)
        tabs.append(jnp.where(valid[None], bias.reshape((n_heads,) + flat) * LOG2E, NEG_INF))
    tab = jnp.stack(tabs)
    return jnp.swapaxes(tab.reshape(3, n_heads // 2, 2 * flat[0], flat[1]), -1, -2)


def _split_hi_lo(x):
    c = x * 65537.0
    hi = c - (c - x)
    return hi.astype(BF16), (x - hi).astype(BF16)


def _layer_norm(y, g, b):
    mu = jnp.mean(y, axis=-1, keepdims=True)
    yc = y - mu
    var = jnp.mean(yc * yc, axis=-1, keepdims=True)
    return yc * lax.rsqrt(var + LN_EPS) * g + b


def _route(logits):
    lane = _lane_iota().astype(F32)
    big = float(LANES)
    is_g = lane < N_GROUPS
    gl = jnp.where(is_g, logits, NEG_INF)
    g_max = jnp.max(gl, axis=-1, keepdims=True)
    g_idx = jnp.min(jnp.where(gl == g_max, lane, big), axis=-1, keepdims=True)
    g_w = 1.0 / jnp.sum(jnp.where(is_g, jnp.exp(gl - g_max), 0.0), axis=-1, keepdims=True)
    base = N_GROUPS + EXPERTS_PER_GROUP * g_idx
    el = jnp.where((lane >= base) & (lane < base + EXPERTS_PER_GROUP), logits, NEG_INF)
    v1 = jnp.max(el, axis=-1, keepdims=True)
    i1 = jnp.min(jnp.where(el == v1, lane, big), axis=-1, keepdims=True)
    el2 = jnp.where(lane == i1, NEG_INF, el)
    v2 = jnp.max(el2, axis=-1, keepdims=True)
    i2 = jnp.min(jnp.where(el2 == v2, lane, big), axis=-1, keepdims=True)
    t = jnp.exp(v2 - v1)
    w1 = g_w / (1.0 + t)
    w2 = g_w * t / (1.0 + t)
    return jnp.where(lane == 0, i1 - N_GROUPS,
                     jnp.where(lane == 1, i2 - N_GROUPS, jnp.where(lane == 2, w1, jnp.where(lane == 3, w2, 0.0))))


def _outproj_kernel(oal_ref, oac_ref, obl_ref, obc_ref, hl_ref, hc_ref, w_ref, gate_ref, lng_ref, lnb_ref,
                    sh_ref, sc_ref, wr_ref, br_ref, h1_ref, f_ref, route_ref, *, alpha, n_lat):
    ka = oal_ref.shape[1]
    oa = _read_split_stream(oal_ref, oac_ref, n_lat)
    ob = _read_split_stream(obl_ref, obc_ref, n_lat)
    h = _read_split_stream(hl_ref, hc_ref, n_lat)
    o = (jnp.dot(oa, w_ref[:ka, :], preferred_element_type=F32)
         + jnp.dot(ob, w_ref[ka:, :], preferred_element_type=F32))
    h1 = _layer_norm(alpha * h + gate_ref[...] * o, lng_ref[...], lnb_ref[...])
    h1_ref[...] = h1
    f = h1 * sc_ref[...] + sh_ref[...]
    f_ref[:, 0, :] = f
    f_hi, f_lo = _split_hi_lo(f)
    hi_prod = jnp.dot(f_hi, wr_ref[...], preferred_element_type=F32)
    lo_prod = jnp.dot(f_lo, wr_ref[...], preferred_element_type=F32)
    logits = (hi_prod[:, :LANES] + hi_prod[:, LANES:]) + (lo_prod[:, :LANES] + lo_prod[:, LANES:]) + br_ref[...]
    route_ref[...] = _route(logits)


def _outproj_call(oa, ob, h, w16, gate, ln_g, ln_b, shift, scale1p, w_route, b_route, *, n_rows, s_len, alpha):
    b, _, d = h[0].shape
    ka, kb = oa[0].shape[2], ob[0].shape[2]
    n_lat = s_len // TM
    seg = lambda bb, j: (bb, jnp.where(j >= n_lat, 1, 0), 0, 0)
    row = lambda bb, j: (bb, j, 0)
    const = lambda bb, j: (0, 0)
    return pl.pallas_call(
        functools.partial(_outproj_kernel, alpha=alpha, n_lat=n_lat),
        grid=(b, n_rows // TM),
        in_specs=_split_stream_specs(*oa, n_lat) + _split_stream_specs(*ob, n_lat) + _split_stream_specs(*h, n_lat) + [
            pl.BlockSpec((ka + kb, d), const),
            pl.BlockSpec((None, None, 1, d), seg),
            pl.BlockSpec((1, d), const),
            pl.BlockSpec((1, d), const),
            pl.BlockSpec((None, None, 1, d), seg),
            pl.BlockSpec((None, None, 1, d), seg),
            pl.BlockSpec((d, 2 * LANES), const),
            pl.BlockSpec((1, LANES), const),
        ],
        out_specs=[
            pl.BlockSpec((None, TM, d), row),
            pl.BlockSpec((None, TM, 1, d), lambda bb, j: (bb, j, 0, 0)),
            pl.BlockSpec((None, TM, LANES), row),
        ],
        out_shape=[
            jax.ShapeDtypeStruct((b, n_rows, d), F32),
            jax.ShapeDtypeStruct((b, n_rows, 1, d), F32),
            jax.ShapeDtypeStruct((b, n_rows, LANES), F32),
        ],
        compiler_params=_cparams(2),
        name="outproj",
    )(oa[0], oa[1], ob[0], ob[1], h[0], h[1], w16, gate, ln_g, ln_b, shift, scale1p, w_route, b_route)


def _row_copy(src, src_row, dst, dst_row, sem):
    return pltpu.make_async_copy(src.at[pl.ds(src_row, 1), :], dst.at[pl.ds(dst_row, 1), :], sem)


def _block_copy_for_wait(src, dst, sem):
    return pltpu.make_async_copy(src, dst, sem)


MOE_BUFS = 3
DMA_PRIORITIES = 2
MOE_VMEM_LIMIT = 57 * 1024 * 1024


def _moe_kernel(blk_e_ref, blk_b_ref, blk_used_ref, tok_ref, dst_prev_ref, dst_last_ref, f_ref,
                w1_ref, w3_ref, w2_ref, y_hbm, xs, zbuf, ybuf, w1b, w3b, w2b, ssem):
    del blk_b_ref
    i = pl.program_id(0)
    last = pl.num_programs(0) - 1
    cur = lax.rem(i, MOE_BUFS)
    prev = lax.rem(i + 2, MOE_BUFS)
    used = blk_used_ref[i] > 0
    prev_used = jnp.logical_or(i == 0, blk_used_ref[jnp.maximum(i - 1, 0)] > 0)

    def wait_scatter(s):
        _block_copy_for_wait(ybuf.at[s], y_hbm.at[pl.ds(0, MOE_BLOCK), :], ssem.at[s]).wait()

    def start_scatter(dst_ref, s):
        for r in range(MOE_BLOCK):
            _row_copy(ybuf.at[s], r, y_hbm, dst_ref[0, r], ssem.at[s]).start(priority=r % DMA_PRIORITIES)

    def start_zero_fill(dst_ref, s):
        row0 = pl.multiple_of(dst_ref[0, 0], MOE_BLOCK)
        pltpu.make_async_copy(zbuf, y_hbm.at[pl.ds(row0, MOE_BLOCK), :], ssem.at[s]).start()

    @pl.when(i == 0)
    def _():
        ybuf[...] = jnp.zeros(ybuf.shape, ybuf.dtype)
        zbuf[...] = jnp.zeros(zbuf.shape, zbuf.dtype)

    @pl.when(i >= 2)
    def _():
        wait_scatter(cur)

    @pl.when(jnp.logical_or(i == 0, blk_e_ref[i] != blk_e_ref[jnp.maximum(i - 1, 0)]))
    def _():
        w1b[...] = w1_ref[...].astype(BF16)
        w3b[...] = w3_ref[...].astype(BF16)
        w2b[...] = w2_ref[...].astype(BF16)

    @pl.when(used)
    def _():
        for r in range(MOE_BLOCK):
            xs[pl.ds(r, 1), :] = f_ref[tok_ref[0, r]]
        x = xs[...].astype(BF16)
        start_scatter(dst_prev_ref, prev)
        h1 = jnp.dot(x, w1b[...], preferred_element_type=F32)
        h3 = jnp.dot(x, w3b[...], preferred_element_type=F32)
        act = (h1 / (1.0 + jnp.exp(-h1))) * h3
        ybuf[cur] = jnp.dot(act.astype(BF16), w2b[...], preferred_element_type=F32)

    @pl.when(jnp.logical_and(jnp.logical_not(used), prev_used))
    def _():
        start_scatter(dst_prev_ref, prev)

    @pl.when(jnp.logical_and(jnp.logical_not(used), jnp.logical_not(prev_used)))
    def _():
        start_zero_fill(dst_prev_ref, prev)

    @pl.when(i == last)
    def _():
        @pl.when(used)
        def _():
            start_scatter(dst_last_ref, cur)

        @pl.when(jnp.logical_not(used))
        def _():
            start_zero_fill(dst_last_ref, cur)

        for s in range(MOE_BUFS):
            wait_scatter(s)


def _moe_call(f_rows, blk_e, blk_b, blk_used, slot_tok, slot_dst, w1, w3, w2, layer):
    n_batch, per_batch, _, d = f_rows.shape
    n_blocks = blk_e.shape[0]
    n_slots = n_blocks * MOE_BLOCK
    de = w1.shape[3]
    assert n_blocks >= MOE_BUFS
    smem_block = lambda index_map: pl.BlockSpec((None, 1, MOE_BLOCK), index_map, memory_space=pltpu.SMEM)
    grid_spec = pltpu.PrefetchScalarGridSpec(
        num_scalar_prefetch=3,
        grid=(n_blocks,),
        in_specs=[
            smem_block(lambda i, be, bb, bu: (i, 0, 0)),
            smem_block(lambda i, be, bb, bu: (i, 0, 0)),
            smem_block(lambda i, be, bb, bu: (i + 1, 0, 0)),
            pl.BlockSpec((None, per_batch, 1, d), lambda i, be, bb, bu: (bb[i], 0, 0, 0), pipeline_mode=pl.Buffered(1)),
            pl.BlockSpec((None, None, d, de), lambda i, be, bb, bu: (layer, be[i], 0, 0)),
            pl.BlockSpec((None, None, d, de), lambda i, be, bb, bu: (layer, be[i], 0, 0)),
            pl.BlockSpec((None, None, de, d), lambda i, be, bb, bu: (layer, be[i], 0, 0)),
        ],
        out_specs=pl.BlockSpec(memory_space=pl.ANY),
        scratch_shapes=[
            pltpu.VMEM((MOE_BLOCK, d), F32),
            pltpu.VMEM((MOE_BLOCK, d), F32),
            pltpu.VMEM((MOE_BUFS, MOE_BLOCK, d), F32),
            pltpu.VMEM((d, de), BF16),
            pltpu.VMEM((d, de), BF16),
            pltpu.VMEM((de, d), BF16),
            pltpu.SemaphoreType.DMA((MOE_BUFS,)),
        ],
    )
    tok3 = slot_tok.reshape(n_blocks, 1, MOE_BLOCK)
    n_rows = n_slots + MOE_BLOCK
    first = (n_slots + jnp.arange(MOE_BLOCK, dtype=jnp.int32))[None, :]
    dst3 = jnp.concatenate([first, slot_dst], axis=0).reshape(n_blocks + 1, 1, MOE_BLOCK)
    return pl.pallas_call(
        _moe_kernel,
        grid_spec=grid_spec,
        out_shape=jax.ShapeDtypeStruct((n_rows, d), F32),
        compiler_params=pltpu.CompilerParams(dimension_semantics=("arbitrary",), vmem_limit_bytes=MOE_VMEM_LIMIT),
        name="moe_experts",
    )(blk_e, blk_b, blk_used, tok3, dst3, dst3, f_rows, w1, w3, w2)


def _moe_plan(route, t, n_batch):
    n_assign = TOP_K * t
    per_batch = t // n_batch
    n_seg = n_batch * N_EXPERTS
    segs = jnp.arange(n_seg, dtype=jnp.int32)
    e_flat = route[:, :TOP_K].astype(jnp.int32).reshape(n_assign)
    tok_batch = jnp.arange(n_assign, dtype=jnp.int32) // (TOP_K * per_batch)
    seg_flat = tok_batch * N_EXPERTS + e_flat
    assert n_seg * n_assign < 2 ** 31
    order = jnp.sort(seg_flat * n_assign + jnp.arange(n_assign, dtype=jnp.int32)) % n_assign
    counts = jnp.sum((seg_flat[:, None] == segs[None, :]).astype(jnp.int32), axis=0)
    starts = jnp.cumsum(counts) - counts
    padded = (counts + MOE_BLOCK - 1) // MOE_BLOCK * MOE_BLOCK
    p_ends = jnp.cumsum(padded)
    p_starts = p_ends - padded
    n_blocks = (n_assign + MOE_BLOCK - 1) // MOE_BLOCK + n_seg
    blk_start = jnp.arange(n_blocks, dtype=jnp.int32) * MOE_BLOCK
    blk_seg = jnp.minimum(jnp.sum((p_ends[None, :] <= blk_start[:, None]).astype(jnp.int32), axis=1), n_seg - 1)
    blk_b = blk_seg // N_EXPERTS
    of_block = lambda v: jnp.sum(jnp.where(blk_seg[:, None] == segs[None, :], v[None, :], 0), axis=1)[:, None]
    slot = blk_start[:, None] + jnp.arange(MOE_BLOCK, dtype=jnp.int32)[None, :]
    j = slot - of_block(p_starts)
    valid = j < of_block(counts)
    a_idx = order[jnp.clip(of_block(starts) + j, 0, n_assign - 1)]
    slot_tok = jnp.where(valid, a_idx // TOP_K - blk_b[:, None] * per_batch, 0).astype(jnp.int32)
    spare = n_assign + slot - of_block(starts + counts)
    slot_dst = jnp.where(valid, (a_idx % TOP_K) * t + a_idx // TOP_K, spare).astype(jnp.int32)
    blk_used = (blk_start < p_ends[n_seg - 1]).astype(jnp.int32)
    return blk_seg % N_EXPERTS, blk_b, blk_used, slot_tok, slot_dst


def _combine_kernel(h_ref, y0_ref, y1_ref, route_ref, gate_ref, lng_ref, lnb_ref, o_ref, *, alpha):
    r = route_ref[...]
    y = r[:, 2:3] * y0_ref[...] + r[:, 3:4] * y1_ref[...]
    o_ref[...] = _layer_norm(alpha * h_ref[...] + gate_ref[...] * y, lng_ref[...], lnb_ref[...])


def _combine_call(h1, y, route, gate, ln_g, ln_b, *, s_len, alpha):
    b, n_rows, d = h1.shape
    n_lat = s_len // TM
    seg = lambda bb, j: (bb, jnp.where(j >= n_lat, 1, 0), 0, 0)
    row = lambda bb, j: (bb, j, 0)
    const = lambda bb, j: (0, 0)
    blocks_per_batch = n_rows // TM
    blocks_per_k = b * blocks_per_batch
    return pl.pallas_call(
        functools.partial(_combine_kernel, alpha=alpha),
        grid=(b, n_rows // TM),
        in_specs=[
            pl.BlockSpec((None, TM, d), row),
            pl.BlockSpec((TM, d), lambda bb, j: (bb * blocks_per_batch + j, 0)),
            pl.BlockSpec((TM, d), lambda bb, j: (blocks_per_k + bb * blocks_per_batch + j, 0)),
            pl.BlockSpec((None, TM, LANES), row),
            pl.BlockSpec((None, None, 1, d), seg),
            pl.BlockSpec((1, d), const),
            pl.BlockSpec((1, d), const),
        ],
        out_specs=pl.BlockSpec((None, TM, d), row),
        out_shape=jax.ShapeDtypeStruct((b, n_rows, d), F32),
        compiler_params=_cparams(2),
        name="combine",
    )(h1, y, y, route, gate, ln_g, ln_b)


def _pair_heads(w, axis):
    w = jnp.moveaxis(w, axis, -1)
    lead = w.shape[:-1]
    n_heads = w.shape[-1] // HEAD_DIM
    w = w.reshape(lead + (2, n_heads // 2, HEAD_DIM)).swapaxes(-3, -2).reshape(lead + (n_heads * HEAD_DIM,))
    return jnp.moveaxis(w, -1, axis)


def _rope_tables(s_len, n_ctx):
    t = jnp.arange(s_len, dtype=jnp.int32)
    row = (t // GRID_W).astype(F32)
    col = (t % GRID_W).astype(F32)
    n_freq = HEAD_DIM // 4
    inv = ROPE_THETA ** (-jnp.arange(n_freq, dtype=F32) / n_freq)
    ang = jnp.concatenate([row[:, None] * inv, col[:, None] * inv], -1)
    cos = jnp.concatenate([jnp.cos(ang), jnp.ones((n_ctx, HEAD_DIM // 2), F32)], 0)
    sin = jnp.concatenate([jnp.sin(ang), jnp.zeros((n_ctx, HEAD_DIM // 2), F32)], 0)
    return jnp.tile(cos, (1, 4)), jnp.concatenate([-sin, sin, -sin, sin], -1)


def _lambda_init(layer_idx):
    return 0.8 - 0.6 * float(np.exp(-0.3 * layer_idx))


def _router_params(w_group, b_group, w_router, b_router):
    d = w_group.shape[0]
    pad = LANES - N_GROUPS - N_EXPERTS
    w = jnp.concatenate([w_group, w_router, jnp.zeros((d, pad), F32)], axis=1)
    bb = jnp.concatenate([b_group, b_router, jnp.zeros((pad,), F32)])
    w_hi, w_lo = _split_hi_lo(w)
    return jnp.concatenate([w_hi, w_lo], axis=1), bb.reshape(1, LANES)


def _moe_layer(f, route, w1, w3, w2, layer):
    b, n_rows = f.shape[:2]
    t = b * n_rows
    blk_e, blk_b, blk_used, slot_tok, slot_dst = _moe_plan(route.reshape(t, LANES), t, b)
    return _moe_call(f, blk_e, blk_b, blk_used, slot_tok, slot_dst, w1, w3, w2, layer)


def kernel(x, c, ctx, c_ctx, mod_w, mod_b, ln_g, ln_b, ab_w_in, ab_w_out, diff_lambda, diff_subln_g, gqa_qk_g,
           cd_w_in, cd_w_out, win_sink, na_rpb, moe_w_group, moe_b_group, moe_w_router, moe_b_router,
           moe_w1, moe_w3, moe_w2):
    b, s, d = x.shape
    n_ctx = ctx.shape[1]
    n = s + n_ctx
    depth = mod_w.shape[0]
    rows = s // GRID_W
    assert depth == 2, "layer pattern implemented: one differential/GQA layer, then one window/neighbourhood layer"
    assert s % TM == 0 and n_ctx % TM == 0 and s % TK == 0 and b + 1 <= 8
    assert s % TQ_DIFF == 0 and s % TQ_GQA == 0 and n_ctx % TQ_GQA == 0
    assert rows >= NA_SLAB and rows % NA_QROWS == 0 and s % GRID_W == 0
    alpha = (2.0 * depth) ** 0.25
    qw = d // 2
    kvw = qw // 4
    assert qw == 4 * LANES and kvw == LANES

    cos128, sin128 = _rope_tables(s, n_ctx)
    cvec = jnp.concatenate([c, c_ctx[None, :], jnp.zeros((8 - b - 1, d), F32)], axis=0)
    mods_all = _mods_call(cvec, mod_w, mod_b)

    def seg_mods(i):
        m = mods_all[i]
        lat = m[:b].reshape(b, 6, d)
        cm = jnp.broadcast_to(m[b].reshape(1, 6, d), (b, 6, d))
        ms = jnp.stack([lat, cm], axis=1)
        pick = lambda k, one: (ms[:, :, k] + one)[:, :, None, :]
        return pick(0, 0.0), pick(1, 1.0), pick(2, 0.0), pick(3, 0.0), pick(4, 1.0), pick(5, 0.0)

    n_lat = s // TM
    stream0 = (x, ctx, 0)

    shift1, scale1p, gate1, shift2, scale2p, gate2 = seg_mods(0)
    w_in = ab_w_in[0]
    q_d, q_g, k_d, v_d = (w_in[:, k * qw:(k + 1) * qw] for k in range(4))
    k_g, v_g = w_in[:, 4 * qw:4 * qw + kvw], w_in[:, 4 * qw + kvw:]
    w_qk = jnp.concatenate([q_d, _pair_heads(q_g, 1), k_d, k_g], axis=1).astype(BF16)
    roles0 = ((True, None, True),) * 4 + ((True, 0, True),) * 4 + ((True, None, False),) * 4 + ((True, 1, False),)
    w_vt = jnp.concatenate([v_d, v_g], axis=1).T.astype(BF16)
    g128 = jnp.tile(gqa_qk_g[0], (1, 2))
    p0, vt0 = _inproj_call(*stream0, n, shift1, scale1p, w_qk, w_vt, cos128, sin128, g128, roles0, s)
    diff_attn = functools.partial(_diff_attn_call, p0, vt0, diff_lambda[0], diff_subln_g[0], s_len=s, n_ctx=n_ctx,
                                  n_heads=4, q_blk0=0, k_blk0=8, v_blk0=0, lam_init=_lambda_init(0))
    gqa_attn = functools.partial(_gqa_attn_call, p0, vt0, s_len=s, n_ctx=n_ctx, q_width=qw, q_col0=qw,
                                 k_blk=12, v_blk=4)
    o_d = (diff_attn(for_ctx=False), diff_attn(for_ctx=True), 0)
    o_g = (gqa_attn(for_ctx=False), gqa_attn(for_ctx=True), 0)
    w_out = jnp.concatenate([ab_w_out[0][:qw], _pair_heads(ab_w_out[0][qw:], 0)], axis=0).astype(BF16)
    w_route, b_route = _router_params(moe_w_group[0], moe_b_group[0], moe_w_router[0], moe_b_router[0])
    h1, f, route = _outproj_call(o_d, o_g, stream0, w_out, gate1, ln_g[0, 0:1], ln_b[0, 0:1], shift2, scale2p,
                                 w_route, b_route, n_rows=n, s_len=s, alpha=alpha)
    y = _moe_layer(f, route, moe_w1, moe_w3, moe_w2, 0)
    h = _combine_call(h1, y, route, gate2, ln_g[0, 1:2], ln_b[0, 1:2], s_len=s, alpha=alpha)

    shift1, scale1p, gate1, shift2, scale2p, gate2 = seg_mods(1)
    w_in = cd_w_in[0]
    q_w, q_n = w_in[:, :qw], w_in[:, qw:2 * qw]
    k_w, v_w = w_in[:, 2 * qw:2 * qw + kvw], w_in[:, 2 * qw + kvw:2 * qw + 2 * kvw]
    k_n, v_n = w_in[:, 2 * qw + 2 * kvw:3 * qw + 2 * kvw], w_in[:, 3 * qw + 2 * kvw:]
    w_qk = jnp.concatenate([_pair_heads(q_w, 1), q_n, k_w, k_n], axis=1).astype(BF16)
    roles1 = ((True, None, True),) * 4 + ((False, None, True),) * 4 + ((True, None, False),) \
        + ((False, None, False),) * 4
    w_vt = jnp.concatenate([v_w, v_n], axis=1).T.astype(BF16)
    stream1 = (h, h, n_lat)
    p1, vt1 = _inproj_call(*stream1, n, shift1, scale1p, w_qk, w_vt, cos128, sin128, jnp.ones((2, LANES), F32),
                           roles1, s)
    sink_row = jnp.repeat(win_sink[0].reshape(2, 4).T.reshape(8), TQ_WIN).reshape(1, 8 * TQ_WIN)
    o_w = _win_attn_call(p1, vt1, sink_row, s_len=s, n_ctx=n_ctx, q_width=qw, k_blk=8, v_blk=0)
    o_n = _na_attn_call(p1, vt1, _na_bias_tables(na_rpb[0], rows), s_len=s, n_ctx=n_ctx, n_pairs=4,
                        q_blk0=4, k_blk0=9, v_blk0=1)
    w_out = jnp.concatenate([_pair_heads(cd_w_out[0][:qw], 0), cd_w_out[0][qw:]], axis=0).astype(BF16)
    w_route, b_route = _router_params(moe_w_group[1], moe_b_group[1], moe_w_router[1], moe_b_router[1])
    h1, f, route = _outproj_call((o_w, o_w, 0), (o_n, o_n, 0), stream1, w_out, gate1, ln_g[1, 0:1], ln_b[1, 0:1],
                                 shift2, scale2p, w_route, b_route, n_rows=s, s_len=s, alpha=alpha)
    y = _moe_layer(f, route, moe_w1, moe_w3, moe_w2, 1)
    return _combine_call(h1, y, route, gate2, ln_g[1, 1:2], ln_b[1, 1:2], s_len=s, alpha=alpha)
```
